```python
import jax, jax.numpy as jnp
from jax import lax
import numpy as np

D_MODEL = 2048
BATCH = 8
SEQ = 8192
DEPTH = 2

N_META = 16
N_MIXERS = 2
CONV_WIDTH = 3
HEAD_DIM = 64
N_Q_HEADS = D_MODEL // HEAD_DIM
N_KV_HEADS = N_Q_HEADS // 8
GROUP = N_Q_HEADS // N_KV_HEADS
WINDOW = 128
BLOCK = 128
ROPE_THETA = 10000.0
D_FF = 4 * D_MODEL
RMS_EPS = 1e-5
NEG_INF = -1e30

kernel_name = "hybrid_shortconv_swa_sink_block"


def rms_norm(x, g):
    xf = x.astype(jnp.float32)
    var = jnp.mean(xf * xf, axis=-1, keepdims=True)
    return (xf * lax.rsqrt(var + RMS_EPS)).astype(x.dtype) * g


def short_conv_mixer(h, w_in, conv_w, w_out):
    bcu = h @ w_in
    b_gate, c_gate, u = jnp.split(bcu, 3, axis=-1)
    v = c_gate * u
    L = v.shape[1]
    vp = jnp.pad(v, ((0, 0), (CONV_WIDTH - 1, 0), (0, 0)))
    conv = conv_w[0] * vp[:, 0:L]
    for k in range(1, CONV_WIDTH):
        conv = conv + conv_w[k] * vp[:, k:k + L]
    return (b_gate * conv) @ w_out


def rope_tables(n_pos, offset):
    pos = jnp.arange(n_pos, dtype=jnp.float32) - offset
    inv = ROPE_THETA ** (-jnp.arange(0, HEAD_DIM, 2, dtype=jnp.float32) / HEAD_DIM)
    ang = pos[:, None] * inv[None, :]
    return jnp.cos(ang), jnp.sin(ang)


def apply_rope(x, cos, sin):
    x1, x2 = jnp.split(x, 2, axis=-1)
    c = cos[None, :, None, :]
    s = sin[None, :, None, :]
    return jnp.concatenate([x1 * c - x2 * s, x2 * c + x1 * s], axis=-1).astype(x.dtype)


def swa_sink_mixer(h, w_qkv, sinks, w_o):
    Bsz, L, _ = h.shape
    pad = (-L) % BLOCK
    P = L + pad
    nb = P // BLOCK
    hp = jnp.pad(h, ((0, 0), (pad, 0), (0, 0)))
    qkv = hp @ w_qkv
    q, k, v = jnp.split(qkv, [N_Q_HEADS * HEAD_DIM, (N_Q_HEADS + N_KV_HEADS) * HEAD_DIM], axis=-1)
    q = q.reshape(Bsz, P, N_Q_HEADS, HEAD_DIM)
    k = k.reshape(Bsz, P, N_KV_HEADS, HEAD_DIM)
    v = v.reshape(Bsz, P, N_KV_HEADS, HEAD_DIM)
    cos, sin = rope_tables(P, pad)
    q = apply_rope(q, cos, sin)
    k = apply_rope(k, cos, sin)

    qb = q.reshape(Bsz, nb, BLOCK, N_KV_HEADS, GROUP, HEAD_DIM)
    kb = k.reshape(Bsz, nb, BLOCK, N_KV_HEADS, HEAD_DIM)
    vb = v.reshape(Bsz, nb, BLOCK, N_KV_HEADS, HEAD_DIM)
    zpad = ((0, 0), (1, 0), (0, 0), (0, 0), (0, 0))
    k_band = jnp.concatenate([jnp.pad(kb[:, :-1], zpad), kb], axis=2)
    v_band = jnp.concatenate([jnp.pad(vb[:, :-1], zpad), vb], axis=2)

    scale = HEAD_DIM ** -0.5
    s = jnp.einsum("bnqhgd,bnkhd->bhgnqk", qb, k_band).astype(jnp.float32) * scale

    blk = jnp.arange(nb)[:, None, None] * BLOCK
    q_idx = blk + jnp.arange(BLOCK)[None, :, None]
    k_idx = blk - BLOCK + jnp.arange(2 * BLOCK)[None, None, :]
    diff = q_idx - k_idx
    allowed = (diff >= 0) & (diff < WINDOW) & (k_idx >= pad)
    s = jnp.where(allowed[None, None, None], s, NEG_INF)

    sink = sinks.astype(jnp.float32).reshape(N_KV_HEADS, GROUP)[None, :, :, None, None, None]
    m = jnp.maximum(jnp.max(s, axis=-1, keepdims=True), sink)
    e = jnp.exp(s - m)
    den = jnp.sum(e, axis=-1, keepdims=True) + jnp.exp(sink - m)
    p = (e / den).astype(v.dtype)

    o = jnp.einsum("bhgnqk,bnkhd->bnqhgd", p, v_band).reshape(Bsz, P, N_Q_HEADS * HEAD_DIM)
    return o[:, pad:] @ w_o


def squared_relu_mlp(h, w_up, w_down):
    a = jax.nn.relu(h @ w_up)
    return (a * a) @ w_down


def _fwd_setup_inputs(seed: int = 0) -> dict:
    key = jax.random.key(seed)
    ks = jax.random.split(key, 20)
    D = D_MODEL
    f32 = jnp.float32

    def nrm(k, shape, scale):
        return jax.random.normal(k, shape, f32) * scale

    def gain(k):
        return jnp.ones((D,), f32) + 0.02 * jax.random.normal(k, (D,), f32)

    return {
        "x": nrm(ks[0], (BATCH, SEQ, D), 1.0),
        "meta_tokens": nrm(ks[1], (N_META, D), 1.0),
        "norm_mix_0": gain(ks[2]),
        "w_in_conv": nrm(ks[3], (D, 3 * D), D ** -0.5),
        "conv_w": nrm(ks[4], (CONV_WIDTH, D), CONV_WIDTH ** -0.5),
        "w_out_conv": nrm(ks[5], (D, D), D ** -0.5),
        "norm_mlp_0": gain(ks[6]),
        "w_up_0": nrm(ks[7], (D, D_FF), D ** -0.5),
        "w_down_0": nrm(ks[8], (D_FF, D), D_FF ** -0.5),
        "norm_mix_1": gain(ks[9]),
        "w_qkv": nrm(ks[10], (D, (N_Q_HEADS + 2 * N_KV_HEADS) * HEAD_DIM), D ** -0.5),
        "attn_sinks": nrm(ks[11], (N_Q_HEADS,), 0.5),
        "w_o": nrm(ks[12], (N_Q_HEADS * HEAD_DIM, D), (N_Q_HEADS * HEAD_DIM) ** -0.5),
        "norm_mlp_1": gain(ks[13]),
        "w_up_1": nrm(ks[14], (D, D_FF), D ** -0.5),
        "w_down_1": nrm(ks[15], (D_FF, D), D_FF ** -0.5),
        "norm_final": gain(ks[16]),
    }


def _fwd_reference(x, meta_tokens, norm_mix_0, w_in_conv, conv_w, w_out_conv, norm_mlp_0, w_up_0, w_down_0,
              norm_mix_1, w_qkv, attn_sinks, w_o, norm_mlp_1, w_up_1, w_down_1, norm_final):
    Bsz = x.shape[0]
    meta = jnp.broadcast_to(meta_tokens[None].astype(x.dtype), (Bsz, N_META, D_MODEL))
    h = jnp.concatenate([meta, x], axis=1)

    mixers = [
        lambda t: short_conv_mixer(t, w_in_conv, conv_w, w_out_conv),
        lambda t: swa_sink_mixer(t, w_qkv, attn_sinks, w_o),
    ]
    mix_norms = [norm_mix_0, norm_mix_1]
    mlps = [(norm_mlp_0, w_up_0, w_down_0), (norm_mlp_1, w_up_1, w_down_1)]

    for i in range(DEPTH):
        h = h + mixers[i % N_MIXERS](rms_norm(h, mix_norms[i]))
        g, wu, wd = mlps[i]
        h = h + squared_relu_mlp(rms_norm(h, g), wu, wd)

    out = rms_norm(h, norm_final)
    return out[:, N_META:]


import jax as _jax
import jax.numpy as _jnp

TWIN_FORMAT = 'train_step'
FWD_PARAMS = ['x', 'meta_tokens', 'norm_mix_0', 'w_in_conv', 'conv_w', 'w_out_conv', 'norm_mlp_0', 'w_up_0', 'w_down_0', 'norm_mix_1', 'w_qkv', 'attn_sinks', 'w_o', 'norm_mlp_1', 'w_up_1', 'w_down_1', 'norm_final']
TWIN_WEIGHTS = ['meta_tokens', 'norm_mix_0', 'w_in_conv', 'conv_w', 'w_out_conv', 'norm_mlp_0', 'w_up_0', 'w_down_0', 'norm_mix_1', 'w_qkv', 'attn_sinks', 'w_o', 'norm_mlp_1', 'w_up_1', 'w_down_1', 'norm_final']
TWIN_DIFF_INPUT = 'x'
TWIN_INPUTS = ['x', 'meta_tokens', 'norm_mix_0', 'w_in_conv', 'conv_w', 'w_out_conv', 'norm_mlp_0', 'w_up_0', 'w_down_0', 'norm_mix_1', 'w_qkv', 'attn_sinks', 'w_o', 'norm_mlp_1', 'w_up_1', 'w_down_1', 'norm_final', 'loss_target', 'm_meta_tokens', 'm_norm_mix_0', 'm_w_in_conv', 'm_conv_w', 'm_w_out_conv', 'm_norm_mlp_0', 'm_w_up_0', 'm_w_down_0', 'm_norm_mix_1', 'm_w_qkv', 'm_attn_sinks', 'm_w_o', 'm_norm_mlp_1', 'm_w_up_1', 'm_w_down_1', 'm_norm_final', 'v_meta_tokens', 'v_norm_mix_0', 'v_w_in_conv', 'v_conv_w', 'v_w_out_conv', 'v_norm_mlp_0', 'v_w_up_0', 'v_w_down_0', 'v_norm_mix_1', 'v_w_qkv', 'v_attn_sinks', 'v_w_o', 'v_norm_mlp_1', 'v_w_up_1', 'v_w_down_1', 'v_norm_final']
TWIN_OUTPUTS = ['loss', 'grad_x', 'grad_meta_tokens', 'grad_norm_mix_0', 'grad_w_in_conv', 'grad_conv_w', 'grad_w_out_conv', 'grad_norm_mlp_0', 'grad_w_up_0', 'grad_w_down_0', 'grad_norm_mix_1', 'grad_w_qkv', 'grad_attn_sinks', 'grad_w_o', 'grad_norm_mlp_1', 'grad_w_up_1', 'grad_w_down_1', 'grad_norm_final', 'delta_meta_tokens', 'delta_norm_mix_0', 'delta_w_in_conv', 'delta_conv_w', 'delta_w_out_conv', 'delta_norm_mlp_0', 'delta_w_up_0', 'delta_w_down_0', 'delta_norm_mix_1', 'delta_w_qkv', 'delta_attn_sinks', 'delta_w_o', 'delta_norm_mlp_1', 'delta_w_up_1', 'delta_w_down_1', 'delta_norm_final', 'new_m_meta_tokens', 'new_m_norm_mix_0', 'new_m_w_in_conv', 'new_m_conv_w', 'new_m_w_out_conv', 'new_m_norm_mlp_0', 'new_m_w_up_0', 'new_m_w_down_0', 'new_m_norm_mix_1', 'new_m_w_qkv', 'new_m_attn_sinks', 'new_m_w_o', 'new_m_norm_mlp_1', 'new_m_w_up_1', 'new_m_w_down_1', 'new_m_norm_final', 'new_v_meta_tokens', 'new_v_norm_mix_0', 'new_v_w_in_conv', 'new_v_conv_w', 'new_v_w_out_conv', 'new_v_norm_mlp_0', 'new_v_w_up_0', 'new_v_w_down_0', 'new_v_norm_mix_1', 'new_v_w_qkv', 'new_v_attn_sinks', 'new_v_w_o', 'new_v_norm_mlp_1', 'new_v_w_up_1', 'new_v_w_down_1', 'new_v_norm_final']
TWIN_LEAF_KINDS = {'loss': 'loss', 'grad_x': 'grad_x', 'grad_meta_tokens': 'grad_w', 'grad_norm_mix_0': 'grad_w', 'grad_w_in_conv': 'grad_w', 'grad_conv_w': 'grad_w', 'grad_w_out_conv': 'grad_w', 'grad_norm_mlp_0': 'grad_w', 'grad_w_up_0': 'grad_w', 'grad_w_down_0': 'grad_w', 'grad_norm_mix_1': 'grad_w', 'grad_w_qkv': 'grad_w', 'grad_attn_sinks': 'grad_w', 'grad_w_o': 'grad_w', 'grad_norm_mlp_1': 'grad_w', 'grad_w_up_1': 'grad_w', 'grad_w_down_1': 'grad_w', 'grad_norm_final': 'grad_w', 'delta_meta_tokens': 'delta_w', 'delta_norm_mix_0': 'delta_w', 'delta_w_in_conv': 'delta_w', 'delta_conv_w': 'delta_w', 'delta_w_out_conv': 'delta_w', 'delta_norm_mlp_0': 'delta_w', 'delta_w_up_0': 'delta_w', 'delta_w_down_0': 'delta_w', 'delta_norm_mix_1': 'delta_w', 'delta_w_qkv': 'delta_w', 'delta_attn_sinks': 'delta_w', 'delta_w_o': 'delta_w', 'delta_norm_mlp_1': 'delta_w', 'delta_w_up_1': 'delta_w', 'delta_w_down_1': 'delta_w', 'delta_norm_final': 'delta_w', 'new_m_meta_tokens': 'new_m', 'new_m_norm_mix_0': 'new_m', 'new_m_w_in_conv': 'new_m', 'new_m_conv_w': 'new_m', 'new_m_w_out_conv': 'new_m', 'new_m_norm_mlp_0': 'new_m', 'new_m_w_up_0': 'new_m', 'new_m_w_down_0': 'new_m', 'new_m_norm_mix_1': 'new_m', 'new_m_w_qkv': 'new_m', 'new_m_attn_sinks': 'new_m', 'new_m_w_o': 'new_m', 'new_m_norm_mlp_1': 'new_m', 'new_m_w_up_1': 'new_m', 'new_m_w_down_1': 'new_m', 'new_m_norm_final': 'new_m', 'new_v_meta_tokens': 'new_v', 'new_v_norm_mix_0': 'new_v', 'new_v_w_in_conv': 'new_v', 'new_v_conv_w': 'new_v', 'new_v_w_out_conv': 'new_v', 'new_v_norm_mlp_0': 'new_v', 'new_v_w_up_0': 'new_v', 'new_v_w_down_0': 'new_v', 'new_v_norm_mix_1': 'new_v', 'new_v_w_qkv': 'new_v', 'new_v_attn_sinks': 'new_v', 'new_v_w_o': 'new_v', 'new_v_norm_mlp_1': 'new_v', 'new_v_w_up_1': 'new_v', 'new_v_w_down_1': 'new_v', 'new_v_norm_final': 'new_v'}


def _forward(args):
    return _fwd_reference(*[args[k] for k in FWD_PARAMS])


def _output_shape():
    def fwd():
        inp = _fwd_setup_inputs(0)
        return _fwd_reference(*[inp[k] for k in FWD_PARAMS])
    out = _jax.eval_shape(fwd)
    return out.shape, out.dtype

N_MICROBATCH = 1
ADAM_LR = 0.001
ADAM_B1 = 0.9
ADAM_B2 = 0.999
ADAM_EPS = 1e-08
ADAM_WD = 0.01
ADAM_STEP = 10
PER_EXAMPLE_BATCH_AXIS = {'x': 0, 'loss_target': 0}
SHARED_INPUTS = []
_WEIGHT_DTYPES = {'meta_tokens': _jnp.float32, 'norm_mix_0': _jnp.float32, 'w_in_conv': _jnp.float32, 'conv_w': _jnp.float32, 'w_out_conv': _jnp.float32, 'norm_mlp_0': _jnp.float32, 'w_up_0': _jnp.float32, 'w_down_0': _jnp.float32, 'norm_mix_1': _jnp.float32, 'w_qkv': _jnp.float32, 'attn_sinks': _jnp.float32, 'w_o': _jnp.float32, 'norm_mlp_1': _jnp.float32, 'w_up_1': _jnp.float32, 'w_down_1': _jnp.float32, 'norm_final': _jnp.float32}
MOMENT_SCALE = {'meta_tokens': 3.631818e-03, 'norm_mix_0': 1.719882e-01, 'w_in_conv': 9.924463e-02, 'conv_w': 9.962782e-02, 'w_out_conv': 9.912304e-02, 'norm_mlp_0': 1.001376e-01, 'w_up_0': 4.955886e-02, 'w_down_0': 1.091912e-01, 'norm_mix_1': 3.809185e-02, 'w_qkv': 3.507546e-02, 'attn_sinks': 2.444729e-03, 'w_o': 3.864580e-02, 'norm_mlp_1': 8.075012e-02, 'w_up_1': 4.146385e-02, 'w_down_1': 9.106936e-02, 'norm_final': 3.251556e+01}


def _to_microbatches(a, axis):
    t = _jnp.moveaxis(a, axis, 0)
    t = t.reshape((N_MICROBATCH, t.shape[0] // N_MICROBATCH) + t.shape[1:])
    return _jnp.moveaxis(t, 1, axis + 1)


def setup_inputs(seed: int = 0) -> dict:
    inp = _fwd_setup_inputs(seed)
    key = _jax.random.fold_in(_jax.random.key(seed), 7919)
    shape, _ = _output_shape()
    out = dict(inp)
    out["loss_target"] = _jax.random.normal(_jax.random.fold_in(key, 0), shape, _jnp.float32)
    for i, name in enumerate(TWIN_WEIGHTS):
        w = inp[name].astype(_jnp.float32)
        if MOMENT_SCALE is None:
            s = _jnp.sqrt(_jnp.mean(_jnp.square(w)) + 1e-30)
        else:
            s = MOMENT_SCALE[name]
        km, kv = _jax.random.split(_jax.random.fold_in(key, i + 1))
        out[name] = w
        out["m_" + name] = s * _jax.random.normal(km, w.shape, _jnp.float32)
        out["v_" + name] = (s * s) * _jax.random.uniform(kv, w.shape, _jnp.float32, 0.5, 1.5)
    if N_MICROBATCH > 1:
        for name, axis in PER_EXAMPLE_BATCH_AXIS.items():
            out[name] = _to_microbatches(out[name], axis)
    return {'x': out['x'], 'meta_tokens': out['meta_tokens'], 'norm_mix_0': out['norm_mix_0'], 'w_in_conv': out['w_in_conv'], 'conv_w': out['conv_w'], 'w_out_conv': out['w_out_conv'], 'norm_mlp_0': out['norm_mlp_0'], 'w_up_0': out['w_up_0'], 'w_down_0': out['w_down_0'], 'norm_mix_1': out['norm_mix_1'], 'w_qkv': out['w_qkv'], 'attn_sinks': out['attn_sinks'], 'w_o': out['w_o'], 'norm_mlp_1': out['norm_mlp_1'], 'w_up_1': out['w_up_1'], 'w_down_1': out['w_down_1'], 'norm_final': out['norm_final'], 'loss_target': out['loss_target'], 'm_meta_tokens': out['m_meta_tokens'], 'm_norm_mix_0': out['m_norm_mix_0'], 'm_w_in_conv': out['m_w_in_conv'], 'm_conv_w': out['m_conv_w'], 'm_w_out_conv': out['m_w_out_conv'], 'm_norm_mlp_0': out['m_norm_mlp_0'], 'm_w_up_0': out['m_w_up_0'], 'm_w_down_0': out['m_w_down_0'], 'm_norm_mix_1': out['m_norm_mix_1'], 'm_w_qkv': out['m_w_qkv'], 'm_attn_sinks': out['m_attn_sinks'], 'm_w_o': out['m_w_o'], 'm_norm_mlp_1': out['m_norm_mlp_1'], 'm_w_up_1': out['m_w_up_1'], 'm_w_down_1': out['m_w_down_1'], 'm_norm_final': out['m_norm_final'], 'v_meta_tokens': out['v_meta_tokens'], 'v_norm_mix_0': out['v_norm_mix_0'], 'v_w_in_conv': out['v_w_in_conv'], 'v_conv_w': out['v_conv_w'], 'v_w_out_conv': out['v_w_out_conv'], 'v_norm_mlp_0': out['v_norm_mlp_0'], 'v_w_up_0': out['v_w_up_0'], 'v_w_down_0': out['v_w_down_0'], 'v_norm_mix_1': out['v_norm_mix_1'], 'v_w_qkv': out['v_w_qkv'], 'v_attn_sinks': out['v_attn_sinks'], 'v_w_o': out['v_w_o'], 'v_norm_mlp_1': out['v_norm_mlp_1'], 'v_w_up_1': out['v_w_up_1'], 'v_w_down_1': out['v_w_down_1'], 'v_norm_final': out['v_norm_final']}


def _loss(weights, diff, rest, loss_target):
    with _jax.named_scope("forward"):
        args = {**rest, TWIN_DIFF_INPUT: diff, **{k: w.astype(_WEIGHT_DTYPES[k]) for k, w in weights.items()}}
        y = _forward(args)
    with _jax.named_scope("loss_head"):
        err = _jnp.square(y.astype(_jnp.float32) - loss_target)
        return 0.5 * _jnp.sum(_jnp.mean(err, axis=-1)) if err.ndim else 0.5 * err


def _adamw(w, g, m, v):
    m = ADAM_B1 * m + (1.0 - ADAM_B1) * g
    v = ADAM_B2 * v + (1.0 - ADAM_B2) * _jnp.square(g)
    m_hat = m / (1.0 - ADAM_B1 ** ADAM_STEP)
    v_hat = v / (1.0 - ADAM_B2 ** ADAM_STEP)
    delta = -ADAM_LR * (m_hat / (_jnp.sqrt(v_hat) + ADAM_EPS) + ADAM_WD * w)
    return delta, m, v


def reference(x, meta_tokens, norm_mix_0, w_in_conv, conv_w, w_out_conv, norm_mlp_0, w_up_0, w_down_0, norm_mix_1, w_qkv, attn_sinks, w_o, norm_mlp_1, w_up_1, w_down_1, norm_final, loss_target, m_meta_tokens, m_norm_mix_0, m_w_in_conv, m_conv_w, m_w_out_conv, m_norm_mlp_0, m_w_up_0, m_w_down_0, m_norm_mix_1, m_w_qkv, m_attn_sinks, m_w_o, m_norm_mlp_1, m_w_up_1, m_w_down_1, m_norm_final, v_meta_tokens, v_norm_mix_0, v_w_in_conv, v_conv_w, v_w_out_conv, v_norm_mlp_0, v_w_up_0, v_w_down_0, v_norm_mix_1, v_w_qkv, v_attn_sinks, v_w_o, v_norm_mlp_1, v_w_up_1, v_w_down_1, v_norm_final):
    given = dict(x=x, meta_tokens=meta_tokens, norm_mix_0=norm_mix_0, w_in_conv=w_in_conv, conv_w=conv_w, w_out_conv=w_out_conv, norm_mlp_0=norm_mlp_0, w_up_0=w_up_0, w_down_0=w_down_0, norm_mix_1=norm_mix_1, w_qkv=w_qkv, attn_sinks=attn_sinks, w_o=w_o, norm_mlp_1=norm_mlp_1, w_up_1=w_up_1, w_down_1=w_down_1, norm_final=norm_final, loss_target=loss_target, m_meta_tokens=m_meta_tokens, m_norm_mix_0=m_norm_mix_0, m_w_in_conv=m_w_in_conv, m_conv_w=m_conv_w, m_w_out_conv=m_w_out_conv, m_norm_mlp_0=m_norm_mlp_0, m_w_up_0=m_w_up_0, m_w_down_0=m_w_down_0, m_norm_mix_1=m_norm_mix_1, m_w_qkv=m_w_qkv, m_attn_sinks=m_attn_sinks, m_w_o=m_w_o, m_norm_mlp_1=m_norm_mlp_1, m_w_up_1=m_w_up_1, m_w_down_1=m_w_down_1, m_norm_final=m_norm_final, v_meta_tokens=v_meta_tokens, v_norm_mix_0=v_norm_mix_0, v_w_in_conv=v_w_in_conv, v_conv_w=v_conv_w, v_w_out_conv=v_w_out_conv, v_norm_mlp_0=v_norm_mlp_0, v_w_up_0=v_w_up_0, v_w_down_0=v_w_down_0, v_norm_mix_1=v_norm_mix_1, v_w_qkv=v_w_qkv, v_attn_sinks=v_attn_sinks, v_w_o=v_w_o, v_norm_mlp_1=v_norm_mlp_1, v_w_up_1=v_w_up_1, v_w_down_1=v_w_down_1, v_norm_final=v_norm_final)
    weights = {n: given[n] for n in TWIN_WEIGHTS}
    shared = {n: given[n] for n in SHARED_INPUTS}
    per_example = {n: given[n] for n in ['x']}
    grad_fn = _jax.value_and_grad(_loss, argnums=(0, 1))

    def one_microbatch(ex, loss_target):
        ex = dict(ex)
        diff = ex.pop(TWIN_DIFF_INPUT)
        return grad_fn(weights, diff, {**shared, **ex}, loss_target)

    if N_MICROBATCH == 1:
        loss, (grad_w, grad_x) = one_microbatch(per_example, given["loss_target"])
    else:
        def body(carry, xs):
            loss_sum, grad_sum = carry
            l_k, (gw_k, gx_k) = one_microbatch(xs[0], xs[1])
            with _jax.named_scope("update"):
                return (loss_sum + l_k, _jax.tree.map(_jnp.add, grad_sum, gw_k)), gx_k

        init = (_jnp.zeros((), _jnp.float32), _jax.tree.map(_jnp.zeros_like, weights))
        (loss, grad_w), grad_x = _jax.lax.scan(body, init, (per_example, given["loss_target"]))
    with _jax.named_scope("update"):
        delta_w, new_m, new_v = {}, {}, {}
        for n in TWIN_WEIGHTS:
            delta_w[n], new_m[n], new_v[n] = _adamw(weights[n], grad_w[n], given["m_" + n], given["v_" + n])
    return (loss, grad_x, *[grad_w[n] for n in TWIN_WEIGHTS], *[delta_w[n] for n in TWIN_WEIGHTS],
            *[new_m[n] for n in TWIN_WEIGHTS], *[new_v[n] for n in TWIN_WEIGHTS])
```

```python
import functools

import jax
import jax.numpy as jnp
from jax import lax
from jax.experimental import pallas as pl
from jax.experimental.pallas import tpu as pltpu

F32 = jnp.float32
BF16 = jnp.bfloat16

HEAD_DIM = 64
GROUP = 8
BLOCK = 128
N_DEV = 8
RMS_EPS = 1e-5
NEG_INF = -1e30
ROPE_THETA = 10000.0
ADAM_LR = 0.001
ADAM_B1 = 0.9
ADAM_B2 = 0.999
ADAM_EPS = 1e-08
ADAM_WD = 0.01
ADAM_STEP = 10
VMEM_LIMIT = 60 * 1024 * 1024
MESH = pl.DeviceIdType.MESH


def _tile(n, target, mult):
    best = None
    for t in range(mult, min(n, target) + 1, mult):
        if n % t == 0:
            best = t
    return best if best is not None else n


def _params(sem):
    return pltpu.CompilerParams(dimension_semantics=sem, vmem_limit_bytes=VMEM_LIMIT)


def mm_nt(a, bt, *, name, out_dtype, epi=None, extra=None):
    M, K = a.shape
    N = bt.shape[0]
    tm = _tile(M, 1664, 16)
    tn = _tile(N, 512, 128)

    def body(*refs):
        if extra is not None:
            a_ref, b_ref, e_ref, o_ref = refs
        else:
            a_ref, b_ref, o_ref = refs
        acc = lax.dot_general(a_ref[...], b_ref[...], (((1,), (1,)), ((), ())), preferred_element_type=F32)
        if epi == "relu":
            acc = jnp.maximum(acc, 0.0)
        elif epi == "mul2a":
            acc = acc * (2.0 * e_ref[...].astype(F32))
        o_ref[...] = acc.astype(o_ref.dtype)

    in_specs = [pl.BlockSpec((tm, K), lambda i, j: (i, 0)), pl.BlockSpec((tn, K), lambda i, j: (j, 0))]
    args = [a, bt]
    if extra is not None:
        in_specs.append(pl.BlockSpec((tm, tn), lambda i, j: (i, j)))
        args.append(extra)
    return pl.pallas_call(
        body, name=name, grid=(M // tm, N // tn), in_specs=in_specs,
        out_specs=pl.BlockSpec((tm, tn), lambda i, j: (i, j)),
        out_shape=jax.ShapeDtypeStruct((M, N), out_dtype),
        compiler_params=_params(("parallel", "parallel")),
    )(*args)


def _rms_parts(h):
    rstd = lax.rsqrt(jnp.mean(h * h, axis=-1, keepdims=True) + RMS_EPS)
    return h * rstd, rstd


def mm_row(a, b, *, name, epi, a_sq=False, res=None, g=None, h=None, dres=None):
    M, K = a.shape
    N = b.shape[1]
    tm = _tile(M, 416 if epi == "normbwd" else 640, 16)
    tk = _tile(K, 512, 128)
    nk = K // tk
    rc = _tile(tm, 160, 8)

    def body(*refs):
        acc_ref = refs[-1]
        refs = refs[:-1]
        a_ref, b_ref = refs[:2]
        i = pl.program_id(0)
        k = pl.program_id(1)
        av = a_ref[...]
        if a_sq:
            av = av.astype(F32)
            av = (av * av).astype(BF16)
        part = jnp.dot(av, b_ref[...], preferred_element_type=F32)

        @pl.when(k == 0)
        def _():
            acc_ref[...] = part

        @pl.when(k > 0)
        def _():
            acc_ref[...] += part

        @pl.when(k == nk - 1)
        def _():
            if epi == "res":
                res_ref, o_ref = refs[2:]
                o_ref[...] = res_ref[...] + acc_ref[...]
            elif epi == "resnorm":
                res_ref, g_ref, o_ref, n_ref = refs[2:]
                for r0 in range(0, tm, rc):
                    rows = pl.ds(r0, rc)
                    hn = res_ref[rows, :] + acc_ref[rows, :]
                    o_ref[rows, :] = hn
                    xhat, _ = _rms_parts(hn)
                    n_ref[rows, :] = (xhat * g_ref[...]).astype(BF16)
            else:
                h_ref, g_ref, dres_ref, dh_ref, dhb_ref, dg_ref = refs[2:]
                dg = jnp.zeros((1, N), F32)
                for r0 in range(0, tm, rc):
                    rows = pl.ds(r0, rc)
                    dn = acc_ref[rows, :]
                    xhat, rstd = _rms_parts(h_ref[rows, :])
                    dxh = dn * g_ref[...]
                    dh = dres_ref[rows, :] + rstd * (dxh - xhat * jnp.mean(dxh * xhat, axis=-1, keepdims=True))
                    dh_ref[rows, :] = dh
                    dhb_ref[rows, :] = dh.astype(BF16)
                    dg = dg + jnp.sum(dn * xhat, axis=0, keepdims=True)

                @pl.when(i == 0)
                def _():
                    dg_ref[...] = dg

                @pl.when(i > 0)
                def _():
                    dg_ref[...] += dg

    row_spec = pl.BlockSpec((tm, N), lambda i, k: (i, 0))
    vec_spec = pl.BlockSpec((1, N), lambda i, k: (0, 0))
    in_specs = [pl.BlockSpec((tm, tk), lambda i, k: (i, k)), pl.BlockSpec((tk, N), lambda i, k: (k, 0))]
    args = [a, b]
    if epi == "res":
        in_specs += [row_spec]
        args += [res]
        out_specs = row_spec
        out_shape = jax.ShapeDtypeStruct((M, N), F32)
    elif epi == "resnorm":
        in_specs += [row_spec, vec_spec]
        args += [res, g]
        out_specs = [row_spec, row_spec]
        out_shape = [jax.ShapeDtypeStruct((M, N), F32), jax.ShapeDtypeStruct((M, N), BF16)]
    else:
        in_specs += [row_spec, vec_spec, row_spec]
        args += [h, g, dres]
        out_specs = [row_spec, row_spec, vec_spec]
        out_shape = [jax.ShapeDtypeStruct((M, N), F32), jax.ShapeDtypeStruct((M, N), BF16),
                     jax.ShapeDtypeStruct((1, N), F32)]
    return pl.pallas_call(
        body, name=name, grid=(M // tm, nk), in_specs=in_specs, out_specs=out_specs, out_shape=out_shape,
        scratch_shapes=[pltpu.VMEM((tm, N), F32)],
        compiler_params=_params(("arbitrary", "arbitrary")),
    )(*args)


def mm_tn(a, b, *, name, a_sq=False):
    T, M = a.shape
    N = b.shape[1]
    tm = _tile(M, 1024, 128)
    tk = _tile(T, 640, 16)
    nk = T // tk

    def body(a_ref, b_ref, o_ref, acc_ref):
        k = pl.program_id(1)
        av = a_ref[...]
        if a_sq:
            av = av.astype(F32)
            av = (av * av).astype(BF16)
        part = lax.dot_general(av, b_ref[...], (((0,), (0,)), ((), ())), preferred_element_type=F32)

        @pl.when(k == 0)
        def _():
            acc_ref[...] = part

        @pl.when(k > 0)
        def _():
            acc_ref[...] += part

        @pl.when(k == nk - 1)
        def _():
            o_ref[...] = acc_ref[...].astype(BF16)

    return pl.pallas_call(
        body, name=name, grid=(M // tm, nk),
        in_specs=[pl.BlockSpec((tk, tm), lambda i, k: (k, i)), pl.BlockSpec((tk, N), lambda i, k: (k, 0))],
        out_specs=pl.BlockSpec((tm, N), lambda i, k: (i, 0)),
        out_shape=jax.ShapeDtypeStruct((M, N), BF16),
        scratch_shapes=[pltpu.VMEM((tm, N), F32)],
        compiler_params=_params(("parallel", "arbitrary")),
    )(a, b)


def norm_fwd(h, g, *, name):
    R, D = h.shape
    tr = _tile(R, 320, 16)

    def body(h_ref, g_ref, n_ref):
        xhat, _ = _rms_parts(h_ref[...])
        n_ref[...] = (xhat * g_ref[...]).astype(BF16)

    return pl.pallas_call(
        body, name=name, grid=(R // tr,),
        in_specs=[pl.BlockSpec((tr, D), lambda i: (i, 0)), pl.BlockSpec((1, D), lambda i: (0, 0))],
        out_specs=pl.BlockSpec((tr, D), lambda i: (i, 0)),
        out_shape=jax.ShapeDtypeStruct((R, D), BF16),
        compiler_params=_params(("parallel",)),
    )(h, g)


def _shift_down(v, halo, s):
    n = v.shape[0]
    row = lax.broadcasted_iota(jnp.int32, v.shape, 0)
    out = pltpu.roll(v, s, axis=0)
    hn = halo.shape[0]
    for r in range(s):
        out = jnp.where(row == r, halo[hn - s + r:hn - s + r + 1, :], out)
    return out


def _shift_up(v, halo, s):
    n = v.shape[0]
    row = lax.broadcasted_iota(jnp.int32, v.shape, 0)
    out = pltpu.roll(v, n - s, axis=0)
    for r in range(s):
        out = jnp.where(row == n - s + r, halo[r:r + 1, :], out)
    return out


HALO = 16


def conv_fwd(bcu, conv_w, *, name):
    R, D3 = bcu.shape
    D = D3 // 3
    tr = _tile(R, 320, 16)
    tc = _tile(D, 512, 128)
    nc = D // tc
    hb = tr // HALO

    def body(b_ref, c_ref, u_ref, ch_ref, uh_ref, w_ref, o_ref):
        i = pl.program_id(0)
        v = c_ref[...].astype(F32) * u_ref[...].astype(F32)
        vh = ch_ref[...].astype(F32) * uh_ref[...].astype(F32)
        vh = jnp.where(i > 0, vh, 0.0)
        v1 = _shift_down(v, vh, 1)
        v2 = _shift_down(v, vh, 2)
        conv = w_ref[0:1, :] * v2 + w_ref[1:2, :] * v1 + w_ref[2:3, :] * v
        o_ref[...] = (b_ref[...].astype(F32) * conv).astype(BF16)

    def prev(i):
        return jnp.maximum(i * hb - 1, 0)

    return pl.pallas_call(
        body, name=name, grid=(R // tr, nc),
        in_specs=[pl.BlockSpec((tr, tc), lambda i, j: (i, j)),
                  pl.BlockSpec((tr, tc), lambda i, j: (i, nc + j)),
                  pl.BlockSpec((tr, tc), lambda i, j: (i, 2 * nc + j)),
                  pl.BlockSpec((HALO, tc), lambda i, j: (prev(i), nc + j)),
                  pl.BlockSpec((HALO, tc), lambda i, j: (prev(i), 2 * nc + j)),
                  pl.BlockSpec((3, tc), lambda i, j: (0, j))],
        out_specs=pl.BlockSpec((tr, tc), lambda i, j: (i, j)),
        out_shape=jax.ShapeDtypeStruct((R, D), BF16),
        compiler_params=_params(("parallel", "parallel")),
    )(bcu, bcu, bcu, bcu, bcu, conv_w)


def conv_bwd(bcu, conv_w, dg, *, name):
    R, D3 = bcu.shape
    D = D3 // 3
    tr = _tile(R, 320, 16)
    tc = _tile(D, 512, 128)
    nc = D // tc
    hb = tr // HALO
    nt = R // tr
    last_h = R // HALO - 1

    def body(b_ref, c_ref, u_ref, ch_ref, uh_ref, bn_ref, dg_ref, dgn_ref, w_ref, db_ref, dc_ref, du_ref, dw_ref):
        i = pl.program_id(1)
        b = b_ref[...].astype(F32)
        c = c_ref[...].astype(F32)
        u = u_ref[...].astype(F32)
        v = c * u
        vh = jnp.where(i > 0, ch_ref[...].astype(F32) * uh_ref[...].astype(F32), 0.0)
        v1 = _shift_down(v, vh, 1)
        v2 = _shift_down(v, vh, 2)
        conv = w_ref[0:1, :] * v2 + w_ref[1:2, :] * v1 + w_ref[2:3, :] * v
        dgv = dg_ref[...]
        db_ref[...] = (dgv * conv).astype(BF16)
        dconv = dgv * b
        dconv_n = jnp.where(i < nt - 1, dgn_ref[...] * bn_ref[...].astype(F32), 0.0)
        d1 = _shift_up(dconv, dconv_n, 1)
        d2 = _shift_up(dconv, dconv_n, 2)
        dv = w_ref[2:3, :] * dconv + w_ref[1:2, :] * d1 + w_ref[0:1, :] * d2
        dc_ref[...] = (dv * u).astype(BF16)
        du_ref[...] = (dv * c).astype(BF16)
        row8 = lax.broadcasted_iota(jnp.int32, (8, tc), 0)
        dw = jnp.where(row8 == 0, jnp.sum(dconv * v2, axis=0, keepdims=True),
                       jnp.where(row8 == 1, jnp.sum(dconv * v1, axis=0, keepdims=True),
                                 jnp.where(row8 == 2, jnp.sum(dconv * v, axis=0, keepdims=True), 0.0)))

        @pl.when(i == 0)
        def _():
            dw_ref[...] = dw

        @pl.when(i > 0)
        def _():
            dw_ref[...] += dw

    def prev(i):
        return jnp.maximum(i * hb - 1, 0)

    def nxt(i):
        return jnp.minimum((i + 1) * hb, last_h)

    cur = lambda off: pl.BlockSpec((tr, tc), lambda j, i: (i, off * nc + j))
    out_blk = pl.BlockSpec((tr, tc), lambda j, i: (i, j))
    db, dc, du, dw = pl.pallas_call(
        body, name=name, grid=(nc, nt),
        in_specs=[cur(0), cur(1), cur(2),
                  pl.BlockSpec((HALO, tc), lambda j, i: (prev(i), nc + j)),
                  pl.BlockSpec((HALO, tc), lambda j, i: (prev(i), 2 * nc + j)),
                  pl.BlockSpec((HALO, tc), lambda j, i: (nxt(i), j)),
                  pl.BlockSpec((tr, tc), lambda j, i: (i, j)),
                  pl.BlockSpec((HALO, tc), lambda j, i: (nxt(i), j)),
                  pl.BlockSpec((3, tc), lambda j, i: (0, j))],
        out_specs=[out_blk, out_blk, out_blk, pl.BlockSpec((8, tc), lambda j, i: (0, j))],
        out_shape=[jax.ShapeDtypeStruct((R, D), BF16)] * 3 + [jax.ShapeDtypeStruct((8, D), F32)],
        compiler_params=_params(("arbitrary", "arbitrary")),
    )(bcu, bcu, bcu, bcu, bcu, bcu, dg, dg, conv_w)
    return db, dc, du, dw


def _swap32(x):
    w = x.shape[1]
    lane = lax.broadcasted_iota(jnp.int32, x.shape, 1)
    return jnp.where((lane & (HEAD_DIM - 1)) < HEAD_DIM // 2, pltpu.roll(x, w - HEAD_DIM // 2, axis=1),
                     pltpu.roll(x, HEAD_DIM // 2, axis=1))


def _rope(x, cos, sin):
    return x * cos + _swap32(x) * sin


def rope_fwd(qkv, cos, sin, *, n_q, n_kv, name):
    R, W = qkv.shape
    qw = n_q * HEAD_DIM
    kw = n_kv * HEAD_DIM
    tr = _tile(R, 320, 16)

    def expand(y, ref, c):
        lane = lax.broadcasted_iota(jnp.int32, y.shape, 1)
        lo = jnp.where(lane < HEAD_DIM, y, 0.0)
        hi = jnp.where(lane >= HEAD_DIM, y, 0.0)
        g0 = 2 * c
        ref[:, 256 * g0:256 * g0 + 128] = lo.astype(BF16)
        ref[:, 256 * g0 + 128:256 * g0 + 256] = pltpu.roll(lo, HEAD_DIM, axis=1).astype(BF16)
        ref[:, 256 * (g0 + 1):256 * (g0 + 1) + 128] = pltpu.roll(hi, HEAD_DIM, axis=1).astype(BF16)
        ref[:, 256 * (g0 + 1) + 128:256 * (g0 + 1) + 256] = hi.astype(BF16)

    def body(x_ref, c_ref, s_ref, q_ref, k_ref, v_ref):
        cos = c_ref[...]
        sin = s_ref[...]
        for c in range(qw // 128):
            x = x_ref[:, 128 * c:128 * (c + 1)]
            q_ref[:, 128 * c:128 * (c + 1)] = (_rope(x, cos, sin) * (HEAD_DIM ** -0.5)).astype(BF16)
        for c in range(kw // 128):
            expand(_rope(x_ref[:, qw + 128 * c:qw + 128 * (c + 1)], cos, sin), k_ref, c)
            expand(x_ref[:, qw + kw + 128 * c:qw + kw + 128 * (c + 1)], v_ref, c)

    row = lambda w: pl.BlockSpec((tr, w), lambda i: (i, 0))
    return pl.pallas_call(
        body, name=name, grid=(R // tr,),
        in_specs=[row(W), row(128), row(128)],
        out_specs=[row(qw), row(4 * kw), row(4 * kw)],
        out_shape=[jax.ShapeDtypeStruct((R, qw), BF16), jax.ShapeDtypeStruct((R, 4 * kw), BF16),
                   jax.ShapeDtypeStruct((R, 4 * kw), BF16)],
        compiler_params=_params(("parallel",)),
    )(qkv, cos, sin)


def rope_bwd(dq, dkx, dvx, cos, sin, *, n_q, n_kv, name):
    R = dq.shape[0]
    qw = n_q * HEAD_DIM
    kw = n_kv * HEAD_DIM
    W = qw + 2 * kw
    tr = _tile(R, 320, 16)

    def fold(ref, c):
        lane = lax.broadcasted_iota(jnp.int32, (tr, 128), 1)
        x0 = ref[:, 128 * (2 * c):128 * (2 * c + 1)]
        x1 = ref[:, 128 * (2 * c + 1):128 * (2 * c + 2)]
        f0 = x0 + pltpu.roll(x0, HEAD_DIM, axis=1)
        f1 = x1 + pltpu.roll(x1, HEAD_DIM, axis=1)
        return jnp.where(lane < HEAD_DIM, f0, f1)

    def body(dq_ref, dk_ref, dv_ref, c_ref, s_ref, o_ref):
        cos = c_ref[...]
        nsin = -s_ref[...]
        for c in range(qw // 128):
            y = dq_ref[:, 128 * c:128 * (c + 1)]
            o_ref[:, 128 * c:128 * (c + 1)] = (_rope(y, cos, nsin) * (HEAD_DIM ** -0.5)).astype(BF16)
        for c in range(kw // 128):
            o_ref[:, qw + 128 * c:qw + 128 * (c + 1)] = _rope(fold(dk_ref, c), cos, nsin).astype(BF16)
            o_ref[:, qw + kw + 128 * c:qw + kw + 128 * (c + 1)] = fold(dv_ref, c).astype(BF16)

    row = lambda w: pl.BlockSpec((tr, w), lambda i: (i, 0))
    return pl.pallas_call(
        body, name=name, grid=(R // tr,),
        in_specs=[row(qw), row(2 * kw), row(2 * kw), row(128), row(128)],
        out_specs=row(W),
        out_shape=jax.ShapeDtypeStruct((R, W), BF16),
        compiler_params=_params(("parallel",)),
    )(dq, dkx, dvx, cos, sin)


def _band_bias(n, pad):
    row = lax.broadcasted_iota(jnp.int32, (BLOCK, 2 * BLOCK), 0)
    col = lax.broadcasted_iota(jnp.int32, (BLOCK, 2 * BLOCK), 1)
    kmin = jnp.where(n == 0, BLOCK + pad, jnp.where(n == 1, pad, 0))
    allowed = (col > row) & (col <= row + BLOCK) & (col >= kmin)
    return jnp.where(allowed, 0.0, NEG_INF).astype(F32)


def _dot_nt(a, b):
    return lax.dot_general(a, b, (((1,), (1,)), ((), ())), preferred_element_type=F32)


def _dot_tn(a, b):
    return lax.dot_general(a, b, (((0,), (0,)), ((), ())), preferred_element_type=F32)


def _band(prev_ref, cur_ref, c0):
    return jnp.concatenate([prev_ref[:, c0:c0 + 128], cur_ref[:, c0:c0 + 128]], axis=0)


def attn_fwd(q, kx, vx, sinks, *, pad, name):
    R, qw = q.shape
    n_kv = kx.shape[1] // 256
    nb = R // BLOCK

    def body(s_ref, q_ref, kc_ref, kp_ref, vc_ref, vp_ref, o_ref, l_ref):
        n = pl.program_id(0)
        bias = _band_bias(n, pad)
        lane = lax.broadcasted_iota(jnp.int32, (BLOCK, 128), 1)

        def softmax(s, sink):
            m = jnp.maximum(jnp.max(s, axis=1, keepdims=True), sink)
            e = jnp.exp(s - m)
            den = jnp.sum(e, axis=1, keepdims=True) + jnp.exp(sink - m)
            return (e * (1.0 / den)).astype(BF16), m + jnp.log(den)

        for g in range(n_kv):
            k2a = _band(kp_ref, kc_ref, 256 * g)
            k2b = _band(kp_ref, kc_ref, 256 * g + 128)
            v2a = _band(vp_ref, vc_ref, 256 * g)
            v2b = _band(vp_ref, vc_ref, 256 * g + 128)
            for p in range(GROUP // 2):
                c0 = 128 * (g * (GROUP // 2) + p)
                he = GROUP * g + 2 * p
                q2 = q_ref[:, c0:c0 + 128]
                pe, le = softmax(_dot_nt(q2, k2a) + bias, s_ref[he])
                po, lo = softmax(_dot_nt(q2, k2b) + bias, s_ref[he + 1])
                o2 = jnp.dot(pe, v2a, preferred_element_type=F32) + jnp.dot(po, v2b, preferred_element_type=F32)
                o_ref[:, c0:c0 + 128] = o2.astype(BF16)
                l_ref[:, c0:c0 + 128] = jnp.where(lane < HEAD_DIM, le, lo)

    cur = lambda w: pl.BlockSpec((BLOCK, w), lambda n: (n, 0))
    prv = lambda w: pl.BlockSpec((BLOCK, w), lambda n: (jnp.maximum(n - 1, 0), 0))
    kxw = kx.shape[1]
    return pl.pallas_call(
        body, name=name, grid=(nb,),
        in_specs=[pl.BlockSpec(memory_space=pltpu.SMEM), cur(qw), cur(kxw), prv(kxw), cur(kxw), prv(kxw)],
        out_specs=[cur(qw), cur(qw)],
        out_shape=[jax.ShapeDtypeStruct((R, qw), BF16), jax.ShapeDtypeStruct((R, qw), F32)],
        compiler_params=_params(("parallel",)),
    )(sinks, q, kx, kx, vx, vx)


def attn_bwd(q, kx, vx, o, do, lse, sinks, *, pad, name):
    R, qw = q.shape
    n_kv = kx.shape[1] // 256
    nb = R // BLOCK
    kw2 = n_kv * 128

    def body(s_ref, q_ref, do_ref, o_ref, l_ref, kc_ref, kp_ref, vc_ref, vp_ref,
             dq_ref, dk_ref, dv_ref, ds_ref, cdk, cdv):
        n = pl.program_id(0)

        @pl.when(n == 0)
        def _():
            cdk[...] = jnp.zeros_like(cdk)
            cdv[...] = jnp.zeros_like(cdv)
            ds_ref[...] = jnp.zeros_like(ds_ref)

        @pl.when(n < nb)
        def _():
            bias = _band_bias(n, pad)
            lane = lax.broadcasted_iota(jnp.int32, (BLOCK, 128), 1)
            lane2 = lax.broadcasted_iota(jnp.int32, (2 * BLOCK, 128), 1)
            lane1 = lax.broadcasted_iota(jnp.int32, (1, 128), 1)
            dsink = jnp.zeros((1, 128), F32)
            for g in range(n_kv):
                k2a = _band(kp_ref, kc_ref, 256 * g)
                k2b = _band(kp_ref, kc_ref, 256 * g + 128)
                v2a = _band(vp_ref, vc_ref, 256 * g)
                v2b = _band(vp_ref, vc_ref, 256 * g + 128)
                dka = jnp.zeros((2 * BLOCK, 128), F32)
                dkb = jnp.zeros((2 * BLOCK, 128), F32)
                dva = jnp.zeros((2 * BLOCK, 128), F32)
                dvb = jnp.zeros((2 * BLOCK, 128), F32)
                for p in range(GROUP // 2):
                    c0 = 128 * (g * (GROUP // 2) + p)
                    he = GROUP * g + 2 * p
                    q2 = q_ref[:, c0:c0 + 128]
                    do2 = do_ref[:, c0:c0 + 128]
                    l2 = l_ref[:, c0:c0 + 128]
                    prod = do2.astype(F32) * o_ref[:, c0:c0 + 128].astype(F32)
                    l2r = pltpu.roll(l2, HEAD_DIM, axis=1)
                    halves = ((k2a, v2a, lane < HEAD_DIM, he), (k2b, v2b, lane >= HEAD_DIM, he + 1))
                    dq2 = jnp.zeros((BLOCK, 128), F32)
                    outs = []
                    for (k2, v2, mine, h) in halves:
                        lse_t = jnp.where(mine, l2, l2r)
                        delta = jnp.sum(jnp.where(mine, prod, 0.0), axis=1, keepdims=True)
                        pr = jnp.exp(_dot_nt(q2, k2) + bias - jnp.concatenate([lse_t, lse_t], axis=1))
                        dp = _dot_nt(do2, v2)
                        dsc = (pr * (dp - delta)).astype(BF16)
                        prb = pr.astype(BF16)
                        dq2 = dq2 + jnp.dot(dsc, k2, preferred_element_type=F32)
                        outs.append((_dot_tn(dsc, q2), _dot_tn(prb, do2)))
                        psink = jnp.exp(s_ref[h] - lse_t)
                        col = jnp.sum(psink * delta, axis=0, keepdims=True)
                        dsink = dsink - jnp.where(lane1 == h, col, 0.0)
                    dq_ref[:, c0:c0 + 128] = dq2
                    dka = dka + outs[0][0]
                    dva = dva + outs[0][1]
                    dkb = dkb + outs[1][0]
                    dvb = dvb + outs[1][1]
                gc = pl.ds(128 * g, 128)
                dk2 = jnp.where(lane2 < HEAD_DIM, dka, dkb)
                dv2 = jnp.where(lane2 < HEAD_DIM, dva, dvb)
                dk_ref[:, gc] = cdk[:, gc] + dk2[0:BLOCK]
                dv_ref[:, gc] = cdv[:, gc] + dv2[0:BLOCK]
                cdk[:, gc] = dk2[BLOCK:2 * BLOCK]
                cdv[:, gc] = dv2[BLOCK:2 * BLOCK]
            ds_ref[0:1, :] += dsink

        @pl.when(n == nb)
        def _():
            dk_ref[...] = cdk[...]
            dv_ref[...] = cdv[...]

    cur = lambda w: pl.BlockSpec((BLOCK, w), lambda n: (jnp.minimum(n, nb - 1), 0))
    prv = lambda w: pl.BlockSpec((BLOCK, w), lambda n: (jnp.clip(n - 1, 0, nb - 1), 0))
    kxw = kx.shape[1]
    return pl.pallas_call(
        body, name=name, grid=(nb + 1,),
        in_specs=[pl.BlockSpec(memory_space=pltpu.SMEM), cur(qw), cur(qw), cur(qw), cur(qw),
                  cur(kxw), prv(kxw), cur(kxw), prv(kxw)],
        out_specs=[cur(qw), prv(kw2), prv(kw2), pl.BlockSpec((8, 128), lambda n: (0, 0))],
        out_shape=[jax.ShapeDtypeStruct((R, qw), F32), jax.ShapeDtypeStruct((R, kw2), F32),
                   jax.ShapeDtypeStruct((R, kw2), F32), jax.ShapeDtypeStruct((8, 128), F32)],
        scratch_shapes=[pltpu.VMEM((BLOCK, kw2), F32), pltpu.VMEM((BLOCK, kw2), F32)],
        compiler_params=_params(("arbitrary",)),
    )(sinks, q, do, o, lse, kx, kx, vx, vx)


def loss_bwd(h, g, target, *, name):
    R, D = h.shape
    nb = R // BLOCK

    def body(h_ref, g_ref, t_ref, loss_ref, dh_ref, dhb_ref, dg_ref):
        n = pl.program_id(0)
        xhat, rstd = _rms_parts(h_ref[...])
        gv = g_ref[...]
        diff = jnp.where(n > 0, xhat * gv - t_ref[...], 0.0)
        part = (0.5 / D) * jnp.sum(jnp.sum(diff * diff, axis=1, keepdims=True), axis=0, keepdims=True)
        dout = diff * (1.0 / D)
        dxh = dout * gv
        dh = rstd * (dxh - xhat * jnp.mean(dxh * xhat, axis=-1, keepdims=True))
        dh_ref[...] = dh
        dhb_ref[...] = dh.astype(BF16)
        dg = jnp.sum(dout * xhat, axis=0, keepdims=True)

        @pl.when(n == 0)
        def _():
            loss_ref[...] = jnp.zeros_like(loss_ref) + part
            dg_ref[...] = dg

        @pl.when(n > 0)
        def _():
            loss_ref[...] += part
            dg_ref[...] += dg

    blk = pl.BlockSpec((BLOCK, D), lambda n: (n, 0))
    return pl.pallas_call(
        body, name=name, grid=(nb,),
        in_specs=[blk, pl.BlockSpec((1, D), lambda n: (0, 0)),
                  pl.BlockSpec((BLOCK, D), lambda n: (jnp.maximum(n - 1, 0), 0))],
        out_specs=[pl.BlockSpec((8, 128), lambda n: (0, 0)), blk, blk, pl.BlockSpec((1, D), lambda n: (0, 0))],
        out_shape=[jax.ShapeDtypeStruct((8, 128), F32), jax.ShapeDtypeStruct((R, D), F32),
                   jax.ShapeDtypeStruct((R, D), BF16), jax.ShapeDtypeStruct((1, D), F32)],
        compiler_params=_params(("arbitrary",)),
    )(h, g, target)


def _adam_math(w, g, m, v):
    m = ADAM_B1 * m + (1.0 - ADAM_B1) * g
    v = ADAM_B2 * v + (1.0 - ADAM_B2) * (g * g)
    m_hat = m / (1.0 - ADAM_B1 ** ADAM_STEP)
    v_hat = v / (1.0 - ADAM_B2 ** ADAM_STEP)
    delta = -ADAM_LR * (m_hat / (jnp.sqrt(v_hat) + ADAM_EPS) + ADAM_WD * w)
    return delta, m, v


def adam(w, m, v, g, *, name):
    r, C = w.shape
    parts = g.ndim == 3
    tr = _tile(r, 128, 8)

    def body(w_ref, m_ref, v_ref, g_ref, go_ref, d_ref, mo_ref, vo_ref):
        if parts:
            gv = g_ref[0].astype(F32)
            for s in range(1, N_DEV):
                gv = gv + g_ref[s].astype(F32)
        else:
            gv = g_ref[...]
        delta, mn, vn = _adam_math(w_ref[...], gv, m_ref[...], v_ref[...])
        go_ref[...] = gv
        d_ref[...] = delta
        mo_ref[...] = mn
        vo_ref[...] = vn

    blk = pl.BlockSpec((tr, C), lambda i: (i, 0))
    gblk = pl.BlockSpec((N_DEV, tr, C), lambda i: (0, i, 0)) if parts else blk
    return pl.pallas_call(
        body, name=name, grid=(r // tr,),
        in_specs=[blk, blk, blk, gblk], out_specs=[blk] * 4,
        out_shape=[jax.ShapeDtypeStruct((r, C), F32)] * 4,
        compiler_params=_params(("parallel",)),
    )(w, m, v, g)


def sum_parts(parts, *, name):
    _, r, C = parts.shape
    tr = _tile(r, 128, 8)

    def body(p_ref, o_ref):
        acc = p_ref[0].astype(F32)
        for s in range(1, N_DEV):
            acc = acc + p_ref[s].astype(F32)
        o_ref[...] = acc

    return pl.pallas_call(
        body, name=name, grid=(r // tr,),
        in_specs=[pl.BlockSpec((N_DEV, tr, C), lambda i: (0, i, 0))],
        out_specs=pl.BlockSpec((tr, C), lambda i: (i, 0)),
        out_shape=jax.ShapeDtypeStruct((r, C), F32),
        compiler_params=_params(("parallel",)),
    )(parts)


def _coords():
    return lax.axis_index("x"), lax.axis_index("y"), lax.axis_index("c")


def _peer(m):
    x, y, c = _coords()
    px = 1 - x if m & 4 else x
    py = 1 - y if m & 2 else y
    pc = 1 - c if m & 1 else c
    return (px, py, pc), 4 * px + 2 * py + pc


def exchange(items, *, all_to_all, name):
    n = len(items)
    if all_to_all:
        out_shape = [jax.ShapeDtypeStruct(a.shape, a.dtype) for a in items]
    else:
        out_shape = [jax.ShapeDtypeStruct((N_DEV,) + a.shape, a.dtype) for a in items]

    def body(*refs):
        ins, outs = refs[:n], refs[n:2 * n]
        send_sems, recv_sems, local_sems = refs[2 * n:]
        x, y, c = _coords()
        me = 4 * x + 2 * y + c

        def src(i, idx):
            return ins[i].at[idx] if all_to_all else ins[i]

        local = [pltpu.make_async_copy(src(i, me), outs[i].at[me], local_sems.at[i]) for i in range(n)]
        for cp in local:
            cp.start()
        sends = []
        for m in range(1, N_DEV):
            peer, pidx = _peer(m)
            for i in range(n):
                k = i * (N_DEV - 1) + m - 1
                cp = pltpu.make_async_remote_copy(src_ref=src(i, pidx), dst_ref=outs[i].at[me],
                                                  send_sem=send_sems.at[k], recv_sem=recv_sems.at[k],
                                                  device_id=peer, device_id_type=MESH)
                cp.start()
                sends.append(cp)
        for m in range(1, N_DEV):
            peer, pidx = _peer(m)
            for i in range(n):
                k = i * (N_DEV - 1) + m - 1
                pltpu.make_async_remote_copy(src_ref=src(i, pidx), dst_ref=outs[i].at[pidx],
                                             send_sem=send_sems.at[k], recv_sem=recv_sems.at[k],
                                             device_id=peer, device_id_type=MESH).wait_recv()
        for cp in sends:
            cp.wait_send()
        for cp in local:
            cp.wait()

    any_spec = pl.BlockSpec(memory_space=pl.ANY)
    return pl.pallas_call(
        body, name=name, in_specs=[any_spec] * n, out_specs=[any_spec] * n, out_shape=out_shape,
        scratch_shapes=[pltpu.SemaphoreType.DMA((n * (N_DEV - 1),)), pltpu.SemaphoreType.DMA((n * (N_DEV - 1),)),
                        pltpu.SemaphoreType.DMA((n,))],
    )(*items)


def kernel(x, meta_tokens, norm_mix_0, w_in_conv, conv_w, w_out_conv, norm_mlp_0, w_up_0, w_down_0, norm_mix_1, w_qkv, attn_sinks, w_o, norm_mlp_1, w_up_1, w_down_1, norm_final, loss_target, m_meta_tokens, m_norm_mix_0, m_w_in_conv, m_conv_w, m_w_out_conv, m_norm_mlp_0, m_w_up_0, m_w_down_0, m_norm_mix_1, m_w_qkv, m_attn_sinks, m_w_o, m_norm_mlp_1, m_w_up_1, m_w_down_1, m_norm_final, v_meta_tokens, v_norm_mix_0, v_w_in_conv, v_conv_w, v_w_out_conv, v_norm_mlp_0, v_w_up_0, v_w_down_0, v_norm_mix_1, v_w_qkv, v_attn_sinks, v_w_o, v_norm_mlp_1, v_w_up_1, v_w_down_1, v_norm_final):
    L, D = x.shape[1], x.shape[2]
    n_meta = meta_tokens.shape[0]
    pad = BLOCK - n_meta
    R = BLOCK + L
    n_q = D // HEAD_DIM
    n_kv = n_q // GROUP
    assert n_kv % 2 == 0 and L % BLOCK == 0 and D % 128 == 0
    x = x.reshape(L, D)
    target = loss_target.reshape(L, D)
    x_id, y_id, c_id = _coords()
    me = 4 * x_id + 2 * y_id + c_id

    col_names = ("in", "up0", "qkv", "up1")
    col_w = dict(zip(col_names, (w_in_conv, w_up_0, w_qkv, w_up_1)))
    row_names = ("out", "down0", "o", "down1")
    row_w = dict(zip(row_names, (w_out_conv, w_down_0, w_o, w_down_1)))
    items = [col_w[k].T.astype(BF16) for k in col_names] + [row_w[k].astype(BF16) for k in row_names]
    items += [meta_tokens, conv_w]
    gathered = exchange(items, all_to_all=False, name="comm_gather")
    W = {k: g.reshape(-1, D) for k, g in zip(col_names + row_names, gathered[:8])}
    meta_full = jnp.transpose(gathered[8], (1, 0, 2)).reshape(n_meta, D)
    conv_full = jnp.transpose(gathered[9], (1, 0, 2)).reshape(conv_w.shape[0], D)

    vec = lambda a: a.reshape(1, D)
    pos = jnp.arange(R, dtype=F32) - pad
    inv = ROPE_THETA ** (-jnp.arange(0, HEAD_DIM, 2, dtype=F32) / HEAD_DIM)
    ang = pos[:, None] * inv[None, :]
    cos32, sin32 = jnp.cos(ang), jnp.sin(ang)
    cos = jnp.concatenate([cos32] * 4, axis=1)
    sin = jnp.concatenate([-sin32, sin32, -sin32, sin32], axis=1)

    h0 = jnp.concatenate([jnp.zeros((pad, D), F32), meta_full, x], axis=0)
    n0 = norm_fwd(h0, vec(norm_mix_0), name="norm0")
    bcu = mm_nt(n0, W["in"], name="in_proj", out_dtype=BF16)
    gated = conv_fwd(bcu, conv_full, name="conv_fwd")
    h1, n1 = mm_row(gated, W["out"], name="out_proj", epi="resnorm", res=h0, g=vec(norm_mlp_0))
    a0 = mm_nt(n1, W["up0"], name="up0", out_dtype=BF16, epi="relu")
    h2, n2 = mm_row(a0, W["down0"], name="down0", epi="resnorm", a_sq=True, res=h1, g=vec(norm_mix_1))
    qkv = mm_nt(n2, W["qkv"], name="qkv_proj", out_dtype=F32)
    q, kx, vx = rope_fwd(qkv, cos, sin, n_q=n_q, n_kv=n_kv, name="rope_fwd")
    o, lse = attn_fwd(q, kx, vx, attn_sinks, pad=pad, name="attn_fwd")
    h3, n3 = mm_row(o, W["o"], name="o_proj", epi="resnorm", res=h2, g=vec(norm_mlp_1))
    a1 = mm_nt(n3, W["up1"], name="up1", out_dtype=BF16, epi="relu")
    h4 = mm_row(a1, W["down1"], name="down1", epi="res", a_sq=True, res=h3)

    loss_part, dh4, dh4b, dg_final = loss_bwd(h4, vec(norm_final), target, name="loss_bwd")
    loss = lax.psum(loss_part[0, 0], ("x", "y", "c"))

    dW = {}
    dup1 = mm_nt(dh4b, W["down1"], name="d_down1", out_dtype=BF16, epi="mul2a", extra=a1)
    dW["down1"] = mm_tn(a1, dh4b, name="dw_down1", a_sq=True)
    dh3, dh3b, dg_mlp1 = mm_row(dup1, W["up1"], name="d_up1", epi="normbwd", h=h3, g=vec(norm_mlp_1), dres=dh4)
    dW["up1"] = mm_tn(dup1, n3, name="dw_up1")

    do = mm_nt(dh3b, W["o"], name="d_o", out_dtype=BF16)
    dW["o"] = mm_tn(o, dh3b, name="dw_o")
    dq, dkx, dvx, dsinks = attn_bwd(q, kx, vx, o, do, lse, attn_sinks, pad=pad, name="attn_bwd")
    dqkv = rope_bwd(dq, dkx, dvx, cos, sin, n_q=n_q, n_kv=n_kv, name="rope_bwd")
    dh2, dh2b, dg_mix1 = mm_row(dqkv, W["qkv"], name="d_qkv", epi="normbwd", h=h2, g=vec(norm_mix_1), dres=dh3)
    dW["qkv"] = mm_tn(dqkv, n2, name="dw_qkv")

    dup0 = mm_nt(dh2b, W["down0"], name="d_down0", out_dtype=BF16, epi="mul2a", extra=a0)
    dW["down0"] = mm_tn(a0, dh2b, name="dw_down0", a_sq=True)
    dh1, dh1b, dg_mlp0 = mm_row(dup0, W["up0"], name="d_up0", epi="normbwd", h=h1, g=vec(norm_mlp_0), dres=dh2)
    dW["up0"] = mm_tn(dup0, n1, name="dw_up0")

    dgated = mm_nt(dh1b, W["out"], name="d_out", out_dtype=F32)
    dW["out"] = mm_tn(gated, dh1b, name="dw_out")
    db, dc, du, dconv = conv_bwd(bcu, conv_full, dgated, name="conv_bwd")
    dbcu = jnp.concatenate([db, dc, du], axis=1)
    dh0, _, dg_mix0 = mm_row(dbcu, W["in"], name="d_in", epi="normbwd", h=h0, g=vec(norm_mix_0), dres=dh1)
    dW["in"] = mm_tn(dbcu, n0, name="dw_in")
    grad_x = dh0[BLOCK:].reshape(1, L, D)

    names = col_names + row_names
    recv = exchange([dW[k].reshape(N_DEV, -1, D) for k in names], all_to_all=True, name="comm_a2a")
    recv = dict(zip(names, recv))

    n_sink = attn_sinks.shape[0]
    slab = jnp.concatenate([
        dg_mix0, dg_mlp0, dg_mix1, dg_mlp1, dg_final,
        jnp.pad(dsinks[0:1, :n_sink], ((0, 0), (0, D - n_sink))), jnp.zeros((2, D), F32),
        dconv, dh0[pad:BLOCK]], axis=0)
    slabs = exchange([slab], all_to_all=False, name="comm_small")[0]
    small = sum_parts(slabs, name="sum_small")
    cols = D // N_DEV
    my_cols = lambda a: lax.dynamic_slice_in_dim(a, me * cols, cols, axis=1)

    grads, deltas, new_m, new_v = {}, {}, {}, {}

    def update(key, w, m, v, g, shape):
        s2 = (1, -1) if w.ndim == 1 else w.shape
        g_, d_, m_, v_ = adam(w.reshape(s2), m.reshape(s2), v.reshape(s2), g if g.ndim == 3 else g.reshape(s2),
                              name="adam_" + key)
        grads[key], deltas[key], new_m[key], new_v[key] = (t.reshape(shape) for t in (g_, d_, m_, v_))

    update("meta_tokens", meta_tokens, m_meta_tokens, v_meta_tokens, my_cols(small[16:16 + n_meta]), meta_tokens.shape)
    update("norm_mix_0", norm_mix_0, m_norm_mix_0, v_norm_mix_0, small[0], (D,))
    update("conv_w", conv_w, m_conv_w, v_conv_w, my_cols(small[8:8 + conv_w.shape[0]]), conv_w.shape)
    update("norm_mlp_0", norm_mlp_0, m_norm_mlp_0, v_norm_mlp_0, small[1], (D,))
    update("norm_mix_1", norm_mix_1, m_norm_mix_1, v_norm_mix_1, small[2], (D,))
    update("attn_sinks", attn_sinks, m_attn_sinks, v_attn_sinks, small[5, :n_sink], (n_sink,))
    update("norm_mlp_1", norm_mlp_1, m_norm_mlp_1, v_norm_mlp_1, small[3], (D,))
    update("norm_final", norm_final, m_norm_final, v_norm_final, small[4], (D,))
    big = {"in": ("w_in_conv", w_in_conv, m_w_in_conv, v_w_in_conv), "up0": ("w_up_0", w_up_0, m_w_up_0, v_w_up_0),
           "qkv": ("w_qkv", w_qkv, m_w_qkv, v_w_qkv), "up1": ("w_up_1", w_up_1, m_w_up_1, v_w_up_1),
           "out": ("w_out_conv", w_out_conv, m_w_out_conv, v_w_out_conv),
           "down0": ("w_down_0", w_down_0, m_w_down_0, v_w_down_0), "o": ("w_o", w_o, m_w_o, v_w_o),
           "down1": ("w_down_1", w_down_1, m_w_down_1, v_w_down_1)}
    for k in col_names:
        key, w, m, v = big[k]
        update(key, w, m, v, sum_parts(recv[k], name="sum_" + k).T, w.shape)
    for k in row_names:
        key, w, m, v = big[k]
        update(key, w, m, v, recv[k], w.shape)

    order = ("meta_tokens", "norm_mix_0", "w_in_conv", "conv_w", "w_out_conv", "norm_mlp_0", "w_up_0", "w_down_0",
             "norm_mix_1", "w_qkv", "attn_sinks", "w_o", "norm_mlp_1", "w_up_1", "w_down_1", "norm_final")
    return (loss, grad_x, *[grads[k] for k in order], *[deltas[k] for k in order],
            *[new_m[k] for k in order], *[new_v[k] for k in order])
```

```python
import functools

import jax
import jax.numpy as jnp
from jax import lax
from jax.experimental import pallas as pl
from jax.experimental.pallas import tpu as pltpu

F32 = jnp.float32
BF16 = jnp.bfloat16

HEAD_DIM = 64
GROUP = 8
BLOCK = 128
N_DEV = 8
RMS_EPS = 1e-5
NEG_INF = -1e30
ROPE_THETA = 10000.0
ADAM_LR = 0.001
ADAM_B1 = 0.9
ADAM_B2 = 0.999
ADAM_EPS = 1e-08
ADAM_WD = 0.01
ADAM_STEP = 10
VMEM_LIMIT = 60 * 1024 * 1024
MESH = pl.DeviceIdType.MESH


def _tile(n, target, mult):
    best = None
    for t in range(mult, min(n, target) + 1, mult):
        if n % t == 0:
            best = t
    return best if best is not None else n


def _params(sem):
    return pltpu.CompilerParams(dimension_semantics=sem, vmem_limit_bytes=VMEM_LIMIT)


ANY_SPEC = pl.BlockSpec(memory_space=pl.ANY)


def mm_nt(a, bt, *, name, out_dtype, epi=None, extra=None, dep=None):
    M, K = a.shape
    N = bt.shape[0]
    tm = _tile(M, 1664, 16)
    tn = _tile(N, 512, 128)

    def body(*refs):
        a_ref, b_ref, e_ref, o_ref = refs[0], refs[1], refs[2], refs[-1]
        acc = lax.dot_general(a_ref[...], b_ref[...], (((1,), (1,)), ((), ())), preferred_element_type=F32)
        if epi == "relu":
            acc = jnp.maximum(acc, 0.0)
        elif epi == "mul2a":
            acc = acc * (2.0 * e_ref[...].astype(F32))
        o_ref[...] = acc.astype(o_ref.dtype)

    in_specs = [pl.BlockSpec((tm, K), lambda i, j: (i, 0)), pl.BlockSpec((tn, K), lambda i, j: (j, 0))]
    args = [a, bt]
    if extra is not None:
        in_specs.append(pl.BlockSpec((tm, tn), lambda i, j: (i, j)))
        args.append(extra)
    if dep is not None:
        in_specs.append(ANY_SPEC)
        args.append(dep)
    return pl.pallas_call(
        body, name=name, grid=(M // tm, N // tn), in_specs=in_specs,
        out_specs=pl.BlockSpec((tm, tn), lambda i, j: (i, j)),
        out_shape=jax.ShapeDtypeStruct((M, N), out_dtype),
        compiler_params=_params(("parallel", "parallel")),
    )(*args)


def _rms_parts(h):
    rstd = lax.rsqrt(jnp.mean(h * h, axis=-1, keepdims=True) + RMS_EPS)
    return h * rstd, rstd


def mm_row(a, b, dep, *, name, epi, a_sq=False, res=None, g=None, h=None, dres=None):
    M, K = a.shape
    N = b.shape[1]
    tm = _tile(M, 416 if epi == "normbwd" else 640, 16)
    tk = _tile(K, 512, 128)
    nk = K // tk
    rc = _tile(tm, 160, 8)

    def body(*refs):
        acc_ref = refs[-1]
        a_ref, b_ref = refs[:2]
        refs = refs[1:-1]
        i = pl.program_id(0)
        k = pl.program_id(1)
        av = a_ref[...]
        if a_sq:
            av = av.astype(F32)
            av = (av * av).astype(BF16)
        part = jnp.dot(av, b_ref[...], preferred_element_type=F32)

        @pl.when(k == 0)
        def _():
            acc_ref[...] = part

        @pl.when(k > 0)
        def _():
            acc_ref[...] += part

        @pl.when(k == nk - 1)
        def _():
            if epi == "res":
                res_ref, o_ref = refs[2:]
                o_ref[...] = res_ref[...] + acc_ref[...]
            elif epi == "resnorm":
                res_ref, g_ref, o_ref, n_ref = refs[2:]
                for r0 in range(0, tm, rc):
                    rows = pl.ds(r0, rc)
                    hn = res_ref[rows, :] + acc_ref[rows, :]
                    o_ref[rows, :] = hn
                    xhat, _ = _rms_parts(hn)
                    n_ref[rows, :] = (xhat * g_ref[...]).astype(BF16)
            else:
                h_ref, g_ref, dres_ref, dh_ref, dhb_ref, dg_ref = refs[2:]
                dg = jnp.zeros((1, N), F32)
                for r0 in range(0, tm, rc):
                    rows = pl.ds(r0, rc)
                    dn = acc_ref[rows, :]
                    xhat, rstd = _rms_parts(h_ref[rows, :])
                    dxh = dn * g_ref[...]
                    dh = dres_ref[rows, :] + rstd * (dxh - xhat * jnp.mean(dxh * xhat, axis=-1, keepdims=True))
                    dh_ref[rows, :] = dh
                    dhb_ref[rows, :] = dh.astype(BF16)
                    dg = dg + jnp.sum(dn * xhat, axis=0, keepdims=True)

                @pl.when(i == 0)
                def _():
                    dg_ref[...] = dg

                @pl.when(i > 0)
                def _():
                    dg_ref[...] += dg

    row_spec = pl.BlockSpec((tm, N), lambda i, k: (i, 0))
    vec_spec = pl.BlockSpec((1, N), lambda i, k: (0, 0))
    in_specs = [pl.BlockSpec((tm, tk), lambda i, k: (i, k)), pl.BlockSpec((tk, N), lambda i, k: (k, 0)), ANY_SPEC]
    args = [a, b, dep]
    if epi == "res":
        in_specs += [row_spec]
        args += [res]
        out_specs = row_spec
        out_shape = jax.ShapeDtypeStruct((M, N), F32)
    elif epi == "resnorm":
        in_specs += [row_spec, vec_spec]
        args += [res, g]
        out_specs = [row_spec, row_spec]
        out_shape = [jax.ShapeDtypeStruct((M, N), F32), jax.ShapeDtypeStruct((M, N), BF16)]
    else:
        in_specs += [row_spec, vec_spec, row_spec]
        args += [h, g, dres]
        out_specs = [row_spec, row_spec, vec_spec]
        out_shape = [jax.ShapeDtypeStruct((M, N), F32), jax.ShapeDtypeStruct((M, N), BF16),
                     jax.ShapeDtypeStruct((1, N), F32)]
    return pl.pallas_call(
        body, name=name, grid=(M // tm, nk), in_specs=in_specs, out_specs=out_specs, out_shape=out_shape,
        scratch_shapes=[pltpu.VMEM((tm, N), F32)],
        compiler_params=_params(("arbitrary", "arbitrary")),
    )(*args)


def mm_tn(a, b, *, name, a_sq=False):
    T, M = a.shape
    N = b.shape[1]
    tm = _tile(M, 1024, 128)
    tk = _tile(T, 640, 16)
    nk = T // tk

    def body(a_ref, b_ref, o_ref, acc_ref):
        k = pl.program_id(1)
        av = a_ref[...]
        if a_sq:
            av = av.astype(F32)
            av = (av * av).astype(BF16)
        part = lax.dot_general(av, b_ref[...], (((0,), (0,)), ((), ())), preferred_element_type=F32)

        @pl.when(k == 0)
        def _():
            acc_ref[...] = part

        @pl.when(k > 0)
        def _():
            acc_ref[...] += part

        @pl.when(k == nk - 1)
        def _():
            o_ref[...] = acc_ref[...].astype(BF16)

    return pl.pallas_call(
        body, name=name, grid=(M // tm, nk),
        in_specs=[pl.BlockSpec((tk, tm), lambda i, k: (k, i)), pl.BlockSpec((tk, N), lambda i, k: (k, 0))],
        out_specs=pl.BlockSpec((tm, N), lambda i, k: (i, 0)),
        out_shape=jax.ShapeDtypeStruct((M, N), BF16),
        scratch_shapes=[pltpu.VMEM((tm, N), F32)],
        compiler_params=_params(("parallel", "arbitrary")),
    )(a, b)


def norm_fwd(h, g, dep, *, name):
    R, D = h.shape
    tr = _tile(R, 320, 16)

    def body(h_ref, g_ref, dep_ref, n_ref):
        xhat, _ = _rms_parts(h_ref[...])
        n_ref[...] = (xhat * g_ref[...]).astype(BF16)

    return pl.pallas_call(
        body, name=name, grid=(R // tr,),
        in_specs=[pl.BlockSpec((tr, D), lambda i: (i, 0)), pl.BlockSpec((1, D), lambda i: (0, 0)), ANY_SPEC],
        out_specs=pl.BlockSpec((tr, D), lambda i: (i, 0)),
        out_shape=jax.ShapeDtypeStruct((R, D), BF16),
        compiler_params=_params(("parallel",)),
    )(h, g, dep)


def _shift_down(v, halo, s):
    n = v.shape[0]
    row = lax.broadcasted_iota(jnp.int32, v.shape, 0)
    out = pltpu.roll(v, s, axis=0)
    hn = halo.shape[0]
    for r in range(s):
        out = jnp.where(row == r, halo[hn - s + r:hn - s + r + 1, :], out)
    return out


def _shift_up(v, halo, s):
    n = v.shape[0]
    row = lax.broadcasted_iota(jnp.int32, v.shape, 0)
    out = pltpu.roll(v, n - s, axis=0)
    for r in range(s):
        out = jnp.where(row == n - s + r, halo[r:r + 1, :], out)
    return out


HALO = 16


def conv_fwd(bcu, conv_w, *, name):
    R, D3 = bcu.shape
    D = D3 // 3
    tr = _tile(R, 320, 16)
    tc = _tile(D, 512, 128)
    nc = D // tc
    hb = tr // HALO

    def body(b_ref, c_ref, u_ref, ch_ref, uh_ref, w_ref, o_ref):
        i = pl.program_id(0)
        v = c_ref[...].astype(F32) * u_ref[...].astype(F32)
        vh = ch_ref[...].astype(F32) * uh_ref[...].astype(F32)
        vh = jnp.where(i > 0, vh, 0.0)
        v1 = _shift_down(v, vh, 1)
        v2 = _shift_down(v, vh, 2)
        conv = w_ref[0:1, :] * v2 + w_ref[1:2, :] * v1 + w_ref[2:3, :] * v
        o_ref[...] = (b_ref[...].astype(F32) * conv).astype(BF16)

    def prev(i):
        return jnp.maximum(i * hb - 1, 0)

    return pl.pallas_call(
        body, name=name, grid=(R // tr, nc),
        in_specs=[pl.BlockSpec((tr, tc), lambda i, j: (i, j)),
                  pl.BlockSpec((tr, tc), lambda i, j: (i, nc + j)),
                  pl.BlockSpec((tr, tc), lambda i, j: (i, 2 * nc + j)),
                  pl.BlockSpec((HALO, tc), lambda i, j: (prev(i), nc + j)),
                  pl.BlockSpec((HALO, tc), lambda i, j: (prev(i), 2 * nc + j)),
                  pl.BlockSpec((3, tc), lambda i, j: (0, j))],
        out_specs=pl.BlockSpec((tr, tc), lambda i, j: (i, j)),
        out_shape=jax.ShapeDtypeStruct((R, D), BF16),
        compiler_params=_params(("parallel", "parallel")),
    )(bcu, bcu, bcu, bcu, bcu, conv_w)


def conv_bwd(bcu, conv_w, dg, *, name):
    R, D3 = bcu.shape
    D = D3 // 3
    tr = _tile(R, 320, 16)
    tc = _tile(D, 512, 128)
    nc = D // tc
    hb = tr // HALO
    nt = R // tr
    last_h = R // HALO - 1

    def body(b_ref, c_ref, u_ref, ch_ref, uh_ref, bn_ref, dg_ref, dgn_ref, w_ref, db_ref, dc_ref, du_ref, dw_ref):
        i = pl.program_id(1)
        b = b_ref[...].astype(F32)
        c = c_ref[...].astype(F32)
        u = u_ref[...].astype(F32)
        v = c * u
        vh = jnp.where(i > 0, ch_ref[...].astype(F32) * uh_ref[...].astype(F32), 0.0)
        v1 = _shift_down(v, vh, 1)
        v2 = _shift_down(v, vh, 2)
        conv = w_ref[0:1, :] * v2 + w_ref[1:2, :] * v1 + w_ref[2:3, :] * v
        dgv = dg_ref[...]
        db_ref[...] = (dgv * conv).astype(BF16)
        dconv = dgv * b
        dconv_n = jnp.where(i < nt - 1, dgn_ref[...] * bn_ref[...].astype(F32), 0.0)
        d1 = _shift_up(dconv, dconv_n, 1)
        d2 = _shift_up(dconv, dconv_n, 2)
        dv = w_ref[2:3, :] * dconv + w_ref[1:2, :] * d1 + w_ref[0:1, :] * d2
        dc_ref[...] = (dv * u).astype(BF16)
        du_ref[...] = (dv * c).astype(BF16)
        row8 = lax.broadcasted_iota(jnp.int32, (8, tc), 0)
        dw = jnp.where(row8 == 0, jnp.sum(dconv * v2, axis=0, keepdims=True),
                       jnp.where(row8 == 1, jnp.sum(dconv * v1, axis=0, keepdims=True),
                                 jnp.where(row8 == 2, jnp.sum(dconv * v, axis=0, keepdims=True), 0.0)))

        @pl.when(i == 0)
        def _():
            dw_ref[...] = dw

        @pl.when(i > 0)
        def _():
            dw_ref[...] += dw

    def prev(i):
        return jnp.maximum(i * hb - 1, 0)

    def nxt(i):
        return jnp.minimum((i + 1) * hb, last_h)

    cur = lambda off: pl.BlockSpec((tr, tc), lambda j, i: (i, off * nc + j))
    out_blk = pl.BlockSpec((tr, tc), lambda j, i: (i, j))
    db, dc, du, dw = pl.pallas_call(
        body, name=name, grid=(nc, nt),
        in_specs=[cur(0), cur(1), cur(2),
                  pl.BlockSpec((HALO, tc), lambda j, i: (prev(i), nc + j)),
                  pl.BlockSpec((HALO, tc), lambda j, i: (prev(i), 2 * nc + j)),
                  pl.BlockSpec((HALO, tc), lambda j, i: (nxt(i), j)),
                  pl.BlockSpec((tr, tc), lambda j, i: (i, j)),
                  pl.BlockSpec((HALO, tc), lambda j, i: (nxt(i), j)),
                  pl.BlockSpec((3, tc), lambda j, i: (0, j))],
        out_specs=[out_blk, out_blk, out_blk, pl.BlockSpec((8, tc), lambda j, i: (0, j))],
        out_shape=[jax.ShapeDtypeStruct((R, D), BF16)] * 3 + [jax.ShapeDtypeStruct((8, D), F32)],
        compiler_params=_params(("arbitrary", "arbitrary")),
    )(bcu, bcu, bcu, bcu, bcu, bcu, dg, dg, conv_w)
    return db, dc, du, dw


def _swap32(x):
    w = x.shape[1]
    lane = lax.broadcasted_iota(jnp.int32, x.shape, 1)
    return jnp.where((lane & (HEAD_DIM - 1)) < HEAD_DIM // 2, pltpu.roll(x, w - HEAD_DIM // 2, axis=1),
                     pltpu.roll(x, HEAD_DIM // 2, axis=1))


def _rope(x, cos, sin):
    return x * cos + _swap32(x) * sin


def rope_fwd(qkv, cos, sin, *, n_q, n_kv, name):
    R, W = qkv.shape
    qw = n_q * HEAD_DIM
    kw = n_kv * HEAD_DIM
    tr = _tile(R, 320, 16)

    def expand(y, ref, c):
        lane = lax.broadcasted_iota(jnp.int32, y.shape, 1)
        lo = jnp.where(lane < HEAD_DIM, y, 0.0)
        hi = jnp.where(lane >= HEAD_DIM, y, 0.0)
        g0 = 2 * c
        ref[:, 256 * g0:256 * g0 + 128] = lo.astype(BF16)
        ref[:, 256 * g0 + 128:256 * g0 + 256] = pltpu.roll(lo, HEAD_DIM, axis=1).astype(BF16)
        ref[:, 256 * (g0 + 1):256 * (g0 + 1) + 128] = pltpu.roll(hi, HEAD_DIM, axis=1).astype(BF16)
        ref[:, 256 * (g0 + 1) + 128:256 * (g0 + 1) + 256] = hi.astype(BF16)

    def body(x_ref, c_ref, s_ref, q_ref, k_ref, v_ref):
        cos = c_ref[...]
        sin = s_ref[...]
        for c in range(qw // 128):
            x = x_ref[:, 128 * c:128 * (c + 1)]
            q_ref[:, 128 * c:128 * (c + 1)] = (_rope(x, cos, sin) * (HEAD_DIM ** -0.5)).astype(BF16)
        for c in range(kw // 128):
            expand(_rope(x_ref[:, qw + 128 * c:qw + 128 * (c + 1)], cos, sin), k_ref, c)
            expand(x_ref[:, qw + kw + 128 * c:qw + kw + 128 * (c + 1)], v_ref, c)

    row = lambda w: pl.BlockSpec((tr, w), lambda i: (i, 0))
    return pl.pallas_call(
        body, name=name, grid=(R // tr,),
        in_specs=[row(W), row(128), row(128)],
        out_specs=[row(qw), row(4 * kw), row(4 * kw)],
        out_shape=[jax.ShapeDtypeStruct((R, qw), BF16), jax.ShapeDtypeStruct((R, 4 * kw), BF16),
                   jax.ShapeDtypeStruct((R, 4 * kw), BF16)],
        compiler_params=_params(("parallel",)),
    )(qkv, cos, sin)


def rope_bwd(dq, dkx, dvx, cos, sin, *, n_q, n_kv, name):
    R = dq.shape[0]
    qw = n_q * HEAD_DIM
    kw = n_kv * HEAD_DIM
    W = qw + 2 * kw
    tr = _tile(R, 320, 16)

    def fold(ref, c):
        lane = lax.broadcasted_iota(jnp.int32, (tr, 128), 1)
        x0 = ref[:, 128 * (2 * c):128 * (2 * c + 1)]
        x1 = ref[:, 128 * (2 * c + 1):128 * (2 * c + 2)]
        f0 = x0 + pltpu.roll(x0, HEAD_DIM, axis=1)
        f1 = x1 + pltpu.roll(x1, HEAD_DIM, axis=1)
        return jnp.where(lane < HEAD_DIM, f0, f1)

    def body(dq_ref, dk_ref, dv_ref, c_ref, s_ref, o_ref):
        cos = c_ref[...]
        nsin = -s_ref[...]
        for c in range(qw // 128):
            y = dq_ref[:, 128 * c:128 * (c + 1)]
            o_ref[:, 128 * c:128 * (c + 1)] = (_rope(y, cos, nsin) * (HEAD_DIM ** -0.5)).astype(BF16)
        for c in range(kw // 128):
            o_ref[:, qw + 128 * c:qw + 128 * (c + 1)] = _rope(fold(dk_ref, c), cos, nsin).astype(BF16)
            o_ref[:, qw + kw + 128 * c:qw + kw + 128 * (c + 1)] = fold(dv_ref, c).astype(BF16)

    row = lambda w: pl.BlockSpec((tr, w), lambda i: (i, 0))
    return pl.pallas_call(
        body, name=name, grid=(R // tr,),
        in_specs=[row(qw), row(2 * kw), row(2 * kw), row(128), row(128)],
        out_specs=row(W),
        out_shape=jax.ShapeDtypeStruct((R, W), BF16),
        compiler_params=_params(("parallel",)),
    )(dq, dkx, dvx, cos, sin)


def _band_bias(n, pad):
    row = lax.broadcasted_iota(jnp.int32, (BLOCK, 2 * BLOCK), 0)
    col = lax.broadcasted_iota(jnp.int32, (BLOCK, 2 * BLOCK), 1)
    kmin = jnp.where(n == 0, BLOCK + pad, jnp.where(n == 1, pad, 0))
    allowed = (col > row) & (col <= row + BLOCK) & (col >= kmin)
    return jnp.where(allowed, 0.0, NEG_INF).astype(F32)


def _dot_nt(a, b):
    return lax.dot_general(a, b, (((1,), (1,)), ((), ())), preferred_element_type=F32)


def _dot_tn(a, b):
    return lax.dot_general(a, b, (((0,), (0,)), ((), ())), preferred_element_type=F32)


def _band(prev_ref, cur_ref, c0):
    return jnp.concatenate([prev_ref[:, c0:c0 + 128], cur_ref[:, c0:c0 + 128]], axis=0)


def attn_fwd(q, kx, vx, sinks, *, pad, name):
    R, qw = q.shape
    n_kv = kx.shape[1] // 256
    nb = R // BLOCK

    def body(s_ref, q_ref, kc_ref, kp_ref, vc_ref, vp_ref, o_ref, l_ref):
        n = pl.program_id(0)
        bias = _band_bias(n, pad)
        lane = lax.broadcasted_iota(jnp.int32, (BLOCK, 128), 1)

        def softmax(s, sink):
            m = jnp.maximum(jnp.max(s, axis=1, keepdims=True), sink)
            e = jnp.exp(s - m)
            den = jnp.sum(e, axis=1, keepdims=True) + jnp.exp(sink - m)
            return (e * (1.0 / den)).astype(BF16), m + jnp.log(den)

        for g in range(n_kv):
            k2a = _band(kp_ref, kc_ref, 256 * g)
            k2b = _band(kp_ref, kc_ref, 256 * g + 128)
            v2a = _band(vp_ref, vc_ref, 256 * g)
            v2b = _band(vp_ref, vc_ref, 256 * g + 128)
            for p in range(GROUP // 2):
                c0 = 128 * (g * (GROUP // 2) + p)
                he = GROUP * g + 2 * p
                q2 = q_ref[:, c0:c0 + 128]
                pe, le = softmax(_dot_nt(q2, k2a) + bias, s_ref[he])
                po, lo = softmax(_dot_nt(q2, k2b) + bias, s_ref[he + 1])
                o2 = jnp.dot(pe, v2a, preferred_element_type=F32) + jnp.dot(po, v2b, preferred_element_type=F32)
                o_ref[:, c0:c0 + 128] = o2.astype(BF16)
                l_ref[:, c0:c0 + 128] = jnp.where(lane < HEAD_DIM, le, lo)

    cur = lambda w: pl.BlockSpec((BLOCK, w), lambda n: (n, 0))
    prv = lambda w: pl.BlockSpec((BLOCK, w), lambda n: (jnp.maximum(n - 1, 0), 0))
    kxw = kx.shape[1]
    return pl.pallas_call(
        body, name=name, grid=(nb,),
        in_specs=[pl.BlockSpec(memory_space=pltpu.SMEM), cur(qw), cur(kxw), prv(kxw), cur(kxw), prv(kxw)],
        out_specs=[cur(qw), cur(qw)],
        out_shape=[jax.ShapeDtypeStruct((R, qw), BF16), jax.ShapeDtypeStruct((R, qw), F32)],
        compiler_params=_params(("parallel",)),
    )(sinks, q, kx, kx, vx, vx)


def attn_bwd(q, kx, vx, o, do, lse, sinks, *, pad, name):
    R, qw = q.shape
    n_kv = kx.shape[1] // 256
    nb = R // BLOCK
    kw2 = n_kv * 128

    def body(s_ref, q_ref, do_ref, o_ref, l_ref, kc_ref, kp_ref, vc_ref, vp_ref,
             dq_ref, dk_ref, dv_ref, ds_ref, cdk, cdv):
        n = pl.program_id(0)

        @pl.when(n == 0)
        def _():
            cdk[...] = jnp.zeros_like(cdk)
            cdv[...] = jnp.zeros_like(cdv)
            ds_ref[...] = jnp.zeros_like(ds_ref)

        @pl.when(n < nb)
        def _():
            bias = _band_bias(n, pad)
            lane = lax.broadcasted_iota(jnp.int32, (BLOCK, 128), 1)
            lane2 = lax.broadcasted_iota(jnp.int32, (2 * BLOCK, 128), 1)
            lane1 = lax.broadcasted_iota(jnp.int32, (1, 128), 1)
            dsink = jnp.zeros((1, 128), F32)
            for g in range(n_kv):
                k2a = _band(kp_ref, kc_ref, 256 * g)
                k2b = _band(kp_ref, kc_ref, 256 * g + 128)
                v2a = _band(vp_ref, vc_ref, 256 * g)
                v2b = _band(vp_ref, vc_ref, 256 * g + 128)
                dka = jnp.zeros((2 * BLOCK, 128), F32)
                dkb = jnp.zeros((2 * BLOCK, 128), F32)
                dva = jnp.zeros((2 * BLOCK, 128), F32)
                dvb = jnp.zeros((2 * BLOCK, 128), F32)
                for p in range(GROUP // 2):
                    c0 = 128 * (g * (GROUP // 2) + p)
                    he = GROUP * g + 2 * p
                    q2 = q_ref[:, c0:c0 + 128]
                    do2 = do_ref[:, c0:c0 + 128]
                    l2 = l_ref[:, c0:c0 + 128]
                    prod = do2.astype(F32) * o_ref[:, c0:c0 + 128].astype(F32)
                    l2r = pltpu.roll(l2, HEAD_DIM, axis=1)
                    halves = ((k2a, v2a, lane < HEAD_DIM, he), (k2b, v2b, lane >= HEAD_DIM, he + 1))
                    dq2 = jnp.zeros((BLOCK, 128), F32)
                    outs = []
                    for (k2, v2, mine, h) in halves:
                        lse_t = jnp.where(mine, l2, l2r)
                        delta = jnp.sum(jnp.where(mine, prod, 0.0), axis=1, keepdims=True)
                        pr = jnp.exp(_dot_nt(q2, k2) + bias - jnp.concatenate([lse_t, lse_t], axis=1))
                        dp = _dot_nt(do2, v2)
                        dsc = (pr * (dp - delta)).astype(BF16)
                        prb = pr.astype(BF16)
                        dq2 = dq2 + jnp.dot(dsc, k2, preferred_element_type=F32)
                        outs.append((_dot_tn(dsc, q2), _dot_tn(prb, do2)))
                        psink = jnp.exp(s_ref[h] - lse_t)
                        col = jnp.sum(psink * delta, axis=0, keepdims=True)
                        dsink = dsink - jnp.where(lane1 == h, col, 0.0)
                    dq_ref[:, c0:c0 + 128] = dq2
                    dka = dka + outs[0][0]
                    dva = dva + outs[0][1]
                    dkb = dkb + outs[1][0]
                    dvb = dvb + outs[1][1]
                gc = pl.ds(128 * g, 128)
                dk2 = jnp.where(lane2 < HEAD_DIM, dka, dkb)
                dv2 = jnp.where(lane2 < HEAD_DIM, dva, dvb)
                dk_ref[:, gc] = cdk[:, gc] + dk2[0:BLOCK]
                dv_ref[:, gc] = cdv[:, gc] + dv2[0:BLOCK]
                cdk[:, gc] = dk2[BLOCK:2 * BLOCK]
                cdv[:, gc] = dv2[BLOCK:2 * BLOCK]
            ds_ref[0:1, :] += dsink

        @pl.when(n == nb)
        def _():
            dk_ref[...] = cdk[...]
            dv_ref[...] = cdv[...]

    cur = lambda w: pl.BlockSpec((BLOCK, w), lambda n: (jnp.minimum(n, nb - 1), 0))
    prv = lambda w: pl.BlockSpec((BLOCK, w), lambda n: (jnp.clip(n - 1, 0, nb - 1), 0))
    kxw = kx.shape[1]
    return pl.pallas_call(
        body, name=name, grid=(nb + 1,),
        in_specs=[pl.BlockSpec(memory_space=pltpu.SMEM), cur(qw), cur(qw), cur(qw), cur(qw),
                  cur(kxw), prv(kxw), cur(kxw), prv(kxw)],
        out_specs=[cur(qw), prv(kw2), prv(kw2), pl.BlockSpec((8, 128), lambda n: (0, 0))],
        out_shape=[jax.ShapeDtypeStruct((R, qw), F32), jax.ShapeDtypeStruct((R, kw2), F32),
                   jax.ShapeDtypeStruct((R, kw2), F32), jax.ShapeDtypeStruct((8, 128), F32)],
        scratch_shapes=[pltpu.VMEM((BLOCK, kw2), F32), pltpu.VMEM((BLOCK, kw2), F32)],
        compiler_params=_params(("arbitrary",)),
    )(sinks, q, do, o, lse, kx, kx, vx, vx)


def loss_bwd(h, g, target, *, name):
    R, D = h.shape
    nb = R // BLOCK

    def body(h_ref, g_ref, t_ref, loss_ref, dh_ref, dhb_ref, dg_ref):
        n = pl.program_id(0)
        xhat, rstd = _rms_parts(h_ref[...])
        gv = g_ref[...]
        diff = jnp.where(n > 0, xhat * gv - t_ref[...], 0.0)
        part = (0.5 / D) * jnp.sum(jnp.sum(diff * diff, axis=1, keepdims=True), axis=0, keepdims=True)
        dout = diff * (1.0 / D)
        dxh = dout * gv
        dh = rstd * (dxh - xhat * jnp.mean(dxh * xhat, axis=-1, keepdims=True))
        dh_ref[...] = dh
        dhb_ref[...] = dh.astype(BF16)
        dg = jnp.sum(dout * xhat, axis=0, keepdims=True)

        @pl.when(n == 0)
        def _():
            loss_ref[...] = jnp.zeros_like(loss_ref) + part
            dg_ref[...] = dg

        @pl.when(n > 0)
        def _():
            loss_ref[...] += part
            dg_ref[...] += dg

    blk = pl.BlockSpec((BLOCK, D), lambda n: (n, 0))
    return pl.pallas_call(
        body, name=name, grid=(nb,),
        in_specs=[blk, pl.BlockSpec((1, D), lambda n: (0, 0)),
                  pl.BlockSpec((BLOCK, D), lambda n: (jnp.maximum(n - 1, 0), 0))],
        out_specs=[pl.BlockSpec((8, 128), lambda n: (0, 0)), blk, blk, pl.BlockSpec((1, D), lambda n: (0, 0))],
        out_shape=[jax.ShapeDtypeStruct((8, 128), F32), jax.ShapeDtypeStruct((R, D), F32),
                   jax.ShapeDtypeStruct((R, D), BF16), jax.ShapeDtypeStruct((1, D), F32)],
        compiler_params=_params(("arbitrary",)),
    )(h, g, target)


def _adam_math(w, g, m, v):
    m = ADAM_B1 * m + (1.0 - ADAM_B1) * g
    v = ADAM_B2 * v + (1.0 - ADAM_B2) * (g * g)
    m_hat = m / (1.0 - ADAM_B1 ** ADAM_STEP)
    v_hat = v / (1.0 - ADAM_B2 ** ADAM_STEP)
    delta = -ADAM_LR * (m_hat / (jnp.sqrt(v_hat) + ADAM_EPS) + ADAM_WD * w)
    return delta, m, v


def adam(w, m, v, g, *, name):
    r, C = w.shape
    parts = g.ndim == 3
    tr = _tile(r, 128, 8)

    def body(w_ref, m_ref, v_ref, g_ref, go_ref, d_ref, mo_ref, vo_ref):
        if parts:
            gv = g_ref[0].astype(F32)
            for s in range(1, N_DEV):
                gv = gv + g_ref[s].astype(F32)
        else:
            gv = g_ref[...]
        delta, mn, vn = _adam_math(w_ref[...], gv, m_ref[...], v_ref[...])
        go_ref[...] = gv
        d_ref[...] = delta
        mo_ref[...] = mn
        vo_ref[...] = vn

    blk = pl.BlockSpec((tr, C), lambda i: (i, 0))
    gblk = pl.BlockSpec((N_DEV, tr, C), lambda i: (0, i, 0)) if parts else blk
    return pl.pallas_call(
        body, name=name, grid=(r // tr,),
        in_specs=[blk, blk, blk, gblk], out_specs=[blk] * 4,
        out_shape=[jax.ShapeDtypeStruct((r, C), F32)] * 4,
        compiler_params=_params(("parallel",)),
    )(w, m, v, g)


def sum_parts(parts, *, name):
    _, r, C = parts.shape
    tr = _tile(r, 128, 8)

    def body(p_ref, o_ref):
        acc = p_ref[0].astype(F32)
        for s in range(1, N_DEV):
            acc = acc + p_ref[s].astype(F32)
        o_ref[...] = acc

    return pl.pallas_call(
        body, name=name, grid=(r // tr,),
        in_specs=[pl.BlockSpec((N_DEV, tr, C), lambda i: (0, i, 0))],
        out_specs=pl.BlockSpec((tr, C), lambda i: (i, 0)),
        out_shape=jax.ShapeDtypeStruct((r, C), F32),
        compiler_params=_params(("parallel",)),
    )(parts)


def _coords():
    return lax.axis_index("x"), lax.axis_index("y"), lax.axis_index("c")


def _peer(m):
    x, y, c = _coords()
    px = 1 - x if m & 4 else x
    py = 1 - y if m & 2 else y
    pc = 1 - c if m & 1 else c
    return (px, py, pc), 4 * px + 2 * py + pc


def exchange(items, *, all_to_all, name):
    n = len(items)
    if all_to_all:
        out_shape = [jax.ShapeDtypeStruct(a.shape, a.dtype) for a in items]
    else:
        out_shape = [jax.ShapeDtypeStruct((N_DEV,) + a.shape, a.dtype) for a in items]

    def body(*refs):
        ins, outs = refs[:n], refs[n:2 * n]
        send_sems, recv_sems, local_sems = refs[2 * n:]
        x, y, c = _coords()
        me = 4 * x + 2 * y + c

        def src(i, idx):
            return ins[i].at[idx] if all_to_all else ins[i]

        local = [pltpu.make_async_copy(src(i, me), outs[i].at[me], local_sems.at[i]) for i in range(n)]
        for cp in local:
            cp.start()
        sends = []
        for m in range(1, N_DEV):
            peer, pidx = _peer(m)
            for i in range(n):
                k = i * (N_DEV - 1) + m - 1
                cp = pltpu.make_async_remote_copy(src_ref=src(i, pidx), dst_ref=outs[i].at[me],
                                                  send_sem=send_sems.at[k], recv_sem=recv_sems.at[k],
                                                  device_id=peer, device_id_type=MESH)
                cp.start()
                sends.append(cp)
        for m in range(1, N_DEV):
            peer, pidx = _peer(m)
            for i in range(n):
                k = i * (N_DEV - 1) + m - 1
                pltpu.make_async_remote_copy(src_ref=src(i, pidx), dst_ref=outs[i].at[pidx],
                                             send_sem=send_sems.at[k], recv_sem=recv_sems.at[k],
                                             device_id=peer, device_id_type=MESH).wait_recv()
        for cp in sends:
            cp.wait_send()
        for cp in local:
            cp.wait()

    any_spec = pl.BlockSpec(memory_space=pl.ANY)
    return pl.pallas_call(
        body, name=name, in_specs=[any_spec] * n, out_specs=[any_spec] * n, out_shape=out_shape,
        scratch_shapes=[pltpu.SemaphoreType.DMA((n * (N_DEV - 1),)), pltpu.SemaphoreType.DMA((n * (N_DEV - 1),)),
                        pltpu.SemaphoreType.DMA((n,))],
    )(*items)


HBM_SPEC = pl.BlockSpec(memory_space=pltpu.HBM)
SEM_SPEC = pl.BlockSpec(memory_space=pltpu.SEMAPHORE)
SPLIT_PARAMS = pltpu.CompilerParams(has_side_effects=pltpu.SideEffectType.DATAFLOW_SIDE_EFFECTING)


def _split_copies(src_ref, land_ref, send_sems, recv_sems, all_to_all):
    x, y, c = _coords()
    me = 4 * x + 2 * y + c
    copies = []
    for m in range(1, N_DEV):
        peer, pidx = _peer(m)
        copies.append(pltpu.make_async_remote_copy(
            src_ref=src_ref.at[pidx] if all_to_all else src_ref, dst_ref=land_ref.at[me],
            send_sem=send_sems.at[m - 1], recv_sem=recv_sems.at[m - 1], device_id=peer, device_id_type=MESH))
    return copies


def copy_start(items, *, all_to_all, name):
    n = len(items)
    lands = [lax.empty(a.shape if all_to_all else (N_DEV,) + a.shape, a.dtype) for a in items]

    def body(*refs):
        srcs, lnds, outs = refs[:n], refs[n:2 * n], refs[2 * n:]
        for i in range(n):
            for cp in _split_copies(srcs[i], lnds[i], outs[4 * i], outs[4 * i + 1], all_to_all):
                cp.start()
        outs[4 * n][...] = jnp.zeros((8, 128), F32)

    out_shape, out_specs, aliases = [], [], {}
    for i, (a, l) in enumerate(zip(items, lands)):
        out_shape += [pltpu.SemaphoreType.DMA((N_DEV - 1,)), pltpu.SemaphoreType.DMA((N_DEV - 1,)),
                      pltpu.HBM(a.shape, a.dtype), pltpu.HBM(l.shape, l.dtype)]
        out_specs += [SEM_SPEC, SEM_SPEC, HBM_SPEC, HBM_SPEC]
        aliases[i] = 4 * i + 2
        aliases[n + i] = 4 * i + 3
    out_shape.append(jax.ShapeDtypeStruct((8, 128), F32))
    out_specs.append(pl.BlockSpec(memory_space=pltpu.VMEM))
    hbm = lambda a: pltpu.with_memory_space_constraint(a, pltpu.HBM)
    res = pl.pallas_call(
        body, name=name, in_specs=[HBM_SPEC] * (2 * n), out_specs=out_specs, out_shape=out_shape,
        input_output_aliases=aliases, compiler_params=SPLIT_PARAMS,
    )(*[hbm(a) for a in items], *[hbm(l) for l in lands])
    return [tuple(res[4 * i:4 * i + 4]) for i in range(n)], res[4 * n]


def copy_wait(handle, after, *, all_to_all, name):
    send_sems, recv_sems, src, land = handle

    def body(src_ref, land_ref, send_ref, recv_ref, after_ref, src_out, got_ref, local_sem):
        for cp in _split_copies(src_ref, land_ref, send_ref, recv_ref, all_to_all):
            cp.wait_send()
            cp.wait_recv()
        x, y, c = _coords()
        me = 4 * x + 2 * y + c
        own = pltpu.make_async_copy(src_ref.at[me] if all_to_all else src_ref, got_ref.at[me], local_sem)
        own.start()
        own.wait()

    return pl.pallas_call(
        body, name=name, in_specs=[HBM_SPEC, HBM_SPEC, SEM_SPEC, SEM_SPEC, ANY_SPEC],
        out_specs=[HBM_SPEC, HBM_SPEC],
        out_shape=[pltpu.HBM(src.shape, src.dtype), pltpu.HBM(land.shape, land.dtype)],
        input_output_aliases={0: 0, 1: 1}, scratch_shapes=[pltpu.SemaphoreType.DMA(())],
        compiler_params=SPLIT_PARAMS,
    )(src, land, send_sems, recv_sems, after)[1]


def kernel(x, meta_tokens, norm_mix_0, w_in_conv, conv_w, w_out_conv, norm_mlp_0, w_up_0, w_down_0, norm_mix_1, w_qkv, attn_sinks, w_o, norm_mlp_1, w_up_1, w_down_1, norm_final, loss_target, m_meta_tokens, m_norm_mix_0, m_w_in_conv, m_conv_w, m_w_out_conv, m_norm_mlp_0, m_w_up_0, m_w_down_0, m_norm_mix_1, m_w_qkv, m_attn_sinks, m_w_o, m_norm_mlp_1, m_w_up_1, m_w_down_1, m_norm_final, v_meta_tokens, v_norm_mix_0, v_w_in_conv, v_conv_w, v_w_out_conv, v_norm_mlp_0, v_w_up_0, v_w_down_0, v_norm_mix_1, v_w_qkv, v_attn_sinks, v_w_o, v_norm_mlp_1, v_w_up_1, v_w_down_1, v_norm_final):
    L, D = x.shape[1], x.shape[2]
    n_meta = meta_tokens.shape[0]
    pad = BLOCK - n_meta
    R = BLOCK + L
    n_q = D // HEAD_DIM
    n_kv = n_q // GROUP
    assert n_kv % 2 == 0 and L % BLOCK == 0 and D % 128 == 0
    x = x.reshape(L, D)
    target = loss_target.reshape(L, D)
    x_id, y_id, c_id = _coords()
    me = 4 * x_id + 2 * y_id + c_id

    col_names = ("in", "up0", "qkv", "up1")
    col_w = dict(zip(col_names, (w_in_conv, w_up_0, w_qkv, w_up_1)))
    row_names = ("out", "down0", "o", "down1")
    row_w = dict(zip(row_names, (w_out_conv, w_down_0, w_o, w_down_1)))
    shard = {k: col_w[k].T.astype(BF16) for k in col_names}
    shard.update({k: row_w[k].astype(BF16) for k in row_names})
    use_order = ("in", "out", "up0", "down0", "qkv", "o", "up1", "down1")
    handles, token = copy_start([shard[k] for k in use_order], all_to_all=False, name="gather_start")
    handles = dict(zip(use_order, handles))

    def weight(k, after):
        return copy_wait(handles[k], after, all_to_all=False, name="gather_wait_" + k).reshape(-1, D)

    small_in = exchange([meta_tokens, conv_w], all_to_all=False, name="comm_gather")
    meta_full = jnp.transpose(small_in[0], (1, 0, 2)).reshape(n_meta, D)
    conv_full = jnp.transpose(small_in[1], (1, 0, 2)).reshape(conv_w.shape[0], D)

    vec = lambda a: a.reshape(1, D)
    pos = jnp.arange(R, dtype=F32) - pad
    inv = ROPE_THETA ** (-jnp.arange(0, HEAD_DIM, 2, dtype=F32) / HEAD_DIM)
    ang = pos[:, None] * inv[None, :]
    cos32, sin32 = jnp.cos(ang), jnp.sin(ang)
    cos = jnp.concatenate([cos32] * 4, axis=1)
    sin = jnp.concatenate([-sin32, sin32, -sin32, sin32], axis=1)

    W = {}
    h0 = jnp.concatenate([jnp.zeros((pad, D), F32), meta_full, x], axis=0)
    n0 = norm_fwd(h0, vec(norm_mix_0), token, name="norm0")
    W["in"] = weight("in", n0)
    bcu = mm_nt(n0, W["in"], name="in_proj", out_dtype=BF16)
    gated = conv_fwd(bcu, conv_full, name="conv_fwd")
    W["out"] = weight("out", gated)
    h1, n1 = mm_row(gated, W["out"], gated, name="out_proj", epi="resnorm", res=h0, g=vec(norm_mlp_0))
    W["up0"] = weight("up0", n1)
    a0 = mm_nt(n1, W["up0"], name="up0", out_dtype=BF16, epi="relu")
    W["down0"] = weight("down0", a0)
    h2, n2 = mm_row(a0, W["down0"], a0, name="down0", epi="resnorm", a_sq=True, res=h1, g=vec(norm_mix_1))
    W["qkv"] = weight("qkv", n2)
    qkv = mm_nt(n2, W["qkv"], name="qkv_proj", out_dtype=F32)
    q, kx, vx = rope_fwd(qkv, cos, sin, n_q=n_q, n_kv=n_kv, name="rope_fwd")
    o, lse = attn_fwd(q, kx, vx, attn_sinks, pad=pad, name="attn_fwd")
    W["o"] = weight("o", o)
    h3, n3 = mm_row(o, W["o"], o, name="o_proj", epi="resnorm", res=h2, g=vec(norm_mlp_1))
    W["up1"] = weight("up1", n3)
    a1 = mm_nt(n3, W["up1"], name="up1", out_dtype=BF16, epi="relu")
    W["down1"] = weight("down1", a1)
    h4 = mm_row(a1, W["down1"], a1, name="down1", epi="res", a_sq=True, res=h3)

    loss_part, dh4, dh4b, dg_final = loss_bwd(h4, vec(norm_final), target, name="loss_bwd")
    loss = lax.psum(loss_part[0, 0], ("x", "y", "c"))

    sent = {}

    def scatter(k, dw):
        (sent[k],), tok = copy_start([dw.reshape(N_DEV, -1, D)], all_to_all=True, name="a2a_start_" + k)
        return tok

    t = scatter("down1", mm_tn(a1, dh4b, name="dw_down1", a_sq=True))
    dup1 = mm_nt(dh4b, W["down1"], name="d_down1", out_dtype=BF16, epi="mul2a", extra=a1, dep=t)
    t = scatter("up1", mm_tn(dup1, n3, name="dw_up1"))
    dh3, dh3b, dg_mlp1 = mm_row(dup1, W["up1"], t, name="d_up1", epi="normbwd", h=h3, g=vec(norm_mlp_1), dres=dh4)

    t = scatter("o", mm_tn(o, dh3b, name="dw_o"))
    do = mm_nt(dh3b, W["o"], name="d_o", out_dtype=BF16, dep=t)
    dq, dkx, dvx, dsinks = attn_bwd(q, kx, vx, o, do, lse, attn_sinks, pad=pad, name="attn_bwd")
    dqkv = rope_bwd(dq, dkx, dvx, cos, sin, n_q=n_q, n_kv=n_kv, name="rope_bwd")
    t = scatter("qkv", mm_tn(dqkv, n2, name="dw_qkv"))
    dh2, dh2b, dg_mix1 = mm_row(dqkv, W["qkv"], t, name="d_qkv", epi="normbwd", h=h2, g=vec(norm_mix_1), dres=dh3)

    t = scatter("down0", mm_tn(a0, dh2b, name="dw_down0", a_sq=True))
    dup0 = mm_nt(dh2b, W["down0"], name="d_down0", out_dtype=BF16, epi="mul2a", extra=a0, dep=t)
    t = scatter("up0", mm_tn(dup0, n1, name="dw_up0"))
    dh1, dh1b, dg_mlp0 = mm_row(dup0, W["up0"], t, name="d_up0", epi="normbwd", h=h1, g=vec(norm_mlp_0), dres=dh2)

    t = scatter("out", mm_tn(gated, dh1b, name="dw_out"))
    dgated = mm_nt(dh1b, W["out"], name="d_out", out_dtype=F32, dep=t)
    db, dc, du, dconv = conv_bwd(bcu, conv_full, dgated, name="conv_bwd")
    dbcu = jnp.concatenate([db, dc, du], axis=1)
    t = scatter("in", mm_tn(dbcu, n0, name="dw_in"))
    dh0, _, dg_mix0 = mm_row(dbcu, W["in"], t, name="d_in", epi="normbwd", h=h0, g=vec(norm_mix_0), dres=dh1)
    grad_x = dh0[BLOCK:].reshape(1, L, D)

    recv = {k: copy_wait(sent[k], dh0, all_to_all=True, name="a2a_wait_" + k)
            for k in ("down1", "up1", "o", "qkv", "down0", "up0", "out", "in")}

    n_sink = attn_sinks.shape[0]
    slab = jnp.concatenate([
        dg_mix0, dg_mlp0, dg_mix1, dg_mlp1, dg_final,
        jnp.pad(dsinks[0:1, :n_sink], ((0, 0), (0, D - n_sink))), jnp.zeros((2, D), F32),
        dconv, dh0[pad:BLOCK]], axis=0)
    slabs = exchange([slab], all_to_all=False, name="comm_small")[0]
    small = sum_parts(slabs, name="sum_small")
    cols = D // N_DEV
    my_cols = lambda a: lax.dynamic_slice_in_dim(a, me * cols, cols, axis=1)

    grads, deltas, new_m, new_v = {}, {}, {}, {}

    def update(key, w, m, v, g, shape):
        s2 = (1, -1) if w.ndim == 1 else w.shape
        g_, d_, m_, v_ = adam(w.reshape(s2), m.reshape(s2), v.reshape(s2), g if g.ndim == 3 else g.reshape(s2),
                              name="adam_" + key)
        grads[key], deltas[key], new_m[key], new_v[key] = (t.reshape(shape) for t in (g_, d_, m_, v_))

    update("meta_tokens", meta_tokens, m_meta_tokens, v_meta_tokens, my_cols(small[16:16 + n_meta]), meta_tokens.shape)
    update("norm_mix_0", norm_mix_0, m_norm_mix_0, v_norm_mix_0, small[0], (D,))
    update("conv_w", conv_w, m_conv_w, v_conv_w, my_cols(small[8:8 + conv_w.shape[0]]), conv_w.shape)
    update("norm_mlp_0", norm_mlp_0, m_norm_mlp_0, v_norm_mlp_0, small[1], (D,))
    update("norm_mix_1", norm_mix_1, m_norm_mix_1, v_norm_mix_1, small[2], (D,))
    update("attn_sinks", attn_sinks, m_attn_sinks, v_attn_sinks, small[5, :n_sink], (n_sink,))
    update("norm_mlp_1", norm_mlp_1, m_norm_mlp_1, v_norm_mlp_1, small[3], (D,))
    update("norm_final", norm_final, m_norm_final, v_norm_final, small[4], (D,))
    big = {"in": ("w_in_conv", w_in_conv, m_w_in_conv, v_w_in_conv), "up0": ("w_up_0", w_up_0, m_w_up_0, v_w_up_0),
           "qkv": ("w_qkv", w_qkv, m_w_qkv, v_w_qkv), "up1": ("w_up_1", w_up_1, m_w_up_1, v_w_up_1),
           "out": ("w_out_conv", w_out_conv, m_w_out_conv, v_w_out_conv),
           "down0": ("w_down_0", w_down_0, m_w_down_0, v_w_down_0), "o": ("w_o", w_o, m_w_o, v_w_o),
           "down1": ("w_down_1", w_down_1, m_w_down_1, v_w_down_1)}
    for k in col_names:
        key, w, m, v = big[k]
        update(key, w, m, v, sum_parts(recv[k], name="sum_" + k).T, w.shape)
    for k in row_names:
        key, w, m, v = big[k]
        update(key, w, m, v, recv[k], w.shape)

    order = ("meta_tokens", "norm_mix_0", "w_in_conv", "conv_w", "w_out_conv", "norm_mlp_0", "w_up_0", "w_down_0",
             "norm_mix_1", "w_qkv", "attn_sinks", "w_o", "norm_mlp_1", "w_up_1", "w_down_1", "norm_final")
    return (loss, grad_x, *[grads[k] for k in order], *[deltas[k] for k in order],
            *[new_m[k] for k in order], *[new_v[k] for k in order])
```

```python
import functools

import jax
import jax.numpy as jnp
from jax import lax
from jax.experimental import pallas as pl
from jax.experimental.pallas import tpu as pltpu

F32 = jnp.float32
BF16 = jnp.bfloat16

HEAD_DIM = 64
GROUP = 8
BLOCK = 128
N_DEV = 8
RMS_EPS = 1e-5
NEG_INF = -1e30
ROPE_THETA = 10000.0
ADAM_LR = 0.001
ADAM_B1 = 0.9
ADAM_B2 = 0.999
ADAM_EPS = 1e-08
ADAM_WD = 0.01
ADAM_STEP = 10
VMEM_LIMIT = 60 * 1024 * 1024
MESH = pl.DeviceIdType.MESH


def _tile(n, target, mult):
    best = None
    for t in range(mult, min(n, target) + 1, mult):
        if n % t == 0:
            best = t
    return best if best is not None else n


def _params(sem):
    return pltpu.CompilerParams(dimension_semantics=sem, vmem_limit_bytes=VMEM_LIMIT)


ANY_SPEC = pl.BlockSpec(memory_space=pl.ANY)


def mm_nt(a, bt, *, name, out_dtype, epi=None, extra=None, dep=None):
    M, K = a.shape
    N = bt.shape[0]
    tm = _tile(M, 1664, 16)
    tn = _tile(N, 512, 128)

    def body(*refs):
        a_ref, b_ref, e_ref, o_ref = refs[0], refs[1], refs[2], refs[-1]
        acc = lax.dot_general(a_ref[...], b_ref[...], (((1,), (1,)), ((), ())), preferred_element_type=F32)
        if epi == "relu":
            acc = jnp.maximum(acc, 0.0)
        elif epi == "mul2a":
            acc = acc * (2.0 * e_ref[...].astype(F32))
        o_ref[...] = acc.astype(o_ref.dtype)

    in_specs = [pl.BlockSpec((tm, K), lambda i, j: (i, 0)), pl.BlockSpec((tn, K), lambda i, j: (j, 0))]
    args = [a, bt]
    if extra is not None:
        in_specs.append(pl.BlockSpec((tm, tn), lambda i, j: (i, j)))
        args.append(extra)
    if dep is not None:
        in_specs.append(ANY_SPEC)
        args.append(dep)
    return pl.pallas_call(
        body, name=name, grid=(M // tm, N // tn), in_specs=in_specs,
        out_specs=pl.BlockSpec((tm, tn), lambda i, j: (i, j)),
        out_shape=jax.ShapeDtypeStruct((M, N), out_dtype),
        compiler_params=_params(("parallel", "parallel")),
    )(*args)


def _rms_parts(h):
    rstd = lax.rsqrt(jnp.mean(h * h, axis=-1, keepdims=True) + RMS_EPS)
    return h * rstd, rstd


def mm_row(a, b, dep, *, name, epi, a_sq=False, res=None, g=None, h=None, dres=None):
    M, K = a.shape
    N = b.shape[1]
    tm = _tile(M, 416 if epi == "normbwd" else 640, 16)
    tk = _tile(K, 512, 128)
    nk = K // tk
    rc = _tile(tm, 160, 8)

    def body(*refs):
        acc_ref = refs[-1]
        a_ref, b_ref = refs[:2]
        refs = refs[1:-1]
        i = pl.program_id(0)
        k = pl.program_id(1)
        av = a_ref[...]
        if a_sq:
            av = av.astype(F32)
            av = (av * av).astype(BF16)
        part = jnp.dot(av, b_ref[...], preferred_element_type=F32)

        @pl.when(k == 0)
        def _():
            acc_ref[...] = part

        @pl.when(k > 0)
        def _():
            acc_ref[...] += part

        @pl.when(k == nk - 1)
        def _():
            if epi == "res":
                res_ref, o_ref = refs[2:]
                o_ref[...] = res_ref[...] + acc_ref[...]
            elif epi == "resnorm":
                res_ref, g_ref, o_ref, n_ref = refs[2:]
                for r0 in range(0, tm, rc):
                    rows = pl.ds(r0, rc)
                    hn = res_ref[rows, :] + acc_ref[rows, :]
                    o_ref[rows, :] = hn
                    xhat, _ = _rms_parts(hn)
                    n_ref[rows, :] = (xhat * g_ref[...]).astype(BF16)
            else:
                h_ref, g_ref, dres_ref, dh_ref, dhb_ref, dg_ref = refs[2:]
                dg = jnp.zeros((1, N), F32)
                for r0 in range(0, tm, rc):
                    rows = pl.ds(r0, rc)
                    dn = acc_ref[rows, :]
                    xhat, rstd = _rms_parts(h_ref[rows, :])
                    dxh = dn * g_ref[...]
                    dh = dres_ref[rows, :] + rstd * (dxh - xhat * jnp.mean(dxh * xhat, axis=-1, keepdims=True))
                    dh_ref[rows, :] = dh
                    dhb_ref[rows, :] = dh.astype(BF16)
                    dg = dg + jnp.sum(dn * xhat, axis=0, keepdims=True)

                @pl.when(i == 0)
                def _():
                    dg_ref[...] = dg

                @pl.when(i > 0)
                def _():
                    dg_ref[...] += dg

    row_spec = pl.BlockSpec((tm, N), lambda i, k: (i, 0))
    vec_spec = pl.BlockSpec((1, N), lambda i, k: (0, 0))
    in_specs = [pl.BlockSpec((tm, tk), lambda i, k: (i, k)), pl.BlockSpec((tk, N), lambda i, k: (k, 0)), ANY_SPEC]
    args = [a, b, dep]
    if epi == "res":
        in_specs += [row_spec]
        args += [res]
        out_specs = row_spec
        out_shape = jax.ShapeDtypeStruct((M, N), F32)
    elif epi == "resnorm":
        in_specs += [row_spec, vec_spec]
        args += [res, g]
        out_specs = [row_spec, row_spec]
        out_shape = [jax.ShapeDtypeStruct((M, N), F32), jax.ShapeDtypeStruct((M, N), BF16)]
    else:
        in_specs += [row_spec, vec_spec, row_spec]
        args += [h, g, dres]
        out_specs = [row_spec, row_spec, vec_spec]
        out_shape = [jax.ShapeDtypeStruct((M, N), F32), jax.ShapeDtypeStruct((M, N), BF16),
                     jax.ShapeDtypeStruct((1, N), F32)]
    return pl.pallas_call(
        body, name=name, grid=(M // tm, nk), in_specs=in_specs, out_specs=out_specs, out_shape=out_shape,
        scratch_shapes=[pltpu.VMEM((tm, N), F32)],
        compiler_params=_params(("arbitrary", "arbitrary")),
    )(*args)


def mm_tn(a, b, *, name, a_sq=False):
    T, M = a.shape
    N = b.shape[1]
    tm = _tile(M, 1024, 128)
    tk = _tile(T, 640, 16)
    nk = T // tk

    def body(a_ref, b_ref, o_ref, acc_ref):
        k = pl.program_id(1)
        av = a_ref[...]
        if a_sq:
            av = av.astype(F32)
            av = (av * av).astype(BF16)
        part = lax.dot_general(av, b_ref[...], (((0,), (0,)), ((), ())), preferred_element_type=F32)

        @pl.when(k == 0)
        def _():
            acc_ref[...] = part

        @pl.when(k > 0)
        def _():
            acc_ref[...] += part

        @pl.when(k == nk - 1)
        def _():
            o_ref[...] = acc_ref[...].astype(BF16)

    return pl.pallas_call(
        body, name=name, grid=(M // tm, nk),
        in_specs=[pl.BlockSpec((tk, tm), lambda i, k: (k, i)), pl.BlockSpec((tk, N), lambda i, k: (k, 0))],
        out_specs=pl.BlockSpec((tm, N), lambda i, k: (i, 0)),
        out_shape=jax.ShapeDtypeStruct((M, N), BF16),
        scratch_shapes=[pltpu.VMEM((tm, N), F32)],
        compiler_params=_params(("parallel", "arbitrary")),
    )(a, b)


def norm_fwd(h, g, dep, *, name):
    R, D = h.shape
    tr = _tile(R, 320, 16)

    def body(h_ref, g_ref, dep_ref, n_ref):
        xhat, _ = _rms_parts(h_ref[...])
        n_ref[...] = (xhat * g_ref[...]).astype(BF16)

    return pl.pallas_call(
        body, name=name, grid=(R // tr,),
        in_specs=[pl.BlockSpec((tr, D), lambda i: (i, 0)), pl.BlockSpec((1, D), lambda i: (0, 0)), ANY_SPEC],
        out_specs=pl.BlockSpec((tr, D), lambda i: (i, 0)),
        out_shape=jax.ShapeDtypeStruct((R, D), BF16),
        compiler_params=_params(("parallel",)),
    )(h, g, dep)


def _shift_down(v, halo, s):
    n = v.shape[0]
    row = lax.broadcasted_iota(jnp.int32, v.shape, 0)
    out = pltpu.roll(v, s, axis=0)
    hn = halo.shape[0]
    for r in range(s):
        out = jnp.where(row == r, halo[hn - s + r:hn - s + r + 1, :], out)
    return out


def _shift_up(v, halo, s):
    n = v.shape[0]
    row = lax.broadcasted_iota(jnp.int32, v.shape, 0)
    out = pltpu.roll(v, n - s, axis=0)
    for r in range(s):
        out = jnp.where(row == n - s + r, halo[r:r + 1, :], out)
    return out


HALO = 16


def conv_fwd(bcu, conv_w, *, name):
    R, D3 = bcu.shape
    D = D3 // 3
    tr = _tile(R, 320, 16)
    tc = _tile(D, 512, 128)
    nc = D // tc
    hb = tr // HALO

    def body(b_ref, c_ref, u_ref, ch_ref, uh_ref, w_ref, o_ref):
        i = pl.program_id(0)
        v = c_ref[...].astype(F32) * u_ref[...].astype(F32)
        vh = ch_ref[...].astype(F32) * uh_ref[...].astype(F32)
        vh = jnp.where(i > 0, vh, 0.0)
        v1 = _shift_down(v, vh, 1)
        v2 = _shift_down(v, vh, 2)
        conv = w_ref[0:1, :] * v2 + w_ref[1:2, :] * v1 + w_ref[2:3, :] * v
        o_ref[...] = (b_ref[...].astype(F32) * conv).astype(BF16)

    def prev(i):
        return jnp.maximum(i * hb - 1, 0)

    return pl.pallas_call(
        body, name=name, grid=(R // tr, nc),
        in_specs=[pl.BlockSpec((tr, tc), lambda i, j: (i, j)),
                  pl.BlockSpec((tr, tc), lambda i, j: (i, nc + j)),
                  pl.BlockSpec((tr, tc), lambda i, j: (i, 2 * nc + j)),
                  pl.BlockSpec((HALO, tc), lambda i, j: (prev(i), nc + j)),
                  pl.BlockSpec((HALO, tc), lambda i, j: (prev(i), 2 * nc + j)),
                  pl.BlockSpec((3, tc), lambda i, j: (0, j))],
        out_specs=pl.BlockSpec((tr, tc), lambda i, j: (i, j)),
        out_shape=jax.ShapeDtypeStruct((R, D), BF16),
        compiler_params=_params(("parallel", "parallel")),
    )(bcu, bcu, bcu, bcu, bcu, conv_w)


def conv_bwd(bcu, conv_w, dg, *, name):
    R, D3 = bcu.shape
    D = D3 // 3
    tr = _tile(R, 320, 16)
    tc = _tile(D, 512, 128)
    nc = D // tc
    hb = tr // HALO
    nt = R // tr
    last_h = R // HALO - 1

    def body(b_ref, c_ref, u_ref, ch_ref, uh_ref, bn_ref, dg_ref, dgn_ref, w_ref, db_ref, dc_ref, du_ref, dw_ref):
        i = pl.program_id(1)
        b = b_ref[...].astype(F32)
        c = c_ref[...].astype(F32)
        u = u_ref[...].astype(F32)
        v = c * u
        vh = jnp.where(i > 0, ch_ref[...].astype(F32) * uh_ref[...].astype(F32), 0.0)
        v1 = _shift_down(v, vh, 1)
        v2 = _shift_down(v, vh, 2)
        conv = w_ref[0:1, :] * v2 + w_ref[1:2, :] * v1 + w_ref[2:3, :] * v
        dgv = dg_ref[...]
        db_ref[...] = (dgv * conv).astype(BF16)
        dconv = dgv * b
        dconv_n = jnp.where(i < nt - 1, dgn_ref[...] * bn_ref[...].astype(F32), 0.0)
        d1 = _shift_up(dconv, dconv_n, 1)
        d2 = _shift_up(dconv, dconv_n, 2)
        dv = w_ref[2:3, :] * dconv + w_ref[1:2, :] * d1 + w_ref[0:1, :] * d2
        dc_ref[...] = (dv * u).astype(BF16)
        du_ref[...] = (dv * c).astype(BF16)
        row8 = lax.broadcasted_iota(jnp.int32, (8, tc), 0)
        dw = jnp.where(row8 == 0, jnp.sum(dconv * v2, axis=0, keepdims=True),
                       jnp.where(row8 == 1, jnp.sum(dconv * v1, axis=0, keepdims=True),
                                 jnp.where(row8 == 2, jnp.sum(dconv * v, axis=0, keepdims=True), 0.0)))

        @pl.when(i == 0)
        def _():
            dw_ref[...] = dw

        @pl.when(i > 0)
        def _():
            dw_ref[...] += dw

    def prev(i):
        return jnp.maximum(i * hb - 1, 0)

    def nxt(i):
        return jnp.minimum((i + 1) * hb, last_h)

    cur = lambda off: pl.BlockSpec((tr, tc), lambda j, i: (i, off * nc + j))
    out_blk = pl.BlockSpec((tr, tc), lambda j, i: (i, j))
    db, dc, du, dw = pl.pallas_call(
        body, name=name, grid=(nc, nt),
        in_specs=[cur(0), cur(1), cur(2),
                  pl.BlockSpec((HALO, tc), lambda j, i: (prev(i), nc + j)),
                  pl.BlockSpec((HALO, tc), lambda j, i: (prev(i), 2 * nc + j)),
                  pl.BlockSpec((HALO, tc), lambda j, i: (nxt(i), j)),
                  pl.BlockSpec((tr, tc), lambda j, i: (i, j)),
                  pl.BlockSpec((HALO, tc), lambda j, i: (nxt(i), j)),
                  pl.BlockSpec((3, tc), lambda j, i: (0, j))],
        out_specs=[out_blk, out_blk, out_blk, pl.BlockSpec((8, tc), lambda j, i: (0, j))],
        out_shape=[jax.ShapeDtypeStruct((R, D), BF16)] * 3 + [jax.ShapeDtypeStruct((8, D), F32)],
        compiler_params=_params(("arbitrary", "arbitrary")),
    )(bcu, bcu, bcu, bcu, bcu, bcu, dg, dg, conv_w)
    return db, dc, du, dw


def _swap32(x):
    w = x.shape[1]
    lane = lax.broadcasted_iota(jnp.int32, x.shape, 1)
    return jnp.where((lane & (HEAD_DIM - 1)) < HEAD_DIM // 2, pltpu.roll(x, w - HEAD_DIM // 2, axis=1),
                     pltpu.roll(x, HEAD_DIM // 2, axis=1))


def _rope(x, cos, sin):
    return x * cos + _swap32(x) * sin


def rope_fwd(qkv, cos, sin, *, n_q, n_kv, name):
    R, W = qkv.shape
    qw = n_q * HEAD_DIM
    kw = n_kv * HEAD_DIM
    tr = _tile(R, 320, 16)

    def expand(y, ref, c):
        lane = lax.broadcasted_iota(jnp.int32, y.shape, 1)
        lo = jnp.where(lane < HEAD_DIM, y, 0.0)
        hi = jnp.where(lane >= HEAD_DIM, y, 0.0)
        g0 = 2 * c
        ref[:, 256 * g0:256 * g0 + 128] = lo.astype(BF16)
        ref[:, 256 * g0 + 128:256 * g0 + 256] = pltpu.roll(lo, HEAD_DIM, axis=1).astype(BF16)
        ref[:, 256 * (g0 + 1):256 * (g0 + 1) + 128] = pltpu.roll(hi, HEAD_DIM, axis=1).astype(BF16)
        ref[:, 256 * (g0 + 1) + 128:256 * (g0 + 1) + 256] = hi.astype(BF16)

    def body(x_ref, c_ref, s_ref, q_ref, k_ref, v_ref):
        cos = c_ref[...]
        sin = s_ref[...]
        for c in range(qw // 128):
            x = x_ref[:, 128 * c:128 * (c + 1)]
            q_ref[:, 128 * c:128 * (c + 1)] = (_rope(x, cos, sin) * (HEAD_DIM ** -0.5)).astype(BF16)
        for c in range(kw // 128):
            expand(_rope(x_ref[:, qw + 128 * c:qw + 128 * (c + 1)], cos, sin), k_ref, c)
            expand(x_ref[:, qw + kw + 128 * c:qw + kw + 128 * (c + 1)], v_ref, c)

    row = lambda w: pl.BlockSpec((tr, w), lambda i: (i, 0))
    return pl.pallas_call(
        body, name=name, grid=(R // tr,),
        in_specs=[row(W), row(128), row(128)],
        out_specs=[row(qw), row(4 * kw), row(4 * kw)],
        out_shape=[jax.ShapeDtypeStruct((R, qw), BF16), jax.ShapeDtypeStruct((R, 4 * kw), BF16),
                   jax.ShapeDtypeStruct((R, 4 * kw), BF16)],
        compiler_params=_params(("parallel",)),
    )(qkv, cos, sin)


def rope_bwd(dq, dkx, dvx, cos, sin, *, n_q, n_kv, name):
    R = dq.shape[0]
    qw = n_q * HEAD_DIM
    kw = n_kv * HEAD_DIM
    W = qw + 2 * kw
    tr = _tile(R, 320, 16)

    def fold(ref, c):
        lane = lax.broadcasted_iota(jnp.int32, (tr, 128), 1)
        x0 = ref[:, 128 * (2 * c):128 * (2 * c + 1)]
        x1 = ref[:, 128 * (2 * c + 1):128 * (2 * c + 2)]
        f0 = x0 + pltpu.roll(x0, HEAD_DIM, axis=1)
        f1 = x1 + pltpu.roll(x1, HEAD_DIM, axis=1)
        return jnp.where(lane < HEAD_DIM, f0, f1)

    def body(dq_ref, dk_ref, dv_ref, c_ref, s_ref, o_ref):
        cos = c_ref[...]
        nsin = -s_ref[...]
        for c in range(qw // 128):
            y = dq_ref[:, 128 * c:128 * (c + 1)]
            o_ref[:, 128 * c:128 * (c + 1)] = (_rope(y, cos, nsin) * (HEAD_DIM ** -0.5)).astype(BF16)
        for c in range(kw // 128):
            o_ref[:, qw + 128 * c:qw + 128 * (c + 1)] = _rope(fold(dk_ref, c), cos, nsin).astype(BF16)
            o_ref[:, qw + kw + 128 * c:qw + kw + 128 * (c + 1)] = fold(dv_ref, c).astype(BF16)

    row = lambda w: pl.BlockSpec((tr, w), lambda i: (i, 0))
    return pl.pallas_call(
        body, name=name, grid=(R // tr,),
        in_specs=[row(qw), row(2 * kw), row(2 * kw), row(128), row(128)],
        out_specs=row(W),
        out_shape=jax.ShapeDtypeStruct((R, W), BF16),
        compiler_params=_params(("parallel",)),
    )(dq, dkx, dvx, cos, sin)


def _band_bias(n, pad):
    row = lax.broadcasted_iota(jnp.int32, (BLOCK, 2 * BLOCK), 0)
    col = lax.broadcasted_iota(jnp.int32, (BLOCK, 2 * BLOCK), 1)
    kmin = jnp.where(n == 0, BLOCK + pad, jnp.where(n == 1, pad, 0))
    allowed = (col > row) & (col <= row + BLOCK) & (col >= kmin)
    return jnp.where(allowed, 0.0, NEG_INF).astype(F32)


def _dot_nt(a, b):
    return lax.dot_general(a, b, (((1,), (1,)), ((), ())), preferred_element_type=F32)


def _dot_tn(a, b):
    return lax.dot_general(a, b, (((0,), (0,)), ((), ())), preferred_element_type=F32)


def _band(prev_ref, cur_ref, c0):
    return jnp.concatenate([prev_ref[:, c0:c0 + 128], cur_ref[:, c0:c0 + 128]], axis=0)


def attn_fwd(q, kx, vx, sinks, *, pad, name):
    R, qw = q.shape
    n_kv = kx.shape[1] // 256
    nb = R // BLOCK

    def body(s_ref, q_ref, kc_ref, kp_ref, vc_ref, vp_ref, o_ref, l_ref):
        n = pl.program_id(0)
        bias = _band_bias(n, pad)
        lane = lax.broadcasted_iota(jnp.int32, (BLOCK, 128), 1)

        def softmax(s, sink):
            m = jnp.maximum(jnp.max(s, axis=1, keepdims=True), sink)
            e = jnp.exp(s - m)
            den = jnp.sum(e, axis=1, keepdims=True) + jnp.exp(sink - m)
            return (e * (1.0 / den)).astype(BF16), m + jnp.log(den)

        for g in range(n_kv):
            k2a = _band(kp_ref, kc_ref, 256 * g)
            k2b = _band(kp_ref, kc_ref, 256 * g + 128)
            v2a = _band(vp_ref, vc_ref, 256 * g)
            v2b = _band(vp_ref, vc_ref, 256 * g + 128)
            for p in range(GROUP // 2):
                c0 = 128 * (g * (GROUP // 2) + p)
                he = GROUP * g + 2 * p
                q2 = q_ref[:, c0:c0 + 128]
                pe, le = softmax(_dot_nt(q2, k2a) + bias, s_ref[he])
                po, lo = softmax(_dot_nt(q2, k2b) + bias, s_ref[he + 1])
                o2 = jnp.dot(pe, v2a, preferred_element_type=F32) + jnp.dot(po, v2b, preferred_element_type=F32)
                o_ref[:, c0:c0 + 128] = o2.astype(BF16)
                l_ref[:, c0:c0 + 128] = jnp.where(lane < HEAD_DIM, le, lo)

    cur = lambda w: pl.BlockSpec((BLOCK, w), lambda n: (n, 0))
    prv = lambda w: pl.BlockSpec((BLOCK, w), lambda n: (jnp.maximum(n - 1, 0), 0))
    kxw = kx.shape[1]
    return pl.pallas_call(
        body, name=name, grid=(nb,),
        in_specs=[pl.BlockSpec(memory_space=pltpu.SMEM), cur(qw), cur(kxw), prv(kxw), cur(kxw), prv(kxw)],
        out_specs=[cur(qw), cur(qw)],
        out_shape=[jax.ShapeDtypeStruct((R, qw), BF16), jax.ShapeDtypeStruct((R, qw), F32)],
        compiler_params=_params(("parallel",)),
    )(sinks, q, kx, kx, vx, vx)


def attn_bwd(q, kx, vx, o, do, lse, sinks, *, pad, name):
    R, qw = q.shape
    n_kv = kx.shape[1] // 256
    nb = R // BLOCK
    kw2 = n_kv * 128

    def body(s_ref, q_ref, do_ref, o_ref, l_ref, kc_ref, kp_ref, vc_ref, vp_ref,
             dq_ref, dk_ref, dv_ref, ds_ref, cdk, cdv):
        n = pl.program_id(0)

        @pl.when(n == 0)
        def _():
            cdk[...] = jnp.zeros_like(cdk)
            cdv[...] = jnp.zeros_like(cdv)
            ds_ref[...] = jnp.zeros_like(ds_ref)

        @pl.when(n < nb)
        def _():
            bias = _band_bias(n, pad)
            lane = lax.broadcasted_iota(jnp.int32, (BLOCK, 128), 1)
            lane2 = lax.broadcasted_iota(jnp.int32, (2 * BLOCK, 128), 1)
            lane1 = lax.broadcasted_iota(jnp.int32, (1, 128), 1)
            dsink = jnp.zeros((1, 128), F32)
            for g in range(n_kv):
                k2a = _band(kp_ref, kc_ref, 256 * g)
                k2b = _band(kp_ref, kc_ref, 256 * g + 128)
                v2a = _band(vp_ref, vc_ref, 256 * g)
                v2b = _band(vp_ref, vc_ref, 256 * g + 128)
                dka = jnp.zeros((2 * BLOCK, 128), F32)
                dkb = jnp.zeros((2 * BLOCK, 128), F32)
                dva = jnp.zeros((2 * BLOCK, 128), F32)
                dvb = jnp.zeros((2 * BLOCK, 128), F32)
                for p in range(GROUP // 2):
                    c0 = 128 * (g * (GROUP // 2) + p)
                    he = GROUP * g + 2 * p
                    q2 = q_ref[:, c0:c0 + 128]
                    do2 = do_ref[:, c0:c0 + 128]
                    l2 = l_ref[:, c0:c0 + 128]
                    prod = do2.astype(F32) * o_ref[:, c0:c0 + 128].astype(F32)
                    l2r = pltpu.roll(l2, HEAD_DIM, axis=1)
                    halves = ((k2a, v2a, lane < HEAD_DIM, he), (k2b, v2b, lane >= HEAD_DIM, he + 1))
                    dq2 = jnp.zeros((BLOCK, 128), F32)
                    outs = []
                    for (k2, v2, mine, h) in halves:
                        lse_t = jnp.where(mine, l2, l2r)
                        delta = jnp.sum(jnp.where(mine, prod, 0.0), axis=1, keepdims=True)
                        pr = jnp.exp(_dot_nt(q2, k2) + bias - jnp.concatenate([lse_t, lse_t], axis=1))
                        dp = _dot_nt(do2, v2)
                        dsc = (pr * (dp - delta)).astype(BF16)
                        prb = pr.astype(BF16)
                        dq2 = dq2 + jnp.dot(dsc, k2, preferred_element_type=F32)
                        outs.append((_dot_tn(dsc, q2), _dot_tn(prb, do2)))
                        psink = jnp.exp(s_ref[h] - lse_t)
                        col = jnp.sum(psink * delta, axis=0, keepdims=True)
                        dsink = dsink - jnp.where(lane1 == h, col, 0.0)
                    dq_ref[:, c0:c0 + 128] = dq2
                    dka = dka + outs[0][0]
                    dva = dva + outs[0][1]
                    dkb = dkb + outs[1][0]
                    dvb = dvb + outs[1][1]
                gc = pl.ds(128 * g, 128)
                dk2 = jnp.where(lane2 < HEAD_DIM, dka, dkb)
                dv2 = jnp.where(lane2 < HEAD_DIM, dva, dvb)
                dk_ref[:, gc] = cdk[:, gc] + dk2[0:BLOCK]
                dv_ref[:, gc] = cdv[:, gc] + dv2[0:BLOCK]
                cdk[:, gc] = dk2[BLOCK:2 * BLOCK]
                cdv[:, gc] = dv2[BLOCK:2 * BLOCK]
            ds_ref[0:1, :] += dsink

        @pl.when(n == nb)
        def _():
            dk_ref[...] = cdk[...]
            dv_ref[...] = cdv[...]

    cur = lambda w: pl.BlockSpec((BLOCK, w), lambda n: (jnp.minimum(n, nb - 1), 0))
    prv = lambda w: pl.BlockSpec((BLOCK, w), lambda n: (jnp.clip(n - 1, 0, nb - 1), 0))
    kxw = kx.shape[1]
    return pl.pallas_call(
        body, name=name, grid=(nb + 1,),
        in_specs=[pl.BlockSpec(memory_space=pltpu.SMEM), cur(qw), cur(qw), cur(qw), cur(qw),
                  cur(kxw), prv(kxw), cur(kxw), prv(kxw)],
        out_specs=[cur(qw), prv(kw2), prv(kw2), pl.BlockSpec((8, 128), lambda n: (0, 0))],
        out_shape=[jax.ShapeDtypeStruct((R, qw), F32), jax.ShapeDtypeStruct((R, kw2), F32),
                   jax.ShapeDtypeStruct((R, kw2), F32), jax.ShapeDtypeStruct((8, 128), F32)],
        scratch_shapes=[pltpu.VMEM((BLOCK, kw2), F32), pltpu.VMEM((BLOCK, kw2), F32)],
        compiler_params=_params(("arbitrary",)),
    )(sinks, q, do, o, lse, kx, kx, vx, vx)


def loss_bwd(h, g, target, *, name):
    R, D = h.shape
    nb = R // BLOCK

    def body(h_ref, g_ref, t_ref, loss_ref, dh_ref, dhb_ref, dg_ref):
        n = pl.program_id(0)
        xhat, rstd = _rms_parts(h_ref[...])
        gv = g_ref[...]
        diff = jnp.where(n > 0, xhat * gv - t_ref[...], 0.0)
        part = (0.5 / D) * jnp.sum(jnp.sum(diff * diff, axis=1, keepdims=True), axis=0, keepdims=True)
        dout = diff * (1.0 / D)
        dxh = dout * gv
        dh = rstd * (dxh - xhat * jnp.mean(dxh * xhat, axis=-1, keepdims=True))
        dh_ref[...] = dh
        dhb_ref[...] = dh.astype(BF16)
        dg = jnp.sum(dout * xhat, axis=0, keepdims=True)

        @pl.when(n == 0)
        def _():
            loss_ref[...] = jnp.zeros_like(loss_ref) + part
            dg_ref[...] = dg

        @pl.when(n > 0)
        def _():
            loss_ref[...] += part
            dg_ref[...] += dg

    blk = pl.BlockSpec((BLOCK, D), lambda n: (n, 0))
    return pl.pallas_call(
        body, name=name, grid=(nb,),
        in_specs=[blk, pl.BlockSpec((1, D), lambda n: (0, 0)),
                  pl.BlockSpec((BLOCK, D), lambda n: (jnp.maximum(n - 1, 0), 0))],
        out_specs=[pl.BlockSpec((8, 128), lambda n: (0, 0)), blk, blk, pl.BlockSpec((1, D), lambda n: (0, 0))],
        out_shape=[jax.ShapeDtypeStruct((8, 128), F32), jax.ShapeDtypeStruct((R, D), F32),
                   jax.ShapeDtypeStruct((R, D), BF16), jax.ShapeDtypeStruct((1, D), F32)],
        compiler_params=_params(("arbitrary",)),
    )(h, g, target)


def _adam_math(w, g, m, v):
    m = ADAM_B1 * m + (1.0 - ADAM_B1) * g
    v = ADAM_B2 * v + (1.0 - ADAM_B2) * (g * g)
    m_hat = m / (1.0 - ADAM_B1 ** ADAM_STEP)
    v_hat = v / (1.0 - ADAM_B2 ** ADAM_STEP)
    delta = -ADAM_LR * (m_hat / (jnp.sqrt(v_hat) + ADAM_EPS) + ADAM_WD * w)
    return delta, m, v


def adam(w, m, v, g, *, name):
    r, C = w.shape
    tr = _tile(r, 128, 8)

    def body(w_ref, m_ref, v_ref, g_ref, d_ref, mo_ref, vo_ref):
        d_ref[...], mo_ref[...], vo_ref[...] = _adam_math(w_ref[...], g_ref[...], m_ref[...], v_ref[...])

    blk = pl.BlockSpec((tr, C), lambda i: (i, 0))
    return pl.pallas_call(
        body, name=name, grid=(r // tr,), in_specs=[blk] * 4, out_specs=[blk] * 3,
        out_shape=[jax.ShapeDtypeStruct((r, C), F32)] * 3,
        compiler_params=_params(("parallel",)),
    )(w, m, v, g)


def _sum_blocks(p_ref, own_ref, me):
    acc = None
    for s in range(N_DEV):
        blk = jnp.where(me == s, own_ref[...], p_ref[s]).astype(F32)
        acc = blk if acc is None else acc + blk
    return acc


def _parts_specs(r, C, tr):
    blk = pl.BlockSpec((tr, C), lambda i, me: (i, 0))
    parts = pl.BlockSpec((N_DEV, tr, C), lambda i, me: (0, i, 0))
    own = pl.BlockSpec((None, tr, C), lambda i, me: (me[0], i, 0))
    return blk, parts, own


def adam_parts(w, m, v, parts, own, me, *, name):
    r, C = w.shape
    tr = _tile(r, 128, 8)

    def body(me_ref, w_ref, m_ref, v_ref, p_ref, own_ref, go_ref, d_ref, mo_ref, vo_ref):
        gv = _sum_blocks(p_ref, own_ref, me_ref[0])
        go_ref[...] = gv
        d_ref[...], mo_ref[...], vo_ref[...] = _adam_math(w_ref[...], gv, m_ref[...], v_ref[...])

    blk, pblk, oblk = _parts_specs(r, C, tr)
    return pl.pallas_call(
        body, name=name,
        grid_spec=pltpu.PrefetchScalarGridSpec(num_scalar_prefetch=1, grid=(r // tr,),
                                               in_specs=[blk, blk, blk, pblk, oblk], out_specs=[blk] * 4),
        out_shape=[jax.ShapeDtypeStruct((r, C), F32)] * 4,
        compiler_params=_params(("parallel",)),
    )(me, w, m, v, parts, own)


def sum_parts(parts, own, me, *, name):
    _, r, C = parts.shape
    tr = _tile(r, 128, 8)

    def body(me_ref, p_ref, own_ref, o_ref):
        o_ref[...] = _sum_blocks(p_ref, own_ref, me_ref[0])

    blk, pblk, oblk = _parts_specs(r, C, tr)
    return pl.pallas_call(
        body, name=name,
        grid_spec=pltpu.PrefetchScalarGridSpec(num_scalar_prefetch=1, grid=(r // tr,),
                                               in_specs=[pblk, oblk], out_specs=blk),
        out_shape=jax.ShapeDtypeStruct((r, C), F32),
        compiler_params=_params(("parallel",)),
    )(me, parts, own)


def cast_place(w, me, *, name):
    r, C = w.shape
    tr = _tile(r, 256, 16)

    def body(me_ref, w_ref, s_ref, l_ref):
        v = w_ref[...].astype(BF16)
        s_ref[...] = v
        l_ref[...] = v

    blk = pl.BlockSpec((tr, C), lambda i, me: (i, 0))
    return pl.pallas_call(
        body, name=name,
        grid_spec=pltpu.PrefetchScalarGridSpec(
            num_scalar_prefetch=1, grid=(r // tr,), in_specs=[blk],
            out_specs=[blk, pl.BlockSpec((None, tr, C), lambda i, me: (me[0], i, 0))]),
        out_shape=[jax.ShapeDtypeStruct((r, C), BF16), jax.ShapeDtypeStruct((N_DEV, r, C), BF16)],
        compiler_params=_params(("parallel",)),
    )(me, w)


def _coords():
    return lax.axis_index("x"), lax.axis_index("y"), lax.axis_index("c")


def _peer(m):
    x, y, c = _coords()
    px = 1 - x if m & 4 else x
    py = 1 - y if m & 2 else y
    pc = 1 - c if m & 1 else c
    return (px, py, pc), 4 * px + 2 * py + pc


def exchange(items, *, all_to_all, name):
    n = len(items)
    if all_to_all:
        out_shape = [jax.ShapeDtypeStruct(a.shape, a.dtype) for a in items]
    else:
        out_shape = [jax.ShapeDtypeStruct((N_DEV,) + a.shape, a.dtype) for a in items]

    def body(*refs):
        ins, outs = refs[:n], refs[n:2 * n]
        send_sems, recv_sems, local_sems = refs[2 * n:]
        x, y, c = _coords()
        me = 4 * x + 2 * y + c

        def src(i, idx):
            return ins[i].at[idx] if all_to_all else ins[i]

        local = [pltpu.make_async_copy(src(i, me), outs[i].at[me], local_sems.at[i]) for i in range(n)]
        for cp in local:
            cp.start()
        sends = []
        for m in range(1, N_DEV):
            peer, pidx = _peer(m)
            for i in range(n):
                k = i * (N_DEV - 1) + m - 1
                cp = pltpu.make_async_remote_copy(src_ref=src(i, pidx), dst_ref=outs[i].at[me],
                                                  send_sem=send_sems.at[k], recv_sem=recv_sems.at[k],
                                                  device_id=peer, device_id_type=MESH)
                cp.start()
                sends.append(cp)
        for m in range(1, N_DEV):
            peer, pidx = _peer(m)
            for i in range(n):
                k = i * (N_DEV - 1) + m - 1
                pltpu.make_async_remote_copy(src_ref=src(i, pidx), dst_ref=outs[i].at[pidx],
                                             send_sem=send_sems.at[k], recv_sem=recv_sems.at[k],
                                             device_id=peer, device_id_type=MESH).wait_recv()
        for cp in sends:
            cp.wait_send()
        for cp in local:
            cp.wait()

    any_spec = pl.BlockSpec(memory_space=pl.ANY)
    return pl.pallas_call(
        body, name=name, in_specs=[any_spec] * n, out_specs=[any_spec] * n, out_shape=out_shape,
        scratch_shapes=[pltpu.SemaphoreType.DMA((n * (N_DEV - 1),)), pltpu.SemaphoreType.DMA((n * (N_DEV - 1),)),
                        pltpu.SemaphoreType.DMA((n,))],
    )(*items)


HBM_SPEC = pl.BlockSpec(memory_space=pltpu.HBM)
SEM_SPEC = pl.BlockSpec(memory_space=pltpu.SEMAPHORE)
SPLIT_PARAMS = pltpu.CompilerParams(has_side_effects=pltpu.SideEffectType.DATAFLOW_SIDE_EFFECTING)


def _split_copies(src_ref, land_ref, send_sems, recv_sems, all_to_all):
    x, y, c = _coords()
    me = 4 * x + 2 * y + c
    copies = []
    for m in range(1, N_DEV):
        peer, pidx = _peer(m)
        copies.append(pltpu.make_async_remote_copy(
            src_ref=src_ref.at[pidx] if all_to_all else src_ref, dst_ref=land_ref.at[me],
            send_sem=send_sems.at[m - 1], recv_sem=recv_sems.at[m - 1], device_id=peer, device_id_type=MESH))
    return copies


def copy_start(items, lands=None, *, all_to_all, name):
    n = len(items)
    if lands is None:
        lands = [lax.empty(a.shape if all_to_all else (N_DEV,) + a.shape, a.dtype) for a in items]

    def body(*refs):
        srcs, lnds, outs = refs[:n], refs[n:2 * n], refs[2 * n:]
        for i in range(n):
            for cp in _split_copies(srcs[i], lnds[i], outs[4 * i], outs[4 * i + 1], all_to_all):
                cp.start()
        outs[4 * n][...] = jnp.zeros((8, 128), F32)

    out_shape, out_specs, aliases = [], [], {}
    for i, (a, l) in enumerate(zip(items, lands)):
        out_shape += [pltpu.SemaphoreType.DMA((N_DEV - 1,)), pltpu.SemaphoreType.DMA((N_DEV - 1,)),
                      pltpu.HBM(a.shape, a.dtype), pltpu.HBM(l.shape, l.dtype)]
        out_specs += [SEM_SPEC, SEM_SPEC, HBM_SPEC, HBM_SPEC]
        aliases[i] = 4 * i + 2
        aliases[n + i] = 4 * i + 3
    out_shape.append(jax.ShapeDtypeStruct((8, 128), F32))
    out_specs.append(pl.BlockSpec(memory_space=pltpu.VMEM))
    hbm = lambda a: pltpu.with_memory_space_constraint(a, pltpu.HBM)
    res = pl.pallas_call(
        body, name=name, in_specs=[HBM_SPEC] * (2 * n), out_specs=out_specs, out_shape=out_shape,
        input_output_aliases=aliases, compiler_params=SPLIT_PARAMS,
    )(*[hbm(a) for a in items], *[hbm(l) for l in lands])
    return [tuple(res[4 * i:4 * i + 4]) for i in range(n)], res[4 * n]


def copy_wait(handle, after, *, all_to_all, name):
    send_sems, recv_sems, src, land = handle

    def body(src_ref, land_ref, send_ref, recv_ref, after_ref, src_out, got_ref):
        for cp in _split_copies(src_ref, land_ref, send_ref, recv_ref, all_to_all):
            cp.wait_send()
            cp.wait_recv()

    return pl.pallas_call(
        body, name=name, in_specs=[HBM_SPEC, HBM_SPEC, SEM_SPEC, SEM_SPEC, ANY_SPEC],
        out_specs=[HBM_SPEC, HBM_SPEC],
        out_shape=[pltpu.HBM(src.shape, src.dtype), pltpu.HBM(land.shape, land.dtype)],
        input_output_aliases={0: 0, 1: 1}, compiler_params=SPLIT_PARAMS,
    )(src, land, send_sems, recv_sems, after)


def kernel(x, meta_tokens, norm_mix_0, w_in_conv, conv_w, w_out_conv, norm_mlp_0, w_up_0, w_down_0, norm_mix_1, w_qkv, attn_sinks, w_o, norm_mlp_1, w_up_1, w_down_1, norm_final, loss_target, m_meta_tokens, m_norm_mix_0, m_w_in_conv, m_conv_w, m_w_out_conv, m_norm_mlp_0, m_w_up_0, m_w_down_0, m_norm_mix_1, m_w_qkv, m_attn_sinks, m_w_o, m_norm_mlp_1, m_w_up_1, m_w_down_1, m_norm_final, v_meta_tokens, v_norm_mix_0, v_w_in_conv, v_conv_w, v_w_out_conv, v_norm_mlp_0, v_w_up_0, v_w_down_0, v_norm_mix_1, v_w_qkv, v_attn_sinks, v_w_o, v_norm_mlp_1, v_w_up_1, v_w_down_1, v_norm_final):
    L, D = x.shape[1], x.shape[2]
    n_meta = meta_tokens.shape[0]
    pad = BLOCK - n_meta
    R = BLOCK + L
    n_q = D // HEAD_DIM
    n_kv = n_q // GROUP
    assert n_kv % 2 == 0 and L % BLOCK == 0 and D % 128 == 0
    x = x.reshape(L, D)
    target = loss_target.reshape(L, D)
    x_id, y_id, c_id = _coords()
    me = 4 * x_id + 2 * y_id + c_id

    col_names = ("in", "up0", "qkv", "up1")
    col_w = dict(zip(col_names, (w_in_conv, w_up_0, w_qkv, w_up_1)))
    row_names = ("out", "down0", "o", "down1")
    row_w = dict(zip(row_names, (w_out_conv, w_down_0, w_o, w_down_1)))
    me_arr = jnp.reshape(me, (1,)).astype(jnp.int32)
    natural = {k: col_w[k].T for k in col_names}
    natural.update(row_w)
    use_order = ("in", "out", "up0", "down0", "qkv", "o", "up1", "down1")
    placed = [cast_place(natural[k], me_arr, name="cast_" + k) for k in use_order]
    handles, token = copy_start([p[0] for p in placed], [p[1] for p in placed], all_to_all=False, name="gather_start")
    handles = dict(zip(use_order, handles))

    def weight(k, after):
        return copy_wait(handles[k], after, all_to_all=False, name="gather_wait_" + k)[1].reshape(-1, D)

    small_in = exchange([meta_tokens, conv_w], all_to_all=False, name="comm_gather")
    meta_full = jnp.transpose(small_in[0], (1, 0, 2)).reshape(n_meta, D)
    conv_full = jnp.transpose(small_in[1], (1, 0, 2)).reshape(conv_w.shape[0], D)

    vec = lambda a: a.reshape(1, D)
    pos = jnp.arange(R, dtype=F32) - pad
    inv = ROPE_THETA ** (-jnp.arange(0, HEAD_DIM, 2, dtype=F32) / HEAD_DIM)
    ang = pos[:, None] * inv[None, :]
    cos32, sin32 = jnp.cos(ang), jnp.sin(ang)
    cos = jnp.concatenate([cos32] * 4, axis=1)
    sin = jnp.concatenate([-sin32, sin32, -sin32, sin32], axis=1)

    W = {}
    h0 = jnp.concatenate([jnp.zeros((pad, D), F32), meta_full, x], axis=0)
    n0 = norm_fwd(h0, vec(norm_mix_0), token, name="norm0")
    W["in"] = weight("in", n0)
    bcu = mm_nt(n0, W["in"], name="in_proj", out_dtype=BF16)
    gated = conv_fwd(bcu, conv_full, name="conv_fwd")
    W["out"] = weight("out", gated)
    h1, n1 = mm_row(gated, W["out"], gated, name="out_proj", epi="resnorm", res=h0, g=vec(norm_mlp_0))
    W["up0"] = weight("up0", n1)
    a0 = mm_nt(n1, W["up0"], name="up0", out_dtype=BF16, epi="relu")
    W["down0"] = weight("down0", a0)
    h2, n2 = mm_row(a0, W["down0"], a0, name="down0", epi="resnorm", a_sq=True, res=h1, g=vec(norm_mix_1))
    W["qkv"] = weight("qkv", n2)
    qkv = mm_nt(n2, W["qkv"], name="qkv_proj", out_dtype=F32)
    q, kx, vx = rope_fwd(qkv, cos, sin, n_q=n_q, n_kv=n_kv, name="rope_fwd")
    o, lse = attn_fwd(q, kx, vx, attn_sinks, pad=pad, name="attn_fwd")
    W["o"] = weight("o", o)
    h3, n3 = mm_row(o, W["o"], o, name="o_proj", epi="resnorm", res=h2, g=vec(norm_mlp_1))
    W["up1"] = weight("up1", n3)
    a1 = mm_nt(n3, W["up1"], name="up1", out_dtype=BF16, epi="relu")
    W["down1"] = weight("down1", a1)
    h4 = mm_row(a1, W["down1"], a1, name="down1", epi="res", a_sq=True, res=h3)

    loss_part, dh4, dh4b, dg_final = loss_bwd(h4, vec(norm_final), target, name="loss_bwd")
    loss = lax.psum(loss_part[0, 0], ("x", "y", "c"))

    sent = {}

    def scatter(k, dw):
        (sent[k],), tok = copy_start([dw.reshape(N_DEV, -1, D)], all_to_all=True, name="a2a_start_" + k)
        return tok

    t = scatter("down1", mm_tn(a1, dh4b, name="dw_down1", a_sq=True))
    dup1 = mm_nt(dh4b, W["down1"], name="d_down1", out_dtype=BF16, epi="mul2a", extra=a1, dep=t)
    t = scatter("up1", mm_tn(dup1, n3, name="dw_up1"))
    dh3, dh3b, dg_mlp1 = mm_row(dup1, W["up1"], t, name="d_up1", epi="normbwd", h=h3, g=vec(norm_mlp_1), dres=dh4)

    t = scatter("o", mm_tn(o, dh3b, name="dw_o"))
    do = mm_nt(dh3b, W["o"], name="d_o", out_dtype=BF16, dep=t)
    dq, dkx, dvx, dsinks = attn_bwd(q, kx, vx, o, do, lse, attn_sinks, pad=pad, name="attn_bwd")
    dqkv = rope_bwd(dq, dkx, dvx, cos, sin, n_q=n_q, n_kv=n_kv, name="rope_bwd")
    t = scatter("qkv", mm_tn(dqkv, n2, name="dw_qkv"))
    dh2, dh2b, dg_mix1 = mm_row(dqkv, W["qkv"], t, name="d_qkv", epi="normbwd", h=h2, g=vec(norm_mix_1), dres=dh3)

    t = scatter("down0", mm_tn(a0, dh2b, name="dw_down0", a_sq=True))
    dup0 = mm_nt(dh2b, W["down0"], name="d_down0", out_dtype=BF16, epi="mul2a", extra=a0, dep=t)
    t = scatter("up0", mm_tn(dup0, n1, name="dw_up0"))
    dh1, dh1b, dg_mlp0 = mm_row(dup0, W["up0"], t, name="d_up0", epi="normbwd", h=h1, g=vec(norm_mlp_0), dres=dh2)

    t = scatter("out", mm_tn(gated, dh1b, name="dw_out"))
    dgated = mm_nt(dh1b, W["out"], name="d_out", out_dtype=F32, dep=t)
    db, dc, du, dconv = conv_bwd(bcu, conv_full, dgated, name="conv_bwd")
    dbcu = jnp.concatenate([db, dc, du], axis=1)
    t = scatter("in", mm_tn(dbcu, n0, name="dw_in"))
    dh0, _, dg_mix0 = mm_row(dbcu, W["in"], t, name="d_in", epi="normbwd", h=h0, g=vec(norm_mix_0), dres=dh1)
    grad_x = dh0[BLOCK:].reshape(1, L, D)

    recv = {}
    for k in ("down1", "up1", "o", "qkv", "down0", "up0", "out", "in"):
        sent[k], recv[k] = copy_wait(sent[k], dh0, all_to_all=True, name="a2a_wait_" + k)

    n_sink = attn_sinks.shape[0]
    slab = jnp.concatenate([
        dg_mix0, dg_mlp0, dg_mix1, dg_mlp1, dg_final,
        jnp.pad(dsinks[0:1, :n_sink], ((0, 0), (0, D - n_sink))), jnp.zeros((2, D), F32),
        dconv, dh0[pad:BLOCK]], axis=0)
    slabs = exchange([slab], all_to_all=False, name="comm_small")[0]
    small = sum_parts(slabs, slabs, me_arr, name="sum_small")
    cols = D // N_DEV
    my_cols = lambda a: lax.dynamic_slice_in_dim(a, me * cols, cols, axis=1)

    grads, deltas, new_m, new_v = {}, {}, {}, {}

    def update(key, w, m, v, g, shape):
        s2 = (1, -1) if w.ndim == 1 else w.shape
        if g.ndim == 3:
            g_, d_, m_, v_ = adam_parts(w, m, v, g, sent[key_of[key]], me_arr, name="adam_" + key)
        else:
            g_ = g.reshape(s2)
            d_, m_, v_ = adam(w.reshape(s2), m.reshape(s2), v.reshape(s2), g_, name="adam_" + key)
        grads[key], deltas[key], new_m[key], new_v[key] = (t.reshape(shape) for t in (g_, d_, m_, v_))

    update("meta_tokens", meta_tokens, m_meta_tokens, v_meta_tokens, my_cols(small[16:16 + n_meta]), meta_tokens.shape)
    update("norm_mix_0", norm_mix_0, m_norm_mix_0, v_norm_mix_0, small[0], (D,))
    update("conv_w", conv_w, m_conv_w, v_conv_w, my_cols(small[8:8 + conv_w.shape[0]]), conv_w.shape)
    update("norm_mlp_0", norm_mlp_0, m_norm_mlp_0, v_norm_mlp_0, small[1], (D,))
    update("norm_mix_1", norm_mix_1, m_norm_mix_1, v_norm_mix_1, small[2], (D,))
    update("attn_sinks", attn_sinks, m_attn_sinks, v_attn_sinks, small[5, :n_sink], (n_sink,))
    update("norm_mlp_1", norm_mlp_1, m_norm_mlp_1, v_norm_mlp_1, small[3], (D,))
    update("norm_final", norm_final, m_norm_final, v_norm_final, small[4], (D,))
    big = {"in": ("w_in_conv", w_in_conv, m_w_in_conv, v_w_in_conv), "up0": ("w_up_0", w_up_0, m_w_up_0, v_w_up_0),
           "qkv": ("w_qkv", w_qkv, m_w_qkv, v_w_qkv), "up1": ("w_up_1", w_up_1, m_w_up_1, v_w_up_1),
           "out": ("w_out_conv", w_out_conv, m_w_out_conv, v_w_out_conv),
           "down0": ("w_down_0", w_down_0, m_w_down_0, v_w_down_0), "o": ("w_o", w_o, m_w_o, v_w_o),
           "down1": ("w_down_1", w_down_1, m_w_down_1, v_w_down_1)}
    key_of = {big[k][0]: k for k in big}
    for k in col_names:
        key, w, m, v = big[k]
        update(key, w, m, v, sum_parts(recv[k], sent[k], me_arr, name="sum_" + k).T, w.shape)
    for k in row_names:
        key, w, m, v = big[k]
        update(key, w, m, v, recv[k], w.shape)

    order = ("meta_tokens", "norm_mix_0", "w_in_conv", "conv_w", "w_out_conv", "norm_mlp_0", "w_up_0", "w_down_0",
             "norm_mix_1", "w_qkv", "attn_sinks", "w_o", "norm_mlp_1", "w_up_1", "w_down_1", "norm_final")
    return (loss, grad_x, *[grads[k] for k in order], *[deltas[k] for k in order],
            *[new_m[k] for k in order], *[new_v[k] for k in order])
```

```python
import functools

import jax
import jax.numpy as jnp
from jax import lax
from jax.experimental import pallas as pl
from jax.experimental.pallas import tpu as pltpu

F32 = jnp.float32
BF16 = jnp.bfloat16

HEAD_DIM = 64
GROUP = 8
BLOCK = 128
N_DEV = 8
RMS_EPS = 1e-5
NEG_INF = -1e30
ROPE_THETA = 10000.0
ADAM_LR = 0.001
ADAM_B1 = 0.9
ADAM_B2 = 0.999
ADAM_EPS = 1e-08
ADAM_WD = 0.01
ADAM_STEP = 10
VMEM_LIMIT = 60 * 1024 * 1024
MESH = pl.DeviceIdType.MESH


def _tile(n, target, mult):
    best = None
    for t in range(mult, min(n, target) + 1, mult):
        if n % t == 0:
            best = t
    return best if best is not None else n


def _params(sem):
    return pltpu.CompilerParams(dimension_semantics=sem, vmem_limit_bytes=VMEM_LIMIT)


ANY_SPEC = pl.BlockSpec(memory_space=pl.ANY)


LEFT_BLOCK_BYTES = 14 * 1024 * 1024


def mm(a, b, *, name, out_dtype, b_rows_are_n, epi=None, extra=None, a_sq=False, dep=None):
    M, K = a.shape
    N = b.shape[0] if b_rows_are_n else b.shape[1]
    tm = _tile(M, LEFT_BLOCK_BYTES // (2 * K), 16)
    tn = _tile(N, 512 if K <= 4096 else 256, 128)

    def body(*refs):
        a_ref, b_ref, e_ref, o_ref = refs[0], refs[1], refs[2], refs[-1]
        av = a_ref[...]
        if a_sq:
            av = av.astype(F32)
            av = (av * av).astype(BF16)
        dims = (((1,), (1,)), ((), ())) if b_rows_are_n else (((1,), (0,)), ((), ()))
        acc = lax.dot_general(av, b_ref[...], dims, preferred_element_type=F32)
        if epi == "relu":
            acc = jnp.maximum(acc, 0.0)
        elif epi == "mul2a":
            acc = acc * (2.0 * e_ref[...].astype(F32))
        elif epi == "add":
            acc = acc + e_ref[...]
        o_ref[...] = acc.astype(o_ref.dtype)

    b_spec = pl.BlockSpec((tn, K), lambda i, j: (j, 0)) if b_rows_are_n else pl.BlockSpec((K, tn), lambda i, j: (0, j))
    in_specs = [pl.BlockSpec((tm, K), lambda i, j: (i, 0)), b_spec]
    args = [a, b]
    if extra is not None:
        in_specs.append(pl.BlockSpec((tm, tn), lambda i, j: (i, j)))
        args.append(extra)
    if dep is not None:
        in_specs.append(ANY_SPEC)
        args.append(dep)
    return pl.pallas_call(
        body, name=name, grid=(M // tm, N // tn), in_specs=in_specs,
        out_specs=pl.BlockSpec((tm, tn), lambda i, j: (i, j)),
        out_shape=jax.ShapeDtypeStruct((M, N), out_dtype),
        compiler_params=_params(("parallel", "parallel")),
    )(*args)


def _rms_parts(h):
    rstd = lax.rsqrt(jnp.mean(h * h, axis=-1, keepdims=True) + RMS_EPS)
    return h * rstd, rstd


def mm_tn(a, b, *, name, a_sq=False):
    T, M = a.shape
    N = b.shape[1]
    tm = _tile(M, 1024, 128)
    tk = _tile(T, 2080, 16)
    nk = T // tk

    def body(a_ref, b_ref, o_ref, acc_ref):
        k = pl.program_id(1)
        av = a_ref[...]
        if a_sq:
            av = av.astype(F32)
            av = (av * av).astype(BF16)
        part = lax.dot_general(av, b_ref[...], (((0,), (0,)), ((), ())), preferred_element_type=F32)

        @pl.when(k == 0)
        def _():
            acc_ref[...] = part

        @pl.when(k > 0)
        def _():
            acc_ref[...] += part

        @pl.when(k == nk - 1)
        def _():
            o_ref[...] = acc_ref[...].astype(BF16)

    return pl.pallas_call(
        body, name=name, grid=(M // tm, nk),
        in_specs=[pl.BlockSpec((tk, tm), lambda i, k: (k, i)), pl.BlockSpec((tk, N), lambda i, k: (k, 0))],
        out_specs=pl.BlockSpec((tm, N), lambda i, k: (i, 0)),
        out_shape=jax.ShapeDtypeStruct((M, N), BF16),
        scratch_shapes=[pltpu.VMEM((tm, N), F32)],
        compiler_params=_params(("parallel", "arbitrary")),
    )(a, b)


def norm_fwd(h, g, dep, *, name):
    R, D = h.shape
    tr = _tile(R, 320, 16)

    def body(h_ref, g_ref, dep_ref, n_ref):
        xhat, _ = _rms_parts(h_ref[...])
        n_ref[...] = (xhat * g_ref[...]).astype(BF16)

    return pl.pallas_call(
        body, name=name, grid=(R // tr,),
        in_specs=[pl.BlockSpec((tr, D), lambda i: (i, 0)), pl.BlockSpec((1, D), lambda i: (0, 0)), ANY_SPEC],
        out_specs=pl.BlockSpec((tr, D), lambda i: (i, 0)),
        out_shape=jax.ShapeDtypeStruct((R, D), BF16),
        compiler_params=_params(("parallel",)),
    )(h, g, dep)


def norm_bwd(dn, h, g, dres, *, name):
    R, D = h.shape
    tr = _tile(R, 320, 16)

    def body(dn_ref, h_ref, g_ref, dres_ref, dh_ref, dhb_ref, dg_ref):
        i = pl.program_id(0)
        dn = dn_ref[...]
        xhat, rstd = _rms_parts(h_ref[...])
        dxh = dn * g_ref[...]
        dh = dres_ref[...] + rstd * (dxh - xhat * jnp.mean(dxh * xhat, axis=-1, keepdims=True))
        dh_ref[...] = dh
        dhb_ref[...] = dh.astype(BF16)
        dg = jnp.sum(dn * xhat, axis=0, keepdims=True)

        @pl.when(i == 0)
        def _():
            dg_ref[...] = dg

        @pl.when(i > 0)
        def _():
            dg_ref[...] += dg

    row = pl.BlockSpec((tr, D), lambda i: (i, 0))
    vec = pl.BlockSpec((1, D), lambda i: (0, 0))
    return pl.pallas_call(
        body, name=name, grid=(R // tr,), in_specs=[row, row, vec, row], out_specs=[row, row, vec],
        out_shape=[jax.ShapeDtypeStruct((R, D), F32), jax.ShapeDtypeStruct((R, D), BF16),
                   jax.ShapeDtypeStruct((1, D), F32)],
        compiler_params=_params(("arbitrary",)),
    )(dn, h, g, dres)


def _shift_down(v, halo, s):
    n = v.shape[0]
    row = lax.broadcasted_iota(jnp.int32, v.shape, 0)
    out = pltpu.roll(v, s, axis=0)
    hn = halo.shape[0]
    for r in range(s):
        out = jnp.where(row == r, halo[hn - s + r:hn - s + r + 1, :], out)
    return out


def _shift_up(v, halo, s):
    n = v.shape[0]
    row = lax.broadcasted_iota(jnp.int32, v.shape, 0)
    out = pltpu.roll(v, n - s, axis=0)
    for r in range(s):
        out = jnp.where(row == n - s + r, halo[r:r + 1, :], out)
    return out


HALO = 16


def conv_fwd(bcu, conv_w, *, name):
    R, D3 = bcu.shape
    D = D3 // 3
    tr = _tile(R, 320, 16)
    tc = _tile(D, 512, 128)
    nc = D // tc
    hb = tr // HALO

    def body(b_ref, c_ref, u_ref, ch_ref, uh_ref, w_ref, o_ref):
        i = pl.program_id(0)
        v = c_ref[...].astype(F32) * u_ref[...].astype(F32)
        vh = ch_ref[...].astype(F32) * uh_ref[...].astype(F32)
        vh = jnp.where(i > 0, vh, 0.0)
        v1 = _shift_down(v, vh, 1)
        v2 = _shift_down(v, vh, 2)
        conv = w_ref[0:1, :] * v2 + w_ref[1:2, :] * v1 + w_ref[2:3, :] * v
        o_ref[...] = (b_ref[...].astype(F32) * conv).astype(BF16)

    def prev(i):
        return jnp.maximum(i * hb - 1, 0)

    return pl.pallas_call(
        body, name=name, grid=(R // tr, nc),
        in_specs=[pl.BlockSpec((tr, tc), lambda i, j: (i, j)),
                  pl.BlockSpec((tr, tc), lambda i, j: (i, nc + j)),
                  pl.BlockSpec((tr, tc), lambda i, j: (i, 2 * nc + j)),
                  pl.BlockSpec((HALO, tc), lambda i, j: (prev(i), nc + j)),
                  pl.BlockSpec((HALO, tc), lambda i, j: (prev(i), 2 * nc + j)),
                  pl.BlockSpec((3, tc), lambda i, j: (0, j))],
        out_specs=pl.BlockSpec((tr, tc), lambda i, j: (i, j)),
        out_shape=jax.ShapeDtypeStruct((R, D), BF16),
        compiler_params=_params(("parallel", "parallel")),
    )(bcu, bcu, bcu, bcu, bcu, conv_w)


def conv_bwd(bcu, conv_w, dg, *, name):
    R, D3 = bcu.shape
    D = D3 // 3
    tr = _tile(R, 320, 16)
    tc = _tile(D, 512, 128)
    nc = D // tc
    hb = tr // HALO
    nt = R // tr
    last_h = R // HALO - 1

    def body(b_ref, c_ref, u_ref, ch_ref, uh_ref, bn_ref, dg_ref, dgn_ref, w_ref, db_ref, dc_ref, du_ref, dw_ref):
        i = pl.program_id(1)
        b = b_ref[...].astype(F32)
        c = c_ref[...].astype(F32)
        u = u_ref[...].astype(F32)
        v = c * u
        vh = jnp.where(i > 0, ch_ref[...].astype(F32) * uh_ref[...].astype(F32), 0.0)
        v1 = _shift_down(v, vh, 1)
        v2 = _shift_down(v, vh, 2)
        conv = w_ref[0:1, :] * v2 + w_ref[1:2, :] * v1 + w_ref[2:3, :] * v
        dgv = dg_ref[...]
        db_ref[...] = (dgv * conv).astype(BF16)
        dconv = dgv * b
        dconv_n = jnp.where(i < nt - 1, dgn_ref[...] * bn_ref[...].astype(F32), 0.0)
        d1 = _shift_up(dconv, dconv_n, 1)
        d2 = _shift_up(dconv, dconv_n, 2)
        dv = w_ref[2:3, :] * dconv + w_ref[1:2, :] * d1 + w_ref[0:1, :] * d2
        dc_ref[...] = (dv * u).astype(BF16)
        du_ref[...] = (dv * c).astype(BF16)
        row8 = lax.broadcasted_iota(jnp.int32, (8, tc), 0)
        dw = jnp.where(row8 == 0, jnp.sum(dconv * v2, axis=0, keepdims=True),
                       jnp.where(row8 == 1, jnp.sum(dconv * v1, axis=0, keepdims=True),
                                 jnp.where(row8 == 2, jnp.sum(dconv * v, axis=0, keepdims=True), 0.0)))

        @pl.when(i == 0)
        def _():
            dw_ref[...] = dw

        @pl.when(i > 0)
        def _():
            dw_ref[...] += dw

    def prev(i):
        return jnp.maximum(i * hb - 1, 0)

    def nxt(i):
        return jnp.minimum((i + 1) * hb, last_h)

    cur = lambda off: pl.BlockSpec((tr, tc), lambda j, i: (i, off * nc + j))
    out_blk = pl.BlockSpec((tr, tc), lambda j, i: (i, j))
    db, dc, du, dw = pl.pallas_call(
        body, name=name, grid=(nc, nt),
        in_specs=[cur(0), cur(1), cur(2),
                  pl.BlockSpec((HALO, tc), lambda j, i: (prev(i), nc + j)),
                  pl.BlockSpec((HALO, tc), lambda j, i: (prev(i), 2 * nc + j)),
                  pl.BlockSpec((HALO, tc), lambda j, i: (nxt(i), j)),
                  pl.BlockSpec((tr, tc), lambda j, i: (i, j)),
                  pl.BlockSpec((HALO, tc), lambda j, i: (nxt(i), j)),
                  pl.BlockSpec((3, tc), lambda j, i: (0, j))],
        out_specs=[out_blk, out_blk, out_blk, pl.BlockSpec((8, tc), lambda j, i: (0, j))],
        out_shape=[jax.ShapeDtypeStruct((R, D), BF16)] * 3 + [jax.ShapeDtypeStruct((8, D), F32)],
        compiler_params=_params(("arbitrary", "arbitrary")),
    )(bcu, bcu, bcu, bcu, bcu, bcu, dg, dg, conv_w)
    return db, dc, du, dw


def _swap32(x):
    w = x.shape[1]
    lane = lax.broadcasted_iota(jnp.int32, x.shape, 1)
    return jnp.where((lane & (HEAD_DIM - 1)) < HEAD_DIM // 2, pltpu.roll(x, w - HEAD_DIM // 2, axis=1),
                     pltpu.roll(x, HEAD_DIM // 2, axis=1))


def _rope(x, cos, sin):
    return x * cos + _swap32(x) * sin


def rope_fwd(qkv, cos, sin, *, n_q, n_kv, name):
    R, W = qkv.shape
    qw = n_q * HEAD_DIM
    kw = n_kv * HEAD_DIM
    tr = _tile(R, 320, 16)

    def expand(y, ref, c):
        lane = lax.broadcasted_iota(jnp.int32, y.shape, 1)
        lo = jnp.where(lane < HEAD_DIM, y, 0.0)
        hi = jnp.where(lane >= HEAD_DIM, y, 0.0)
        g0 = 2 * c
        ref[:, 256 * g0:256 * g0 + 128] = lo.astype(BF16)
        ref[:, 256 * g0 + 128:256 * g0 + 256] = pltpu.roll(lo, HEAD_DIM, axis=1).astype(BF16)
        ref[:, 256 * (g0 + 1):256 * (g0 + 1) + 128] = pltpu.roll(hi, HEAD_DIM, axis=1).astype(BF16)
        ref[:, 256 * (g0 + 1) + 128:256 * (g0 + 1) + 256] = hi.astype(BF16)

    def body(x_ref, c_ref, s_ref, q_ref, k_ref, v_ref):
        cos = c_ref[...]
        sin = s_ref[...]
        for c in range(qw // 128):
            x = x_ref[:, 128 * c:128 * (c + 1)]
            q_ref[:, 128 * c:128 * (c + 1)] = (_rope(x, cos, sin) * (HEAD_DIM ** -0.5)).astype(BF16)
        for c in range(kw // 128):
            expand(_rope(x_ref[:, qw + 128 * c:qw + 128 * (c + 1)], cos, sin), k_ref, c)
            expand(x_ref[:, qw + kw + 128 * c:qw + kw + 128 * (c + 1)], v_ref, c)

    row = lambda w: pl.BlockSpec((tr, w), lambda i: (i, 0))
    return pl.pallas_call(
        body, name=name, grid=(R // tr,),
        in_specs=[row(W), row(128), row(128)],
        out_specs=[row(qw), row(4 * kw), row(4 * kw)],
        out_shape=[jax.ShapeDtypeStruct((R, qw), BF16), jax.ShapeDtypeStruct((R, 4 * kw), BF16),
                   jax.ShapeDtypeStruct((R, 4 * kw), BF16)],
        compiler_params=_params(("parallel",)),
    )(qkv, cos, sin)


def rope_bwd(dq, dkx, dvx, cos, sin, *, n_q, n_kv, name):
    R = dq.shape[0]
    qw = n_q * HEAD_DIM
    kw = n_kv * HEAD_DIM
    W = qw + 2 * kw
    tr = _tile(R, 320, 16)

    def fold(ref, c):
        lane = lax.broadcasted_iota(jnp.int32, (tr, 128), 1)
        x0 = ref[:, 128 * (2 * c):128 * (2 * c + 1)]
        x1 = ref[:, 128 * (2 * c + 1):128 * (2 * c + 2)]
        f0 = x0 + pltpu.roll(x0, HEAD_DIM, axis=1)
        f1 = x1 + pltpu.roll(x1, HEAD_DIM, axis=1)
        return jnp.where(lane < HEAD_DIM, f0, f1)

    def body(dq_ref, dk_ref, dv_ref, c_ref, s_ref, o_ref):
        cos = c_ref[...]
        nsin = -s_ref[...]
        for c in range(qw // 128):
            y = dq_ref[:, 128 * c:128 * (c + 1)]
            o_ref[:, 128 * c:128 * (c + 1)] = (_rope(y, cos, nsin) * (HEAD_DIM ** -0.5)).astype(BF16)
        for c in range(kw // 128):
            o_ref[:, qw + 128 * c:qw + 128 * (c + 1)] = _rope(fold(dk_ref, c), cos, nsin).astype(BF16)
            o_ref[:, qw + kw + 128 * c:qw + kw + 128 * (c + 1)] = fold(dv_ref, c).astype(BF16)

    row = lambda w: pl.BlockSpec((tr, w), lambda i: (i, 0))
    return pl.pallas_call(
        body, name=name, grid=(R // tr,),
        in_specs=[row(qw), row(2 * kw), row(2 * kw), row(128), row(128)],
        out_specs=row(W),
        out_shape=jax.ShapeDtypeStruct((R, W), BF16),
        compiler_params=_params(("parallel",)),
    )(dq, dkx, dvx, cos, sin)


def _band_bias(n, pad):
    row = lax.broadcasted_iota(jnp.int32, (BLOCK, 2 * BLOCK), 0)
    col = lax.broadcasted_iota(jnp.int32, (BLOCK, 2 * BLOCK), 1)
    kmin = jnp.where(n == 0, BLOCK + pad, jnp.where(n == 1, pad, 0))
    allowed = (col > row) & (col <= row + BLOCK) & (col >= kmin)
    return jnp.where(allowed, 0.0, NEG_INF).astype(F32)


def _dot_nt(a, b):
    return lax.dot_general(a, b, (((1,), (1,)), ((), ())), preferred_element_type=F32)


def _dot_tn(a, b):
    return lax.dot_general(a, b, (((0,), (0,)), ((), ())), preferred_element_type=F32)


def _band(prev_ref, cur_ref, c0):
    return jnp.concatenate([prev_ref[:, c0:c0 + 128], cur_ref[:, c0:c0 + 128]], axis=0)


def attn_fwd(q, kx, vx, sinks, *, pad, name):
    R, qw = q.shape
    n_kv = kx.shape[1] // 256
    nb = R // BLOCK

    def body(s_ref, q_ref, kc_ref, kp_ref, vc_ref, vp_ref, o_ref, l_ref):
        n = pl.program_id(0)
        bias = _band_bias(n, pad)
        lane = lax.broadcasted_iota(jnp.int32, (BLOCK, 128), 1)

        def softmax(s, sink):
            m = jnp.maximum(jnp.max(s, axis=1, keepdims=True), sink)
            e = jnp.exp(s - m)
            den = jnp.sum(e, axis=1, keepdims=True) + jnp.exp(sink - m)
            return (e * (1.0 / den)).astype(BF16), m + jnp.log(den)

        for g in range(n_kv):
            k2a = _band(kp_ref, kc_ref, 256 * g)
            k2b = _band(kp_ref, kc_ref, 256 * g + 128)
            v2a = _band(vp_ref, vc_ref, 256 * g)
            v2b = _band(vp_ref, vc_ref, 256 * g + 128)
            for p in range(GROUP // 2):
                c0 = 128 * (g * (GROUP // 2) + p)
                he = GROUP * g + 2 * p
                q2 = q_ref[:, c0:c0 + 128]
                pe, le = softmax(_dot_nt(q2, k2a) + bias, s_ref[he])
                po, lo = softmax(_dot_nt(q2, k2b) + bias, s_ref[he + 1])
                o2 = jnp.dot(pe, v2a, preferred_element_type=F32) + jnp.dot(po, v2b, preferred_element_type=F32)
                o_ref[:, c0:c0 + 128] = o2.astype(BF16)
                l_ref[:, c0:c0 + 128] = jnp.where(lane < HEAD_DIM, le, lo)

    cur = lambda w: pl.BlockSpec((BLOCK, w), lambda n: (n, 0))
    prv = lambda w: pl.BlockSpec((BLOCK, w), lambda n: (jnp.maximum(n - 1, 0), 0))
    kxw = kx.shape[1]
    return pl.pallas_call(
        body, name=name, grid=(nb,),
        in_specs=[pl.BlockSpec(memory_space=pltpu.SMEM), cur(qw), cur(kxw), prv(kxw), cur(kxw), prv(kxw)],
        out_specs=[cur(qw), cur(qw)],
        out_shape=[jax.ShapeDtypeStruct((R, qw), BF16), jax.ShapeDtypeStruct((R, qw), F32)],
        compiler_params=_params(("parallel",)),
    )(sinks, q, kx, kx, vx, vx)


def attn_bwd(q, kx, vx, o, do, lse, sinks, *, pad, name):
    R, qw = q.shape
    n_kv = kx.shape[1] // 256
    nb = R // BLOCK
    kw2 = n_kv * 128

    def body(s_ref, q_ref, do_ref, o_ref, l_ref, kc_ref, kp_ref, vc_ref, vp_ref,
             dq_ref, dk_ref, dv_ref, ds_ref, cdk, cdv):
        n = pl.program_id(0)

        @pl.when(n == 0)
        def _():
            cdk[...] = jnp.zeros_like(cdk)
            cdv[...] = jnp.zeros_like(cdv)
            ds_ref[...] = jnp.zeros_like(ds_ref)

        @pl.when(n < nb)
        def _():
            bias = _band_bias(n, pad)
            lane = lax.broadcasted_iota(jnp.int32, (BLOCK, 128), 1)
            lane2 = lax.broadcasted_iota(jnp.int32, (2 * BLOCK, 128), 1)
            lane1 = lax.broadcasted_iota(jnp.int32, (1, 128), 1)
            dsink = jnp.zeros((1, 128), F32)
            for g in range(n_kv):
                k2a = _band(kp_ref, kc_ref, 256 * g)
                k2b = _band(kp_ref, kc_ref, 256 * g + 128)
                v2a = _band(vp_ref, vc_ref, 256 * g)
                v2b = _band(vp_ref, vc_ref, 256 * g + 128)
                dka = jnp.zeros((2 * BLOCK, 128), F32)
                dkb = jnp.zeros((2 * BLOCK, 128), F32)
                dva = jnp.zeros((2 * BLOCK, 128), F32)
                dvb = jnp.zeros((2 * BLOCK, 128), F32)
                for p in range(GROUP // 2):
                    c0 = 128 * (g * (GROUP // 2) + p)
                    he = GROUP * g + 2 * p
                    q2 = q_ref[:, c0:c0 + 128]
                    do2 = do_ref[:, c0:c0 + 128]
                    l2 = l_ref[:, c0:c0 + 128]
                    prod = do2.astype(F32) * o_ref[:, c0:c0 + 128].astype(F32)
                    l2r = pltpu.roll(l2, HEAD_DIM, axis=1)
                    halves = ((k2a, v2a, lane < HEAD_DIM, he), (k2b, v2b, lane >= HEAD_DIM, he + 1))
                    dq2 = jnp.zeros((BLOCK, 128), F32)
                    outs = []
                    for (k2, v2, mine, h) in halves:
                        lse_t = jnp.where(mine, l2, l2r)
                        delta = jnp.sum(jnp.where(mine, prod, 0.0), axis=1, keepdims=True)
                        pr = jnp.exp(_dot_nt(q2, k2) + bias - jnp.concatenate([lse_t, lse_t], axis=1))
                        dp = _dot_nt(do2, v2)
                        dsc = (pr * (dp - delta)).astype(BF16)
                        prb = pr.astype(BF16)
                        dq2 = dq2 + jnp.dot(dsc, k2, preferred_element_type=F32)
                        outs.append((_dot_tn(dsc, q2), _dot_tn(prb, do2)))
                        psink = jnp.exp(s_ref[h] - lse_t)
                        col = jnp.sum(psink * delta, axis=0, keepdims=True)
                        dsink = dsink - jnp.where(lane1 == h, col, 0.0)
                    dq_ref[:, c0:c0 + 128] = dq2
                    dka = dka + outs[0][0]
                    dva = dva + outs[0][1]
                    dkb = dkb + outs[1][0]
                    dvb = dvb + outs[1][1]
                gc = pl.ds(128 * g, 128)
                dk2 = jnp.where(lane2 < HEAD_DIM, dka, dkb)
                dv2 = jnp.where(lane2 < HEAD_DIM, dva, dvb)
                dk_ref[:, gc] = cdk[:, gc] + dk2[0:BLOCK]
                dv_ref[:, gc] = cdv[:, gc] + dv2[0:BLOCK]
                cdk[:, gc] = dk2[BLOCK:2 * BLOCK]
                cdv[:, gc] = dv2[BLOCK:2 * BLOCK]
            ds_ref[0:1, :] += dsink

        @pl.when(n == nb)
        def _():
            dk_ref[...] = cdk[...]
            dv_ref[...] = cdv[...]

    cur = lambda w: pl.BlockSpec((BLOCK, w), lambda n: (jnp.minimum(n, nb - 1), 0))
    prv = lambda w: pl.BlockSpec((BLOCK, w), lambda n: (jnp.clip(n - 1, 0, nb - 1), 0))
    kxw = kx.shape[1]
    return pl.pallas_call(
        body, name=name, grid=(nb + 1,),
        in_specs=[pl.BlockSpec(memory_space=pltpu.SMEM), cur(qw), cur(qw), cur(qw), cur(qw),
                  cur(kxw), prv(kxw), cur(kxw), prv(kxw)],
        out_specs=[cur(qw), prv(kw2), prv(kw2), pl.BlockSpec((8, 128), lambda n: (0, 0))],
        out_shape=[jax.ShapeDtypeStruct((R, qw), F32), jax.ShapeDtypeStruct((R, kw2), F32),
                   jax.ShapeDtypeStruct((R, kw2), F32), jax.ShapeDtypeStruct((8, 128), F32)],
        scratch_shapes=[pltpu.VMEM((BLOCK, kw2), F32), pltpu.VMEM((BLOCK, kw2), F32)],
        compiler_params=_params(("arbitrary",)),
    )(sinks, q, do, o, lse, kx, kx, vx, vx)


def loss_bwd(h, g, target, *, name):
    R, D = h.shape
    nb = R // BLOCK

    def body(h_ref, g_ref, t_ref, loss_ref, dh_ref, dhb_ref, dg_ref):
        n = pl.program_id(0)
        xhat, rstd = _rms_parts(h_ref[...])
        gv = g_ref[...]
        diff = jnp.where(n > 0, xhat * gv - t_ref[...], 0.0)
        part = (0.5 / D) * jnp.sum(jnp.sum(diff * diff, axis=1, keepdims=True), axis=0, keepdims=True)
        dout = diff * (1.0 / D)
        dxh = dout * gv
        dh = rstd * (dxh - xhat * jnp.mean(dxh * xhat, axis=-1, keepdims=True))
        dh_ref[...] = dh
        dhb_ref[...] = dh.astype(BF16)
        dg = jnp.sum(dout * xhat, axis=0, keepdims=True)

        @pl.when(n == 0)
        def _():
            loss_ref[...] = jnp.zeros_like(loss_ref) + part
            dg_ref[...] = dg

        @pl.when(n > 0)
        def _():
            loss_ref[...] += part
            dg_ref[...] += dg

    blk = pl.BlockSpec((BLOCK, D), lambda n: (n, 0))
    return pl.pallas_call(
        body, name=name, grid=(nb,),
        in_specs=[blk, pl.BlockSpec((1, D), lambda n: (0, 0)),
                  pl.BlockSpec((BLOCK, D), lambda n: (jnp.maximum(n - 1, 0), 0))],
        out_specs=[pl.BlockSpec((8, 128), lambda n: (0, 0)), blk, blk, pl.BlockSpec((1, D), lambda n: (0, 0))],
        out_shape=[jax.ShapeDtypeStruct((8, 128), F32), jax.ShapeDtypeStruct((R, D), F32),
                   jax.ShapeDtypeStruct((R, D), BF16), jax.ShapeDtypeStruct((1, D), F32)],
        compiler_params=_params(("arbitrary",)),
    )(h, g, target)


def _adam_math(w, g, m, v):
    m = ADAM_B1 * m + (1.0 - ADAM_B1) * g
    v = ADAM_B2 * v + (1.0 - ADAM_B2) * (g * g)
    m_hat = m / (1.0 - ADAM_B1 ** ADAM_STEP)
    v_hat = v / (1.0 - ADAM_B2 ** ADAM_STEP)
    delta = -ADAM_LR * (m_hat / (jnp.sqrt(v_hat) + ADAM_EPS) + ADAM_WD * w)
    return delta, m, v


def adam(w, m, v, g, *, name):
    r, C = w.shape
    tr = _tile(r, 128, 8)

    def body(w_ref, m_ref, v_ref, g_ref, d_ref, mo_ref, vo_ref):
        d_ref[...], mo_ref[...], vo_ref[...] = _adam_math(w_ref[...], g_ref[...], m_ref[...], v_ref[...])

    blk = pl.BlockSpec((tr, C), lambda i: (i, 0))
    return pl.pallas_call(
        body, name=name, grid=(r // tr,), in_specs=[blk] * 4, out_specs=[blk] * 3,
        out_shape=[jax.ShapeDtypeStruct((r, C), F32)] * 3,
        compiler_params=_params(("parallel",)),
    )(w, m, v, g)


def _sum_blocks(p_ref, own_ref, me):
    acc = None
    for s in range(N_DEV):
        blk = jnp.where(me == s, own_ref[...], p_ref[s]).astype(F32)
        acc = blk if acc is None else acc + blk
    return acc


def _parts_specs(r, C, tr):
    blk = pl.BlockSpec((tr, C), lambda i, me: (i, 0))
    parts = pl.BlockSpec((N_DEV, tr, C), lambda i, me: (0, i, 0))
    own = pl.BlockSpec((None, tr, C), lambda i, me: (me[0], i, 0))
    return blk, parts, own


def adam_parts(w, m, v, parts, own, me, *, name):
    r, C = w.shape
    tr = _tile(r, 128, 8)

    def body(me_ref, w_ref, m_ref, v_ref, p_ref, own_ref, go_ref, d_ref, mo_ref, vo_ref):
        gv = _sum_blocks(p_ref, own_ref, me_ref[0])
        go_ref[...] = gv
        d_ref[...], mo_ref[...], vo_ref[...] = _adam_math(w_ref[...], gv, m_ref[...], v_ref[...])

    blk, pblk, oblk = _parts_specs(r, C, tr)
    return pl.pallas_call(
        body, name=name,
        grid_spec=pltpu.PrefetchScalarGridSpec(num_scalar_prefetch=1, grid=(r // tr,),
                                               in_specs=[blk, blk, blk, pblk, oblk], out_specs=[blk] * 4),
        out_shape=[jax.ShapeDtypeStruct((r, C), F32)] * 4,
        compiler_params=_params(("parallel",)),
    )(me, w, m, v, parts, own)


def sum_parts(parts, own, me, *, name):
    _, r, C = parts.shape
    tr = _tile(r, 128, 8)

    def body(me_ref, p_ref, own_ref, o_ref):
        o_ref[...] = _sum_blocks(p_ref, own_ref, me_ref[0])

    blk, pblk, oblk = _parts_specs(r, C, tr)
    return pl.pallas_call(
        body, name=name,
        grid_spec=pltpu.PrefetchScalarGridSpec(num_scalar_prefetch=1, grid=(r // tr,),
                                               in_specs=[pblk, oblk], out_specs=blk),
        out_shape=jax.ShapeDtypeStruct((r, C), F32),
        compiler_params=_params(("parallel",)),
    )(me, parts, own)


def cast_place(w, me, *, name):
    r, C = w.shape
    tr = _tile(r, 256, 16)

    def body(me_ref, w_ref, s_ref, l_ref):
        v = w_ref[...].astype(BF16)
        s_ref[...] = v
        l_ref[...] = v

    blk = pl.BlockSpec((tr, C), lambda i, me: (i, 0))
    return pl.pallas_call(
        body, name=name,
        grid_spec=pltpu.PrefetchScalarGridSpec(
            num_scalar_prefetch=1, grid=(r // tr,), in_specs=[blk],
            out_specs=[blk, pl.BlockSpec((None, tr, C), lambda i, me: (me[0], i, 0))]),
        out_shape=[jax.ShapeDtypeStruct((r, C), BF16), jax.ShapeDtypeStruct((N_DEV, r, C), BF16)],
        compiler_params=_params(("parallel",)),
    )(me, w)


def _coords():
    return lax.axis_index("x"), lax.axis_index("y"), lax.axis_index("c")


def _peer(m):
    x, y, c = _coords()
    px = 1 - x if m & 4 else x
    py = 1 - y if m & 2 else y
    pc = 1 - c if m & 1 else c
    return (px, py, pc), 4 * px + 2 * py + pc


def exchange(items, *, all_to_all, name):
    n = len(items)
    if all_to_all:
        out_shape = [jax.ShapeDtypeStruct(a.shape, a.dtype) for a in items]
    else:
        out_shape = [jax.ShapeDtypeStruct((N_DEV,) + a.shape, a.dtype) for a in items]

    def body(*refs):
        ins, outs = refs[:n], refs[n:2 * n]
        send_sems, recv_sems, local_sems = refs[2 * n:]
        x, y, c = _coords()
        me = 4 * x + 2 * y + c

        def src(i, idx):
            return ins[i].at[idx] if all_to_all else ins[i]

        local = [pltpu.make_async_copy(src(i, me), outs[i].at[me], local_sems.at[i]) for i in range(n)]
        for cp in local:
            cp.start()
        sends = []
        for m in range(1, N_DEV):
            peer, pidx = _peer(m)
            for i in range(n):
                k = i * (N_DEV - 1) + m - 1
                cp = pltpu.make_async_remote_copy(src_ref=src(i, pidx), dst_ref=outs[i].at[me],
                                                  send_sem=send_sems.at[k], recv_sem=recv_sems.at[k],
                                                  device_id=peer, device_id_type=MESH)
                cp.start()
                sends.append(cp)
        for m in range(1, N_DEV):
            peer, pidx = _peer(m)
            for i in range(n):
                k = i * (N_DEV - 1) + m - 1
                pltpu.make_async_remote_copy(src_ref=src(i, pidx), dst_ref=outs[i].at[pidx],
                                             send_sem=send_sems.at[k], recv_sem=recv_sems.at[k],
                                             device_id=peer, device_id_type=MESH).wait_recv()
        for cp in sends:
            cp.wait_send()
        for cp in local:
            cp.wait()

    any_spec = pl.BlockSpec(memory_space=pl.ANY)
    return pl.pallas_call(
        body, name=name, in_specs=[any_spec] * n, out_specs=[any_spec] * n, out_shape=out_shape,
        scratch_shapes=[pltpu.SemaphoreType.DMA((n * (N_DEV - 1),)), pltpu.SemaphoreType.DMA((n * (N_DEV - 1),)),
                        pltpu.SemaphoreType.DMA((n,))],
    )(*items)


HBM_SPEC = pl.BlockSpec(memory_space=pltpu.HBM)
SEM_SPEC = pl.BlockSpec(memory_space=pltpu.SEMAPHORE)
SPLIT_PARAMS = pltpu.CompilerParams(has_side_effects=pltpu.SideEffectType.DATAFLOW_SIDE_EFFECTING)


def _split_copies(src_ref, land_ref, send_sems, recv_sems, all_to_all):
    x, y, c = _coords()
    me = 4 * x + 2 * y + c
    copies = []
    for m in range(1, N_DEV):
        peer, pidx = _peer(m)
        copies.append(pltpu.make_async_remote_copy(
            src_ref=src_ref.at[pidx] if all_to_all else src_ref, dst_ref=land_ref.at[me],
            send_sem=send_sems.at[m - 1], recv_sem=recv_sems.at[m - 1], device_id=peer, device_id_type=MESH))
    return copies


def copy_start(items, lands=None, *, all_to_all, name):
    n = len(items)
    if lands is None:
        lands = [lax.empty(a.shape if all_to_all else (N_DEV,) + a.shape, a.dtype) for a in items]

    def body(*refs):
        srcs, lnds, outs = refs[:n], refs[n:2 * n], refs[2 * n:]
        for i in range(n):
            for cp in _split_copies(srcs[i], lnds[i], outs[4 * i], outs[4 * i + 1], all_to_all):
                cp.start()
        outs[4 * n][...] = jnp.zeros((8, 128), F32)

    out_shape, out_specs, aliases = [], [], {}
    for i, (a, l) in enumerate(zip(items, lands)):
        out_shape += [pltpu.SemaphoreType.DMA((N_DEV - 1,)), pltpu.SemaphoreType.DMA((N_DEV - 1,)),
                      pltpu.HBM(a.shape, a.dtype), pltpu.HBM(l.shape, l.dtype)]
        out_specs += [SEM_SPEC, SEM_SPEC, HBM_SPEC, HBM_SPEC]
        aliases[i] = 4 * i + 2
        aliases[n + i] = 4 * i + 3
    out_shape.append(jax.ShapeDtypeStruct((8, 128), F32))
    out_specs.append(pl.BlockSpec(memory_space=pltpu.VMEM))
    hbm = lambda a: pltpu.with_memory_space_constraint(a, pltpu.HBM)
    res = pl.pallas_call(
        body, name=name, in_specs=[HBM_SPEC] * (2 * n), out_specs=out_specs, out_shape=out_shape,
        input_output_aliases=aliases, compiler_params=SPLIT_PARAMS,
    )(*[hbm(a) for a in items], *[hbm(l) for l in lands])
    return [tuple(res[4 * i:4 * i + 4]) for i in range(n)], res[4 * n]


def copy_wait(handle, after, *, all_to_all, name):
    send_sems, recv_sems, src, land = handle

    def body(src_ref, land_ref, send_ref, recv_ref, after_ref, src_out, got_ref):
        for cp in _split_copies(src_ref, land_ref, send_ref, recv_ref, all_to_all):
            cp.wait_send()
            cp.wait_recv()

    return pl.pallas_call(
        body, name=name, in_specs=[HBM_SPEC, HBM_SPEC, SEM_SPEC, SEM_SPEC, ANY_SPEC],
        out_specs=[HBM_SPEC, HBM_SPEC],
        out_shape=[pltpu.HBM(src.shape, src.dtype), pltpu.HBM(land.shape, land.dtype)],
        input_output_aliases={0: 0, 1: 1}, compiler_params=SPLIT_PARAMS,
    )(src, land, send_sems, recv_sems, after)


def kernel(x, meta_tokens, norm_mix_0, w_in_conv, conv_w, w_out_conv, norm_mlp_0, w_up_0, w_down_0, norm_mix_1, w_qkv, attn_sinks, w_o, norm_mlp_1, w_up_1, w_down_1, norm_final, loss_target, m_meta_tokens, m_norm_mix_0, m_w_in_conv, m_conv_w, m_w_out_conv, m_norm_mlp_0, m_w_up_0, m_w_down_0, m_norm_mix_1, m_w_qkv, m_attn_sinks, m_w_o, m_norm_mlp_1, m_w_up_1, m_w_down_1, m_norm_final, v_meta_tokens, v_norm_mix_0, v_w_in_conv, v_conv_w, v_w_out_conv, v_norm_mlp_0, v_w_up_0, v_w_down_0, v_norm_mix_1, v_w_qkv, v_attn_sinks, v_w_o, v_norm_mlp_1, v_w_up_1, v_w_down_1, v_norm_final):
    L, D = x.shape[1], x.shape[2]
    n_meta = meta_tokens.shape[0]
    pad = BLOCK - n_meta
    R = BLOCK + L
    n_q = D // HEAD_DIM
    n_kv = n_q // GROUP
    assert n_kv % 2 == 0 and L % BLOCK == 0 and D % 128 == 0
    x = x.reshape(L, D)
    target = loss_target.reshape(L, D)
    x_id, y_id, c_id = _coords()
    me = 4 * x_id + 2 * y_id + c_id

    col_names = ("in", "up0", "qkv", "up1")
    col_w = dict(zip(col_names, (w_in_conv, w_up_0, w_qkv, w_up_1)))
    row_names = ("out", "down0", "o", "down1")
    row_w = dict(zip(row_names, (w_out_conv, w_down_0, w_o, w_down_1)))
    me_arr = jnp.reshape(me, (1,)).astype(jnp.int32)
    natural = {k: col_w[k].T for k in col_names}
    natural.update(row_w)
    use_order = ("in", "out", "up0", "down0", "qkv", "o", "up1", "down1")
    placed = [cast_place(natural[k], me_arr, name="cast_" + k) for k in use_order]
    handles, token = copy_start([p[0] for p in placed], [p[1] for p in placed], all_to_all=False, name="gather_start")
    handles = dict(zip(use_order, handles))

    def weight(k, after):
        return copy_wait(handles[k], after, all_to_all=False, name="gather_wait_" + k)[1].reshape(-1, D)

    small_in = exchange([meta_tokens, conv_w], all_to_all=False, name="comm_gather")
    meta_full = jnp.transpose(small_in[0], (1, 0, 2)).reshape(n_meta, D)
    conv_full = jnp.transpose(small_in[1], (1, 0, 2)).reshape(conv_w.shape[0], D)

    vec = lambda a: a.reshape(1, D)
    pos = jnp.arange(R, dtype=F32) - pad
    inv = ROPE_THETA ** (-jnp.arange(0, HEAD_DIM, 2, dtype=F32) / HEAD_DIM)
    ang = pos[:, None] * inv[None, :]
    cos32, sin32 = jnp.cos(ang), jnp.sin(ang)
    cos = jnp.concatenate([cos32] * 4, axis=1)
    sin = jnp.concatenate([-sin32, sin32, -sin32, sin32], axis=1)

    W = {}
    h0 = jnp.concatenate([jnp.zeros((pad, D), F32), meta_full, x], axis=0)
    n0 = norm_fwd(h0, vec(norm_mix_0), token, name="norm0")
    W["in"] = weight("in", n0)
    bcu = mm(n0, W["in"], name="in_proj", out_dtype=BF16, b_rows_are_n=True)
    gated = conv_fwd(bcu, conv_full, name="conv_fwd")
    W["out"] = weight("out", gated)
    h1 = mm(gated, W["out"], name="out_proj", out_dtype=F32, b_rows_are_n=False, epi="add", extra=h0)
    n1 = norm_fwd(h1, vec(norm_mlp_0), h1, name="norm1")
    W["up0"] = weight("up0", n1)
    a0 = mm(n1, W["up0"], name="up0", out_dtype=BF16, b_rows_are_n=True, epi="relu")
    W["down0"] = weight("down0", a0)
    h2 = mm(a0, W["down0"], name="down0", out_dtype=F32, b_rows_are_n=False, epi="add", extra=h1, a_sq=True)
    n2 = norm_fwd(h2, vec(norm_mix_1), h2, name="norm2")
    W["qkv"] = weight("qkv", n2)
    qkv = mm(n2, W["qkv"], name="qkv_proj", out_dtype=F32, b_rows_are_n=True)
    q, kx, vx = rope_fwd(qkv, cos, sin, n_q=n_q, n_kv=n_kv, name="rope_fwd")
    o, lse = attn_fwd(q, kx, vx, attn_sinks, pad=pad, name="attn_fwd")
    W["o"] = weight("o", o)
    h3 = mm(o, W["o"], name="o_proj", out_dtype=F32, b_rows_are_n=False, epi="add", extra=h2)
    n3 = norm_fwd(h3, vec(norm_mlp_1), h3, name="norm3")
    W["up1"] = weight("up1", n3)
    a1 = mm(n3, W["up1"], name="up1", out_dtype=BF16, b_rows_are_n=True, epi="relu")
    W["down1"] = weight("down1", a1)
    h4 = mm(a1, W["down1"], name="down1", out_dtype=F32, b_rows_are_n=False, epi="add", extra=h3, a_sq=True)

    loss_part, dh4, dh4b, dg_final = loss_bwd(h4, vec(norm_final), target, name="loss_bwd")
    loss = lax.psum(loss_part[0, 0], ("x", "y", "c"))

    sent = {}

    def scatter(k, dw):
        (sent[k],), tok = copy_start([dw.reshape(N_DEV, -1, D)], all_to_all=True, name="a2a_start_" + k)
        return tok

    t = scatter("down1", mm_tn(a1, dh4b, name="dw_down1", a_sq=True))
    dup1 = mm(dh4b, W["down1"], name="d_down1", out_dtype=BF16, b_rows_are_n=True, epi="mul2a", extra=a1, dep=t)
    t = scatter("up1", mm_tn(dup1, n3, name="dw_up1"))
    dn3 = mm(dup1, W["up1"], name="d_up1", out_dtype=F32, b_rows_are_n=False, dep=t)
    dh3, dh3b, dg_mlp1 = norm_bwd(dn3, h3, vec(norm_mlp_1), dh4, name="norm3_bwd")

    t = scatter("o", mm_tn(o, dh3b, name="dw_o"))
    do = mm(dh3b, W["o"], name="d_o", out_dtype=BF16, b_rows_are_n=True, dep=t)
    dq, dkx, dvx, dsinks = attn_bwd(q, kx, vx, o, do, lse, attn_sinks, pad=pad, name="attn_bwd")
    dqkv = rope_bwd(dq, dkx, dvx, cos, sin, n_q=n_q, n_kv=n_kv, name="rope_bwd")
    t = scatter("qkv", mm_tn(dqkv, n2, name="dw_qkv"))
    dn2 = mm(dqkv, W["qkv"], name="d_qkv", out_dtype=F32, b_rows_are_n=False, dep=t)
    dh2, dh2b, dg_mix1 = norm_bwd(dn2, h2, vec(norm_mix_1), dh3, name="norm2_bwd")

    t = scatter("down0", mm_tn(a0, dh2b, name="dw_down0", a_sq=True))
    dup0 = mm(dh2b, W["down0"], name="d_down0", out_dtype=BF16, b_rows_are_n=True, epi="mul2a", extra=a0, dep=t)
    t = scatter("up0", mm_tn(dup0, n1, name="dw_up0"))
    dn1 = mm(dup0, W["up0"], name="d_up0", out_dtype=F32, b_rows_are_n=False, dep=t)
    dh1, dh1b, dg_mlp0 = norm_bwd(dn1, h1, vec(norm_mlp_0), dh2, name="norm1_bwd")

    t = scatter("out", mm_tn(gated, dh1b, name="dw_out"))
    dgated = mm(dh1b, W["out"], name="d_out", out_dtype=F32, b_rows_are_n=True, dep=t)
    db, dc, du, dconv = conv_bwd(bcu, conv_full, dgated, name="conv_bwd")
    dbcu = jnp.concatenate([db, dc, du], axis=1)
    t = scatter("in", mm_tn(dbcu, n0, name="dw_in"))
    dn0 = mm(dbcu, W["in"], name="d_in", out_dtype=F32, b_rows_are_n=False, dep=t)
    dh0, _, dg_mix0 = norm_bwd(dn0, h0, vec(norm_mix_0), dh1, name="norm0_bwd")
    grad_x = dh0[BLOCK:].reshape(1, L, D)

    recv = {}
    for k in ("down1", "up1", "o", "qkv", "down0", "up0", "out", "in"):
        sent[k], recv[k] = copy_wait(sent[k], dh0, all_to_all=True, name="a2a_wait_" + k)

    n_sink = attn_sinks.shape[0]
    slab = jnp.concatenate([
        dg_mix0, dg_mlp0, dg_mix1, dg_mlp1, dg_final,
        jnp.pad(dsinks[0:1, :n_sink], ((0, 0), (0, D - n_sink))), jnp.zeros((2, D), F32),
        dconv, dh0[pad:BLOCK]], axis=0)
    slabs = exchange([slab], all_to_all=False, name="comm_small")[0]
    small = sum_parts(slabs, slabs, me_arr, name="sum_small")
    cols = D // N_DEV
    my_cols = lambda a: lax.dynamic_slice_in_dim(a, me * cols, cols, axis=1)

    grads, deltas, new_m, new_v = {}, {}, {}, {}

    def update(key, w, m, v, g, shape):
        s2 = (1, -1) if w.ndim == 1 else w.shape
        if g.ndim == 3:
            g_, d_, m_, v_ = adam_parts(w, m, v, g, sent[key_of[key]], me_arr, name="adam_" + key)
        else:
            g_ = g.reshape(s2)
            d_, m_, v_ = adam(w.reshape(s2), m.reshape(s2), v.reshape(s2), g_, name="adam_" + key)
        grads[key], deltas[key], new_m[key], new_v[key] = (t.reshape(shape) for t in (g_, d_, m_, v_))

    update("meta_tokens", meta_tokens, m_meta_tokens, v_meta_tokens, my_cols(small[16:16 + n_meta]), meta_tokens.shape)
    update("norm_mix_0", norm_mix_0, m_norm_mix_0, v_norm_mix_0, small[0], (D,))
    update("conv_w", conv_w, m_conv_w, v_conv_w, my_cols(small[8:8 + conv_w.shape[0]]), conv_w.shape)
    update("norm_mlp_0", norm_mlp_0, m_norm_mlp_0, v_norm_mlp_0, small[1], (D,))
    update("norm_mix_1", norm_mix_1, m_norm_mix_1, v_norm_mix_1, small[2], (D,))
    update("attn_sinks", attn_sinks, m_attn_sinks, v_attn_sinks, small[5, :n_sink], (n_sink,))
    update("norm_mlp_1", norm_mlp_1, m_norm_mlp_1, v_norm_mlp_1, small[3], (D,))
    update("norm_final", norm_final, m_norm_final, v_norm_final, small[4], (D,))
    big = {"in": ("w_in_conv", w_in_conv, m_w_in_conv, v_w_in_conv), "up0": ("w_up_0", w_up_0, m_w_up_0, v_w_up_0),
           "qkv": ("w_qkv", w_qkv, m_w_qkv, v_w_qkv), "up1": ("w_up_1", w_up_1, m_w_up_1, v_w_up_1),
           "out": ("w_out_conv", w_out_conv, m_w_out_conv, v_w_out_conv),
           "down0": ("w_down_0", w_down_0, m_w_down_0, v_w_down_0), "o": ("w_o", w_o, m_w_o, v_w_o),
           "down1": ("w_down_1", w_down_1, m_w_down_1, v_w_down_1)}
    key_of = {big[k][0]: k for k in big}
    for k in col_names:
        key, w, m, v = big[k]
        update(key, w, m, v, sum_parts(recv[k], sent[k], me_arr, name="sum_" + k).T, w.shape)
    for k in row_names:
        key, w, m, v = big[k]
        update(key, w, m, v, recv[k], w.shape)

    order = ("meta_tokens", "norm_mix_0", "w_in_conv", "conv_w", "w_out_conv", "norm_mlp_0", "w_up_0", "w_down_0",
             "norm_mix_1", "w_qkv", "attn_sinks", "w_o", "norm_mlp_1", "w_up_1", "w_down_1", "norm_final")
    return (loss, grad_x, *[grads[k] for k in order], *[deltas[k] for k in order],
            *[new_m[k] for k in order], *[new_v[k] for k in order])
```

```python
import functools

import jax
import jax.numpy as jnp
from jax import lax
from jax.experimental import pallas as pl
from jax.experimental.pallas import tpu as pltpu

F32 = jnp.float32
BF16 = jnp.bfloat16

HEAD_DIM = 64
GROUP = 8
BLOCK = 128
N_DEV = 8
RMS_EPS = 1e-5
NEG_INF = -1e30
ROPE_THETA = 10000.0
ADAM_LR = 0.001
ADAM_B1 = 0.9
ADAM_B2 = 0.999
ADAM_EPS = 1e-08
ADAM_WD = 0.01
ADAM_STEP = 10
VMEM_LIMIT = 60 * 1024 * 1024
MESH = pl.DeviceIdType.MESH


def _tile(n, target, mult):
    best = None
    for t in range(mult, min(n, target) + 1, mult):
        if n % t == 0:
            best = t
    return best if best is not None else n


def _params(sem):
    return pltpu.CompilerParams(dimension_semantics=sem, vmem_limit_bytes=VMEM_LIMIT)


ANY_SPEC = pl.BlockSpec(memory_space=pl.ANY)


LEFT_BLOCK_BYTES = 14 * 1024 * 1024


def mm(a, b, *, name, out_dtype, b_rows_are_n, epi=None, extra=None, a_sq=False, dep=None):
    M, K = a.shape
    N = b.shape[0] if b_rows_are_n else b.shape[1]
    tm = _tile(M, LEFT_BLOCK_BYTES // (2 * K), 16)
    tn = _tile(N, 512 if K <= 4096 else 256, 128)

    def body(*refs):
        a_ref, b_ref, e_ref, o_ref = refs[0], refs[1], refs[2], refs[-1]
        av = a_ref[...]
        if a_sq:
            av = av.astype(F32)
            av = (av * av).astype(BF16)
        dims = (((1,), (1,)), ((), ())) if b_rows_are_n else (((1,), (0,)), ((), ()))
        acc = lax.dot_general(av, b_ref[...], dims, preferred_element_type=F32)
        if epi == "relu":
            acc = jnp.maximum(acc, 0.0)
        elif epi == "mul2a":
            acc = acc * (2.0 * e_ref[...].astype(F32))
        elif epi == "add":
            acc = acc + e_ref[...]
        o_ref[...] = acc.astype(o_ref.dtype)

    b_spec = pl.BlockSpec((tn, K), lambda i, j: (j, 0)) if b_rows_are_n else pl.BlockSpec((K, tn), lambda i, j: (0, j))
    in_specs = [pl.BlockSpec((tm, K), lambda i, j: (i, 0)), b_spec]
    args = [a, b]
    if extra is not None:
        in_specs.append(pl.BlockSpec((tm, tn), lambda i, j: (i, j)))
        args.append(extra)
    if dep is not None:
        in_specs.append(ANY_SPEC)
        args.append(dep)
    return pl.pallas_call(
        body, name=name, grid=(M // tm, N // tn), in_specs=in_specs,
        out_specs=pl.BlockSpec((tm, tn), lambda i, j: (i, j)),
        out_shape=jax.ShapeDtypeStruct((M, N), out_dtype),
        compiler_params=_params(("parallel", "parallel")),
    )(*args)


def _rms_parts(h):
    rstd = lax.rsqrt(jnp.mean(h * h, axis=-1, keepdims=True) + RMS_EPS)
    return h * rstd, rstd


def mm_tn(a, b, *, name, a_sq=False):
    T, M = a.shape
    N = b.shape[1]
    tm = _tile(M, 1024, 128)
    tk = _tile(T, 2080, 16)
    nk = T // tk

    def body(a_ref, b_ref, o_ref, acc_ref):
        k = pl.program_id(1)
        av = a_ref[...]
        if a_sq:
            av = av.astype(F32)
            av = (av * av).astype(BF16)
        part = lax.dot_general(av, b_ref[...], (((0,), (0,)), ((), ())), preferred_element_type=F32)

        @pl.when(k == 0)
        def _():
            acc_ref[...] = part

        @pl.when(k > 0)
        def _():
            acc_ref[...] += part

        @pl.when(k == nk - 1)
        def _():
            o_ref[...] = acc_ref[...].astype(BF16)

    return pl.pallas_call(
        body, name=name, grid=(M // tm, nk),
        in_specs=[pl.BlockSpec((tk, tm), lambda i, k: (k, i)), pl.BlockSpec((tk, N), lambda i, k: (k, 0))],
        out_specs=pl.BlockSpec((tm, N), lambda i, k: (i, 0)),
        out_shape=jax.ShapeDtypeStruct((M, N), BF16),
        scratch_shapes=[pltpu.VMEM((tm, N), F32)],
        compiler_params=_params(("parallel", "arbitrary")),
    )(a, b)


def norm_fwd(h, g, dep, *, name):
    R, D = h.shape
    tr = _tile(R, 320, 16)

    def body(h_ref, g_ref, dep_ref, n_ref):
        xhat, _ = _rms_parts(h_ref[...])
        n_ref[...] = (xhat * g_ref[...]).astype(BF16)

    return pl.pallas_call(
        body, name=name, grid=(R // tr,),
        in_specs=[pl.BlockSpec((tr, D), lambda i: (i, 0)), pl.BlockSpec((1, D), lambda i: (0, 0)), ANY_SPEC],
        out_specs=pl.BlockSpec((tr, D), lambda i: (i, 0)),
        out_shape=jax.ShapeDtypeStruct((R, D), BF16),
        compiler_params=_params(("parallel",)),
    )(h, g, dep)


def norm_bwd(dn, h, g, dres, *, name):
    R, D = h.shape
    tr = _tile(R, 320, 16)

    def body(dn_ref, h_ref, g_ref, dres_ref, dh_ref, dhb_ref, dg_ref):
        i = pl.program_id(0)
        dn = dn_ref[...]
        xhat, rstd = _rms_parts(h_ref[...])
        dxh = dn * g_ref[...]
        dh = dres_ref[...] + rstd * (dxh - xhat * jnp.mean(dxh * xhat, axis=-1, keepdims=True))
        dh_ref[...] = dh
        dhb_ref[...] = dh.astype(BF16)
        dg = jnp.sum(dn * xhat, axis=0, keepdims=True)

        @pl.when(i == 0)
        def _():
            dg_ref[...] = dg

        @pl.when(i > 0)
        def _():
            dg_ref[...] += dg

    row = pl.BlockSpec((tr, D), lambda i: (i, 0))
    vec = pl.BlockSpec((1, D), lambda i: (0, 0))
    return pl.pallas_call(
        body, name=name, grid=(R // tr,), in_specs=[row, row, vec, row], out_specs=[row, row, vec],
        out_shape=[jax.ShapeDtypeStruct((R, D), F32), jax.ShapeDtypeStruct((R, D), BF16),
                   jax.ShapeDtypeStruct((1, D), F32)],
        compiler_params=_params(("arbitrary",)),
    )(dn, h, g, dres)


def first_norm(head, x, g, dep, *, name):
    L, D = x.shape
    nb = L // BLOCK + 1

    def body(head_ref, x_ref, g_ref, dep_ref, h_ref, n_ref):
        i = pl.program_id(0)
        hv = jnp.where(i == 0, head_ref[...], x_ref[...])
        h_ref[...] = hv
        xhat, _ = _rms_parts(hv)
        n_ref[...] = (xhat * g_ref[...]).astype(BF16)

    blk = pl.BlockSpec((BLOCK, D), lambda i: (i, 0))
    return pl.pallas_call(
        body, name=name, grid=(nb,),
        in_specs=[pl.BlockSpec((BLOCK, D), lambda i: (0, 0)),
                  pl.BlockSpec((BLOCK, D), lambda i: (jnp.maximum(i - 1, 0), 0)),
                  pl.BlockSpec((1, D), lambda i: (0, 0)), ANY_SPEC],
        out_specs=[blk, blk],
        out_shape=[jax.ShapeDtypeStruct((BLOCK + L, D), F32), jax.ShapeDtypeStruct((BLOCK + L, D), BF16)],
        compiler_params=_params(("parallel",)),
    )(head, x, g, dep)


def last_norm_bwd(dn, h, g, dres, *, name):
    R, D = h.shape
    nb = R // BLOCK

    def body(dn_ref, h_ref, g_ref, dres_ref, dhead_ref, dx_ref, dg_ref):
        i = pl.program_id(0)
        dn = dn_ref[...]
        xhat, rstd = _rms_parts(h_ref[...])
        dxh = dn * g_ref[...]
        dh = dres_ref[...] + rstd * (dxh - xhat * jnp.mean(dxh * xhat, axis=-1, keepdims=True))
        dg = jnp.sum(dn * xhat, axis=0, keepdims=True)
        dx_ref[...] = dh

        @pl.when(i == 0)
        def _():
            dhead_ref[...] = dh
            dg_ref[...] = dg

        @pl.when(i > 0)
        def _():
            dg_ref[...] += dg

    blk = pl.BlockSpec((BLOCK, D), lambda i: (i, 0))
    vec = pl.BlockSpec((1, D), lambda i: (0, 0))
    return pl.pallas_call(
        body, name=name, grid=(nb,), in_specs=[blk, blk, vec, blk],
        out_specs=[pl.BlockSpec((BLOCK, D), lambda i: (0, 0)),
                   pl.BlockSpec((BLOCK, D), lambda i: (jnp.maximum(i - 1, 0), 0)), vec],
        out_shape=[jax.ShapeDtypeStruct((BLOCK, D), F32), jax.ShapeDtypeStruct((R - BLOCK, D), F32),
                   jax.ShapeDtypeStruct((1, D), F32)],
        compiler_params=_params(("arbitrary",)),
    )(dn, h, g, dres)


def _shift_down(v, halo, s):
    n = v.shape[0]
    row = lax.broadcasted_iota(jnp.int32, v.shape, 0)
    out = pltpu.roll(v, s, axis=0)
    hn = halo.shape[0]
    for r in range(s):
        out = jnp.where(row == r, halo[hn - s + r:hn - s + r + 1, :], out)
    return out


def _shift_up(v, halo, s):
    n = v.shape[0]
    row = lax.broadcasted_iota(jnp.int32, v.shape, 0)
    out = pltpu.roll(v, n - s, axis=0)
    for r in range(s):
        out = jnp.where(row == n - s + r, halo[r:r + 1, :], out)
    return out


HALO = 16


def conv_fwd(bcu, conv_w, *, name):
    R, D3 = bcu.shape
    D = D3 // 3
    tr = _tile(R, 320, 16)
    tc = _tile(D, 512, 128)
    nc = D // tc
    hb = tr // HALO

    def body(b_ref, c_ref, u_ref, ch_ref, uh_ref, w_ref, o_ref):
        i = pl.program_id(0)
        v = c_ref[...].astype(F32) * u_ref[...].astype(F32)
        vh = ch_ref[...].astype(F32) * uh_ref[...].astype(F32)
        vh = jnp.where(i > 0, vh, 0.0)
        v1 = _shift_down(v, vh, 1)
        v2 = _shift_down(v, vh, 2)
        conv = w_ref[0:1, :] * v2 + w_ref[1:2, :] * v1 + w_ref[2:3, :] * v
        o_ref[...] = (b_ref[...].astype(F32) * conv).astype(BF16)

    def prev(i):
        return jnp.maximum(i * hb - 1, 0)

    return pl.pallas_call(
        body, name=name, grid=(R // tr, nc),
        in_specs=[pl.BlockSpec((tr, tc), lambda i, j: (i, j)),
                  pl.BlockSpec((tr, tc), lambda i, j: (i, nc + j)),
                  pl.BlockSpec((tr, tc), lambda i, j: (i, 2 * nc + j)),
                  pl.BlockSpec((HALO, tc), lambda i, j: (prev(i), nc + j)),
                  pl.BlockSpec((HALO, tc), lambda i, j: (prev(i), 2 * nc + j)),
                  pl.BlockSpec((3, tc), lambda i, j: (0, j))],
        out_specs=pl.BlockSpec((tr, tc), lambda i, j: (i, j)),
        out_shape=jax.ShapeDtypeStruct((R, D), BF16),
        compiler_params=_params(("parallel", "parallel")),
    )(bcu, bcu, bcu, bcu, bcu, conv_w)


def conv_bwd(bcu, conv_w, dg, *, name):
    R, D3 = bcu.shape
    D = D3 // 3
    tr = _tile(R, 320, 16)
    tc = _tile(D, 512, 128)
    nc = D // tc
    hb = tr // HALO
    nt = R // tr
    last_h = R // HALO - 1

    def body(b_ref, c_ref, u_ref, ch_ref, uh_ref, bn_ref, dg_ref, dgn_ref, w_ref, o_ref, dw_ref):
        i = pl.program_id(1)
        part = pl.program_id(2)

        def d_conv():
            return dg_ref[...] * b_ref[...].astype(F32)

        def d_v():
            dconv = d_conv()
            dconv_n = jnp.where(i < nt - 1, dgn_ref[...] * bn_ref[...].astype(F32), 0.0)
            d1 = _shift_up(dconv, dconv_n, 1)
            d2 = _shift_up(dconv, dconv_n, 2)
            return w_ref[2:3, :] * dconv + w_ref[1:2, :] * d1 + w_ref[0:1, :] * d2

        @pl.when(part == 0)
        def _():
            v = c_ref[...].astype(F32) * u_ref[...].astype(F32)
            vh = jnp.where(i > 0, ch_ref[...].astype(F32) * uh_ref[...].astype(F32), 0.0)
            v1 = _shift_down(v, vh, 1)
            v2 = _shift_down(v, vh, 2)
            conv = w_ref[0:1, :] * v2 + w_ref[1:2, :] * v1 + w_ref[2:3, :] * v
            o_ref[...] = (dg_ref[...] * conv).astype(BF16)
            dconv = d_conv()
            row8 = lax.broadcasted_iota(jnp.int32, (8, tc), 0)
            dw = jnp.where(row8 == 0, jnp.sum(dconv * v2, axis=0, keepdims=True),
                           jnp.where(row8 == 1, jnp.sum(dconv * v1, axis=0, keepdims=True),
                                     jnp.where(row8 == 2, jnp.sum(dconv * v, axis=0, keepdims=True), 0.0)))

            @pl.when(i == 0)
            def _():
                dw_ref[...] = dw

            @pl.when(i > 0)
            def _():
                dw_ref[...] += dw

        @pl.when(part == 1)
        def _():
            o_ref[...] = (d_v() * u_ref[...].astype(F32)).astype(BF16)

        @pl.when(part == 2)
        def _():
            o_ref[...] = (d_v() * c_ref[...].astype(F32)).astype(BF16)

    def prev(i):
        return jnp.maximum(i * hb - 1, 0)

    def nxt(i):
        return jnp.minimum((i + 1) * hb, last_h)

    cur = lambda off: pl.BlockSpec((tr, tc), lambda j, i, p: (i, off * nc + j))
    return pl.pallas_call(
        body, name=name, grid=(nc, nt, 3),
        in_specs=[cur(0), cur(1), cur(2),
                  pl.BlockSpec((HALO, tc), lambda j, i, p: (prev(i), nc + j)),
                  pl.BlockSpec((HALO, tc), lambda j, i, p: (prev(i), 2 * nc + j)),
                  pl.BlockSpec((HALO, tc), lambda j, i, p: (nxt(i), j)),
                  pl.BlockSpec((tr, tc), lambda j, i, p: (i, j)),
                  pl.BlockSpec((HALO, tc), lambda j, i, p: (nxt(i), j)),
                  pl.BlockSpec((3, tc), lambda j, i, p: (0, j))],
        out_specs=[pl.BlockSpec((tr, tc), lambda j, i, p: (i, p * nc + j)),
                   pl.BlockSpec((8, tc), lambda j, i, p: (0, j))],
        out_shape=[jax.ShapeDtypeStruct((R, D3), BF16), jax.ShapeDtypeStruct((8, D), F32)],
        compiler_params=_params(("arbitrary", "arbitrary", "arbitrary")),
    )(bcu, bcu, bcu, bcu, bcu, bcu, dg, dg, conv_w)


def _swap32(x):
    w = x.shape[1]
    lane = lax.broadcasted_iota(jnp.int32, x.shape, 1)
    return jnp.where((lane & (HEAD_DIM - 1)) < HEAD_DIM // 2, pltpu.roll(x, w - HEAD_DIM // 2, axis=1),
                     pltpu.roll(x, HEAD_DIM // 2, axis=1))


def _rope(x, cos, sin):
    return x * cos + _swap32(x) * sin


def rope_fwd(qkv, cos, sin, *, n_q, n_kv, name):
    R, W = qkv.shape
    qw = n_q * HEAD_DIM
    kw = n_kv * HEAD_DIM
    tr = _tile(R, 640, 128)

    def expand(y, ref, ref_t, c):
        lane = lax.broadcasted_iota(jnp.int32, y.shape, 1)
        lo = jnp.where(lane < HEAD_DIM, y, 0.0)
        hi = jnp.where(lane >= HEAD_DIM, y, 0.0)
        tiles = (lo, pltpu.roll(lo, HEAD_DIM, axis=1), pltpu.roll(hi, HEAD_DIM, axis=1), hi)
        for t, tile in enumerate(tiles):
            c0 = 512 * c + 128 * t
            ref[:, c0:c0 + 128] = tile.astype(BF16)
            ref_t[c0:c0 + 128, :] = tile.T.astype(BF16)

    def body(x_ref, c_ref, s_ref, q_ref, k_ref, v_ref, kt_ref, vt_ref):
        cos = c_ref[...]
        sin = s_ref[...]
        for c in range(qw // 128):
            x = x_ref[:, 128 * c:128 * (c + 1)]
            q_ref[:, 128 * c:128 * (c + 1)] = (_rope(x, cos, sin) * (HEAD_DIM ** -0.5)).astype(BF16)
        for c in range(kw // 128):
            expand(_rope(x_ref[:, qw + 128 * c:qw + 128 * (c + 1)], cos, sin), k_ref, kt_ref, c)
            expand(x_ref[:, qw + kw + 128 * c:qw + kw + 128 * (c + 1)], v_ref, vt_ref, c)

    row = lambda w: pl.BlockSpec((tr, w), lambda i: (i, 0))
    col = pl.BlockSpec((4 * kw, tr), lambda i: (0, i))
    return pl.pallas_call(
        body, name=name, grid=(R // tr,),
        in_specs=[row(W), row(128), row(128)],
        out_specs=[row(qw), row(4 * kw), row(4 * kw), col, col],
        out_shape=[jax.ShapeDtypeStruct((R, qw), BF16), jax.ShapeDtypeStruct((R, 4 * kw), BF16),
                   jax.ShapeDtypeStruct((R, 4 * kw), BF16), jax.ShapeDtypeStruct((4 * kw, R), BF16),
                   jax.ShapeDtypeStruct((4 * kw, R), BF16)],
        compiler_params=_params(("parallel",)),
    )(qkv, cos, sin)


def rope_bwd(dq, dkx, dvx, cos, sin, *, n_q, n_kv, name):
    R = dq.shape[0]
    qw = n_q * HEAD_DIM
    kw = n_kv * HEAD_DIM
    W = qw + 2 * kw
    tr = _tile(R, 320, 16)

    def fold(ref, c):
        lane = lax.broadcasted_iota(jnp.int32, (tr, 128), 1)
        x0 = ref[:, 128 * (2 * c):128 * (2 * c + 1)]
        x1 = ref[:, 128 * (2 * c + 1):128 * (2 * c + 2)]
        f0 = x0 + pltpu.roll(x0, HEAD_DIM, axis=1)
        f1 = x1 + pltpu.roll(x1, HEAD_DIM, axis=1)
        return jnp.where(lane < HEAD_DIM, f0, f1)

    def body(dq_ref, dk_ref, dv_ref, c_ref, s_ref, o_ref):
        cos = c_ref[...]
        nsin = -s_ref[...]
        for c in range(qw // 128):
            y = dq_ref[:, 128 * c:128 * (c + 1)]
            o_ref[:, 128 * c:128 * (c + 1)] = (_rope(y, cos, nsin) * (HEAD_DIM ** -0.5)).astype(BF16)
        for c in range(kw // 128):
            o_ref[:, qw + 128 * c:qw + 128 * (c + 1)] = _rope(fold(dk_ref, c), cos, nsin).astype(BF16)
            o_ref[:, qw + kw + 128 * c:qw + kw + 128 * (c + 1)] = fold(dv_ref, c).astype(BF16)

    row = lambda w: pl.BlockSpec((tr, w), lambda i: (i, 0))
    return pl.pallas_call(
        body, name=name, grid=(R // tr,),
        in_specs=[row(qw), row(2 * kw), row(2 * kw), row(128), row(128)],
        out_specs=row(W),
        out_shape=jax.ShapeDtypeStruct((R, W), BF16),
        compiler_params=_params(("parallel",)),
    )(dq, dkx, dvx, cos, sin)


def _band_bias(n, pad):
    key = lax.broadcasted_iota(jnp.int32, (2 * BLOCK, BLOCK), 0)
    qry = lax.broadcasted_iota(jnp.int32, (2 * BLOCK, BLOCK), 1)
    kmin = jnp.where(n == 0, BLOCK + pad, jnp.where(n == 1, pad, 0))
    allowed = (key > qry) & (key <= qry + BLOCK) & (key >= kmin)
    return jnp.where(allowed, 0.0, NEG_INF).astype(F32)


def _dot_nt(a, b):
    return lax.dot_general(a, b, (((1,), (1,)), ((), ())), preferred_element_type=F32)


def _band(prev_ref, cur_ref, c0):
    return jnp.concatenate([prev_ref[:, c0:c0 + 128], cur_ref[:, c0:c0 + 128]], axis=0)


def _band_t(prev_ref, cur_ref, r0):
    return jnp.concatenate([prev_ref[r0:r0 + 128, :], cur_ref[r0:r0 + 128, :]], axis=1)


def attn_fwd(q, kx, vxt, sinks, *, pad, name):
    R, qw = q.shape
    n_q = qw // HEAD_DIM
    n_kv = kx.shape[1] // 256
    nb = R // BLOCK

    def body(s_ref, q_ref, kc_ref, kp_ref, vc_ref, vp_ref, o_ref, l_ref):
        n = pl.program_id(0)
        bias = _band_bias(n, pad)
        row = lax.broadcasted_iota(jnp.int32, (128, BLOCK), 0)

        def softmax(k2, q2, sink):
            st = _dot_nt(k2, q2) + bias
            m = jnp.maximum(jnp.max(st, axis=0, keepdims=True), sink)
            e = jnp.exp(st - m)
            den = jnp.sum(e, axis=0, keepdims=True) + jnp.exp(sink - m)
            return e.astype(BF16), 1.0 / den, m + jnp.log(den)

        for g in range(n_kv):
            k2a = _band(kp_ref, kc_ref, 256 * g)
            k2b = _band(kp_ref, kc_ref, 256 * g + 128)
            v2a = _band_t(vp_ref, vc_ref, 256 * g)
            v2b = _band_t(vp_ref, vc_ref, 256 * g + 128)
            for p in range(GROUP // 2):
                c0 = 128 * (g * (GROUP // 2) + p)
                he = GROUP * g + 2 * p
                q2 = q_ref[:, c0:c0 + 128]
                ee, re, le = softmax(k2a, q2, s_ref[he])
                eo, ro, lo = softmax(k2b, q2, s_ref[he + 1])
                o2t = jnp.dot(v2a, ee, preferred_element_type=F32) + jnp.dot(v2b, eo, preferred_element_type=F32)
                o2t = o2t * jnp.where(row < HEAD_DIM, re, ro)
                o_ref[:, c0:c0 + 128] = o2t.T.astype(BF16)
                l_ref[he:he + 1, :] = le
                l_ref[he + 1:he + 2, :] = lo

    cur = lambda w: pl.BlockSpec((BLOCK, w), lambda n: (n, 0))
    prv = lambda w: pl.BlockSpec((BLOCK, w), lambda n: (jnp.maximum(n - 1, 0), 0))
    cur_t = lambda h: pl.BlockSpec((h, BLOCK), lambda n: (0, n))
    prv_t = lambda h: pl.BlockSpec((h, BLOCK), lambda n: (0, jnp.maximum(n - 1, 0)))
    kxw = kx.shape[1]
    return pl.pallas_call(
        body, name=name, grid=(nb,),
        in_specs=[pl.BlockSpec(memory_space=pltpu.SMEM), cur(qw), cur(kxw), prv(kxw), cur_t(kxw), prv_t(kxw)],
        out_specs=[cur(qw), cur_t(n_q)],
        out_shape=[jax.ShapeDtypeStruct((R, qw), BF16), jax.ShapeDtypeStruct((n_q, R), F32)],
        compiler_params=_params(("parallel",)),
    )(sinks, q, kx, kx, vxt, vxt)


def attn_bwd(q, kx, kxt, vx, o, do, lse, sinks, *, pad, name):
    R, qw = q.shape
    n_q = qw // HEAD_DIM
    n_kv = kx.shape[1] // 256
    nb = R // BLOCK
    kw2 = n_kv * 128

    def body(s_ref, q_ref, do_ref, o_ref, l_ref, kc_ref, kp_ref, ktc_ref, ktp_ref, vc_ref, vp_ref,
             dq_ref, dk_ref, dv_ref, ds_ref, cdk, cdv):
        n = pl.program_id(0)

        @pl.when(n == 0)
        def _():
            cdk[...] = jnp.zeros_like(cdk)
            cdv[...] = jnp.zeros_like(cdv)
            ds_ref[...] = jnp.zeros_like(ds_ref)

        @pl.when(n < nb)
        def _():
            bias = _band_bias(n, pad)
            lane2 = lax.broadcasted_iota(jnp.int32, (2 * BLOCK, 128), 1)
            lane1 = lax.broadcasted_iota(jnp.int32, (1, 128), 1)
            sel_r = lax.broadcasted_iota(jnp.int32, (8, 128), 0)
            sel_l = lax.broadcasted_iota(jnp.int32, (8, 128), 1)
            sel = (((sel_r == 0) & (sel_l < HEAD_DIM)) | ((sel_r == 1) & (sel_l >= HEAD_DIM))).astype(BF16)
            dsink = jnp.zeros((1, 128), F32)
            for g in range(n_kv):
                k2a = _band(kp_ref, kc_ref, 256 * g)
                k2b = _band(kp_ref, kc_ref, 256 * g + 128)
                kt2a = _band_t(ktp_ref, ktc_ref, 256 * g)
                kt2b = _band_t(ktp_ref, ktc_ref, 256 * g + 128)
                v2a = _band(vp_ref, vc_ref, 256 * g)
                v2b = _band(vp_ref, vc_ref, 256 * g + 128)
                dka = jnp.zeros((2 * BLOCK, 128), F32)
                dkb = jnp.zeros((2 * BLOCK, 128), F32)
                dva = jnp.zeros((2 * BLOCK, 128), F32)
                dvb = jnp.zeros((2 * BLOCK, 128), F32)
                for p in range(GROUP // 2):
                    c0 = 128 * (g * (GROUP // 2) + p)
                    he = GROUP * g + 2 * p
                    q2 = q_ref[:, c0:c0 + 128]
                    do2 = do_ref[:, c0:c0 + 128]
                    prod = do2.astype(F32) * o_ref[:, c0:c0 + 128].astype(F32)
                    prod_hi = prod.astype(BF16)
                    prod_lo = (prod - prod_hi.astype(F32)).astype(BF16)
                    deltas = _dot_nt(sel, prod_hi) + _dot_nt(sel, prod_lo)
                    halves = ((k2a, kt2a, v2a, he, 0), (k2b, kt2b, v2b, he + 1, 1))
                    dq2t = jnp.zeros((128, BLOCK), F32)
                    outs = []
                    for (k2, kt2, v2, h, r) in halves:
                        lse_h = l_ref[h:h + 1, :]
                        delta = deltas[r:r + 1, :]
                        pt = jnp.exp(_dot_nt(k2, q2) + bias - lse_h)
                        dpt = _dot_nt(v2, do2)
                        dst = (pt * (dpt - delta)).astype(BF16)
                        ptb = pt.astype(BF16)
                        dq2t = dq2t + jnp.dot(kt2, dst, preferred_element_type=F32)
                        outs.append((jnp.dot(dst, q2, preferred_element_type=F32),
                                     jnp.dot(ptb, do2, preferred_element_type=F32)))
                        psink = jnp.exp(s_ref[h] - lse_h)
                        tot = jnp.sum(psink * delta, axis=1, keepdims=True)
                        dsink = dsink - jnp.where(lane1 == h, tot, 0.0)
                    dq_ref[:, c0:c0 + 128] = dq2t.T
                    dka = dka + outs[0][0]
                    dva = dva + outs[0][1]
                    dkb = dkb + outs[1][0]
                    dvb = dvb + outs[1][1]
                gc = pl.ds(128 * g, 128)
                dk2 = jnp.where(lane2 < HEAD_DIM, dka, dkb)
                dv2 = jnp.where(lane2 < HEAD_DIM, dva, dvb)
                dk_ref[:, gc] = cdk[:, gc] + dk2[0:BLOCK]
                dv_ref[:, gc] = cdv[:, gc] + dv2[0:BLOCK]
                cdk[:, gc] = dk2[BLOCK:2 * BLOCK]
                cdv[:, gc] = dv2[BLOCK:2 * BLOCK]
            ds_ref[0:1, :] += dsink

        @pl.when(n == nb)
        def _():
            dk_ref[...] = cdk[...]
            dv_ref[...] = cdv[...]

    cur = lambda w: pl.BlockSpec((BLOCK, w), lambda n: (jnp.minimum(n, nb - 1), 0))
    prv = lambda w: pl.BlockSpec((BLOCK, w), lambda n: (jnp.clip(n - 1, 0, nb - 1), 0))
    cur_t = lambda h: pl.BlockSpec((h, BLOCK), lambda n: (0, jnp.minimum(n, nb - 1)))
    prv_t = lambda h: pl.BlockSpec((h, BLOCK), lambda n: (0, jnp.clip(n - 1, 0, nb - 1)))
    kxw = kx.shape[1]
    return pl.pallas_call(
        body, name=name, grid=(nb + 1,),
        in_specs=[pl.BlockSpec(memory_space=pltpu.SMEM), cur(qw), cur(qw), cur(qw), cur_t(n_q),
                  cur(kxw), prv(kxw), cur_t(kxw), prv_t(kxw), cur(kxw), prv(kxw)],
        out_specs=[cur(qw), prv(kw2), prv(kw2), pl.BlockSpec((8, 128), lambda n: (0, 0))],
        out_shape=[jax.ShapeDtypeStruct((R, qw), F32), jax.ShapeDtypeStruct((R, kw2), F32),
                   jax.ShapeDtypeStruct((R, kw2), F32), jax.ShapeDtypeStruct((8, 128), F32)],
        scratch_shapes=[pltpu.VMEM((BLOCK, kw2), F32), pltpu.VMEM((BLOCK, kw2), F32)],
        compiler_params=_params(("arbitrary",)),
    )(sinks, q, do, o, lse, kx, kx, kxt, kxt, vx, vx)


def loss_bwd(h, g, target, *, name):
    R, D = h.shape
    nb = R // BLOCK

    def body(h_ref, g_ref, t_ref, loss_ref, dh_ref, dhb_ref, dg_ref):
        n = pl.program_id(0)
        xhat, rstd = _rms_parts(h_ref[...])
        gv = g_ref[...]
        diff = jnp.where(n > 0, xhat * gv - t_ref[...], 0.0)
        part = (0.5 / D) * jnp.sum(jnp.sum(diff * diff, axis=1, keepdims=True), axis=0, keepdims=True)
        dout = diff * (1.0 / D)
        dxh = dout * gv
        dh = rstd * (dxh - xhat * jnp.mean(dxh * xhat, axis=-1, keepdims=True))
        dh_ref[...] = dh
        dhb_ref[...] = dh.astype(BF16)
        dg = jnp.sum(dout * xhat, axis=0, keepdims=True)

        @pl.when(n == 0)
        def _():
            loss_ref[...] = jnp.zeros_like(loss_ref) + part
            dg_ref[...] = dg

        @pl.when(n > 0)
        def _():
            loss_ref[...] += part
            dg_ref[...] += dg

    blk = pl.BlockSpec((BLOCK, D), lambda n: (n, 0))
    return pl.pallas_call(
        body, name=name, grid=(nb,),
        in_specs=[blk, pl.BlockSpec((1, D), lambda n: (0, 0)),
                  pl.BlockSpec((BLOCK, D), lambda n: (jnp.maximum(n - 1, 0), 0))],
        out_specs=[pl.BlockSpec((8, 128), lambda n: (0, 0)), blk, blk, pl.BlockSpec((1, D), lambda n: (0, 0))],
        out_shape=[jax.ShapeDtypeStruct((8, 128), F32), jax.ShapeDtypeStruct((R, D), F32),
                   jax.ShapeDtypeStruct((R, D), BF16), jax.ShapeDtypeStruct((1, D), F32)],
        compiler_params=_params(("arbitrary",)),
    )(h, g, target)


def _adam_math(w, g, m, v):
    m = ADAM_B1 * m + (1.0 - ADAM_B1) * g
    v = ADAM_B2 * v + (1.0 - ADAM_B2) * (g * g)
    m_hat = m / (1.0 - ADAM_B1 ** ADAM_STEP)
    v_hat = v / (1.0 - ADAM_B2 ** ADAM_STEP)
    delta = -ADAM_LR * (m_hat / (jnp.sqrt(v_hat) + ADAM_EPS) + ADAM_WD * w)
    return delta, m, v


def adam(w, m, v, g, *, name):
    r, C = w.shape
    tr = _tile(r, 128, 8)

    def body(w_ref, m_ref, v_ref, g_ref, d_ref, mo_ref, vo_ref):
        d_ref[...], mo_ref[...], vo_ref[...] = _adam_math(w_ref[...], g_ref[...], m_ref[...], v_ref[...])

    blk = pl.BlockSpec((tr, C), lambda i: (i, 0))
    return pl.pallas_call(
        body, name=name, grid=(r // tr,), in_specs=[blk] * 4, out_specs=[blk] * 3,
        out_shape=[jax.ShapeDtypeStruct((r, C), F32)] * 3,
        compiler_params=_params(("parallel",)),
    )(w, m, v, g)


def _sum_blocks(p_ref, own_ref, me):
    acc = None
    for s in range(N_DEV):
        blk = jnp.where(me == s, own_ref[...], p_ref[s]).astype(F32)
        acc = blk if acc is None else acc + blk
    return acc


def _parts_specs(r, C, tr):
    blk = pl.BlockSpec((tr, C), lambda i, me: (i, 0))
    parts = pl.BlockSpec((N_DEV, tr, C), lambda i, me: (0, i, 0))
    own = pl.BlockSpec((None, tr, C), lambda i, me: (me[0], i, 0))
    return blk, parts, own


def adam_parts(w, m, v, parts, own, me, *, name):
    r, C = w.shape
    tr = _tile(r, 128, 8)

    def body(me_ref, w_ref, m_ref, v_ref, p_ref, own_ref, go_ref, d_ref, mo_ref, vo_ref):
        gv = _sum_blocks(p_ref, own_ref, me_ref[0])
        go_ref[...] = gv
        d_ref[...], mo_ref[...], vo_ref[...] = _adam_math(w_ref[...], gv, m_ref[...], v_ref[...])

    blk, pblk, oblk = _parts_specs(r, C, tr)
    return pl.pallas_call(
        body, name=name,
        grid_spec=pltpu.PrefetchScalarGridSpec(num_scalar_prefetch=1, grid=(r // tr,),
                                               in_specs=[blk, blk, blk, pblk, oblk], out_specs=[blk] * 4),
        out_shape=[jax.ShapeDtypeStruct((r, C), F32)] * 4,
        compiler_params=_params(("parallel",)),
    )(me, w, m, v, parts, own)


def sum_parts(parts, own, me, *, name):
    _, r, C = parts.shape
    tr = _tile(r, 128, 8)

    def body(me_ref, p_ref, own_ref, o_ref):
        o_ref[...] = _sum_blocks(p_ref, own_ref, me_ref[0])

    blk, pblk, oblk = _parts_specs(r, C, tr)
    return pl.pallas_call(
        body, name=name,
        grid_spec=pltpu.PrefetchScalarGridSpec(num_scalar_prefetch=1, grid=(r // tr,),
                                               in_specs=[pblk, oblk], out_specs=blk),
        out_shape=jax.ShapeDtypeStruct((r, C), F32),
        compiler_params=_params(("parallel",)),
    )(me, parts, own)


def cast_place(w, me, *, name):
    r, C = w.shape
    tr = _tile(r, 256, 16)

    def body(me_ref, w_ref, s_ref, l_ref):
        v = w_ref[...].astype(BF16)
        s_ref[...] = v
        l_ref[...] = v

    blk = pl.BlockSpec((tr, C), lambda i, me: (i, 0))
    return pl.pallas_call(
        body, name=name,
        grid_spec=pltpu.PrefetchScalarGridSpec(
            num_scalar_prefetch=1, grid=(r // tr,), in_specs=[blk],
            out_specs=[blk, pl.BlockSpec((None, tr, C), lambda i, me: (me[0], i, 0))]),
        out_shape=[jax.ShapeDtypeStruct((r, C), BF16), jax.ShapeDtypeStruct((N_DEV, r, C), BF16)],
        compiler_params=_params(("parallel",)),
    )(me, w)


def _coords():
    return lax.axis_index("x"), lax.axis_index("y"), lax.axis_index("c")


def _peer(m):
    x, y, c = _coords()
    px = 1 - x if m & 4 else x
    py = 1 - y if m & 2 else y
    pc = 1 - c if m & 1 else c
    return (px, py, pc), 4 * px + 2 * py + pc


def exchange(items, *, all_to_all, name):
    n = len(items)
    if all_to_all:
        out_shape = [jax.ShapeDtypeStruct(a.shape, a.dtype) for a in items]
    else:
        out_shape = [jax.ShapeDtypeStruct((N_DEV,) + a.shape, a.dtype) for a in items]

    def body(*refs):
        ins, outs = refs[:n], refs[n:2 * n]
        send_sems, recv_sems, local_sems = refs[2 * n:]
        x, y, c = _coords()
        me = 4 * x + 2 * y + c

        def src(i, idx):
            return ins[i].at[idx] if all_to_all else ins[i]

        local = [pltpu.make_async_copy(src(i, me), outs[i].at[me], local_sems.at[i]) for i in range(n)]
        for cp in local:
            cp.start()
        sends = []
        for m in range(1, N_DEV):
            peer, pidx = _peer(m)
            for i in range(n):
                k = i * (N_DEV - 1) + m - 1
                cp = pltpu.make_async_remote_copy(src_ref=src(i, pidx), dst_ref=outs[i].at[me],
                                                  send_sem=send_sems.at[k], recv_sem=recv_sems.at[k],
                                                  device_id=peer, device_id_type=MESH)
                cp.start()
                sends.append(cp)
        for m in range(1, N_DEV):
            peer, pidx = _peer(m)
            for i in range(n):
                k = i * (N_DEV - 1) + m - 1
                pltpu.make_async_remote_copy(src_ref=src(i, pidx), dst_ref=outs[i].at[pidx],
                                             send_sem=send_sems.at[k], recv_sem=recv_sems.at[k],
                                             device_id=peer, device_id_type=MESH).wait_recv()
        for cp in sends:
            cp.wait_send()
        for cp in local:
            cp.wait()

    any_spec = pl.BlockSpec(memory_space=pl.ANY)
    return pl.pallas_call(
        body, name=name, in_specs=[any_spec] * n, out_specs=[any_spec] * n, out_shape=out_shape,
        scratch_shapes=[pltpu.SemaphoreType.DMA((n * (N_DEV - 1),)), pltpu.SemaphoreType.DMA((n * (N_DEV - 1),)),
                        pltpu.SemaphoreType.DMA((n,))],
    )(*items)


HBM_SPEC = pl.BlockSpec(memory_space=pltpu.HBM)
SEM_SPEC = pl.BlockSpec(memory_space=pltpu.SEMAPHORE)
SPLIT_PARAMS = pltpu.CompilerParams(has_side_effects=pltpu.SideEffectType.DATAFLOW_SIDE_EFFECTING)


def _split_copies(src_ref, land_ref, send_sems, recv_sems, all_to_all):
    x, y, c = _coords()
    me = 4 * x + 2 * y + c
    copies = []
    for m in range(1, N_DEV):
        peer, pidx = _peer(m)
        copies.append(pltpu.make_async_remote_copy(
            src_ref=src_ref.at[pidx] if all_to_all else src_ref, dst_ref=land_ref.at[me],
            send_sem=send_sems.at[m - 1], recv_sem=recv_sems.at[m - 1], device_id=peer, device_id_type=MESH))
    return copies


def copy_start(items, lands=None, *, all_to_all, name):
    n = len(items)
    if lands is None:
        lands = [lax.empty(a.shape if all_to_all else (N_DEV,) + a.shape, a.dtype) for a in items]

    def body(*refs):
        srcs, lnds, outs = refs[:n], refs[n:2 * n], refs[2 * n:]
        for i in range(n):
            for cp in _split_copies(srcs[i], lnds[i], outs[4 * i], outs[4 * i + 1], all_to_all):
                cp.start()
        outs[4 * n][...] = jnp.zeros((8, 128), F32)

    out_shape, out_specs, aliases = [], [], {}
    for i, (a, l) in enumerate(zip(items, lands)):
        out_shape += [pltpu.SemaphoreType.DMA((N_DEV - 1,)), pltpu.SemaphoreType.DMA((N_DEV - 1,)),
                      pltpu.HBM(a.shape, a.dtype), pltpu.HBM(l.shape, l.dtype)]
        out_specs += [SEM_SPEC, SEM_SPEC, HBM_SPEC, HBM_SPEC]
        aliases[i] = 4 * i + 2
        aliases[n + i] = 4 * i + 3
    out_shape.append(jax.ShapeDtypeStruct((8, 128), F32))
    out_specs.append(pl.BlockSpec(memory_space=pltpu.VMEM))
    hbm = lambda a: pltpu.with_memory_space_constraint(a, pltpu.HBM)
    res = pl.pallas_call(
        body, name=name, in_specs=[HBM_SPEC] * (2 * n), out_specs=out_specs, out_shape=out_shape,
        input_output_aliases=aliases, compiler_params=SPLIT_PARAMS,
    )(*[hbm(a) for a in items], *[hbm(l) for l in lands])
    return [tuple(res[4 * i:4 * i + 4]) for i in range(n)], res[4 * n]


def copy_wait(handle, after, *, all_to_all, name):
    send_sems, recv_sems, src, land = handle

    def body(src_ref, land_ref, send_ref, recv_ref, after_ref, src_out, got_ref):
        for cp in _split_copies(src_ref, land_ref, send_ref, recv_ref, all_to_all):
            cp.wait_send()
            cp.wait_recv()

    return pl.pallas_call(
        body, name=name, in_specs=[HBM_SPEC, HBM_SPEC, SEM_SPEC, SEM_SPEC, ANY_SPEC],
        out_specs=[HBM_SPEC, HBM_SPEC],
        out_shape=[pltpu.HBM(src.shape, src.dtype), pltpu.HBM(land.shape, land.dtype)],
        input_output_aliases={0: 0, 1: 1}, compiler_params=SPLIT_PARAMS,
    )(src, land, send_sems, recv_sems, after)


def kernel(x, meta_tokens, norm_mix_0, w_in_conv, conv_w, w_out_conv, norm_mlp_0, w_up_0, w_down_0, norm_mix_1, w_qkv, attn_sinks, w_o, norm_mlp_1, w_up_1, w_down_1, norm_final, loss_target, m_meta_tokens, m_norm_mix_0, m_w_in_conv, m_conv_w, m_w_out_conv, m_norm_mlp_0, m_w_up_0, m_w_down_0, m_norm_mix_1, m_w_qkv, m_attn_sinks, m_w_o, m_norm_mlp_1, m_w_up_1, m_w_down_1, m_norm_final, v_meta_tokens, v_norm_mix_0, v_w_in_conv, v_conv_w, v_w_out_conv, v_norm_mlp_0, v_w_up_0, v_w_down_0, v_norm_mix_1, v_w_qkv, v_attn_sinks, v_w_o, v_norm_mlp_1, v_w_up_1, v_w_down_1, v_norm_final):
    L, D = x.shape[1], x.shape[2]
    n_meta = meta_tokens.shape[0]
    pad = BLOCK - n_meta
    R = BLOCK + L
    n_q = D // HEAD_DIM
    n_kv = n_q // GROUP
    assert n_kv % 2 == 0 and L % BLOCK == 0 and D % 128 == 0
    x = x.reshape(L, D)
    target = loss_target.reshape(L, D)
    x_id, y_id, c_id = _coords()
    me = 4 * x_id + 2 * y_id + c_id

    col_names = ("in", "up0", "qkv", "up1")
    col_w = dict(zip(col_names, (w_in_conv, w_up_0, w_qkv, w_up_1)))
    row_names = ("out", "down0", "o", "down1")
    row_w = dict(zip(row_names, (w_out_conv, w_down_0, w_o, w_down_1)))
    me_arr = jnp.reshape(me, (1,)).astype(jnp.int32)
    natural = {k: col_w[k].T for k in col_names}
    natural.update(row_w)
    use_order = ("in", "out", "up0", "down0", "qkv", "o", "up1", "down1")
    placed = [cast_place(natural[k], me_arr, name="cast_" + k) for k in use_order]
    handles, token = copy_start([p[0] for p in placed], [p[1] for p in placed], all_to_all=False, name="gather_start")
    handles = dict(zip(use_order, handles))

    def weight(k, after):
        return copy_wait(handles[k], after, all_to_all=False, name="gather_wait_" + k)[1].reshape(-1, D)

    small_in = exchange([meta_tokens, conv_w], all_to_all=False, name="comm_gather")
    meta_full = jnp.transpose(small_in[0], (1, 0, 2)).reshape(n_meta, D)
    conv_full = jnp.transpose(small_in[1], (1, 0, 2)).reshape(conv_w.shape[0], D)

    vec = lambda a: a.reshape(1, D)
    pos = jnp.arange(R, dtype=F32) - pad
    inv = ROPE_THETA ** (-jnp.arange(0, HEAD_DIM, 2, dtype=F32) / HEAD_DIM)
    ang = pos[:, None] * inv[None, :]
    cos32, sin32 = jnp.cos(ang), jnp.sin(ang)
    cos = jnp.concatenate([cos32] * 4, axis=1)
    sin = jnp.concatenate([-sin32, sin32, -sin32, sin32], axis=1)

    W = {}
    head = jnp.concatenate([jnp.zeros((pad, D), F32), meta_full], axis=0)
    h0, n0 = first_norm(head, x, vec(norm_mix_0), token, name="norm0")
    W["in"] = weight("in", n0)
    bcu = mm(n0, W["in"], name="in_proj", out_dtype=BF16, b_rows_are_n=True)
    gated = conv_fwd(bcu, conv_full, name="conv_fwd")
    W["out"] = weight("out", gated)
    h1 = mm(gated, W["out"], name="out_proj", out_dtype=F32, b_rows_are_n=False, epi="add", extra=h0)
    n1 = norm_fwd(h1, vec(norm_mlp_0), h1, name="norm1")
    W["up0"] = weight("up0", n1)
    a0 = mm(n1, W["up0"], name="up0", out_dtype=BF16, b_rows_are_n=True, epi="relu")
    W["down0"] = weight("down0", a0)
    h2 = mm(a0, W["down0"], name="down0", out_dtype=F32, b_rows_are_n=False, epi="add", extra=h1, a_sq=True)
    n2 = norm_fwd(h2, vec(norm_mix_1), h2, name="norm2")
    W["qkv"] = weight("qkv", n2)
    qkv = mm(n2, W["qkv"], name="qkv_proj", out_dtype=F32, b_rows_are_n=True)
    q, kx, vx, kxt, vxt = rope_fwd(qkv, cos, sin, n_q=n_q, n_kv=n_kv, name="rope_fwd")
    o, lse = attn_fwd(q, kx, vxt, attn_sinks, pad=pad, name="attn_fwd")
    W["o"] = weight("o", o)
    h3 = mm(o, W["o"], name="o_proj", out_dtype=F32, b_rows_are_n=False, epi="add", extra=h2)
    n3 = norm_fwd(h3, vec(norm_mlp_1), h3, name="norm3")
    W["up1"] = weight("up1", n3)
    a1 = mm(n3, W["up1"], name="up1", out_dtype=BF16, b_rows_are_n=True, epi="relu")
    W["down1"] = weight("down1", a1)
    h4 = mm(a1, W["down1"], name="down1", out_dtype=F32, b_rows_are_n=False, epi="add", extra=h3, a_sq=True)

    loss_part, dh4, dh4b, dg_final = loss_bwd(h4, vec(norm_final), target, name="loss_bwd")
    loss = lax.psum(loss_part[0, 0], ("x", "y", "c"))

    sent = {}

    def scatter(k, dw):
        (sent[k],), tok = copy_start([dw.reshape(N_DEV, -1, D)], all_to_all=True, name="a2a_start_" + k)
        return tok

    t = scatter("down1", mm_tn(a1, dh4b, name="dw_down1", a_sq=True))
    dup1 = mm(dh4b, W["down1"], name="d_down1", out_dtype=BF16, b_rows_are_n=True, epi="mul2a", extra=a1, dep=t)
    t = scatter("up1", mm_tn(dup1, n3, name="dw_up1"))
    dn3 = mm(dup1, W["up1"], name="d_up1", out_dtype=F32, b_rows_are_n=False, dep=t)
    dh3, dh3b, dg_mlp1 = norm_bwd(dn3, h3, vec(norm_mlp_1), dh4, name="norm3_bwd")

    t = scatter("o", mm_tn(o, dh3b, name="dw_o"))
    do = mm(dh3b, W["o"], name="d_o", out_dtype=BF16, b_rows_are_n=True, dep=t)
    dq, dkx, dvx, dsinks = attn_bwd(q, kx, kxt, vx, o, do, lse, attn_sinks, pad=pad, name="attn_bwd")
    dqkv = rope_bwd(dq, dkx, dvx, cos, sin, n_q=n_q, n_kv=n_kv, name="rope_bwd")
    t = scatter("qkv", mm_tn(dqkv, n2, name="dw_qkv"))
    dn2 = mm(dqkv, W["qkv"], name="d_qkv", out_dtype=F32, b_rows_are_n=False, dep=t)
    dh2, dh2b, dg_mix1 = norm_bwd(dn2, h2, vec(norm_mix_1), dh3, name="norm2_bwd")

    t = scatter("down0", mm_tn(a0, dh2b, name="dw_down0", a_sq=True))
    dup0 = mm(dh2b, W["down0"], name="d_down0", out_dtype=BF16, b_rows_are_n=True, epi="mul2a", extra=a0, dep=t)
    t = scatter("up0", mm_tn(dup0, n1, name="dw_up0"))
    dn1 = mm(dup0, W["up0"], name="d_up0", out_dtype=F32, b_rows_are_n=False, dep=t)
    dh1, dh1b, dg_mlp0 = norm_bwd(dn1, h1, vec(norm_mlp_0), dh2, name="norm1_bwd")

    t = scatter("out", mm_tn(gated, dh1b, name="dw_out"))
    dgated = mm(dh1b, W["out"], name="d_out", out_dtype=F32, b_rows_are_n=True, dep=t)
    dbcu, dconv = conv_bwd(bcu, conv_full, dgated, name="conv_bwd")
    t = scatter("in", mm_tn(dbcu, n0, name="dw_in"))
    dn0 = mm(dbcu, W["in"], name="d_in", out_dtype=F32, b_rows_are_n=False, dep=t)
    dhead, dx, dg_mix0 = last_norm_bwd(dn0, h0, vec(norm_mix_0), dh1, name="norm0_bwd")
    grad_x = dx.reshape(1, L, D)

    recv = {}
    for k in ("down1", "up1", "o", "qkv", "down0", "up0", "out", "in"):
        sent[k], recv[k] = copy_wait(sent[k], dx, all_to_all=True, name="a2a_wait_" + k)

    n_sink = attn_sinks.shape[0]
    slab = jnp.concatenate([
        dg_mix0, dg_mlp0, dg_mix1, dg_mlp1, dg_final,
        jnp.pad(dsinks[0:1, :n_sink], ((0, 0), (0, D - n_sink))), jnp.zeros((2, D), F32),
        dconv, dhead[pad:BLOCK]], axis=0)
    slabs = exchange([slab], all_to_all=False, name="comm_small")[0]
    small = sum_parts(slabs, slabs, me_arr, name="sum_small")
    cols = D // N_DEV
    my_cols = lambda a: lax.dynamic_slice_in_dim(a, me * cols, cols, axis=1)

    grads, deltas, new_m, new_v = {}, {}, {}, {}

    def update(key, w, m, v, g, shape):
        s2 = (1, -1) if w.ndim == 1 else w.shape
        if g.ndim == 3:
            g_, d_, m_, v_ = adam_parts(w, m, v, g, sent[key_of[key]], me_arr, name="adam_" + key)
        else:
            g_ = g.reshape(s2)
            d_, m_, v_ = adam(w.reshape(s2), m.reshape(s2), v.reshape(s2), g_, name="adam_" + key)
        grads[key], deltas[key], new_m[key], new_v[key] = (t.reshape(shape) for t in (g_, d_, m_, v_))

    update("meta_tokens", meta_tokens, m_meta_tokens, v_meta_tokens, my_cols(small[16:16 + n_meta]), meta_tokens.shape)
    update("norm_mix_0", norm_mix_0, m_norm_mix_0, v_norm_mix_0, small[0], (D,))
    update("conv_w", conv_w, m_conv_w, v_conv_w, my_cols(small[8:8 + conv_w.shape[0]]), conv_w.shape)
    update("norm_mlp_0", norm_mlp_0, m_norm_mlp_0, v_norm_mlp_0, small[1], (D,))
    update("norm_mix_1", norm_mix_1, m_norm_mix_1, v_norm_mix_1, small[2], (D,))
    update("attn_sinks", attn_sinks, m_attn_sinks, v_attn_sinks, small[5, :n_sink], (n_sink,))
    update("norm_mlp_1", norm_mlp_1, m_norm_mlp_1, v_norm_mlp_1, small[3], (D,))
    update("norm_final", norm_final, m_norm_final, v_norm_final, small[4], (D,))
    big = {"in": ("w_in_conv", w_in_conv, m_w_in_conv, v_w_in_conv), "up0": ("w_up_0", w_up_0, m_w_up_0, v_w_up_0),
           "qkv": ("w_qkv", w_qkv, m_w_qkv, v_w_qkv), "up1": ("w_up_1", w_up_1, m_w_up_1, v_w_up_1),
           "out": ("w_out_conv", w_out_conv, m_w_out_conv, v_w_out_conv),
           "down0": ("w_down_0", w_down_0, m_w_down_0, v_w_down_0), "o": ("w_o", w_o, m_w_o, v_w_o),
           "down1": ("w_down_1", w_down_1, m_w_down_1, v_w_down_1)}
    key_of = {big[k][0]: k for k in big}
    for k in col_names:
        key, w, m, v = big[k]
        update(key, w, m, v, sum_parts(recv[k], sent[k], me_arr, name="sum_" + k).T, w.shape)
    for k in row_names:
        key, w, m, v = big[k]
        update(key, w, m, v, recv[k], w.shape)

    order = ("meta_tokens", "norm_mix_0", "w_in_conv", "conv_w", "w_out_conv", "norm_mlp_0", "w_up_0", "w_down_0",
             "norm_mix_1", "w_qkv", "attn_sinks", "w_o", "norm_mlp_1", "w_up_1", "w_down_1", "norm_final")
    return (loss, grad_x, *[grads[k] for k in order], *[deltas[k] for k in order],
            *[new_m[k] for k in order], *[new_v[k] for k in order])
```

```python
import functools

import jax
import jax.numpy as jnp
from jax import lax
from jax.experimental import pallas as pl
from jax.experimental.pallas import tpu as pltpu

F32 = jnp.float32
BF16 = jnp.bfloat16

HEAD_DIM = 64
GROUP = 8
BLOCK = 128
N_DEV = 8
RMS_EPS = 1e-5
NEG_INF = -1e30
ROPE_THETA = 10000.0
ADAM_LR = 0.001
ADAM_B1 = 0.9
ADAM_B2 = 0.999
ADAM_EPS = 1e-08
ADAM_WD = 0.01
ADAM_STEP = 10
VMEM_LIMIT = 60 * 1024 * 1024
MESH = pl.DeviceIdType.MESH


def _tile(n, target, mult):
    best = None
    for t in range(mult, min(n, target) + 1, mult):
        if n % t == 0:
            best = t
    return best if best is not None else n


def _params(sem):
    return pltpu.CompilerParams(dimension_semantics=sem, vmem_limit_bytes=VMEM_LIMIT)


ANY_SPEC = pl.BlockSpec(memory_space=pl.ANY)


LEFT_BLOCK_BYTES = 14 * 1024 * 1024


def mm(a, b, *, name, out_dtype, b_rows_are_n, epi=None, extra=None, a_sq=False, dep=None):
    M, K = a.shape
    N = b.shape[0] if b_rows_are_n else b.shape[1]
    tm = _tile(M, LEFT_BLOCK_BYTES // (2 * K), 16)
    tn = _tile(N, 512 if K <= 4096 else 256, 128)

    def body(*refs):
        a_ref, b_ref, e_ref, o_ref = refs[0], refs[1], refs[2], refs[-1]
        av = a_ref[...]
        if a_sq:
            av = av.astype(F32)
            av = (av * av).astype(BF16)
        dims = (((1,), (1,)), ((), ())) if b_rows_are_n else (((1,), (0,)), ((), ()))
        acc = lax.dot_general(av, b_ref[...], dims, preferred_element_type=F32)
        if epi == "relu":
            acc = jnp.maximum(acc, 0.0)
        elif epi == "mul2a":
            acc = acc * (2.0 * e_ref[...].astype(F32))
        elif epi == "add":
            acc = acc + e_ref[...]
        o_ref[...] = acc.astype(o_ref.dtype)

    b_spec = pl.BlockSpec((tn, K), lambda i, j: (j, 0)) if b_rows_are_n else pl.BlockSpec((K, tn), lambda i, j: (0, j))
    in_specs = [pl.BlockSpec((tm, K), lambda i, j: (i, 0)), b_spec]
    args = [a, b]
    if extra is not None:
        in_specs.append(pl.BlockSpec((tm, tn), lambda i, j: (i, j)))
        args.append(extra)
    if dep is not None:
        in_specs.append(ANY_SPEC)
        args.append(dep)
    return pl.pallas_call(
        body, name=name, grid=(M // tm, N // tn), in_specs=in_specs,
        out_specs=pl.BlockSpec((tm, tn), lambda i, j: (i, j)),
        out_shape=jax.ShapeDtypeStruct((M, N), out_dtype),
        compiler_params=_params(("parallel", "parallel")),
    )(*args)


def _rms_parts(h):
    rstd = lax.rsqrt(jnp.mean(h * h, axis=-1, keepdims=True) + RMS_EPS)
    return h * rstd, rstd


def mm_tn(a, b, *, name, a_sq=False):
    T, M = a.shape
    N = b.shape[1]
    tm = _tile(M, 1024, 128)
    tk = _tile(T, 2080, 16)
    nk = T // tk

    tc = _tile(tm, 256, 128)

    def body(a_ref, b_ref, o_ref, acc_ref):
        k = pl.program_id(1)

        def chunks(first):
            bv = b_ref[...]
            for r0 in range(0, tm, tc):
                av = a_ref[:, r0:r0 + tc]
                if a_sq:
                    av = av.astype(F32)
                    av = (av * av).astype(BF16)
                part = lax.dot_general(av, bv, (((0,), (0,)), ((), ())), preferred_element_type=F32)
                if first:
                    acc_ref[r0:r0 + tc, :] = part
                else:
                    acc_ref[r0:r0 + tc, :] += part

        @pl.when(k == 0)
        def _():
            chunks(True)

        @pl.when(k > 0)
        def _():
            chunks(False)

        @pl.when(k == nk - 1)
        def _():
            o_ref[...] = acc_ref[...].astype(BF16)

    return pl.pallas_call(
        body, name=name, grid=(M // tm, nk),
        in_specs=[pl.BlockSpec((tk, tm), lambda i, k: (k, i)), pl.BlockSpec((tk, N), lambda i, k: (k, 0))],
        out_specs=pl.BlockSpec((tm, N), lambda i, k: (i, 0)),
        out_shape=jax.ShapeDtypeStruct((M, N), BF16),
        scratch_shapes=[pltpu.VMEM((tm, N), F32)],
        compiler_params=_params(("parallel", "arbitrary")),
    )(a, b)


def norm_fwd(h, g, dep, *, name):
    R, D = h.shape
    tr = _tile(R, 320, 16)

    def body(h_ref, g_ref, dep_ref, n_ref):
        xhat, _ = _rms_parts(h_ref[...])
        n_ref[...] = (xhat * g_ref[...]).astype(BF16)

    return pl.pallas_call(
        body, name=name, grid=(R // tr,),
        in_specs=[pl.BlockSpec((tr, D), lambda i: (i, 0)), pl.BlockSpec((1, D), lambda i: (0, 0)), ANY_SPEC],
        out_specs=pl.BlockSpec((tr, D), lambda i: (i, 0)),
        out_shape=jax.ShapeDtypeStruct((R, D), BF16),
        compiler_params=_params(("parallel",)),
    )(h, g, dep)


def norm_bwd(dn, h, g, dres, *, name):
    R, D = h.shape
    tr = _tile(R, 320, 16)

    def body(dn_ref, h_ref, g_ref, dres_ref, dh_ref, dhb_ref, dg_ref):
        i = pl.program_id(0)
        dn = dn_ref[...]
        xhat, rstd = _rms_parts(h_ref[...])
        dxh = dn * g_ref[...]
        dh = dres_ref[...] + rstd * (dxh - xhat * jnp.mean(dxh * xhat, axis=-1, keepdims=True))
        dh_ref[...] = dh
        dhb_ref[...] = dh.astype(BF16)
        dg = jnp.sum(dn * xhat, axis=0, keepdims=True)

        @pl.when(i == 0)
        def _():
            dg_ref[...] = dg

        @pl.when(i > 0)
        def _():
            dg_ref[...] += dg

    row = pl.BlockSpec((tr, D), lambda i: (i, 0))
    vec = pl.BlockSpec((1, D), lambda i: (0, 0))
    return pl.pallas_call(
        body, name=name, grid=(R // tr,), in_specs=[row, row, vec, row], out_specs=[row, row, vec],
        out_shape=[jax.ShapeDtypeStruct((R, D), F32), jax.ShapeDtypeStruct((R, D), BF16),
                   jax.ShapeDtypeStruct((1, D), F32)],
        compiler_params=_params(("arbitrary",)),
    )(dn, h, g, dres)


def first_norm(head, x, g, dep, *, name):
    L, D = x.shape
    nb = L // BLOCK + 1

    def body(head_ref, x_ref, g_ref, dep_ref, h_ref, n_ref):
        i = pl.program_id(0)
        hv = jnp.where(i == 0, head_ref[...], x_ref[...])
        h_ref[...] = hv
        xhat, _ = _rms_parts(hv)
        n_ref[...] = (xhat * g_ref[...]).astype(BF16)

    blk = pl.BlockSpec((BLOCK, D), lambda i: (i, 0))
    return pl.pallas_call(
        body, name=name, grid=(nb,),
        in_specs=[pl.BlockSpec((BLOCK, D), lambda i: (0, 0)),
                  pl.BlockSpec((BLOCK, D), lambda i: (jnp.maximum(i - 1, 0), 0)),
                  pl.BlockSpec((1, D), lambda i: (0, 0)), ANY_SPEC],
        out_specs=[blk, blk],
        out_shape=[jax.ShapeDtypeStruct((BLOCK + L, D), F32), jax.ShapeDtypeStruct((BLOCK + L, D), BF16)],
        compiler_params=_params(("parallel",)),
    )(head, x, g, dep)


def last_norm_bwd(dn, h, g, dres, *, name):
    R, D = h.shape
    nb = R // BLOCK

    def body(dn_ref, h_ref, g_ref, dres_ref, dhead_ref, dx_ref, dg_ref):
        i = pl.program_id(0)
        dn = dn_ref[...]
        xhat, rstd = _rms_parts(h_ref[...])
        dxh = dn * g_ref[...]
        dh = dres_ref[...] + rstd * (dxh - xhat * jnp.mean(dxh * xhat, axis=-1, keepdims=True))
        dg = jnp.sum(dn * xhat, axis=0, keepdims=True)
        dx_ref[...] = dh

        @pl.when(i == 0)
        def _():
            dhead_ref[...] = dh
            dg_ref[...] = dg

        @pl.when(i > 0)
        def _():
            dg_ref[...] += dg

    blk = pl.BlockSpec((BLOCK, D), lambda i: (i, 0))
    vec = pl.BlockSpec((1, D), lambda i: (0, 0))
    return pl.pallas_call(
        body, name=name, grid=(nb,), in_specs=[blk, blk, vec, blk],
        out_specs=[pl.BlockSpec((BLOCK, D), lambda i: (0, 0)),
                   pl.BlockSpec((BLOCK, D), lambda i: (jnp.maximum(i - 1, 0), 0)), vec],
        out_shape=[jax.ShapeDtypeStruct((BLOCK, D), F32), jax.ShapeDtypeStruct((R - BLOCK, D), F32),
                   jax.ShapeDtypeStruct((1, D), F32)],
        compiler_params=_params(("arbitrary",)),
    )(dn, h, g, dres)


HALO = 16


def conv_fwd(bcu, conv_w, *, name):
    R, D3 = bcu.shape
    D = D3 // 3
    tr = _tile(R, 320, 16)
    tc = _tile(D, 512, 128)
    nc = D // tc
    hb = tr // HALO

    def body(b_ref, c_ref, u_ref, ch_ref, uh_ref, w_ref, o_ref, vbuf):
        i = pl.program_id(0)
        v = c_ref[...].astype(F32) * u_ref[...].astype(F32)
        vbuf[0:HALO, :] = jnp.where(i > 0, ch_ref[...].astype(F32) * uh_ref[...].astype(F32), 0.0)
        vbuf[HALO:HALO + tr, :] = v
        v1 = vbuf[HALO - 1:HALO - 1 + tr, :]
        v2 = vbuf[HALO - 2:HALO - 2 + tr, :]
        conv = w_ref[0:1, :] * v2 + w_ref[1:2, :] * v1 + w_ref[2:3, :] * v
        o_ref[...] = (b_ref[...].astype(F32) * conv).astype(BF16)

    def prev(i):
        return jnp.maximum(i * hb - 1, 0)

    return pl.pallas_call(
        body, name=name, grid=(R // tr, nc),
        in_specs=[pl.BlockSpec((tr, tc), lambda i, j: (i, j)),
                  pl.BlockSpec((tr, tc), lambda i, j: (i, nc + j)),
                  pl.BlockSpec((tr, tc), lambda i, j: (i, 2 * nc + j)),
                  pl.BlockSpec((HALO, tc), lambda i, j: (prev(i), nc + j)),
                  pl.BlockSpec((HALO, tc), lambda i, j: (prev(i), 2 * nc + j)),
                  pl.BlockSpec((3, tc), lambda i, j: (0, j))],
        out_specs=pl.BlockSpec((tr, tc), lambda i, j: (i, j)),
        out_shape=jax.ShapeDtypeStruct((R, D), BF16),
        scratch_shapes=[pltpu.VMEM((HALO + tr, tc), F32)],
        compiler_params=_params(("parallel", "parallel")),
    )(bcu, bcu, bcu, bcu, bcu, conv_w)


def conv_bwd(bcu, conv_w, dg, *, name):
    R, D3 = bcu.shape
    D = D3 // 3
    tr = _tile(R, 320, 16)
    tc = _tile(D, 512, 128)
    hb = tr // HALO
    nt = R // tr
    last_h = R // HALO - 1

    def body(x_ref, xp_ref, xn_ref, dg_ref, dgn_ref, w_ref, o_ref, dw_ref, vbuf, dbuf):
        i = pl.program_id(0)
        row8 = lax.broadcasted_iota(jnp.int32, (8, tc), 0)
        for c0 in range(0, D, tc):
            cb, cc, cu = slice(c0, c0 + tc), slice(D + c0, D + c0 + tc), slice(2 * D + c0, 2 * D + c0 + tc)
            w0, w1, w2 = w_ref[0:1, cb], w_ref[1:2, cb], w_ref[2:3, cb]
            b = x_ref[:, cb].astype(F32)
            c = x_ref[:, cc].astype(F32)
            u = x_ref[:, cu].astype(F32)
            v = c * u
            vbuf[0:HALO, :] = jnp.where(i > 0, xp_ref[:, cc].astype(F32) * xp_ref[:, cu].astype(F32), 0.0)
            vbuf[HALO:HALO + tr, :] = v
            v1 = vbuf[HALO - 1:HALO - 1 + tr, :]
            v2 = vbuf[HALO - 2:HALO - 2 + tr, :]
            dgv = dg_ref[:, cb]
            o_ref[:, cb] = (dgv * (w0 * v2 + w1 * v1 + w2 * v)).astype(BF16)
            dconv = dgv * b
            dbuf[0:tr, :] = dconv
            dbuf[tr:tr + HALO, :] = jnp.where(i < nt - 1, dgn_ref[:, cb] * xn_ref[:, cb].astype(F32), 0.0)
            dv = w2 * dconv + w1 * dbuf[1:1 + tr, :] + w0 * dbuf[2:2 + tr, :]
            o_ref[:, cc] = (dv * u).astype(BF16)
            o_ref[:, cu] = (dv * c).astype(BF16)
            dw = jnp.where(row8 == 0, jnp.sum(dconv * v2, axis=0, keepdims=True),
                           jnp.where(row8 == 1, jnp.sum(dconv * v1, axis=0, keepdims=True),
                                     jnp.where(row8 == 2, jnp.sum(dconv * v, axis=0, keepdims=True), 0.0)))

            @pl.when(i == 0)
            def _():
                dw_ref[:, cb] = dw

            @pl.when(i > 0)
            def _():
                dw_ref[:, cb] += dw

    def prev(i):
        return jnp.maximum(i * hb - 1, 0)

    def nxt(i):
        return jnp.minimum((i + 1) * hb, last_h)

    return pl.pallas_call(
        body, name=name, grid=(nt,),
        in_specs=[pl.BlockSpec((tr, D3), lambda i: (i, 0)),
                  pl.BlockSpec((HALO, D3), lambda i: (prev(i), 0)),
                  pl.BlockSpec((HALO, D3), lambda i: (nxt(i), 0)),
                  pl.BlockSpec((tr, D), lambda i: (i, 0)),
                  pl.BlockSpec((HALO, D), lambda i: (nxt(i), 0)),
                  pl.BlockSpec((3, D), lambda i: (0, 0))],
        out_specs=[pl.BlockSpec((tr, D3), lambda i: (i, 0)), pl.BlockSpec((8, D), lambda i: (0, 0))],
        out_shape=[jax.ShapeDtypeStruct((R, D3), BF16), jax.ShapeDtypeStruct((8, D), F32)],
        scratch_shapes=[pltpu.VMEM((HALO + tr, tc), F32), pltpu.VMEM((tr + HALO, tc), F32)],
        compiler_params=_params(("arbitrary",)),
    )(bcu, bcu, bcu, dg, dg, conv_w)


def _swap32(x):
    w = x.shape[1]
    lane = lax.broadcasted_iota(jnp.int32, x.shape, 1)
    return jnp.where((lane & (HEAD_DIM - 1)) < HEAD_DIM // 2, pltpu.roll(x, w - HEAD_DIM // 2, axis=1),
                     pltpu.roll(x, HEAD_DIM // 2, axis=1))


def _rope(x, cos, sin):
    return x * cos + _swap32(x) * sin


def rope_fwd(qkv, cos, sin, *, n_q, n_kv, name):
    R, W = qkv.shape
    qw = n_q * HEAD_DIM
    kw = n_kv * HEAD_DIM
    tr = _tile(R, 640, 128)

    def expand(y, ref, ref_t, c):
        lane = lax.broadcasted_iota(jnp.int32, y.shape, 1)
        lo = jnp.where(lane < HEAD_DIM, y, 0.0)
        hi = jnp.where(lane >= HEAD_DIM, y, 0.0)
        tiles = (lo, pltpu.roll(lo, HEAD_DIM, axis=1), pltpu.roll(hi, HEAD_DIM, axis=1), hi)
        for t, tile in enumerate(tiles):
            c0 = 512 * c + 128 * t
            ref[:, c0:c0 + 128] = tile.astype(BF16)
            ref_t[c0:c0 + 128, :] = tile.T.astype(BF16)

    def body(x_ref, c_ref, s_ref, q_ref, k_ref, v_ref, kt_ref, vt_ref):
        cos = c_ref[...]
        sin = s_ref[...]
        for c in range(qw // 128):
            x = x_ref[:, 128 * c:128 * (c + 1)]
            q_ref[:, 128 * c:128 * (c + 1)] = (_rope(x, cos, sin) * (HEAD_DIM ** -0.5)).astype(BF16)
        for c in range(kw // 128):
            expand(_rope(x_ref[:, qw + 128 * c:qw + 128 * (c + 1)], cos, sin), k_ref, kt_ref, c)
            expand(x_ref[:, qw + kw + 128 * c:qw + kw + 128 * (c + 1)], v_ref, vt_ref, c)

    row = lambda w: pl.BlockSpec((tr, w), lambda i: (i, 0))
    col = pl.BlockSpec((4 * kw, tr), lambda i: (0, i))
    return pl.pallas_call(
        body, name=name, grid=(R // tr,),
        in_specs=[row(W), row(128), row(128)],
        out_specs=[row(qw), row(4 * kw), row(4 * kw), col, col],
        out_shape=[jax.ShapeDtypeStruct((R, qw), BF16), jax.ShapeDtypeStruct((R, 4 * kw), BF16),
                   jax.ShapeDtypeStruct((R, 4 * kw), BF16), jax.ShapeDtypeStruct((4 * kw, R), BF16),
                   jax.ShapeDtypeStruct((4 * kw, R), BF16)],
        compiler_params=_params(("parallel",)),
    )(qkv, cos, sin)


def rope_bwd(dq, dkx, dvx, cos, sin, *, n_q, n_kv, name):
    R = dq.shape[0]
    qw = n_q * HEAD_DIM
    kw = n_kv * HEAD_DIM
    W = qw + 2 * kw
    tr = _tile(R, 320, 16)

    def fold(ref, c):
        lane = lax.broadcasted_iota(jnp.int32, (tr, 128), 1)
        x0 = ref[:, 128 * (2 * c):128 * (2 * c + 1)]
        x1 = ref[:, 128 * (2 * c + 1):128 * (2 * c + 2)]
        f0 = x0 + pltpu.roll(x0, HEAD_DIM, axis=1)
        f1 = x1 + pltpu.roll(x1, HEAD_DIM, axis=1)
        return jnp.where(lane < HEAD_DIM, f0, f1)

    def body(dq_ref, dk_ref, dv_ref, c_ref, s_ref, o_ref):
        cos = c_ref[...]
        nsin = -s_ref[...]
        for c in range(qw // 128):
            y = dq_ref[:, 128 * c:128 * (c + 1)]
            o_ref[:, 128 * c:128 * (c + 1)] = (_rope(y, cos, nsin) * (HEAD_DIM ** -0.5)).astype(BF16)
        for c in range(kw // 128):
            o_ref[:, qw + 128 * c:qw + 128 * (c + 1)] = _rope(fold(dk_ref, c), cos, nsin).astype(BF16)
            o_ref[:, qw + kw + 128 * c:qw + kw + 128 * (c + 1)] = fold(dv_ref, c).astype(BF16)

    row = lambda w: pl.BlockSpec((tr, w), lambda i: (i, 0))
    return pl.pallas_call(
        body, name=name, grid=(R // tr,),
        in_specs=[row(qw), row(2 * kw), row(2 * kw), row(128), row(128)],
        out_specs=row(W),
        out_shape=jax.ShapeDtypeStruct((R, W), BF16),
        compiler_params=_params(("parallel",)),
    )(dq, dkx, dvx, cos, sin)


def _band_bias(n, pad):
    key = lax.broadcasted_iota(jnp.int32, (2 * BLOCK, BLOCK), 0)
    qry = lax.broadcasted_iota(jnp.int32, (2 * BLOCK, BLOCK), 1)
    kmin = jnp.where(n == 0, BLOCK + pad, jnp.where(n == 1, pad, 0))
    allowed = (key > qry) & (key <= qry + BLOCK) & (key >= kmin)
    return jnp.where(allowed, 0.0, NEG_INF).astype(F32)


def _dot_nt(a, b):
    return lax.dot_general(a, b, (((1,), (1,)), ((), ())), preferred_element_type=F32)


def _band(prev_ref, cur_ref, c0):
    return jnp.concatenate([prev_ref[:, c0:c0 + 128], cur_ref[:, c0:c0 + 128]], axis=0)


def _band_t(prev_ref, cur_ref, r0):
    return jnp.concatenate([prev_ref[r0:r0 + 128, :], cur_ref[r0:r0 + 128, :]], axis=1)


def attn_fwd(q, kx, vxt, sinks, *, pad, name):
    R, qw = q.shape
    n_q = qw // HEAD_DIM
    n_kv = kx.shape[1] // 256
    nb = R // BLOCK

    def body(s_ref, q_ref, kc_ref, kp_ref, vc_ref, vp_ref, o_ref, l_ref):
        n = pl.program_id(0)
        bias = _band_bias(n, pad)
        row = lax.broadcasted_iota(jnp.int32, (128, BLOCK), 0)

        def softmax(k2, q2, sink):
            st = _dot_nt(k2, q2) + bias
            m = jnp.maximum(jnp.max(st, axis=0, keepdims=True), sink)
            e = jnp.exp(st - m)
            den = jnp.sum(e, axis=0, keepdims=True) + jnp.exp(sink - m)
            return e.astype(BF16), 1.0 / den, m + jnp.log(den)

        for g in range(n_kv):
            k2a = _band(kp_ref, kc_ref, 256 * g)
            k2b = _band(kp_ref, kc_ref, 256 * g + 128)
            v2a = _band_t(vp_ref, vc_ref, 256 * g)
            v2b = _band_t(vp_ref, vc_ref, 256 * g + 128)
            for p in range(GROUP // 2):
                c0 = 128 * (g * (GROUP // 2) + p)
                he = GROUP * g + 2 * p
                q2 = q_ref[:, c0:c0 + 128]
                ee, re, le = softmax(k2a, q2, s_ref[he])
                eo, ro, lo = softmax(k2b, q2, s_ref[he + 1])
                o2t = jnp.dot(v2a, ee, preferred_element_type=F32) + jnp.dot(v2b, eo, preferred_element_type=F32)
                o2t = o2t * jnp.where(row < HEAD_DIM, re, ro)
                o_ref[:, c0:c0 + 128] = o2t.T.astype(BF16)
                l_ref[he:he + 1, :] = le
                l_ref[he + 1:he + 2, :] = lo

    cur = lambda w: pl.BlockSpec((BLOCK, w), lambda n: (n, 0))
    prv = lambda w: pl.BlockSpec((BLOCK, w), lambda n: (jnp.maximum(n - 1, 0), 0))
    cur_t = lambda h: pl.BlockSpec((h, BLOCK), lambda n: (0, n))
    prv_t = lambda h: pl.BlockSpec((h, BLOCK), lambda n: (0, jnp.maximum(n - 1, 0)))
    kxw = kx.shape[1]
    return pl.pallas_call(
        body, name=name, grid=(nb,),
        in_specs=[pl.BlockSpec(memory_space=pltpu.SMEM), cur(qw), cur(kxw), prv(kxw), cur_t(kxw), prv_t(kxw)],
        out_specs=[cur(qw), cur_t(n_q)],
        out_shape=[jax.ShapeDtypeStruct((R, qw), BF16), jax.ShapeDtypeStruct((n_q, R), F32)],
        compiler_params=_params(("parallel",)),
    )(sinks, q, kx, kx, vxt, vxt)


def attn_bwd(q, kx, kxt, vx, o, do, lse, sinks, *, pad, name):
    R, qw = q.shape
    n_q = qw // HEAD_DIM
    n_kv = kx.shape[1] // 256
    nb = R // BLOCK
    kw2 = n_kv * 128

    def body(s_ref, q_ref, do_ref, o_ref, l_ref, kc_ref, kp_ref, ktc_ref, ktp_ref, vc_ref, vp_ref,
             dq_ref, dk_ref, dv_ref, ds_ref, cdk, cdv):
        n = pl.program_id(0)

        @pl.when(n == 0)
        def _():
            cdk[...] = jnp.zeros_like(cdk)
            cdv[...] = jnp.zeros_like(cdv)
            ds_ref[...] = jnp.zeros_like(ds_ref)

        @pl.when(n < nb)
        def _():
            bias = _band_bias(n, pad)
            lane2 = lax.broadcasted_iota(jnp.int32, (2 * BLOCK, 128), 1)
            lane1 = lax.broadcasted_iota(jnp.int32, (1, 128), 1)
            sel_r = lax.broadcasted_iota(jnp.int32, (8, 128), 0)
            sel_l = lax.broadcasted_iota(jnp.int32, (8, 128), 1)
            sel = (((sel_r == 0) & (sel_l < HEAD_DIM)) | ((sel_r == 1) & (sel_l >= HEAD_DIM))).astype(BF16)
            dsink = jnp.zeros((1, 128), F32)
            for g in range(n_kv):
                k2a = _band(kp_ref, kc_ref, 256 * g)
                k2b = _band(kp_ref, kc_ref, 256 * g + 128)
                kt2a = _band_t(ktp_ref, ktc_ref, 256 * g)
                kt2b = _band_t(ktp_ref, ktc_ref, 256 * g + 128)
                v2a = _band(vp_ref, vc_ref, 256 * g)
                v2b = _band(vp_ref, vc_ref, 256 * g + 128)
                dka = jnp.zeros((2 * BLOCK, 128), F32)
                dkb = jnp.zeros((2 * BLOCK, 128), F32)
                dva = jnp.zeros((2 * BLOCK, 128), F32)
                dvb = jnp.zeros((2 * BLOCK, 128), F32)
                for p in range(GROUP // 2):
                    c0 = 128 * (g * (GROUP // 2) + p)
                    he = GROUP * g + 2 * p
                    q2 = q_ref[:, c0:c0 + 128]
                    do2 = do_ref[:, c0:c0 + 128]
                    prod = do2.astype(F32) * o_ref[:, c0:c0 + 128].astype(F32)
                    prod_hi = prod.astype(BF16)
                    prod_lo = (prod - prod_hi.astype(F32)).astype(BF16)
                    deltas = _dot_nt(sel, prod_hi) + _dot_nt(sel, prod_lo)
                    halves = ((k2a, kt2a, v2a, he, 0), (k2b, kt2b, v2b, he + 1, 1))
                    dq2t = jnp.zeros((128, BLOCK), F32)
                    outs = []
                    for (k2, kt2, v2, h, r) in halves:
                        lse_h = l_ref[h:h + 1, :]
                        delta = deltas[r:r + 1, :]
                        pt = jnp.exp(_dot_nt(k2, q2) + bias - lse_h)
                        dpt = _dot_nt(v2, do2)
                        dst = (pt * (dpt - delta)).astype(BF16)
                        ptb = pt.astype(BF16)
                        dq2t = dq2t + jnp.dot(kt2, dst, preferred_element_type=F32)
                        outs.append((jnp.dot(dst, q2, preferred_element_type=F32),
                                     jnp.dot(ptb, do2, preferred_element_type=F32)))
                        psink = jnp.exp(s_ref[h] - lse_h)
                        tot = jnp.sum(psink * delta, axis=1, keepdims=True)
                        dsink = dsink - jnp.where(lane1 == h, tot, 0.0)
                    dq_ref[:, c0:c0 + 128] = dq2t.T
                    dka = dka + outs[0][0]
                    dva = dva + outs[0][1]
                    dkb = dkb + outs[1][0]
                    dvb = dvb + outs[1][1]
                gc = pl.ds(128 * g, 128)
                dk2 = jnp.where(lane2 < HEAD_DIM, dka, dkb)
                dv2 = jnp.where(lane2 < HEAD_DIM, dva, dvb)
                dk_ref[:, gc] = cdk[:, gc] + dk2[0:BLOCK]
                dv_ref[:, gc] = cdv[:, gc] + dv2[0:BLOCK]
                cdk[:, gc] = dk2[BLOCK:2 * BLOCK]
                cdv[:, gc] = dv2[BLOCK:2 * BLOCK]
            ds_ref[0:1, :] += dsink

        @pl.when(n == nb)
        def _():
            dk_ref[...] = cdk[...]
            dv_ref[...] = cdv[...]

    cur = lambda w: pl.BlockSpec((BLOCK, w), lambda n: (jnp.minimum(n, nb - 1), 0))
    prv = lambda w: pl.BlockSpec((BLOCK, w), lambda n: (jnp.clip(n - 1, 0, nb - 1), 0))
    cur_t = lambda h: pl.BlockSpec((h, BLOCK), lambda n: (0, jnp.minimum(n, nb - 1)))
    prv_t = lambda h: pl.BlockSpec((h, BLOCK), lambda n: (0, jnp.clip(n - 1, 0, nb - 1)))
    kxw = kx.shape[1]
    return pl.pallas_call(
        body, name=name, grid=(nb + 1,),
        in_specs=[pl.BlockSpec(memory_space=pltpu.SMEM), cur(qw), cur(qw), cur(qw), cur_t(n_q),
                  cur(kxw), prv(kxw), cur_t(kxw), prv_t(kxw), cur(kxw), prv(kxw)],
        out_specs=[cur(qw), prv(kw2), prv(kw2), pl.BlockSpec((8, 128), lambda n: (0, 0))],
        out_shape=[jax.ShapeDtypeStruct((R, qw), F32), jax.ShapeDtypeStruct((R, kw2), F32),
                   jax.ShapeDtypeStruct((R, kw2), F32), jax.ShapeDtypeStruct((8, 128), F32)],
        scratch_shapes=[pltpu.VMEM((BLOCK, kw2), F32), pltpu.VMEM((BLOCK, kw2), F32)],
        compiler_params=_params(("arbitrary",)),
    )(sinks, q, do, o, lse, kx, kx, kxt, kxt, vx, vx)


def loss_bwd(h, g, target, *, name):
    R, D = h.shape
    nb = R // BLOCK

    def body(h_ref, g_ref, t_ref, loss_ref, dh_ref, dhb_ref, dg_ref):
        n = pl.program_id(0)
        xhat, rstd = _rms_parts(h_ref[...])
        gv = g_ref[...]
        diff = jnp.where(n > 0, xhat * gv - t_ref[...], 0.0)
        part = (0.5 / D) * jnp.sum(jnp.sum(diff * diff, axis=1, keepdims=True), axis=0, keepdims=True)
        dout = diff * (1.0 / D)
        dxh = dout * gv
        dh = rstd * (dxh - xhat * jnp.mean(dxh * xhat, axis=-1, keepdims=True))
        dh_ref[...] = dh
        dhb_ref[...] = dh.astype(BF16)
        dg = jnp.sum(dout * xhat, axis=0, keepdims=True)

        @pl.when(n == 0)
        def _():
            loss_ref[...] = jnp.zeros_like(loss_ref) + part
            dg_ref[...] = dg

        @pl.when(n > 0)
        def _():
            loss_ref[...] += part
            dg_ref[...] += dg

    blk = pl.BlockSpec((BLOCK, D), lambda n: (n, 0))
    return pl.pallas_call(
        body, name=name, grid=(nb,),
        in_specs=[blk, pl.BlockSpec((1, D), lambda n: (0, 0)),
                  pl.BlockSpec((BLOCK, D), lambda n: (jnp.maximum(n - 1, 0), 0))],
        out_specs=[pl.BlockSpec((8, 128), lambda n: (0, 0)), blk, blk, pl.BlockSpec((1, D), lambda n: (0, 0))],
        out_shape=[jax.ShapeDtypeStruct((8, 128), F32), jax.ShapeDtypeStruct((R, D), F32),
                   jax.ShapeDtypeStruct((R, D), BF16), jax.ShapeDtypeStruct((1, D), F32)],
        compiler_params=_params(("arbitrary",)),
    )(h, g, target)


def _adam_math(w, g, m, v):
    m = ADAM_B1 * m + (1.0 - ADAM_B1) * g
    v = ADAM_B2 * v + (1.0 - ADAM_B2) * (g * g)
    m_hat = m / (1.0 - ADAM_B1 ** ADAM_STEP)
    v_hat = v / (1.0 - ADAM_B2 ** ADAM_STEP)
    delta = -ADAM_LR * (m_hat / (jnp.sqrt(v_hat) + ADAM_EPS) + ADAM_WD * w)
    return delta, m, v


def adam(w, m, v, g, *, name):
    r, C = w.shape
    tr = _tile(r, 128, 8)

    def body(w_ref, m_ref, v_ref, g_ref, d_ref, mo_ref, vo_ref):
        d_ref[...], mo_ref[...], vo_ref[...] = _adam_math(w_ref[...], g_ref[...], m_ref[...], v_ref[...])

    blk = pl.BlockSpec((tr, C), lambda i: (i, 0))
    return pl.pallas_call(
        body, name=name, grid=(r // tr,), in_specs=[blk] * 4, out_specs=[blk] * 3,
        out_shape=[jax.ShapeDtypeStruct((r, C), F32)] * 3,
        compiler_params=_params(("parallel",)),
    )(w, m, v, g)


def _sum_blocks(p_ref, own_ref, me):
    acc = None
    for s in range(N_DEV):
        blk = jnp.where(me == s, own_ref[...], p_ref[s]).astype(F32)
        acc = blk if acc is None else acc + blk
    return acc


def _parts_specs(r, C, tr):
    blk = pl.BlockSpec((tr, C), lambda i, me: (i, 0))
    parts = pl.BlockSpec((N_DEV, tr, C), lambda i, me: (0, i, 0))
    own = pl.BlockSpec((None, tr, C), lambda i, me: (me[0], i, 0))
    return blk, parts, own


def adam_parts(w, m, v, parts, own, me, *, name):
    r, C = w.shape
    tr = _tile(r, 128, 8)

    def body(me_ref, w_ref, m_ref, v_ref, p_ref, own_ref, go_ref, d_ref, mo_ref, vo_ref):
        gv = _sum_blocks(p_ref, own_ref, me_ref[0])
        go_ref[...] = gv
        d_ref[...], mo_ref[...], vo_ref[...] = _adam_math(w_ref[...], gv, m_ref[...], v_ref[...])

    blk, pblk, oblk = _parts_specs(r, C, tr)
    return pl.pallas_call(
        body, name=name,
        grid_spec=pltpu.PrefetchScalarGridSpec(num_scalar_prefetch=1, grid=(r // tr,),
                                               in_specs=[blk, blk, blk, pblk, oblk], out_specs=[blk] * 4),
        out_shape=[jax.ShapeDtypeStruct((r, C), F32)] * 4,
        compiler_params=_params(("parallel",)),
    )(me, w, m, v, parts, own)


def sum_parts(parts, own, me, *, name):
    _, r, C = parts.shape
    tr = _tile(r, 128, 8)

    def body(me_ref, p_ref, own_ref, o_ref):
        o_ref[...] = _sum_blocks(p_ref, own_ref, me_ref[0])

    blk, pblk, oblk = _parts_specs(r, C, tr)
    return pl.pallas_call(
        body, name=name,
        grid_spec=pltpu.PrefetchScalarGridSpec(num_scalar_prefetch=1, grid=(r // tr,),
                                               in_specs=[pblk, oblk], out_specs=blk),
        out_shape=jax.ShapeDtypeStruct((r, C), F32),
        compiler_params=_params(("parallel",)),
    )(me, parts, own)


def cast_place(w, me, *, name):
    r, C = w.shape
    tr = _tile(r, 256, 16)

    def body(me_ref, w_ref, s_ref, l_ref):
        v = w_ref[...].astype(BF16)
        s_ref[...] = v
        l_ref[...] = v

    blk = pl.BlockSpec((tr, C), lambda i, me: (i, 0))
    return pl.pallas_call(
        body, name=name,
        grid_spec=pltpu.PrefetchScalarGridSpec(
            num_scalar_prefetch=1, grid=(r // tr,), in_specs=[blk],
            out_specs=[blk, pl.BlockSpec((None, tr, C), lambda i, me: (me[0], i, 0))]),
        out_shape=[jax.ShapeDtypeStruct((r, C), BF16), jax.ShapeDtypeStruct((N_DEV, r, C), BF16)],
        compiler_params=_params(("parallel",)),
    )(me, w)


def _coords():
    return lax.axis_index("x"), lax.axis_index("y"), lax.axis_index("c")


def _peer(m):
    x, y, c = _coords()
    px = 1 - x if m & 4 else x
    py = 1 - y if m & 2 else y
    pc = 1 - c if m & 1 else c
    return (px, py, pc), 4 * px + 2 * py + pc


def exchange(items, *, all_to_all, name):
    n = len(items)
    if all_to_all:
        out_shape = [jax.ShapeDtypeStruct(a.shape, a.dtype) for a in items]
    else:
        out_shape = [jax.ShapeDtypeStruct((N_DEV,) + a.shape, a.dtype) for a in items]

    def body(*refs):
        ins, outs = refs[:n], refs[n:2 * n]
        send_sems, recv_sems, local_sems = refs[2 * n:]
        x, y, c = _coords()
        me = 4 * x + 2 * y + c

        def src(i, idx):
            return ins[i].at[idx] if all_to_all else ins[i]

        local = [pltpu.make_async_copy(src(i, me), outs[i].at[me], local_sems.at[i]) for i in range(n)]
        for cp in local:
            cp.start()
        sends = []
        for m in range(1, N_DEV):
            peer, pidx = _peer(m)
            for i in range(n):
                k = i * (N_DEV - 1) + m - 1
                cp = pltpu.make_async_remote_copy(src_ref=src(i, pidx), dst_ref=outs[i].at[me],
                                                  send_sem=send_sems.at[k], recv_sem=recv_sems.at[k],
                                                  device_id=peer, device_id_type=MESH)
                cp.start()
                sends.append(cp)
        for m in range(1, N_DEV):
            peer, pidx = _peer(m)
            for i in range(n):
                k = i * (N_DEV - 1) + m - 1
                pltpu.make_async_remote_copy(src_ref=src(i, pidx), dst_ref=outs[i].at[pidx],
                                             send_sem=send_sems.at[k], recv_sem=recv_sems.at[k],
                                             device_id=peer, device_id_type=MESH).wait_recv()
        for cp in sends:
            cp.wait_send()
        for cp in local:
            cp.wait()

    any_spec = pl.BlockSpec(memory_space=pl.ANY)
    return pl.pallas_call(
        body, name=name, in_specs=[any_spec] * n, out_specs=[any_spec] * n, out_shape=out_shape,
        scratch_shapes=[pltpu.SemaphoreType.DMA((n * (N_DEV - 1),)), pltpu.SemaphoreType.DMA((n * (N_DEV - 1),)),
                        pltpu.SemaphoreType.DMA((n,))],
    )(*items)


HBM_SPEC = pl.BlockSpec(memory_space=pltpu.HBM)
SEM_SPEC = pl.BlockSpec(memory_space=pltpu.SEMAPHORE)
SPLIT_PARAMS = pltpu.CompilerParams(has_side_effects=pltpu.SideEffectType.DATAFLOW_SIDE_EFFECTING)


def _split_copies(src_ref, land_ref, send_sems, recv_sems, all_to_all):
    x, y, c = _coords()
    me = 4 * x + 2 * y + c
    copies = []
    for m in range(1, N_DEV):
        peer, pidx = _peer(m)
        copies.append(pltpu.make_async_remote_copy(
            src_ref=src_ref.at[pidx] if all_to_all else src_ref, dst_ref=land_ref.at[me],
            send_sem=send_sems.at[m - 1], recv_sem=recv_sems.at[m - 1], device_id=peer, device_id_type=MESH))
    return copies


def copy_start(items, lands=None, *, all_to_all, name):
    n = len(items)
    if lands is None:
        lands = [lax.empty(a.shape if all_to_all else (N_DEV,) + a.shape, a.dtype) for a in items]

    def body(*refs):
        srcs, lnds, outs = refs[:n], refs[n:2 * n], refs[2 * n:]
        for i in range(n):
            for cp in _split_copies(srcs[i], lnds[i], outs[4 * i], outs[4 * i + 1], all_to_all):
                cp.start()
        outs[4 * n][...] = jnp.zeros((8, 128), F32)

    out_shape, out_specs, aliases = [], [], {}
    for i, (a, l) in enumerate(zip(items, lands)):
        out_shape += [pltpu.SemaphoreType.DMA((N_DEV - 1,)), pltpu.SemaphoreType.DMA((N_DEV - 1,)),
                      pltpu.HBM(a.shape, a.dtype), pltpu.HBM(l.shape, l.dtype)]
        out_specs += [SEM_SPEC, SEM_SPEC, HBM_SPEC, HBM_SPEC]
        aliases[i] = 4 * i + 2
        aliases[n + i] = 4 * i + 3
    out_shape.append(jax.ShapeDtypeStruct((8, 128), F32))
    out_specs.append(pl.BlockSpec(memory_space=pltpu.VMEM))
    hbm = lambda a: pltpu.with_memory_space_constraint(a, pltpu.HBM)
    res = pl.pallas_call(
        body, name=name, in_specs=[HBM_SPEC] * (2 * n), out_specs=out_specs, out_shape=out_shape,
        input_output_aliases=aliases, compiler_params=SPLIT_PARAMS,
    )(*[hbm(a) for a in items], *[hbm(l) for l in lands])
    return [tuple(res[4 * i:4 * i + 4]) for i in range(n)], res[4 * n]


def copy_wait(handle, after, *, all_to_all, name):
    send_sems, recv_sems, src, land = handle

    def body(src_ref, land_ref, send_ref, recv_ref, after_ref, src_out, got_ref):
        for cp in _split_copies(src_ref, land_ref, send_ref, recv_ref, all_to_all):
            cp.wait_send()
            cp.wait_recv()

    return pl.pallas_call(
        body, name=name, in_specs=[HBM_SPEC, HBM_SPEC, SEM_SPEC, SEM_SPEC, ANY_SPEC],
        out_specs=[HBM_SPEC, HBM_SPEC],
        out_shape=[pltpu.HBM(src.shape, src.dtype), pltpu.HBM(land.shape, land.dtype)],
        input_output_aliases={0: 0, 1: 1}, compiler_params=SPLIT_PARAMS,
    )(src, land, send_sems, recv_sems, after)


def kernel(x, meta_tokens, norm_mix_0, w_in_conv, conv_w, w_out_conv, norm_mlp_0, w_up_0, w_down_0, norm_mix_1, w_qkv, attn_sinks, w_o, norm_mlp_1, w_up_1, w_down_1, norm_final, loss_target, m_meta_tokens, m_norm_mix_0, m_w_in_conv, m_conv_w, m_w_out_conv, m_norm_mlp_0, m_w_up_0, m_w_down_0, m_norm_mix_1, m_w_qkv, m_attn_sinks, m_w_o, m_norm_mlp_1, m_w_up_1, m_w_down_1, m_norm_final, v_meta_tokens, v_norm_mix_0, v_w_in_conv, v_conv_w, v_w_out_conv, v_norm_mlp_0, v_w_up_0, v_w_down_0, v_norm_mix_1, v_w_qkv, v_attn_sinks, v_w_o, v_norm_mlp_1, v_w_up_1, v_w_down_1, v_norm_final):
    L, D = x.shape[1], x.shape[2]
    n_meta = meta_tokens.shape[0]
    pad = BLOCK - n_meta
    R = BLOCK + L
    n_q = D // HEAD_DIM
    n_kv = n_q // GROUP
    assert n_kv % 2 == 0 and L % BLOCK == 0 and D % 128 == 0
    x = x.reshape(L, D)
    target = loss_target.reshape(L, D)
    x_id, y_id, c_id = _coords()
    me = 4 * x_id + 2 * y_id + c_id

    col_names = ("in", "up0", "qkv", "up1")
    col_w = dict(zip(col_names, (w_in_conv, w_up_0, w_qkv, w_up_1)))
    row_names = ("out", "down0", "o", "down1")
    row_w = dict(zip(row_names, (w_out_conv, w_down_0, w_o, w_down_1)))
    me_arr = jnp.reshape(me, (1,)).astype(jnp.int32)
    natural = {k: col_w[k].T for k in col_names}
    natural.update(row_w)
    use_order = ("in", "out", "up0", "down0", "qkv", "o", "up1", "down1")
    placed = [cast_place(natural[k], me_arr, name="cast_" + k) for k in use_order]
    handles, token = copy_start([p[0] for p in placed], [p[1] for p in placed], all_to_all=False, name="gather_start")
    handles = dict(zip(use_order, handles))

    def weight(k, after):
        return copy_wait(handles[k], after, all_to_all=False, name="gather_wait_" + k)[1].reshape(-1, D)

    small_in = exchange([meta_tokens, conv_w], all_to_all=False, name="comm_gather")
    meta_full = jnp.transpose(small_in[0], (1, 0, 2)).reshape(n_meta, D)
    conv_full = jnp.transpose(small_in[1], (1, 0, 2)).reshape(conv_w.shape[0], D)

    vec = lambda a: a.reshape(1, D)
    pos = jnp.arange(R, dtype=F32) - pad
    inv = ROPE_THETA ** (-jnp.arange(0, HEAD_DIM, 2, dtype=F32) / HEAD_DIM)
    ang = pos[:, None] * inv[None, :]
    cos32, sin32 = jnp.cos(ang), jnp.sin(ang)
    cos = jnp.concatenate([cos32] * 4, axis=1)
    sin = jnp.concatenate([-sin32, sin32, -sin32, sin32], axis=1)

    W = {}
    head = jnp.concatenate([jnp.zeros((pad, D), F32), meta_full], axis=0)
    h0, n0 = first_norm(head, x, vec(norm_mix_0), token, name="norm0")
    W["in"] = weight("in", n0)
    bcu = mm(n0, W["in"], name="in_proj", out_dtype=BF16, b_rows_are_n=True)
    gated = conv_fwd(bcu, conv_full, name="conv_fwd")
    W["out"] = weight("out", gated)
    h1 = mm(gated, W["out"], name="out_proj", out_dtype=F32, b_rows_are_n=False, epi="add", extra=h0)
    n1 = norm_fwd(h1, vec(norm_mlp_0), h1, name="norm1")
    W["up0"] = weight("up0", n1)
    a0 = mm(n1, W["up0"], name="up0", out_dtype=BF16, b_rows_are_n=True, epi="relu")
    W["down0"] = weight("down0", a0)
    h2 = mm(a0, W["down0"], name="down0", out_dtype=F32, b_rows_are_n=False, epi="add", extra=h1, a_sq=True)
    n2 = norm_fwd(h2, vec(norm_mix_1), h2, name="norm2")
    W["qkv"] = weight("qkv", n2)
    qkv = mm(n2, W["qkv"], name="qkv_proj", out_dtype=F32, b_rows_are_n=True)
    q, kx, vx, kxt, vxt = rope_fwd(qkv, cos, sin, n_q=n_q, n_kv=n_kv, name="rope_fwd")
    o, lse = attn_fwd(q, kx, vxt, attn_sinks, pad=pad, name="attn_fwd")
    W["o"] = weight("o", o)
    h3 = mm(o, W["o"], name="o_proj", out_dtype=F32, b_rows_are_n=False, epi="add", extra=h2)
    n3 = norm_fwd(h3, vec(norm_mlp_1), h3, name="norm3")
    W["up1"] = weight("up1", n3)
    a1 = mm(n3, W["up1"], name="up1", out_dtype=BF16, b_rows_are_n=True, epi="relu")
    W["down1"] = weight("down1", a1)
    h4 = mm(a1, W["down1"], name="down1", out_dtype=F32, b_rows_are_n=False, epi="add", extra=h3, a_sq=True)

    loss_part, dh4, dh4b, dg_final = loss_bwd(h4, vec(norm_final), target, name="loss_bwd")
    loss = lax.psum(loss_part[0, 0], ("x", "y", "c"))

    sent = {}

    def scatter(k, dw):
        (sent[k],), tok = copy_start([dw.reshape(N_DEV, -1, D)], all_to_all=True, name="a2a_start_" + k)
        return tok

    t = scatter("down1", mm_tn(a1, dh4b, name="dw_down1", a_sq=True))
    dup1 = mm(dh4b, W["down1"], name="d_down1", out_dtype=BF16, b_rows_are_n=True, epi="mul2a", extra=a1, dep=t)
    t = scatter("up1", mm_tn(dup1, n3, name="dw_up1"))
    dn3 = mm(dup1, W["up1"], name="d_up1", out_dtype=F32, b_rows_are_n=False, dep=t)
    dh3, dh3b, dg_mlp1 = norm_bwd(dn3, h3, vec(norm_mlp_1), dh4, name="norm3_bwd")

    t = scatter("o", mm_tn(o, dh3b, name="dw_o"))
    do = mm(dh3b, W["o"], name="d_o", out_dtype=BF16, b_rows_are_n=True, dep=t)
    dq, dkx, dvx, dsinks = attn_bwd(q, kx, kxt, vx, o, do, lse, attn_sinks, pad=pad, name="attn_bwd")
    dqkv = rope_bwd(dq, dkx, dvx, cos, sin, n_q=n_q, n_kv=n_kv, name="rope_bwd")
    t = scatter("qkv", mm_tn(dqkv, n2, name="dw_qkv"))
    dn2 = mm(dqkv, W["qkv"], name="d_qkv", out_dtype=F32, b_rows_are_n=False, dep=t)
    dh2, dh2b, dg_mix1 = norm_bwd(dn2, h2, vec(norm_mix_1), dh3, name="norm2_bwd")

    t = scatter("down0", mm_tn(a0, dh2b, name="dw_down0", a_sq=True))
    dup0 = mm(dh2b, W["down0"], name="d_down0", out_dtype=BF16, b_rows_are_n=True, epi="mul2a", extra=a0, dep=t)
    t = scatter("up0", mm_tn(dup0, n1, name="dw_up0"))
    dn1 = mm(dup0, W["up0"], name="d_up0", out_dtype=F32, b_rows_are_n=False, dep=t)
    dh1, dh1b, dg_mlp0 = norm_bwd(dn1, h1, vec(norm_mlp_0), dh2, name="norm1_bwd")

    t = scatter("out", mm_tn(gated, dh1b, name="dw_out"))
    dgated = mm(dh1b, W["out"], name="d_out", out_dtype=F32, b_rows_are_n=True, dep=t)
    dbcu, dconv = conv_bwd(bcu, conv_full, dgated, name="conv_bwd")
    t = scatter("in", mm_tn(dbcu, n0, name="dw_in"))
    dn0 = mm(dbcu, W["in"], name="d_in", out_dtype=F32, b_rows_are_n=False, dep=t)
    dhead, dx, dg_mix0 = last_norm_bwd(dn0, h0, vec(norm_mix_0), dh1, name="norm0_bwd")
    grad_x = dx.reshape(1, L, D)

    recv = {}
    for k in ("down1", "up1", "o", "qkv", "down0", "up0", "out", "in"):
        sent[k], recv[k] = copy_wait(sent[k], dx, all_to_all=True, name="a2a_wait_" + k)

    n_sink = attn_sinks.shape[0]
    slab = jnp.concatenate([
        dg_mix0, dg_mlp0, dg_mix1, dg_mlp1, dg_final,
        jnp.pad(dsinks[0:1, :n_sink], ((0, 0), (0, D - n_sink))), jnp.zeros((2, D), F32),
        dconv, dhead[pad:BLOCK]], axis=0)
    slabs = exchange([slab], all_to_all=False, name="comm_small")[0]
    small = sum_parts(slabs, slabs, me_arr, name="sum_small")
    cols = D // N_DEV
    my_cols = lambda a: lax.dynamic_slice_in_dim(a, me * cols, cols, axis=1)

    grads, deltas, new_m, new_v = {}, {}, {}, {}

    def update(key, w, m, v, g, shape):
        s2 = (1, -1) if w.ndim == 1 else w.shape
        if g.ndim == 3:
            g_, d_, m_, v_ = adam_parts(w, m, v, g, sent[key_of[key]], me_arr, name="adam_" + key)
        else:
            g_ = g.reshape(s2)
            d_, m_, v_ = adam(w.reshape(s2), m.reshape(s2), v.reshape(s2), g_, name="adam_" + key)
        grads[key], deltas[key], new_m[key], new_v[key] = (t.reshape(shape) for t in (g_, d_, m_, v_))

    update("meta_tokens", meta_tokens, m_meta_tokens, v_meta_tokens, my_cols(small[16:16 + n_meta]), meta_tokens.shape)
    update("norm_mix_0", norm_mix_0, m_norm_mix_0, v_norm_mix_0, small[0], (D,))
    update("conv_w", conv_w, m_conv_w, v_conv_w, my_cols(small[8:8 + conv_w.shape[0]]), conv_w.shape)
    update("norm_mlp_0", norm_mlp_0, m_norm_mlp_0, v_norm_mlp_0, small[1], (D,))
    update("norm_mix_1", norm_mix_1, m_norm_mix_1, v_norm_mix_1, small[2], (D,))
    update("attn_sinks", attn_sinks, m_attn_sinks, v_attn_sinks, small[5, :n_sink], (n_sink,))
    update("norm_mlp_1", norm_mlp_1, m_norm_mlp_1, v_norm_mlp_1, small[3], (D,))
    update("norm_final", norm_final, m_norm_final, v_norm_final, small[4], (D,))
    big = {"in": ("w_in_conv", w_in_conv, m_w_in_conv, v_w_in_conv), "up0": ("w_up_0", w_up_0, m_w_up_0, v_w_up_0),
           "qkv": ("w_qkv", w_qkv, m_w_qkv, v_w_qkv), "up1": ("w_up_1", w_up_1, m_w_up_1, v_w_up_1),
           "out": ("w_out_conv", w_out_conv, m_w_out_conv, v_w_out_conv),
           "down0": ("w_down_0", w_down_0, m_w_down_0, v_w_down_0), "o": ("w_o", w_o, m_w_o, v_w_o),
           "down1": ("w_down_1", w_down_1, m_w_down_1, v_w_down_1)}
    key_of = {big[k][0]: k for k in big}
    for k in col_names:
        key, w, m, v = big[k]
        update(key, w, m, v, sum_parts(recv[k], sent[k], me_arr, name="sum_" + k).T, w.shape)
    for k in row_names:
        key, w, m, v = big[k]
        update(key, w, m, v, recv[k], w.shape)

    order = ("meta_tokens", "norm_mix_0", "w_in_conv", "conv_w", "w_out_conv", "norm_mlp_0", "w_up_0", "w_down_0",
             "norm_mix_1", "w_qkv", "attn_sinks", "w_o", "norm_mlp_1", "w_up_1", "w_down_1", "norm_final")
    return (loss, grad_x, *[grads[k] for k in order], *[deltas[k] for k in order],
            *[new_m[k] for k in order], *[new_v[k] for k in order])
```

```python
import functools

import jax
import jax.numpy as jnp
from jax import lax
from jax.experimental import pallas as pl
from jax.experimental.pallas import tpu as pltpu

F32 = jnp.float32
BF16 = jnp.bfloat16

HEAD_DIM = 64
GROUP = 8
BLOCK = 128
N_DEV = 8
RMS_EPS = 1e-5
NEG_INF = -1e30
ROPE_THETA = 10000.0
ADAM_LR = 0.001
ADAM_B1 = 0.9
ADAM_B2 = 0.999
ADAM_EPS = 1e-08
ADAM_WD = 0.01
ADAM_STEP = 10
VMEM_LIMIT = 60 * 1024 * 1024
MESH = pl.DeviceIdType.MESH


def _tile(n, target, mult):
    best = None
    for t in range(mult, min(n, target) + 1, mult):
        if n % t == 0:
            best = t
    return best if best is not None else n


def _params(sem):
    return pltpu.CompilerParams(dimension_semantics=sem, vmem_limit_bytes=VMEM_LIMIT)


ANY_SPEC = pl.BlockSpec(memory_space=pl.ANY)


LEFT_BLOCK_BYTES = 14 * 1024 * 1024


def mm(a, b, *, name, out_dtype, b_rows_are_n, epi=None, extra=None, a_sq=False, dep=None):
    M, K = a.shape
    N = b.shape[0] if b_rows_are_n else b.shape[1]
    tm = _tile(M, LEFT_BLOCK_BYTES // (2 * K), 16)
    tn = _tile(N, 512 if K <= 4096 else 256, 128)

    def body(*refs):
        a_ref, b_ref, e_ref, o_ref = refs[0], refs[1], refs[2], refs[-1]
        av = a_ref[...]
        if a_sq:
            av = av.astype(F32)
            av = (av * av).astype(BF16)
        dims = (((1,), (1,)), ((), ())) if b_rows_are_n else (((1,), (0,)), ((), ()))
        acc = lax.dot_general(av, b_ref[...], dims, preferred_element_type=F32)
        if epi == "relu":
            acc = jnp.maximum(acc, 0.0)
        elif epi == "mul2a":
            acc = acc * (2.0 * e_ref[...].astype(F32))
        elif epi == "add":
            acc = acc + e_ref[...]
        o_ref[...] = acc.astype(o_ref.dtype)

    b_spec = pl.BlockSpec((tn, K), lambda i, j: (j, 0)) if b_rows_are_n else pl.BlockSpec((K, tn), lambda i, j: (0, j))
    in_specs = [pl.BlockSpec((tm, K), lambda i, j: (i, 0)), b_spec]
    args = [a, b]
    if extra is not None:
        in_specs.append(pl.BlockSpec((tm, tn), lambda i, j: (i, j)))
        args.append(extra)
    if dep is not None:
        in_specs.append(ANY_SPEC)
        args.append(dep)
    return pl.pallas_call(
        body, name=name, grid=(M // tm, N // tn), in_specs=in_specs,
        out_specs=pl.BlockSpec((tm, tn), lambda i, j: (i, j)),
        out_shape=jax.ShapeDtypeStruct((M, N), out_dtype),
        compiler_params=_params(("parallel", "parallel")),
    )(*args)


def _rms_parts(h):
    rstd = lax.rsqrt(jnp.mean(h * h, axis=-1, keepdims=True) + RMS_EPS)
    return h * rstd, rstd


def mm_tn(a, b, *, name, a_sq=False):
    T, M = a.shape
    N = b.shape[1]
    tm = _tile(M, 1024, 128)
    tk = _tile(T, 2080, 16)
    nk = T // tk

    tc = _tile(tm, 256, 128)

    def body(a_ref, b_ref, o_ref, acc_ref):
        k = pl.program_id(1)

        def chunks(first):
            bv = b_ref[...]
            for r0 in range(0, tm, tc):
                av = a_ref[:, r0:r0 + tc]
                if a_sq:
                    av = av.astype(F32)
                    av = (av * av).astype(BF16)
                part = lax.dot_general(av, bv, (((0,), (0,)), ((), ())), preferred_element_type=F32)
                if first:
                    acc_ref[r0:r0 + tc, :] = part
                else:
                    acc_ref[r0:r0 + tc, :] += part

        @pl.when(k == 0)
        def _():
            chunks(True)

        @pl.when(k > 0)
        def _():
            chunks(False)

        @pl.when(k == nk - 1)
        def _():
            o_ref[...] = acc_ref[...].astype(BF16)

    return pl.pallas_call(
        body, name=name, grid=(M // tm, nk),
        in_specs=[pl.BlockSpec((tk, tm), lambda i, k: (k, i)), pl.BlockSpec((tk, N), lambda i, k: (k, 0))],
        out_specs=pl.BlockSpec((tm, N), lambda i, k: (i, 0)),
        out_shape=jax.ShapeDtypeStruct((M, N), BF16),
        scratch_shapes=[pltpu.VMEM((tm, N), F32)],
        compiler_params=_params(("parallel", "arbitrary")),
    )(a, b)


def norm_fwd(h, g, dep, *, name):
    R, D = h.shape
    tr = _tile(R, 320, 16)

    def body(h_ref, g_ref, dep_ref, n_ref):
        xhat, _ = _rms_parts(h_ref[...])
        n_ref[...] = (xhat * g_ref[...]).astype(BF16)

    return pl.pallas_call(
        body, name=name, grid=(R // tr,),
        in_specs=[pl.BlockSpec((tr, D), lambda i: (i, 0)), pl.BlockSpec((1, D), lambda i: (0, 0)), ANY_SPEC],
        out_specs=pl.BlockSpec((tr, D), lambda i: (i, 0)),
        out_shape=jax.ShapeDtypeStruct((R, D), BF16),
        compiler_params=_params(("parallel",)),
    )(h, g, dep)


def norm_bwd(dn, h, g, dres, *, name):
    R, D = h.shape
    tr = _tile(R, 320, 16)

    def body(dn_ref, h_ref, g_ref, dres_ref, dh_ref, dhb_ref, dg_ref):
        i = pl.program_id(0)
        dn = dn_ref[...]
        xhat, rstd = _rms_parts(h_ref[...])
        dxh = dn * g_ref[...]
        dh = dres_ref[...] + rstd * (dxh - xhat * jnp.mean(dxh * xhat, axis=-1, keepdims=True))
        dh_ref[...] = dh
        dhb_ref[...] = dh.astype(BF16)
        dg = jnp.sum(dn * xhat, axis=0, keepdims=True)

        @pl.when(i == 0)
        def _():
            dg_ref[...] = dg

        @pl.when(i > 0)
        def _():
            dg_ref[...] += dg

    row = pl.BlockSpec((tr, D), lambda i: (i, 0))
    vec = pl.BlockSpec((1, D), lambda i: (0, 0))
    return pl.pallas_call(
        body, name=name, grid=(R // tr,), in_specs=[row, row, vec, row], out_specs=[row, row, vec],
        out_shape=[jax.ShapeDtypeStruct((R, D), F32), jax.ShapeDtypeStruct((R, D), BF16),
                   jax.ShapeDtypeStruct((1, D), F32)],
        compiler_params=_params(("arbitrary",)),
    )(dn, h, g, dres)


def first_norm(head, x, g, dep, *, name):
    L, D = x.shape
    nb = L // BLOCK + 1

    def body(head_ref, x_ref, g_ref, dep_ref, h_ref, n_ref):
        i = pl.program_id(0)
        hv = jnp.where(i == 0, head_ref[...], x_ref[...])
        h_ref[...] = hv
        xhat, _ = _rms_parts(hv)
        n_ref[...] = (xhat * g_ref[...]).astype(BF16)

    blk = pl.BlockSpec((BLOCK, D), lambda i: (i, 0))
    return pl.pallas_call(
        body, name=name, grid=(nb,),
        in_specs=[pl.BlockSpec((BLOCK, D), lambda i: (0, 0)),
                  pl.BlockSpec((BLOCK, D), lambda i: (jnp.maximum(i - 1, 0), 0)),
                  pl.BlockSpec((1, D), lambda i: (0, 0)), ANY_SPEC],
        out_specs=[blk, blk],
        out_shape=[jax.ShapeDtypeStruct((BLOCK + L, D), F32), jax.ShapeDtypeStruct((BLOCK + L, D), BF16)],
        compiler_params=_params(("parallel",)),
    )(head, x, g, dep)


def last_norm_bwd(dn, h, g, dres, *, name):
    R, D = h.shape
    nb = R // BLOCK

    def body(dn_ref, h_ref, g_ref, dres_ref, dhead_ref, dx_ref, dg_ref):
        i = pl.program_id(0)
        dn = dn_ref[...]
        xhat, rstd = _rms_parts(h_ref[...])
        dxh = dn * g_ref[...]
        dh = dres_ref[...] + rstd * (dxh - xhat * jnp.mean(dxh * xhat, axis=-1, keepdims=True))
        dg = jnp.sum(dn * xhat, axis=0, keepdims=True)
        dx_ref[...] = dh

        @pl.when(i == 0)
        def _():
            dhead_ref[...] = dh
            dg_ref[...] = dg

        @pl.when(i > 0)
        def _():
            dg_ref[...] += dg

    blk = pl.BlockSpec((BLOCK, D), lambda i: (i, 0))
    vec = pl.BlockSpec((1, D), lambda i: (0, 0))
    return pl.pallas_call(
        body, name=name, grid=(nb,), in_specs=[blk, blk, vec, blk],
        out_specs=[pl.BlockSpec((BLOCK, D), lambda i: (0, 0)),
                   pl.BlockSpec((BLOCK, D), lambda i: (jnp.maximum(i - 1, 0), 0)), vec],
        out_shape=[jax.ShapeDtypeStruct((BLOCK, D), F32), jax.ShapeDtypeStruct((R - BLOCK, D), F32),
                   jax.ShapeDtypeStruct((1, D), F32)],
        compiler_params=_params(("arbitrary",)),
    )(dn, h, g, dres)


HALO = 16


def conv_fwd(bcu, conv_w, *, name):
    R, D3 = bcu.shape
    D = D3 // 3
    tr = _tile(R, 320, 16)
    tc = _tile(D, 512, 128)
    nc = D // tc
    hb = tr // HALO

    def body(b_ref, c_ref, u_ref, ch_ref, uh_ref, w_ref, o_ref, vbuf):
        i = pl.program_id(0)
        v = c_ref[...].astype(F32) * u_ref[...].astype(F32)
        vbuf[0:HALO, :] = jnp.where(i > 0, ch_ref[...].astype(F32) * uh_ref[...].astype(F32), 0.0)
        vbuf[HALO:HALO + tr, :] = v
        v1 = vbuf[HALO - 1:HALO - 1 + tr, :]
        v2 = vbuf[HALO - 2:HALO - 2 + tr, :]
        conv = w_ref[0:1, :] * v2 + w_ref[1:2, :] * v1 + w_ref[2:3, :] * v
        o_ref[...] = (b_ref[...].astype(F32) * conv).astype(BF16)

    def prev(i):
        return jnp.maximum(i * hb - 1, 0)

    return pl.pallas_call(
        body, name=name, grid=(R // tr, nc),
        in_specs=[pl.BlockSpec((tr, tc), lambda i, j: (i, j)),
                  pl.BlockSpec((tr, tc), lambda i, j: (i, nc + j)),
                  pl.BlockSpec((tr, tc), lambda i, j: (i, 2 * nc + j)),
                  pl.BlockSpec((HALO, tc), lambda i, j: (prev(i), nc + j)),
                  pl.BlockSpec((HALO, tc), lambda i, j: (prev(i), 2 * nc + j)),
                  pl.BlockSpec((3, tc), lambda i, j: (0, j))],
        out_specs=pl.BlockSpec((tr, tc), lambda i, j: (i, j)),
        out_shape=jax.ShapeDtypeStruct((R, D), BF16),
        scratch_shapes=[pltpu.VMEM((HALO + tr, tc), F32)],
        compiler_params=_params(("parallel", "parallel")),
    )(bcu, bcu, bcu, bcu, bcu, conv_w)


def conv_bwd(bcu, conv_w, dg, *, name):
    R, D3 = bcu.shape
    D = D3 // 3
    tr = _tile(R, 320, 16)
    tc = _tile(D, 512, 128)
    hb = tr // HALO
    nt = R // tr
    last_h = R // HALO - 1

    def body(x_ref, xp_ref, xn_ref, dg_ref, dgn_ref, w_ref, o_ref, dw_ref, vbuf, dbuf):
        i = pl.program_id(0)
        row8 = lax.broadcasted_iota(jnp.int32, (8, tc), 0)
        for c0 in range(0, D, tc):
            cb, cc, cu = slice(c0, c0 + tc), slice(D + c0, D + c0 + tc), slice(2 * D + c0, 2 * D + c0 + tc)
            w0, w1, w2 = w_ref[0:1, cb], w_ref[1:2, cb], w_ref[2:3, cb]
            b = x_ref[:, cb].astype(F32)
            c = x_ref[:, cc].astype(F32)
            u = x_ref[:, cu].astype(F32)
            v = c * u
            vbuf[0:HALO, :] = jnp.where(i > 0, xp_ref[:, cc].astype(F32) * xp_ref[:, cu].astype(F32), 0.0)
            vbuf[HALO:HALO + tr, :] = v
            v1 = vbuf[HALO - 1:HALO - 1 + tr, :]
            v2 = vbuf[HALO - 2:HALO - 2 + tr, :]
            dgv = dg_ref[:, cb]
            o_ref[:, cb] = (dgv * (w0 * v2 + w1 * v1 + w2 * v)).astype(BF16)
            dconv = dgv * b
            dbuf[0:tr, :] = dconv
            dbuf[tr:tr + HALO, :] = jnp.where(i < nt - 1, dgn_ref[:, cb] * xn_ref[:, cb].astype(F32), 0.0)
            dv = w2 * dconv + w1 * dbuf[1:1 + tr, :] + w0 * dbuf[2:2 + tr, :]
            o_ref[:, cc] = (dv * u).astype(BF16)
            o_ref[:, cu] = (dv * c).astype(BF16)
            dw = jnp.where(row8 == 0, jnp.sum(dconv * v2, axis=0, keepdims=True),
                           jnp.where(row8 == 1, jnp.sum(dconv * v1, axis=0, keepdims=True),
                                     jnp.where(row8 == 2, jnp.sum(dconv * v, axis=0, keepdims=True), 0.0)))

            @pl.when(i == 0)
            def _():
                dw_ref[:, cb] = dw

            @pl.when(i > 0)
            def _():
                dw_ref[:, cb] += dw

    def prev(i):
        return jnp.maximum(i * hb - 1, 0)

    def nxt(i):
        return jnp.minimum((i + 1) * hb, last_h)

    return pl.pallas_call(
        body, name=name, grid=(nt,),
        in_specs=[pl.BlockSpec((tr, D3), lambda i: (i, 0)),
                  pl.BlockSpec((HALO, D3), lambda i: (prev(i), 0)),
                  pl.BlockSpec((HALO, D3), lambda i: (nxt(i), 0)),
                  pl.BlockSpec((tr, D), lambda i: (i, 0)),
                  pl.BlockSpec((HALO, D), lambda i: (nxt(i), 0)),
                  pl.BlockSpec((3, D), lambda i: (0, 0))],
        out_specs=[pl.BlockSpec((tr, D3), lambda i: (i, 0)), pl.BlockSpec((8, D), lambda i: (0, 0))],
        out_shape=[jax.ShapeDtypeStruct((R, D3), BF16), jax.ShapeDtypeStruct((8, D), F32)],
        scratch_shapes=[pltpu.VMEM((HALO + tr, tc), F32), pltpu.VMEM((tr + HALO, tc), F32)],
        compiler_params=_params(("arbitrary",)),
    )(bcu, bcu, bcu, dg, dg, conv_w)


def _swap32(x):
    w = x.shape[1]
    lane = lax.broadcasted_iota(jnp.int32, x.shape, 1)
    return jnp.where((lane & (HEAD_DIM - 1)) < HEAD_DIM // 2, pltpu.roll(x, w - HEAD_DIM // 2, axis=1),
                     pltpu.roll(x, HEAD_DIM // 2, axis=1))


def _rope(x, cos, sin):
    return x * cos + _swap32(x) * sin


def rope_fwd(qkv, cos, sin, *, n_q, n_kv, name):
    R, W = qkv.shape
    qw = n_q * HEAD_DIM
    kw = n_kv * HEAD_DIM
    tr = _tile(R, 640, 128)

    def expand(y, ref, ref_t, c):
        lane = lax.broadcasted_iota(jnp.int32, y.shape, 1)
        lo = jnp.where(lane < HEAD_DIM, y, 0.0)
        hi = jnp.where(lane >= HEAD_DIM, y, 0.0)
        tiles = (lo, pltpu.roll(lo, HEAD_DIM, axis=1), pltpu.roll(hi, HEAD_DIM, axis=1), hi)
        for t, tile in enumerate(tiles):
            c0 = 512 * c + 128 * t
            ref[:, c0:c0 + 128] = tile.astype(BF16)
            ref_t[c0:c0 + 128, :] = tile.T.astype(BF16)

    def body(x_ref, c_ref, s_ref, q_ref, k_ref, v_ref, kt_ref, vt_ref):
        cos = c_ref[...]
        sin = s_ref[...]
        for c in range(qw // 128):
            x = x_ref[:, 128 * c:128 * (c + 1)]
            q_ref[:, 128 * c:128 * (c + 1)] = (_rope(x, cos, sin) * (HEAD_DIM ** -0.5)).astype(BF16)
        for c in range(kw // 128):
            expand(_rope(x_ref[:, qw + 128 * c:qw + 128 * (c + 1)], cos, sin), k_ref, kt_ref, c)
            expand(x_ref[:, qw + kw + 128 * c:qw + kw + 128 * (c + 1)], v_ref, vt_ref, c)

    row = lambda w: pl.BlockSpec((tr, w), lambda i: (i, 0))
    col = pl.BlockSpec((4 * kw, tr), lambda i: (0, i))
    return pl.pallas_call(
        body, name=name, grid=(R // tr,),
        in_specs=[row(W), row(128), row(128)],
        out_specs=[row(qw), row(4 * kw), row(4 * kw), col, col],
        out_shape=[jax.ShapeDtypeStruct((R, qw), BF16), jax.ShapeDtypeStruct((R, 4 * kw), BF16),
                   jax.ShapeDtypeStruct((R, 4 * kw), BF16), jax.ShapeDtypeStruct((4 * kw, R), BF16),
                   jax.ShapeDtypeStruct((4 * kw, R), BF16)],
        compiler_params=_params(("parallel",)),
    )(qkv, cos, sin)


def rope_bwd(dq, dkx, dvx, cos, sin, *, n_q, n_kv, name):
    R = dq.shape[0]
    qw = n_q * HEAD_DIM
    kw = n_kv * HEAD_DIM
    W = qw + 2 * kw
    tr = _tile(R, 320, 16)

    def fold(ref, c):
        lane = lax.broadcasted_iota(jnp.int32, (tr, 128), 1)
        x0 = ref[:, 128 * (2 * c):128 * (2 * c + 1)]
        x1 = ref[:, 128 * (2 * c + 1):128 * (2 * c + 2)]
        f0 = x0 + pltpu.roll(x0, HEAD_DIM, axis=1)
        f1 = x1 + pltpu.roll(x1, HEAD_DIM, axis=1)
        return jnp.where(lane < HEAD_DIM, f0, f1)

    def body(dq_ref, dk_ref, dv_ref, c_ref, s_ref, o_ref):
        cos = c_ref[...]
        nsin = -s_ref[...]
        for c in range(qw // 128):
            y = dq_ref[:, 128 * c:128 * (c + 1)]
            o_ref[:, 128 * c:128 * (c + 1)] = (_rope(y, cos, nsin) * (HEAD_DIM ** -0.5)).astype(BF16)
        for c in range(kw // 128):
            o_ref[:, qw + 128 * c:qw + 128 * (c + 1)] = _rope(fold(dk_ref, c), cos, nsin).astype(BF16)
            o_ref[:, qw + kw + 128 * c:qw + kw + 128 * (c + 1)] = fold(dv_ref, c).astype(BF16)

    row = lambda w: pl.BlockSpec((tr, w), lambda i: (i, 0))
    return pl.pallas_call(
        body, name=name, grid=(R // tr,),
        in_specs=[row(qw), row(2 * kw), row(2 * kw), row(128), row(128)],
        out_specs=row(W),
        out_shape=jax.ShapeDtypeStruct((R, W), BF16),
        compiler_params=_params(("parallel",)),
    )(dq, dkx, dvx, cos, sin)


def _band_bias(n, pad):
    key = lax.broadcasted_iota(jnp.int32, (2 * BLOCK, BLOCK), 0)
    qry = lax.broadcasted_iota(jnp.int32, (2 * BLOCK, BLOCK), 1)
    kmin = jnp.where(n == 0, BLOCK + pad, jnp.where(n == 1, pad, 0))
    allowed = (key > qry) & (key <= qry + BLOCK) & (key >= kmin)
    return jnp.where(allowed, 0.0, NEG_INF).astype(F32)


def _dot_nt(a, b):
    return lax.dot_general(a, b, (((1,), (1,)), ((), ())), preferred_element_type=F32)


def _band(prev_ref, cur_ref, c0):
    return jnp.concatenate([prev_ref[:, c0:c0 + 128], cur_ref[:, c0:c0 + 128]], axis=0)


def _band_t(prev_ref, cur_ref, r0):
    return jnp.concatenate([prev_ref[r0:r0 + 128, :], cur_ref[r0:r0 + 128, :]], axis=1)


def attn_fwd(q, kx, vxt, sinks, *, pad, name):
    R, qw = q.shape
    n_q = qw // HEAD_DIM
    n_kv = kx.shape[1] // 256
    nb = R // BLOCK

    def body(s_ref, q_ref, kc_ref, kp_ref, vc_ref, vp_ref, o_ref, l_ref):
        n = pl.program_id(0)
        bias = _band_bias(n, pad)
        row = lax.broadcasted_iota(jnp.int32, (128, BLOCK), 0)

        def softmax(st, sink):
            st = st + bias
            m = jnp.maximum(jnp.max(st, axis=0, keepdims=True), sink)
            e = jnp.exp(st - m)
            den = jnp.sum(e, axis=0, keepdims=True) + jnp.exp(sink - m)
            return e.astype(BF16), 1.0 / den, m + jnp.log(den)

        for g in range(n_kv):
            k2 = jnp.concatenate([_band(kp_ref, kc_ref, 256 * g), _band(kp_ref, kc_ref, 256 * g + 128)], axis=0)
            v2 = jnp.concatenate([_band_t(vp_ref, vc_ref, 256 * g), _band_t(vp_ref, vc_ref, 256 * g + 128)], axis=1)
            for p in range(GROUP // 2):
                c0 = 128 * (g * (GROUP // 2) + p)
                he = GROUP * g + 2 * p
                q2 = q_ref[:, c0:c0 + 128]
                st = _dot_nt(k2, q2)
                ee, re, le = softmax(st[0:2 * BLOCK], s_ref[he])
                eo, ro, lo = softmax(st[2 * BLOCK:4 * BLOCK], s_ref[he + 1])
                o2t = jnp.dot(v2, jnp.concatenate([ee, eo], axis=0), preferred_element_type=F32)
                o2t = o2t * jnp.where(row < HEAD_DIM, re, ro)
                o_ref[:, c0:c0 + 128] = o2t.T.astype(BF16)
                l_ref[he:he + 1, :] = le
                l_ref[he + 1:he + 2, :] = lo

    cur = lambda w: pl.BlockSpec((BLOCK, w), lambda n: (n, 0))
    prv = lambda w: pl.BlockSpec((BLOCK, w), lambda n: (jnp.maximum(n - 1, 0), 0))
    cur_t = lambda h: pl.BlockSpec((h, BLOCK), lambda n: (0, n))
    prv_t = lambda h: pl.BlockSpec((h, BLOCK), lambda n: (0, jnp.maximum(n - 1, 0)))
    kxw = kx.shape[1]
    return pl.pallas_call(
        body, name=name, grid=(nb,),
        in_specs=[pl.BlockSpec(memory_space=pltpu.SMEM), cur(qw), cur(kxw), prv(kxw), cur_t(kxw), prv_t(kxw)],
        out_specs=[cur(qw), cur_t(n_q)],
        out_shape=[jax.ShapeDtypeStruct((R, qw), BF16), jax.ShapeDtypeStruct((n_q, R), F32)],
        compiler_params=_params(("parallel",)),
    )(sinks, q, kx, kx, vxt, vxt)


def attn_bwd(q, kx, kxt, vx, o, do, lse, sinks, *, pad, name):
    R, qw = q.shape
    n_q = qw // HEAD_DIM
    n_kv = kx.shape[1] // 256
    nb = R // BLOCK
    kw2 = n_kv * 128

    def body(s_ref, q_ref, do_ref, o_ref, l_ref, kc_ref, kp_ref, ktc_ref, ktp_ref, vc_ref, vp_ref,
             dq_ref, dk_ref, dv_ref, ds_ref, cdk, cdv):
        n = pl.program_id(0)

        @pl.when(n == 0)
        def _():
            cdk[...] = jnp.zeros_like(cdk)
            cdv[...] = jnp.zeros_like(cdv)
            ds_ref[...] = jnp.zeros_like(ds_ref)

        @pl.when(n < nb)
        def _():
            bias = _band_bias(n, pad)
            lane2 = lax.broadcasted_iota(jnp.int32, (2 * BLOCK, 128), 1)
            lane1 = lax.broadcasted_iota(jnp.int32, (1, 128), 1)
            sel_r = lax.broadcasted_iota(jnp.int32, (8, 128), 0)
            sel_l = lax.broadcasted_iota(jnp.int32, (8, 128), 1)
            sel = (((sel_r == 0) & (sel_l < HEAD_DIM)) | ((sel_r == 1) & (sel_l >= HEAD_DIM))).astype(BF16)
            dsink = jnp.zeros((1, 128), F32)
            for g in range(n_kv):
                k2 = jnp.concatenate([_band(kp_ref, kc_ref, 256 * g), _band(kp_ref, kc_ref, 256 * g + 128)], axis=0)
                kt2 = jnp.concatenate([_band_t(ktp_ref, ktc_ref, 256 * g), _band_t(ktp_ref, ktc_ref, 256 * g + 128)],
                                      axis=1)
                v2 = jnp.concatenate([_band(vp_ref, vc_ref, 256 * g), _band(vp_ref, vc_ref, 256 * g + 128)], axis=0)
                dk4 = jnp.zeros((4 * BLOCK, 128), F32)
                dv4 = jnp.zeros((4 * BLOCK, 128), F32)
                for p in range(GROUP // 2):
                    c0 = 128 * (g * (GROUP // 2) + p)
                    he = GROUP * g + 2 * p
                    q2 = q_ref[:, c0:c0 + 128]
                    do2 = do_ref[:, c0:c0 + 128]
                    prod = do2.astype(F32) * o_ref[:, c0:c0 + 128].astype(F32)
                    prod_hi = prod.astype(BF16)
                    prod_lo = (prod - prod_hi.astype(F32)).astype(BF16)
                    deltas = _dot_nt(sel, prod_hi) + _dot_nt(sel, prod_lo)
                    st = _dot_nt(k2, q2)
                    dpt = _dot_nt(v2, do2)
                    pts, dsts = [], []
                    for r in range(2):
                        h = he + r
                        rows = slice(2 * BLOCK * r, 2 * BLOCK * (r + 1))
                        lse_h = l_ref[h:h + 1, :]
                        delta = deltas[r:r + 1, :]
                        pt = jnp.exp(st[rows] + bias - lse_h)
                        pts.append(pt.astype(BF16))
                        dsts.append((pt * (dpt[rows] - delta)).astype(BF16))
                        psink = jnp.exp(s_ref[h] - lse_h)
                        tot = jnp.sum(psink * delta, axis=1, keepdims=True)
                        dsink = dsink - jnp.where(lane1 == h, tot, 0.0)
                    dst = jnp.concatenate(dsts, axis=0)
                    dq_ref[:, c0:c0 + 128] = jnp.dot(kt2, dst, preferred_element_type=F32).T
                    dk4 = dk4 + jnp.dot(dst, q2, preferred_element_type=F32)
                    dv4 = dv4 + jnp.dot(jnp.concatenate(pts, axis=0), do2, preferred_element_type=F32)
                gc = pl.ds(128 * g, 128)
                dk2 = jnp.where(lane2 < HEAD_DIM, dk4[0:2 * BLOCK], dk4[2 * BLOCK:4 * BLOCK])
                dv2 = jnp.where(lane2 < HEAD_DIM, dv4[0:2 * BLOCK], dv4[2 * BLOCK:4 * BLOCK])
                dk_ref[:, gc] = cdk[:, gc] + dk2[0:BLOCK]
                dv_ref[:, gc] = cdv[:, gc] + dv2[0:BLOCK]
                cdk[:, gc] = dk2[BLOCK:2 * BLOCK]
                cdv[:, gc] = dv2[BLOCK:2 * BLOCK]
            ds_ref[0:1, :] += dsink

        @pl.when(n == nb)
        def _():
            dk_ref[...] = cdk[...]
            dv_ref[...] = cdv[...]

    cur = lambda w: pl.BlockSpec((BLOCK, w), lambda n: (jnp.minimum(n, nb - 1), 0))
    prv = lambda w: pl.BlockSpec((BLOCK, w), lambda n: (jnp.clip(n - 1, 0, nb - 1), 0))
    cur_t = lambda h: pl.BlockSpec((h, BLOCK), lambda n: (0, jnp.minimum(n, nb - 1)))
    prv_t = lambda h: pl.BlockSpec((h, BLOCK), lambda n: (0, jnp.clip(n - 1, 0, nb - 1)))
    kxw = kx.shape[1]
    return pl.pallas_call(
        body, name=name, grid=(nb + 1,),
        in_specs=[pl.BlockSpec(memory_space=pltpu.SMEM), cur(qw), cur(qw), cur(qw), cur_t(n_q),
                  cur(kxw), prv(kxw), cur_t(kxw), prv_t(kxw), cur(kxw), prv(kxw)],
        out_specs=[cur(qw), prv(kw2), prv(kw2), pl.BlockSpec((8, 128), lambda n: (0, 0))],
        out_shape=[jax.ShapeDtypeStruct((R, qw), F32), jax.ShapeDtypeStruct((R, kw2), F32),
                   jax.ShapeDtypeStruct((R, kw2), F32), jax.ShapeDtypeStruct((8, 128), F32)],
        scratch_shapes=[pltpu.VMEM((BLOCK, kw2), F32), pltpu.VMEM((BLOCK, kw2), F32)],
        compiler_params=_params(("arbitrary",)),
    )(sinks, q, do, o, lse, kx, kx, kxt, kxt, vx, vx)


def loss_bwd(h, g, target, *, name):
    R, D = h.shape
    nb = R // BLOCK

    def body(h_ref, g_ref, t_ref, loss_ref, dh_ref, dhb_ref, dg_ref):
        n = pl.program_id(0)
        xhat, rstd = _rms_parts(h_ref[...])
        gv = g_ref[...]
        diff = jnp.where(n > 0, xhat * gv - t_ref[...], 0.0)
        part = (0.5 / D) * jnp.sum(jnp.sum(diff * diff, axis=1, keepdims=True), axis=0, keepdims=True)
        dout = diff * (1.0 / D)
        dxh = dout * gv
        dh = rstd * (dxh - xhat * jnp.mean(dxh * xhat, axis=-1, keepdims=True))
        dh_ref[...] = dh
        dhb_ref[...] = dh.astype(BF16)
        dg = jnp.sum(dout * xhat, axis=0, keepdims=True)

        @pl.when(n == 0)
        def _():
            loss_ref[...] = jnp.zeros_like(loss_ref) + part
            dg_ref[...] = dg

        @pl.when(n > 0)
        def _():
            loss_ref[...] += part
            dg_ref[...] += dg

    blk = pl.BlockSpec((BLOCK, D), lambda n: (n, 0))
    return pl.pallas_call(
        body, name=name, grid=(nb,),
        in_specs=[blk, pl.BlockSpec((1, D), lambda n: (0, 0)),
                  pl.BlockSpec((BLOCK, D), lambda n: (jnp.maximum(n - 1, 0), 0))],
        out_specs=[pl.BlockSpec((8, 128), lambda n: (0, 0)), blk, blk, pl.BlockSpec((1, D), lambda n: (0, 0))],
        out_shape=[jax.ShapeDtypeStruct((8, 128), F32), jax.ShapeDtypeStruct((R, D), F32),
                   jax.ShapeDtypeStruct((R, D), BF16), jax.ShapeDtypeStruct((1, D), F32)],
        compiler_params=_params(("arbitrary",)),
    )(h, g, target)


def _adam_math(w, g, m, v):
    m = ADAM_B1 * m + (1.0 - ADAM_B1) * g
    v = ADAM_B2 * v + (1.0 - ADAM_B2) * (g * g)
    m_hat = m / (1.0 - ADAM_B1 ** ADAM_STEP)
    v_hat = v / (1.0 - ADAM_B2 ** ADAM_STEP)
    delta = -ADAM_LR * (m_hat / (jnp.sqrt(v_hat) + ADAM_EPS) + ADAM_WD * w)
    return delta, m, v


def adam(w, m, v, g, *, name):
    r, C = w.shape
    tr = _tile(r, 128, 8)

    def body(w_ref, m_ref, v_ref, g_ref, d_ref, mo_ref, vo_ref):
        d_ref[...], mo_ref[...], vo_ref[...] = _adam_math(w_ref[...], g_ref[...], m_ref[...], v_ref[...])

    blk = pl.BlockSpec((tr, C), lambda i: (i, 0))
    return pl.pallas_call(
        body, name=name, grid=(r // tr,), in_specs=[blk] * 4, out_specs=[blk] * 3,
        out_shape=[jax.ShapeDtypeStruct((r, C), F32)] * 3,
        compiler_params=_params(("parallel",)),
    )(w, m, v, g)


def _sum_blocks(p_ref, own_ref, me):
    acc = None
    for s in range(N_DEV):
        blk = jnp.where(me == s, own_ref[...], p_ref[s]).astype(F32)
        acc = blk if acc is None else acc + blk
    return acc


def _parts_specs(r, C, tr):
    blk = pl.BlockSpec((tr, C), lambda i, me: (i, 0))
    parts = pl.BlockSpec((N_DEV, tr, C), lambda i, me: (0, i, 0))
    own = pl.BlockSpec((None, tr, C), lambda i, me: (me[0], i, 0))
    return blk, parts, own


def adam_parts(w, m, v, parts, own, me, *, name):
    r, C = w.shape
    tr = _tile(r, 128, 8)

    def body(me_ref, w_ref, m_ref, v_ref, p_ref, own_ref, go_ref, d_ref, mo_ref, vo_ref):
        gv = _sum_blocks(p_ref, own_ref, me_ref[0])
        go_ref[...] = gv
        d_ref[...], mo_ref[...], vo_ref[...] = _adam_math(w_ref[...], gv, m_ref[...], v_ref[...])

    blk, pblk, oblk = _parts_specs(r, C, tr)
    return pl.pallas_call(
        body, name=name,
        grid_spec=pltpu.PrefetchScalarGridSpec(num_scalar_prefetch=1, grid=(r // tr,),
                                               in_specs=[blk, blk, blk, pblk, oblk], out_specs=[blk] * 4),
        out_shape=[jax.ShapeDtypeStruct((r, C), F32)] * 4,
        compiler_params=_params(("parallel",)),
    )(me, w, m, v, parts, own)


def sum_parts(parts, own, me, *, name):
    _, r, C = parts.shape
    tr = _tile(r, 128, 8)

    def body(me_ref, p_ref, own_ref, o_ref):
        o_ref[...] = _sum_blocks(p_ref, own_ref, me_ref[0])

    blk, pblk, oblk = _parts_specs(r, C, tr)
    return pl.pallas_call(
        body, name=name,
        grid_spec=pltpu.PrefetchScalarGridSpec(num_scalar_prefetch=1, grid=(r // tr,),
                                               in_specs=[pblk, oblk], out_specs=blk),
        out_shape=jax.ShapeDtypeStruct((r, C), F32),
        compiler_params=_params(("parallel",)),
    )(me, parts, own)


def cast_place(w, me, *, name):
    r, C = w.shape
    tr = _tile(r, 256, 16)

    def body(me_ref, w_ref, s_ref, l_ref):
        v = w_ref[...].astype(BF16)
        s_ref[...] = v
        l_ref[...] = v

    blk = pl.BlockSpec((tr, C), lambda i, me: (i, 0))
    return pl.pallas_call(
        body, name=name,
        grid_spec=pltpu.PrefetchScalarGridSpec(
            num_scalar_prefetch=1, grid=(r // tr,), in_specs=[blk],
            out_specs=[blk, pl.BlockSpec((None, tr, C), lambda i, me: (me[0], i, 0))]),
        out_shape=[jax.ShapeDtypeStruct((r, C), BF16), jax.ShapeDtypeStruct((N_DEV, r, C), BF16)],
        compiler_params=_params(("parallel",)),
    )(me, w)


def _coords():
    return lax.axis_index("x"), lax.axis_index("y"), lax.axis_index("c")


def _peer(m):
    x, y, c = _coords()
    px = 1 - x if m & 4 else x
    py = 1 - y if m & 2 else y
    pc = 1 - c if m & 1 else c
    return (px, py, pc), 4 * px + 2 * py + pc


def exchange(items, *, all_to_all, name):
    n = len(items)
    if all_to_all:
        out_shape = [jax.ShapeDtypeStruct(a.shape, a.dtype) for a in items]
    else:
        out_shape = [jax.ShapeDtypeStruct((N_DEV,) + a.shape, a.dtype) for a in items]

    def body(*refs):
        ins, outs = refs[:n], refs[n:2 * n]
        send_sems, recv_sems, local_sems = refs[2 * n:]
        x, y, c = _coords()
        me = 4 * x + 2 * y + c

        def src(i, idx):
            return ins[i].at[idx] if all_to_all else ins[i]

        local = [pltpu.make_async_copy(src(i, me), outs[i].at[me], local_sems.at[i]) for i in range(n)]
        for cp in local:
            cp.start()
        sends = []
        for m in range(1, N_DEV):
            peer, pidx = _peer(m)
            for i in range(n):
                k = i * (N_DEV - 1) + m - 1
                cp = pltpu.make_async_remote_copy(src_ref=src(i, pidx), dst_ref=outs[i].at[me],
                                                  send_sem=send_sems.at[k], recv_sem=recv_sems.at[k],
                                                  device_id=peer, device_id_type=MESH)
                cp.start()
                sends.append(cp)
        for m in range(1, N_DEV):
            peer, pidx = _peer(m)
            for i in range(n):
                k = i * (N_DEV - 1) + m - 1
                pltpu.make_async_remote_copy(src_ref=src(i, pidx), dst_ref=outs[i].at[pidx],
                                             send_sem=send_sems.at[k], recv_sem=recv_sems.at[k],
                                             device_id=peer, device_id_type=MESH).wait_recv()
        for cp in sends:
            cp.wait_send()
        for cp in local:
            cp.wait()

    any_spec = pl.BlockSpec(memory_space=pl.ANY)
    return pl.pallas_call(
        body, name=name, in_specs=[any_spec] * n, out_specs=[any_spec] * n, out_shape=out_shape,
        scratch_shapes=[pltpu.SemaphoreType.DMA((n * (N_DEV - 1),)), pltpu.SemaphoreType.DMA((n * (N_DEV - 1),)),
                        pltpu.SemaphoreType.DMA((n,))],
    )(*items)


HBM_SPEC = pl.BlockSpec(memory_space=pltpu.HBM)
SEM_SPEC = pl.BlockSpec(memory_space=pltpu.SEMAPHORE)
SPLIT_PARAMS = pltpu.CompilerParams(has_side_effects=pltpu.SideEffectType.DATAFLOW_SIDE_EFFECTING)


def _split_copies(src_ref, land_ref, send_sems, recv_sems, all_to_all):
    x, y, c = _coords()
    me = 4 * x + 2 * y + c
    copies = []
    for m in range(1, N_DEV):
        peer, pidx = _peer(m)
        copies.append(pltpu.make_async_remote_copy(
            src_ref=src_ref.at[pidx] if all_to_all else src_ref, dst_ref=land_ref.at[me],
            send_sem=send_sems.at[m - 1], recv_sem=recv_sems.at[m - 1], device_id=peer, device_id_type=MESH))
    return copies


def copy_start(items, lands=None, *, all_to_all, name):
    n = len(items)
    if lands is None:
        lands = [lax.empty(a.shape if all_to_all else (N_DEV,) + a.shape, a.dtype) for a in items]

    def body(*refs):
        srcs, lnds, outs = refs[:n], refs[n:2 * n], refs[2 * n:]
        for i in range(n):
            for cp in _split_copies(srcs[i], lnds[i], outs[4 * i], outs[4 * i + 1], all_to_all):
                cp.start()
        outs[4 * n][...] = jnp.zeros((8, 128), F32)

    out_shape, out_specs, aliases = [], [], {}
    for i, (a, l) in enumerate(zip(items, lands)):
        out_shape += [pltpu.SemaphoreType.DMA((N_DEV - 1,)), pltpu.SemaphoreType.DMA((N_DEV - 1,)),
                      pltpu.HBM(a.shape, a.dtype), pltpu.HBM(l.shape, l.dtype)]
        out_specs += [SEM_SPEC, SEM_SPEC, HBM_SPEC, HBM_SPEC]
        aliases[i] = 4 * i + 2
        aliases[n + i] = 4 * i + 3
    out_shape.append(jax.ShapeDtypeStruct((8, 128), F32))
    out_specs.append(pl.BlockSpec(memory_space=pltpu.VMEM))
    hbm = lambda a: pltpu.with_memory_space_constraint(a, pltpu.HBM)
    res = pl.pallas_call(
        body, name=name, in_specs=[HBM_SPEC] * (2 * n), out_specs=out_specs, out_shape=out_shape,
        input_output_aliases=aliases, compiler_params=SPLIT_PARAMS,
    )(*[hbm(a) for a in items], *[hbm(l) for l in lands])
    return [tuple(res[4 * i:4 * i + 4]) for i in range(n)], res[4 * n]


def copy_wait(handle, after, *, all_to_all, name):
    send_sems, recv_sems, src, land = handle

    def body(src_ref, land_ref, send_ref, recv_ref, after_ref, src_out, got_ref):
        for cp in _split_copies(src_ref, land_ref, send_ref, recv_ref, all_to_all):
            cp.wait_send()
            cp.wait_recv()

    return pl.pallas_call(
        body, name=name, in_specs=[HBM_SPEC, HBM_SPEC, SEM_SPEC, SEM_SPEC, ANY_SPEC],
        out_specs=[HBM_SPEC, HBM_SPEC],
        out_shape=[pltpu.HBM(src.shape, src.dtype), pltpu.HBM(land.shape, land.dtype)],
        input_output_aliases={0: 0, 1: 1}, compiler_params=SPLIT_PARAMS,
    )(src, land, send_sems, recv_sems, after)


def kernel(x, meta_tokens, norm_mix_0, w_in_conv, conv_w, w_out_conv, norm_mlp_0, w_up_0, w_down_0, norm_mix_1, w_qkv, attn_sinks, w_o, norm_mlp_1, w_up_1, w_down_1, norm_final, loss_target, m_meta_tokens, m_norm_mix_0, m_w_in_conv, m_conv_w, m_w_out_conv, m_norm_mlp_0, m_w_up_0, m_w_down_0, m_norm_mix_1, m_w_qkv, m_attn_sinks, m_w_o, m_norm_mlp_1, m_w_up_1, m_w_down_1, m_norm_final, v_meta_tokens, v_norm_mix_0, v_w_in_conv, v_conv_w, v_w_out_conv, v_norm_mlp_0, v_w_up_0, v_w_down_0, v_norm_mix_1, v_w_qkv, v_attn_sinks, v_w_o, v_norm_mlp_1, v_w_up_1, v_w_down_1, v_norm_final):
    L, D = x.shape[1], x.shape[2]
    n_meta = meta_tokens.shape[0]
    pad = BLOCK - n_meta
    R = BLOCK + L
    n_q = D // HEAD_DIM
    n_kv = n_q // GROUP
    assert n_kv % 2 == 0 and L % BLOCK == 0 and D % 128 == 0
    x = x.reshape(L, D)
    target = loss_target.reshape(L, D)
    x_id, y_id, c_id = _coords()
    me = 4 * x_id + 2 * y_id + c_id

    col_names = ("in", "up0", "qkv", "up1")
    col_w = dict(zip(col_names, (w_in_conv, w_up_0, w_qkv, w_up_1)))
    row_names = ("out", "down0", "o", "down1")
    row_w = dict(zip(row_names, (w_out_conv, w_down_0, w_o, w_down_1)))
    me_arr = jnp.reshape(me, (1,)).astype(jnp.int32)
    natural = {k: col_w[k].T for k in col_names}
    natural.update(row_w)
    use_order = ("in", "out", "up0", "down0", "qkv", "o", "up1", "down1")
    placed = [cast_place(natural[k], me_arr, name="cast_" + k) for k in use_order]
    handles, token = copy_start([p[0] for p in placed], [p[1] for p in placed], all_to_all=False, name="gather_start")
    handles = dict(zip(use_order, handles))

    def weight(k, after):
        return copy_wait(handles[k], after, all_to_all=False, name="gather_wait_" + k)[1].reshape(-1, D)

    small_in = exchange([meta_tokens, conv_w], all_to_all=False, name="comm_gather")
    meta_full = jnp.transpose(small_in[0], (1, 0, 2)).reshape(n_meta, D)
    conv_full = jnp.transpose(small_in[1], (1, 0, 2)).reshape(conv_w.shape[0], D)

    vec = lambda a: a.reshape(1, D)
    pos = jnp.arange(R, dtype=F32) - pad
    inv = ROPE_THETA ** (-jnp.arange(0, HEAD_DIM, 2, dtype=F32) / HEAD_DIM)
    ang = pos[:, None] * inv[None, :]
    cos32, sin32 = jnp.cos(ang), jnp.sin(ang)
    cos = jnp.concatenate([cos32] * 4, axis=1)
    sin = jnp.concatenate([-sin32, sin32, -sin32, sin32], axis=1)

    W = {}
    head = jnp.concatenate([jnp.zeros((pad, D), F32), meta_full], axis=0)
    h0, n0 = first_norm(head, x, vec(norm_mix_0), token, name="norm0")
    W["in"] = weight("in", n0)
    bcu = mm(n0, W["in"], name="in_proj", out_dtype=BF16, b_rows_are_n=True)
    gated = conv_fwd(bcu, conv_full, name="conv_fwd")
    W["out"] = weight("out", gated)
    h1 = mm(gated, W["out"], name="out_proj", out_dtype=F32, b_rows_are_n=False, epi="add", extra=h0)
    n1 = norm_fwd(h1, vec(norm_mlp_0), h1, name="norm1")
    W["up0"] = weight("up0", n1)
    a0 = mm(n1, W["up0"], name="up0", out_dtype=BF16, b_rows_are_n=True, epi="relu")
    W["down0"] = weight("down0", a0)
    h2 = mm(a0, W["down0"], name="down0", out_dtype=F32, b_rows_are_n=False, epi="add", extra=h1, a_sq=True)
    n2 = norm_fwd(h2, vec(norm_mix_1), h2, name="norm2")
    W["qkv"] = weight("qkv", n2)
    qkv = mm(n2, W["qkv"], name="qkv_proj", out_dtype=F32, b_rows_are_n=True)
    q, kx, vx, kxt, vxt = rope_fwd(qkv, cos, sin, n_q=n_q, n_kv=n_kv, name="rope_fwd")
    o, lse = attn_fwd(q, kx, vxt, attn_sinks, pad=pad, name="attn_fwd")
    W["o"] = weight("o", o)
    h3 = mm(o, W["o"], name="o_proj", out_dtype=F32, b_rows_are_n=False, epi="add", extra=h2)
    n3 = norm_fwd(h3, vec(norm_mlp_1), h3, name="norm3")
    W["up1"] = weight("up1", n3)
    a1 = mm(n3, W["up1"], name="up1", out_dtype=BF16, b_rows_are_n=True, epi="relu")
    W["down1"] = weight("down1", a1)
    h4 = mm(a1, W["down1"], name="down1", out_dtype=F32, b_rows_are_n=False, epi="add", extra=h3, a_sq=True)

    loss_part, dh4, dh4b, dg_final = loss_bwd(h4, vec(norm_final), target, name="loss_bwd")
    loss = lax.psum(loss_part[0, 0], ("x", "y", "c"))

    sent = {}

    def scatter(k, dw):
        (sent[k],), tok = copy_start([dw.reshape(N_DEV, -1, D)], all_to_all=True, name="a2a_start_" + k)
        return tok

    t = scatter("down1", mm_tn(a1, dh4b, name="dw_down1", a_sq=True))
    dup1 = mm(dh4b, W["down1"], name="d_down1", out_dtype=BF16, b_rows_are_n=True, epi="mul2a", extra=a1, dep=t)
    t = scatter("up1", mm_tn(dup1, n3, name="dw_up1"))
    dn3 = mm(dup1, W["up1"], name="d_up1", out_dtype=F32, b_rows_are_n=False, dep=t)
    dh3, dh3b, dg_mlp1 = norm_bwd(dn3, h3, vec(norm_mlp_1), dh4, name="norm3_bwd")

    t = scatter("o", mm_tn(o, dh3b, name="dw_o"))
    do = mm(dh3b, W["o"], name="d_o", out_dtype=BF16, b_rows_are_n=True, dep=t)
    dq, dkx, dvx, dsinks = attn_bwd(q, kx, kxt, vx, o, do, lse, attn_sinks, pad=pad, name="attn_bwd")
    dqkv = rope_bwd(dq, dkx, dvx, cos, sin, n_q=n_q, n_kv=n_kv, name="rope_bwd")
    t = scatter("qkv", mm_tn(dqkv, n2, name="dw_qkv"))
    dn2 = mm(dqkv, W["qkv"], name="d_qkv", out_dtype=F32, b_rows_are_n=False, dep=t)
    dh2, dh2b, dg_mix1 = norm_bwd(dn2, h2, vec(norm_mix_1), dh3, name="norm2_bwd")

    t = scatter("down0", mm_tn(a0, dh2b, name="dw_down0", a_sq=True))
    dup0 = mm(dh2b, W["down0"], name="d_down0", out_dtype=BF16, b_rows_are_n=True, epi="mul2a", extra=a0, dep=t)
    t = scatter("up0", mm_tn(dup0, n1, name="dw_up0"))
    dn1 = mm(dup0, W["up0"], name="d_up0", out_dtype=F32, b_rows_are_n=False, dep=t)
    dh1, dh1b, dg_mlp0 = norm_bwd(dn1, h1, vec(norm_mlp_0), dh2, name="norm1_bwd")

    t = scatter("out", mm_tn(gated, dh1b, name="dw_out"))
    dgated = mm(dh1b, W["out"], name="d_out", out_dtype=F32, b_rows_are_n=True, dep=t)
    dbcu, dconv = conv_bwd(bcu, conv_full, dgated, name="conv_bwd")
    t = scatter("in", mm_tn(dbcu, n0, name="dw_in"))
    dn0 = mm(dbcu, W["in"], name="d_in", out_dtype=F32, b_rows_are_n=False, dep=t)
    dhead, dx, dg_mix0 = last_norm_bwd(dn0, h0, vec(norm_mix_0), dh1, name="norm0_bwd")
    grad_x = dx.reshape(1, L, D)

    recv = {}
    for k in ("down1", "up1", "o", "qkv", "down0", "up0", "out", "in"):
        sent[k], recv[k] = copy_wait(sent[k], dx, all_to_all=True, name="a2a_wait_" + k)

    n_sink = attn_sinks.shape[0]
    slab = jnp.concatenate([
        dg_mix0, dg_mlp0, dg_mix1, dg_mlp1, dg_final,
        jnp.pad(dsinks[0:1, :n_sink], ((0, 0), (0, D - n_sink))), jnp.zeros((2, D), F32),
        dconv, dhead[pad:BLOCK]], axis=0)
    slabs = exchange([slab], all_to_all=False, name="comm_small")[0]
    small = sum_parts(slabs, slabs, me_arr, name="sum_small")
    cols = D // N_DEV
    my_cols = lambda a: lax.dynamic_slice_in_dim(a, me * cols, cols, axis=1)

    grads, deltas, new_m, new_v = {}, {}, {}, {}

    def update(key, w, m, v, g, shape):
        s2 = (1, -1) if w.ndim == 1 else w.shape
        if g.ndim == 3:
            g_, d_, m_, v_ = adam_parts(w, m, v, g, sent[key_of[key]], me_arr, name="adam_" + key)
        else:
            g_ = g.reshape(s2)
            d_, m_, v_ = adam(w.reshape(s2), m.reshape(s2), v.reshape(s2), g_, name="adam_" + key)
        grads[key], deltas[key], new_m[key], new_v[key] = (t.reshape(shape) for t in (g_, d_, m_, v_))

    update("meta_tokens", meta_tokens, m_meta_tokens, v_meta_tokens, my_cols(small[16:16 + n_meta]), meta_tokens.shape)
    update("norm_mix_0", norm_mix_0, m_norm_mix_0, v_norm_mix_0, small[0], (D,))
    update("conv_w", conv_w, m_conv_w, v_conv_w, my_cols(small[8:8 + conv_w.shape[0]]), conv_w.shape)
    update("norm_mlp_0", norm_mlp_0, m_norm_mlp_0, v_norm_mlp_0, small[1], (D,))
    update("norm_mix_1", norm_mix_1, m_norm_mix_1, v_norm_mix_1, small[2], (D,))
    update("attn_sinks", attn_sinks, m_attn_sinks, v_attn_sinks, small[5, :n_sink], (n_sink,))
    update("norm_mlp_1", norm_mlp_1, m_norm_mlp_1, v_norm_mlp_1, small[3], (D,))
    update("norm_final", norm_final, m_norm_final, v_norm_final, small[4], (D,))
    big = {"in": ("w_in_conv", w_in_conv, m_w_in_conv, v_w_in_conv), "up0": ("w_up_0", w_up_0, m_w_up_0, v_w_up_0),
           "qkv": ("w_qkv", w_qkv, m_w_qkv, v_w_qkv), "up1": ("w_up_1", w_up_1, m_w_up_1, v_w_up_1),
           "out": ("w_out_conv", w_out_conv, m_w_out_conv, v_w_out_conv),
           "down0": ("w_down_0", w_down_0, m_w_down_0, v_w_down_0), "o": ("w_o", w_o, m_w_o, v_w_o),
           "down1": ("w_down_1", w_down_1, m_w_down_1, v_w_down_1)}
    key_of = {big[k][0]: k for k in big}
    for k in col_names:
        key, w, m, v = big[k]
        update(key, w, m, v, sum_parts(recv[k], sent[k], me_arr, name="sum_" + k).T, w.shape)
    for k in row_names:
        key, w, m, v = big[k]
        update(key, w, m, v, recv[k], w.shape)

    order = ("meta_tokens", "norm_mix_0", "w_in_conv", "conv_w", "w_out_conv", "norm_mlp_0", "w_up_0", "w_down_0",
             "norm_mix_1", "w_qkv", "attn_sinks", "w_o", "norm_mlp_1", "w_up_1", "w_down_1", "norm_final")
    return (loss, grad_x, *[grads[k] for k in order], *[deltas[k] for k in order],
            *[new_m[k] for k in order], *[new_v[k] for k in order])
```

```python
import functools

import jax
import jax.numpy as jnp
from jax import lax
from jax.experimental import pallas as pl
from jax.experimental.pallas import tpu as pltpu

F32 = jnp.float32
BF16 = jnp.bfloat16

HEAD_DIM = 64
GROUP = 8
BLOCK = 128
N_DEV = 8
RMS_EPS = 1e-5
NEG_INF = -1e30
ROPE_THETA = 10000.0
ADAM_LR = 0.001
ADAM_B1 = 0.9
ADAM_B2 = 0.999
ADAM_EPS = 1e-08
ADAM_WD = 0.01
ADAM_STEP = 10
VMEM_LIMIT = 60 * 1024 * 1024
MESH = pl.DeviceIdType.MESH


def _tile(n, target, mult):
    best = None
    for t in range(mult, min(n, target) + 1, mult):
        if n % t == 0:
            best = t
    return best if best is not None else n


def _params(sem):
    return pltpu.CompilerParams(dimension_semantics=sem, vmem_limit_bytes=VMEM_LIMIT)


ANY_SPEC = pl.BlockSpec(memory_space=pl.ANY)


LEFT_BLOCK_BYTES = 14 * 1024 * 1024
LONG_LEFT_BLOCK_BYTES = 11 * 1024 * 1024


def mm(a, b, *, name, out_dtype, b_rows_are_n, epi=None, extra=None, a_sq=False, dep=None):
    M, K = a.shape
    N = b.shape[0] if b_rows_are_n else b.shape[1]
    tm = _tile(M, (LEFT_BLOCK_BYTES if K <= 6144 else LONG_LEFT_BLOCK_BYTES) // (2 * K), 16)
    tn = _tile(N, 512, 128)

    def body(*refs):
        a_ref, b_ref, e_ref, o_ref = refs[0], refs[1], refs[2], refs[-1]
        av = a_ref[...]
        if a_sq:
            av = av.astype(F32)
            av = (av * av).astype(BF16)
        dims = (((1,), (1,)), ((), ())) if b_rows_are_n else (((1,), (0,)), ((), ()))
        acc = lax.dot_general(av, b_ref[...], dims, preferred_element_type=F32)
        if epi == "relu":
            acc = jnp.maximum(acc, 0.0)
        elif epi == "mul2a":
            acc = acc * (2.0 * e_ref[...].astype(F32))
        elif epi == "add":
            acc = acc + e_ref[...]
        o_ref[...] = acc.astype(o_ref.dtype)

    b_spec = pl.BlockSpec((tn, K), lambda i, j: (j, 0)) if b_rows_are_n else pl.BlockSpec((K, tn), lambda i, j: (0, j))
    in_specs = [pl.BlockSpec((tm, K), lambda i, j: (i, 0)), b_spec]
    args = [a, b]
    if extra is not None:
        in_specs.append(pl.BlockSpec((tm, tn), lambda i, j: (i, j)))
        args.append(extra)
    if dep is not None:
        in_specs.append(ANY_SPEC)
        args.append(dep)
    return pl.pallas_call(
        body, name=name, grid=(M // tm, N // tn), in_specs=in_specs,
        out_specs=pl.BlockSpec((tm, tn), lambda i, j: (i, j)),
        out_shape=jax.ShapeDtypeStruct((M, N), out_dtype),
        compiler_params=_params(("parallel", "parallel")),
    )(*args)


def _rms_parts(h):
    rstd = lax.rsqrt(jnp.mean(h * h, axis=-1, keepdims=True) + RMS_EPS)
    return h * rstd, rstd


def _norm_bwd_rows(dn, h, g, dres):
    xhat, rstd = _rms_parts(h)
    dxh = dn * g
    dh = dres + rstd * (dxh - xhat * jnp.mean(dxh * xhat, axis=-1, keepdims=True))
    return dh, jnp.sum(dn * xhat, axis=0, keepdims=True)


def mm_tn(a, b, dep, *, name, a_sq=False):
    T, M = a.shape
    N = b.shape[1]
    tm = _tile(M, 1024, 128)
    tk = _tile(T, 2080, 16)
    nk = T // tk
    tc = _tile(tm, 256, 128)

    def body(a_ref, b_ref, dep_ref, o_ref, acc_ref):
        k = pl.program_id(1)

        def chunks(first):
            bv = b_ref[...]
            for r0 in range(0, tm, tc):
                av = a_ref[:, r0:r0 + tc]
                if a_sq:
                    av = av.astype(F32)
                    av = (av * av).astype(BF16)
                part = lax.dot_general(av, bv, (((0,), (0,)), ((), ())), preferred_element_type=F32)
                if first:
                    acc_ref[r0:r0 + tc, :] = part
                else:
                    acc_ref[r0:r0 + tc, :] += part

        @pl.when(k == 0)
        def _():
            chunks(True)

        @pl.when(k > 0)
        def _():
            chunks(False)

        @pl.when(k == nk - 1)
        def _():
            o_ref[...] = acc_ref[...].astype(BF16)

    return pl.pallas_call(
        body, name=name, grid=(M // tm, nk),
        in_specs=[pl.BlockSpec((tk, tm), lambda i, k: (k, i)), pl.BlockSpec((tk, N), lambda i, k: (k, 0)), ANY_SPEC],
        out_specs=pl.BlockSpec((tm, N), lambda i, k: (i, 0)),
        out_shape=jax.ShapeDtypeStruct((M, N), BF16),
        scratch_shapes=[pltpu.VMEM((tm, N), F32)],
        compiler_params=_params(("parallel", "arbitrary")),
    )(a, b, dep)


def norm_fwd(h, g, dep, *, name):
    R, D = h.shape
    tr = _tile(R, 320, 16)

    def body(h_ref, g_ref, dep_ref, n_ref):
        xhat, _ = _rms_parts(h_ref[...])
        n_ref[...] = (xhat * g_ref[...]).astype(BF16)

    return pl.pallas_call(
        body, name=name, grid=(R // tr,),
        in_specs=[pl.BlockSpec((tr, D), lambda i: (i, 0)), pl.BlockSpec((1, D), lambda i: (0, 0)), ANY_SPEC],
        out_specs=pl.BlockSpec((tr, D), lambda i: (i, 0)),
        out_shape=jax.ShapeDtypeStruct((R, D), BF16),
        compiler_params=_params(("parallel",)),
    )(h, g, dep)


def norm_bwd(dn, h, g, dres, *, name):
    R, D = h.shape
    tr = _tile(R, 320, 16)

    def body(dn_ref, h_ref, g_ref, dres_ref, dh_ref, dhb_ref, dg_ref):
        i = pl.program_id(0)
        dh, dg = _norm_bwd_rows(dn_ref[...].astype(F32), h_ref[...], g_ref[...], dres_ref[...])
        dh_ref[...] = dh
        dhb_ref[...] = dh.astype(BF16)

        @pl.when(i == 0)
        def _():
            dg_ref[...] = dg

        @pl.when(i > 0)
        def _():
            dg_ref[...] += dg

    row = pl.BlockSpec((tr, D), lambda i: (i, 0))
    vec = pl.BlockSpec((1, D), lambda i: (0, 0))
    return pl.pallas_call(
        body, name=name, grid=(R // tr,), in_specs=[row, row, vec, row], out_specs=[row, row, vec],
        out_shape=[jax.ShapeDtypeStruct((R, D), F32), jax.ShapeDtypeStruct((R, D), BF16),
                   jax.ShapeDtypeStruct((1, D), F32)],
        compiler_params=_params(("arbitrary",)),
    )(dn, h, g, dres)


def first_norm(head, x, g, dep, *, name):
    L, D = x.shape
    nb = L // BLOCK + 1

    def body(head_ref, x_ref, g_ref, dep_ref, h_ref, n_ref):
        i = pl.program_id(0)
        hv = jnp.where(i == 0, head_ref[...], x_ref[...])
        h_ref[...] = hv
        xhat, _ = _rms_parts(hv)
        n_ref[...] = (xhat * g_ref[...]).astype(BF16)

    blk = pl.BlockSpec((BLOCK, D), lambda i: (i, 0))
    return pl.pallas_call(
        body, name=name, grid=(nb,),
        in_specs=[pl.BlockSpec((BLOCK, D), lambda i: (0, 0)),
                  pl.BlockSpec((BLOCK, D), lambda i: (jnp.maximum(i - 1, 0), 0)),
                  pl.BlockSpec((1, D), lambda i: (0, 0)), ANY_SPEC],
        out_specs=[blk, blk],
        out_shape=[jax.ShapeDtypeStruct((BLOCK + L, D), F32), jax.ShapeDtypeStruct((BLOCK + L, D), BF16)],
        compiler_params=_params(("parallel",)),
    )(head, x, g, dep)


def last_norm_bwd(dn, h, g, dres, *, name):
    R, D = h.shape
    nb = R // BLOCK

    def body(dn_ref, h_ref, g_ref, dres_ref, dhead_ref, dx_ref, dg_ref):
        i = pl.program_id(0)
        dh, dg = _norm_bwd_rows(dn_ref[...].astype(F32), h_ref[...], g_ref[...], dres_ref[...])
        dx_ref[...] = dh

        @pl.when(i == 0)
        def _():
            dhead_ref[...] = dh
            dg_ref[...] = dg

        @pl.when(i > 0)
        def _():
            dg_ref[...] += dg

    blk = pl.BlockSpec((BLOCK, D), lambda i: (i, 0))
    vec = pl.BlockSpec((1, D), lambda i: (0, 0))
    return pl.pallas_call(
        body, name=name, grid=(nb,), in_specs=[blk, blk, vec, blk],
        out_specs=[pl.BlockSpec((BLOCK, D), lambda i: (0, 0)),
                   pl.BlockSpec((BLOCK, D), lambda i: (jnp.maximum(i - 1, 0), 0)), vec],
        out_shape=[jax.ShapeDtypeStruct((BLOCK, D), F32), jax.ShapeDtypeStruct((R - BLOCK, D), F32),
                   jax.ShapeDtypeStruct((1, D), F32)],
        compiler_params=_params(("arbitrary",)),
    )(dn, h, g, dres)


HALO = 16


def conv_fwd(bcu, conv_w, *, name):
    R, D3 = bcu.shape
    D = D3 // 3
    tr = _tile(R, 320, 16)
    tc = _tile(D, 512, 128)
    nc = D // tc
    hb = tr // HALO

    def body(b_ref, c_ref, u_ref, ch_ref, uh_ref, w_ref, o_ref, vbuf):
        i = pl.program_id(0)
        v = c_ref[...].astype(F32) * u_ref[...].astype(F32)
        vbuf[0:HALO, :] = jnp.where(i > 0, ch_ref[...].astype(F32) * uh_ref[...].astype(F32), 0.0)
        vbuf[HALO:HALO + tr, :] = v
        v1 = vbuf[HALO - 1:HALO - 1 + tr, :]
        v2 = vbuf[HALO - 2:HALO - 2 + tr, :]
        conv = w_ref[0:1, :] * v2 + w_ref[1:2, :] * v1 + w_ref[2:3, :] * v
        o_ref[...] = (b_ref[...].astype(F32) * conv).astype(BF16)

    def prev(i):
        return jnp.maximum(i * hb - 1, 0)

    return pl.pallas_call(
        body, name=name, grid=(R // tr, nc),
        in_specs=[pl.BlockSpec((tr, tc), lambda i, j: (i, j)),
                  pl.BlockSpec((tr, tc), lambda i, j: (i, nc + j)),
                  pl.BlockSpec((tr, tc), lambda i, j: (i, 2 * nc + j)),
                  pl.BlockSpec((HALO, tc), lambda i, j: (prev(i), nc + j)),
                  pl.BlockSpec((HALO, tc), lambda i, j: (prev(i), 2 * nc + j)),
                  pl.BlockSpec((3, tc), lambda i, j: (0, j))],
        out_specs=pl.BlockSpec((tr, tc), lambda i, j: (i, j)),
        out_shape=jax.ShapeDtypeStruct((R, D), BF16),
        scratch_shapes=[pltpu.VMEM((HALO + tr, tc), F32)],
        compiler_params=_params(("parallel", "parallel")),
    )(bcu, bcu, bcu, bcu, bcu, conv_w)


def conv_bwd(bcu, conv_w, dg, *, name):
    R, D3 = bcu.shape
    D = D3 // 3
    tr = _tile(R, 320, 16)
    tc = _tile(D, 512, 128)
    hb = tr // HALO
    nt = R // tr
    last_h = R // HALO - 1

    def body(x_ref, xp_ref, xn_ref, dg_ref, dgn_ref, w_ref, o_ref, dw_ref, vbuf, dbuf):
        i = pl.program_id(0)
        row8 = lax.broadcasted_iota(jnp.int32, (8, tc), 0)
        for c0 in range(0, D, tc):
            cb, cc, cu = slice(c0, c0 + tc), slice(D + c0, D + c0 + tc), slice(2 * D + c0, 2 * D + c0 + tc)
            w0, w1, w2 = w_ref[0:1, cb], w_ref[1:2, cb], w_ref[2:3, cb]
            b = x_ref[:, cb].astype(F32)
            c = x_ref[:, cc].astype(F32)
            u = x_ref[:, cu].astype(F32)
            v = c * u
            vbuf[0:HALO, :] = jnp.where(i > 0, xp_ref[:, cc].astype(F32) * xp_ref[:, cu].astype(F32), 0.0)
            vbuf[HALO:HALO + tr, :] = v
            v1 = vbuf[HALO - 1:HALO - 1 + tr, :]
            v2 = vbuf[HALO - 2:HALO - 2 + tr, :]
            dgv = dg_ref[:, cb]
            o_ref[:, cb] = (dgv * (w0 * v2 + w1 * v1 + w2 * v)).astype(BF16)
            dconv = dgv * b
            dbuf[0:tr, :] = dconv
            dbuf[tr:tr + HALO, :] = jnp.where(i < nt - 1, dgn_ref[:, cb] * xn_ref[:, cb].astype(F32), 0.0)
            dv = w2 * dconv + w1 * dbuf[1:1 + tr, :] + w0 * dbuf[2:2 + tr, :]
            o_ref[:, cc] = (dv * u).astype(BF16)
            o_ref[:, cu] = (dv * c).astype(BF16)
            dw = jnp.where(row8 == 0, jnp.sum(dconv * v2, axis=0, keepdims=True),
                           jnp.where(row8 == 1, jnp.sum(dconv * v1, axis=0, keepdims=True),
                                     jnp.where(row8 == 2, jnp.sum(dconv * v, axis=0, keepdims=True), 0.0)))

            @pl.when(i == 0)
            def _():
                dw_ref[:, cb] = dw

            @pl.when(i > 0)
            def _():
                dw_ref[:, cb] += dw

    def prev(i):
        return jnp.maximum(i * hb - 1, 0)

    def nxt(i):
        return jnp.minimum((i + 1) * hb, last_h)

    return pl.pallas_call(
        body, name=name, grid=(nt,),
        in_specs=[pl.BlockSpec((tr, D3), lambda i: (i, 0)),
                  pl.BlockSpec((HALO, D3), lambda i: (prev(i), 0)),
                  pl.BlockSpec((HALO, D3), lambda i: (nxt(i), 0)),
                  pl.BlockSpec((tr, D), lambda i: (i, 0)),
                  pl.BlockSpec((HALO, D), lambda i: (nxt(i), 0)),
                  pl.BlockSpec((3, D), lambda i: (0, 0))],
        out_specs=[pl.BlockSpec((tr, D3), lambda i: (i, 0)), pl.BlockSpec((8, D), lambda i: (0, 0))],
        out_shape=[jax.ShapeDtypeStruct((R, D3), BF16), jax.ShapeDtypeStruct((8, D), F32)],
        scratch_shapes=[pltpu.VMEM((HALO + tr, tc), F32), pltpu.VMEM((tr + HALO, tc), F32)],
        compiler_params=_params(("arbitrary",)),
    )(bcu, bcu, bcu, dg, dg, conv_w)


def _swap32(x):
    w = x.shape[1]
    lane = lax.broadcasted_iota(jnp.int32, x.shape, 1)
    return jnp.where((lane & (HEAD_DIM - 1)) < HEAD_DIM // 2, pltpu.roll(x, w - HEAD_DIM // 2, axis=1),
                     pltpu.roll(x, HEAD_DIM // 2, axis=1))


def _rope(x, cos, sin):
    return x * cos + _swap32(x) * sin


def rope_fwd(qkv, cos, sin, *, n_q, n_kv, name):
    R, W = qkv.shape
    qw = n_q * HEAD_DIM
    kw = n_kv * HEAD_DIM
    tr = _tile(R, 640, 128)

    def expand(y, ref, ref_t, c):
        lane = lax.broadcasted_iota(jnp.int32, y.shape, 1)
        lo = jnp.where(lane < HEAD_DIM, y, 0.0)
        hi = jnp.where(lane >= HEAD_DIM, y, 0.0)
        tiles = (lo, pltpu.roll(lo, HEAD_DIM, axis=1), pltpu.roll(hi, HEAD_DIM, axis=1), hi)
        for t, tile in enumerate(tiles):
            c0 = 512 * c + 128 * t
            ref[:, c0:c0 + 128] = tile.astype(BF16)
            ref_t[c0:c0 + 128, :] = tile.T.astype(BF16)

    def body(x_ref, c_ref, s_ref, q_ref, k_ref, v_ref, kt_ref, vt_ref):
        cos = c_ref[...]
        sin = s_ref[...]
        for c in range(qw // 128):
            x = x_ref[:, 128 * c:128 * (c + 1)]
            q_ref[:, 128 * c:128 * (c + 1)] = (_rope(x, cos, sin) * (HEAD_DIM ** -0.5)).astype(BF16)
        for c in range(kw // 128):
            expand(_rope(x_ref[:, qw + 128 * c:qw + 128 * (c + 1)], cos, sin), k_ref, kt_ref, c)
            expand(x_ref[:, qw + kw + 128 * c:qw + kw + 128 * (c + 1)], v_ref, vt_ref, c)

    row = lambda w: pl.BlockSpec((tr, w), lambda i: (i, 0))
    col = pl.BlockSpec((4 * kw, tr), lambda i: (0, i))
    return pl.pallas_call(
        body, name=name, grid=(R // tr,),
        in_specs=[row(W), row(128), row(128)],
        out_specs=[row(qw), row(4 * kw), row(4 * kw), col, col],
        out_shape=[jax.ShapeDtypeStruct((R, qw), BF16), jax.ShapeDtypeStruct((R, 4 * kw), BF16),
                   jax.ShapeDtypeStruct((R, 4 * kw), BF16), jax.ShapeDtypeStruct((4 * kw, R), BF16),
                   jax.ShapeDtypeStruct((4 * kw, R), BF16)],
        compiler_params=_params(("parallel",)),
    )(qkv, cos, sin)


def rope_bwd(dq, dkx, dvx, cos, sin, *, n_q, n_kv, name):
    R = dq.shape[0]
    qw = n_q * HEAD_DIM
    kw = n_kv * HEAD_DIM
    W = qw + 2 * kw
    tr = _tile(R, 320, 16)

    def fold(ref, c):
        lane = lax.broadcasted_iota(jnp.int32, (tr, 128), 1)
        x0 = ref[:, 128 * (2 * c):128 * (2 * c + 1)]
        x1 = ref[:, 128 * (2 * c + 1):128 * (2 * c + 2)]
        f0 = x0 + pltpu.roll(x0, HEAD_DIM, axis=1)
        f1 = x1 + pltpu.roll(x1, HEAD_DIM, axis=1)
        return jnp.where(lane < HEAD_DIM, f0, f1)

    def body(dq_ref, dk_ref, dv_ref, c_ref, s_ref, o_ref):
        cos = c_ref[...]
        nsin = -s_ref[...]
        for c in range(qw // 128):
            y = dq_ref[:, 128 * c:128 * (c + 1)]
            o_ref[:, 128 * c:128 * (c + 1)] = (_rope(y, cos, nsin) * (HEAD_DIM ** -0.5)).astype(BF16)
        for c in range(kw // 128):
            o_ref[:, qw + 128 * c:qw + 128 * (c + 1)] = _rope(fold(dk_ref, c), cos, nsin).astype(BF16)
            o_ref[:, qw + kw + 128 * c:qw + kw + 128 * (c + 1)] = fold(dv_ref, c).astype(BF16)

    row = lambda w: pl.BlockSpec((tr, w), lambda i: (i, 0))
    return pl.pallas_call(
        body, name=name, grid=(R // tr,),
        in_specs=[row(qw), row(2 * kw), row(2 * kw), row(128), row(128)],
        out_specs=row(W),
        out_shape=jax.ShapeDtypeStruct((R, W), BF16),
        compiler_params=_params(("parallel",)),
    )(dq, dkx, dvx, cos, sin)


def _band_bias(n, pad):
    key = lax.broadcasted_iota(jnp.int32, (2 * BLOCK, BLOCK), 0)
    qry = lax.broadcasted_iota(jnp.int32, (2 * BLOCK, BLOCK), 1)
    kmin = jnp.where(n == 0, BLOCK + pad, jnp.where(n == 1, pad, 0))
    allowed = (key > qry) & (key <= qry + BLOCK) & (key >= kmin)
    return jnp.where(allowed, 0.0, NEG_INF).astype(F32)


def _dot_nt(a, b):
    return lax.dot_general(a, b, (((1,), (1,)), ((), ())), preferred_element_type=F32)


def _band(prev_ref, cur_ref, c0):
    return jnp.concatenate([prev_ref[:, c0:c0 + 128], cur_ref[:, c0:c0 + 128]], axis=0)


def _band_t(prev_ref, cur_ref, r0):
    return jnp.concatenate([prev_ref[r0:r0 + 128, :], cur_ref[r0:r0 + 128, :]], axis=1)


def attn_fwd(q, kx, vxt, sinks, *, pad, name):
    R, qw = q.shape
    n_q = qw // HEAD_DIM
    n_kv = kx.shape[1] // 256
    nb = R // BLOCK

    def body(s_ref, q_ref, kc_ref, kp_ref, vc_ref, vp_ref, o_ref, l_ref):
        n = pl.program_id(0)
        bias = _band_bias(n, pad)
        row = lax.broadcasted_iota(jnp.int32, (128, BLOCK), 0)

        def softmax(st, sink):
            st = st + bias
            m = jnp.maximum(jnp.max(st, axis=0, keepdims=True), sink)
            e = jnp.exp(st - m)
            den = jnp.sum(e, axis=0, keepdims=True) + jnp.exp(sink - m)
            return e.astype(BF16), 1.0 / den, m + jnp.log(den)

        for g in range(n_kv):
            k2 = jnp.concatenate([_band(kp_ref, kc_ref, 256 * g), _band(kp_ref, kc_ref, 256 * g + 128)], axis=0)
            v2 = jnp.concatenate([_band_t(vp_ref, vc_ref, 256 * g), _band_t(vp_ref, vc_ref, 256 * g + 128)], axis=1)
            for p in range(GROUP // 2):
                c0 = 128 * (g * (GROUP // 2) + p)
                he = GROUP * g + 2 * p
                q2 = q_ref[:, c0:c0 + 128]
                st = _dot_nt(k2, q2)
                ee, re, le = softmax(st[0:2 * BLOCK], s_ref[he])
                eo, ro, lo = softmax(st[2 * BLOCK:4 * BLOCK], s_ref[he + 1])
                o2t = jnp.dot(v2, jnp.concatenate([ee, eo], axis=0), preferred_element_type=F32)
                o2t = o2t * jnp.where(row < HEAD_DIM, re, ro)
                o_ref[:, c0:c0 + 128] = o2t.T.astype(BF16)
                l_ref[he:he + 1, :] = le
                l_ref[he + 1:he + 2, :] = lo

    cur = lambda w: pl.BlockSpec((BLOCK, w), lambda n: (n, 0))
    prv = lambda w: pl.BlockSpec((BLOCK, w), lambda n: (jnp.maximum(n - 1, 0), 0))
    cur_t = lambda h: pl.BlockSpec((h, BLOCK), lambda n: (0, n))
    prv_t = lambda h: pl.BlockSpec((h, BLOCK), lambda n: (0, jnp.maximum(n - 1, 0)))
    kxw = kx.shape[1]
    return pl.pallas_call(
        body, name=name, grid=(nb,),
        in_specs=[pl.BlockSpec(memory_space=pltpu.SMEM), cur(qw), cur(kxw), prv(kxw), cur_t(kxw), prv_t(kxw)],
        out_specs=[cur(qw), cur_t(n_q)],
        out_shape=[jax.ShapeDtypeStruct((R, qw), BF16), jax.ShapeDtypeStruct((n_q, R), F32)],
        compiler_params=_params(("parallel",)),
    )(sinks, q, kx, kx, vxt, vxt)


def attn_bwd(q, kx, kxt, vx, o, do, lse, sinks, *, pad, name):
    R, qw = q.shape
    n_q = qw // HEAD_DIM
    n_kv = kx.shape[1] // 256
    nb = R // BLOCK
    kw2 = n_kv * 128

    def body(s_ref, q_ref, do_ref, o_ref, l_ref, kc_ref, kp_ref, ktc_ref, ktp_ref, vc_ref, vp_ref,
             dq_ref, dk_ref, dv_ref, ds_ref, cdk, cdv):
        n = pl.program_id(0)

        @pl.when(n == 0)
        def _():
            cdk[...] = jnp.zeros_like(cdk)
            cdv[...] = jnp.zeros_like(cdv)
            ds_ref[...] = jnp.zeros_like(ds_ref)

        @pl.when(n < nb)
        def _():
            bias = _band_bias(n, pad)
            lane2 = lax.broadcasted_iota(jnp.int32, (2 * BLOCK, 128), 1)
            lane1 = lax.broadcasted_iota(jnp.int32, (1, 128), 1)
            sel_r = lax.broadcasted_iota(jnp.int32, (8, 128), 0)
            sel_l = lax.broadcasted_iota(jnp.int32, (8, 128), 1)
            sel = (((sel_r == 0) & (sel_l < HEAD_DIM)) | ((sel_r == 1) & (sel_l >= HEAD_DIM))).astype(BF16)
            dsink = jnp.zeros((1, 128), F32)
            for g in range(n_kv):
                k2 = jnp.concatenate([_band(kp_ref, kc_ref, 256 * g), _band(kp_ref, kc_ref, 256 * g + 128)], axis=0)
                kt2 = jnp.concatenate([_band_t(ktp_ref, ktc_ref, 256 * g), _band_t(ktp_ref, ktc_ref, 256 * g + 128)],
                                      axis=1)
                v2 = jnp.concatenate([_band(vp_ref, vc_ref, 256 * g), _band(vp_ref, vc_ref, 256 * g + 128)], axis=0)
                dk4 = jnp.zeros((4 * BLOCK, 128), F32)
                dv4 = jnp.zeros((4 * BLOCK, 128), F32)
                for p in range(GROUP // 2):
                    c0 = 128 * (g * (GROUP // 2) + p)
                    he = GROUP * g + 2 * p
                    q2 = q_ref[:, c0:c0 + 128]
                    do2 = do_ref[:, c0:c0 + 128]
                    prod = do2.astype(F32) * o_ref[:, c0:c0 + 128].astype(F32)
                    prod_hi = prod.astype(BF16)
                    prod_lo = (prod - prod_hi.astype(F32)).astype(BF16)
                    deltas = _dot_nt(sel, prod_hi) + _dot_nt(sel, prod_lo)
                    st = _dot_nt(k2, q2)
                    dpt = _dot_nt(v2, do2)
                    pts, dsts = [], []
                    for r in range(2):
                        h = he + r
                        rows = slice(2 * BLOCK * r, 2 * BLOCK * (r + 1))
                        lse_h = l_ref[h:h + 1, :]
                        delta = deltas[r:r + 1, :]
                        pt = jnp.exp(st[rows] + bias - lse_h)
                        pts.append(pt.astype(BF16))
                        dsts.append((pt * (dpt[rows] - delta)).astype(BF16))
                        psink = jnp.exp(s_ref[h] - lse_h)
                        tot = jnp.sum(psink * delta, axis=1, keepdims=True)
                        dsink = dsink - jnp.where(lane1 == h, tot, 0.0)
                    dst = jnp.concatenate(dsts, axis=0)
                    dq_ref[:, c0:c0 + 128] = jnp.dot(kt2, dst, preferred_element_type=F32).T
                    dk4 = dk4 + jnp.dot(dst, q2, preferred_element_type=F32)
                    dv4 = dv4 + jnp.dot(jnp.concatenate(pts, axis=0), do2, preferred_element_type=F32)
                gc = pl.ds(128 * g, 128)
                dk2 = jnp.where(lane2 < HEAD_DIM, dk4[0:2 * BLOCK], dk4[2 * BLOCK:4 * BLOCK])
                dv2 = jnp.where(lane2 < HEAD_DIM, dv4[0:2 * BLOCK], dv4[2 * BLOCK:4 * BLOCK])
                dk_ref[:, gc] = cdk[:, gc] + dk2[0:BLOCK]
                dv_ref[:, gc] = cdv[:, gc] + dv2[0:BLOCK]
                cdk[:, gc] = dk2[BLOCK:2 * BLOCK]
                cdv[:, gc] = dv2[BLOCK:2 * BLOCK]
            ds_ref[0:1, :] += dsink

        @pl.when(n == nb)
        def _():
            dk_ref[...] = cdk[...]
            dv_ref[...] = cdv[...]

    cur = lambda w: pl.BlockSpec((BLOCK, w), lambda n: (jnp.minimum(n, nb - 1), 0))
    prv = lambda w: pl.BlockSpec((BLOCK, w), lambda n: (jnp.clip(n - 1, 0, nb - 1), 0))
    cur_t = lambda h: pl.BlockSpec((h, BLOCK), lambda n: (0, jnp.minimum(n, nb - 1)))
    prv_t = lambda h: pl.BlockSpec((h, BLOCK), lambda n: (0, jnp.clip(n - 1, 0, nb - 1)))
    kxw = kx.shape[1]
    return pl.pallas_call(
        body, name=name, grid=(nb + 1,),
        in_specs=[pl.BlockSpec(memory_space=pltpu.SMEM), cur(qw), cur(qw), cur(qw), cur_t(n_q),
                  cur(kxw), prv(kxw), cur_t(kxw), prv_t(kxw), cur(kxw), prv(kxw)],
        out_specs=[cur(qw), prv(kw2), prv(kw2), pl.BlockSpec((8, 128), lambda n: (0, 0))],
        out_shape=[jax.ShapeDtypeStruct((R, qw), F32), jax.ShapeDtypeStruct((R, kw2), F32),
                   jax.ShapeDtypeStruct((R, kw2), F32), jax.ShapeDtypeStruct((8, 128), F32)],
        scratch_shapes=[pltpu.VMEM((BLOCK, kw2), F32), pltpu.VMEM((BLOCK, kw2), F32)],
        compiler_params=_params(("arbitrary",)),
    )(sinks, q, do, o, lse, kx, kx, kxt, kxt, vx, vx)


def loss_bwd(h, g, target, *, name):
    R, D = h.shape
    nb = R // BLOCK

    def body(h_ref, g_ref, t_ref, loss_ref, dh_ref, dhb_ref, dg_ref):
        n = pl.program_id(0)
        xhat, rstd = _rms_parts(h_ref[...])
        gv = g_ref[...]
        diff = jnp.where(n > 0, xhat * gv - t_ref[...], 0.0)
        part = (0.5 / D) * jnp.sum(jnp.sum(diff * diff, axis=1, keepdims=True), axis=0, keepdims=True)
        dout = diff * (1.0 / D)
        dxh = dout * gv
        dh = rstd * (dxh - xhat * jnp.mean(dxh * xhat, axis=-1, keepdims=True))
        dh_ref[...] = dh
        dhb_ref[...] = dh.astype(BF16)
        dg = jnp.sum(dout * xhat, axis=0, keepdims=True)

        @pl.when(n == 0)
        def _():
            loss_ref[...] = jnp.zeros_like(loss_ref) + part
            dg_ref[...] = dg

        @pl.when(n > 0)
        def _():
            loss_ref[...] += part
            dg_ref[...] += dg

    blk = pl.BlockSpec((BLOCK, D), lambda n: (n, 0))
    return pl.pallas_call(
        body, name=name, grid=(nb,),
        in_specs=[blk, pl.BlockSpec((1, D), lambda n: (0, 0)),
                  pl.BlockSpec((BLOCK, D), lambda n: (jnp.maximum(n - 1, 0), 0))],
        out_specs=[pl.BlockSpec((8, 128), lambda n: (0, 0)), blk, blk, pl.BlockSpec((1, D), lambda n: (0, 0))],
        out_shape=[jax.ShapeDtypeStruct((8, 128), F32), jax.ShapeDtypeStruct((R, D), F32),
                   jax.ShapeDtypeStruct((R, D), BF16), jax.ShapeDtypeStruct((1, D), F32)],
        compiler_params=_params(("arbitrary",)),
    )(h, g, target)


def _adam_math(w, g, m, v):
    m = ADAM_B1 * m + (1.0 - ADAM_B1) * g
    v = ADAM_B2 * v + (1.0 - ADAM_B2) * (g * g)
    m_hat = m / (1.0 - ADAM_B1 ** ADAM_STEP)
    v_hat = v / (1.0 - ADAM_B2 ** ADAM_STEP)
    delta = -ADAM_LR * (m_hat / (jnp.sqrt(v_hat) + ADAM_EPS) + ADAM_WD * w)
    return delta, m, v


def adam(w, m, v, g, *, name):
    r, C = w.shape
    tr = _tile(r, 128, 8)

    def body(w_ref, m_ref, v_ref, g_ref, d_ref, mo_ref, vo_ref):
        d_ref[...], mo_ref[...], vo_ref[...] = _adam_math(w_ref[...], g_ref[...], m_ref[...], v_ref[...])

    blk = pl.BlockSpec((tr, C), lambda i: (i, 0))
    return pl.pallas_call(
        body, name=name, grid=(r // tr,), in_specs=[blk] * 4, out_specs=[blk] * 3,
        out_shape=[jax.ShapeDtypeStruct((r, C), F32)] * 3,
        compiler_params=_params(("parallel",)),
    )(w, m, v, g)


def _sum_blocks(p_ref, own_ref, me):
    acc = None
    for s in range(N_DEV):
        blk = jnp.where(me == s, own_ref[...], p_ref[s]).astype(F32)
        acc = blk if acc is None else acc + blk
    return acc


def _parts_specs(r, C, tr):
    blk = pl.BlockSpec((tr, C), lambda i, me: (i, 0))
    parts = pl.BlockSpec((N_DEV, tr, C), lambda i, me: (0, i, 0))
    own = pl.BlockSpec((None, tr, C), lambda i, me: (me[0], i, 0))
    return blk, parts, own


def adam_parts(w, m, v, parts, own, me, *, name):
    r, C = w.shape
    tr = _tile(r, 128, 8)

    def body(me_ref, w_ref, m_ref, v_ref, p_ref, own_ref, go_ref, d_ref, mo_ref, vo_ref):
        gv = _sum_blocks(p_ref, own_ref, me_ref[0])
        go_ref[...] = gv
        d_ref[...], mo_ref[...], vo_ref[...] = _adam_math(w_ref[...], gv, m_ref[...], v_ref[...])

    blk, pblk, oblk = _parts_specs(r, C, tr)
    return pl.pallas_call(
        body, name=name,
        grid_spec=pltpu.PrefetchScalarGridSpec(num_scalar_prefetch=1, grid=(r // tr,),
                                               in_specs=[blk, blk, blk, pblk, oblk], out_specs=[blk] * 4),
        out_shape=[jax.ShapeDtypeStruct((r, C), F32)] * 4,
        compiler_params=_params(("parallel",)),
    )(me, w, m, v, parts, own)


def sum_parts(parts, own, me, *, name):
    _, r, C = parts.shape
    tr = _tile(r, 128, 8)

    def body(me_ref, p_ref, own_ref, o_ref):
        o_ref[...] = _sum_blocks(p_ref, own_ref, me_ref[0])

    blk, pblk, oblk = _parts_specs(r, C, tr)
    return pl.pallas_call(
        body, name=name,
        grid_spec=pltpu.PrefetchScalarGridSpec(num_scalar_prefetch=1, grid=(r // tr,),
                                               in_specs=[pblk, oblk], out_specs=blk),
        out_shape=jax.ShapeDtypeStruct((r, C), F32),
        compiler_params=_params(("parallel",)),
    )(me, parts, own)


def cast_place(w, me, *, name):
    r, C = w.shape
    tr = _tile(r, 256, 16)

    def body(me_ref, w_ref, s_ref, l_ref):
        v = w_ref[...].astype(BF16)
        s_ref[...] = v
        l_ref[...] = v

    blk = pl.BlockSpec((tr, C), lambda i, me: (i, 0))
    return pl.pallas_call(
        body, name=name,
        grid_spec=pltpu.PrefetchScalarGridSpec(
            num_scalar_prefetch=1, grid=(r // tr,), in_specs=[blk],
            out_specs=[blk, pl.BlockSpec((None, tr, C), lambda i, me: (me[0], i, 0))]),
        out_shape=[jax.ShapeDtypeStruct((r, C), BF16), jax.ShapeDtypeStruct((N_DEV, r, C), BF16)],
        compiler_params=_params(("parallel",)),
    )(me, w)


def _coords():
    return lax.axis_index("x"), lax.axis_index("y"), lax.axis_index("c")


def _peer(m):
    x, y, c = _coords()
    px = 1 - x if m & 4 else x
    py = 1 - y if m & 2 else y
    pc = 1 - c if m & 1 else c
    return (px, py, pc), 4 * px + 2 * py + pc


def exchange(items, *, all_to_all, name):
    n = len(items)
    if all_to_all:
        out_shape = [jax.ShapeDtypeStruct(a.shape, a.dtype) for a in items]
    else:
        out_shape = [jax.ShapeDtypeStruct((N_DEV,) + a.shape, a.dtype) for a in items]

    def body(*refs):
        ins, outs = refs[:n], refs[n:2 * n]
        send_sems, recv_sems, local_sems = refs[2 * n:]
        x, y, c = _coords()
        me = 4 * x + 2 * y + c

        def src(i, idx):
            return ins[i].at[idx] if all_to_all else ins[i]

        local = [pltpu.make_async_copy(src(i, me), outs[i].at[me], local_sems.at[i]) for i in range(n)]
        for cp in local:
            cp.start()
        sends = []
        for m in range(1, N_DEV):
            peer, pidx = _peer(m)
            for i in range(n):
                k = i * (N_DEV - 1) + m - 1
                cp = pltpu.make_async_remote_copy(src_ref=src(i, pidx), dst_ref=outs[i].at[me],
                                                  send_sem=send_sems.at[k], recv_sem=recv_sems.at[k],
                                                  device_id=peer, device_id_type=MESH)
                cp.start()
                sends.append(cp)
        for m in range(1, N_DEV):
            peer, pidx = _peer(m)
            for i in range(n):
                k = i * (N_DEV - 1) + m - 1
                pltpu.make_async_remote_copy(src_ref=src(i, pidx), dst_ref=outs[i].at[pidx],
                                             send_sem=send_sems.at[k], recv_sem=recv_sems.at[k],
                                             device_id=peer, device_id_type=MESH).wait_recv()
        for cp in sends:
            cp.wait_send()
        for cp in local:
            cp.wait()

    any_spec = pl.BlockSpec(memory_space=pl.ANY)
    return pl.pallas_call(
        body, name=name, in_specs=[any_spec] * n, out_specs=[any_spec] * n, out_shape=out_shape,
        scratch_shapes=[pltpu.SemaphoreType.DMA((n * (N_DEV - 1),)), pltpu.SemaphoreType.DMA((n * (N_DEV - 1),)),
                        pltpu.SemaphoreType.DMA((n,))],
    )(*items)


HBM_SPEC = pl.BlockSpec(memory_space=pltpu.HBM)
SEM_SPEC = pl.BlockSpec(memory_space=pltpu.SEMAPHORE)
SPLIT_PARAMS = pltpu.CompilerParams(has_side_effects=pltpu.SideEffectType.DATAFLOW_SIDE_EFFECTING)


def _split_copies(src_ref, land_ref, send_sems, recv_sems, all_to_all):
    x, y, c = _coords()
    me = 4 * x + 2 * y + c
    copies = []
    for m in range(1, N_DEV):
        peer, pidx = _peer(m)
        copies.append(pltpu.make_async_remote_copy(
            src_ref=src_ref.at[pidx] if all_to_all else src_ref, dst_ref=land_ref.at[me],
            send_sem=send_sems.at[m - 1], recv_sem=recv_sems.at[m - 1], device_id=peer, device_id_type=MESH))
    return copies


def copy_start(items, lands=None, *, all_to_all, name):
    n = len(items)
    if lands is None:
        lands = [lax.empty(a.shape if all_to_all else (N_DEV,) + a.shape, a.dtype) for a in items]

    def body(*refs):
        srcs, lnds, outs = refs[:n], refs[n:2 * n], refs[2 * n:]
        for i in range(n):
            for cp in _split_copies(srcs[i], lnds[i], outs[4 * i], outs[4 * i + 1], all_to_all):
                cp.start()
        outs[4 * n][...] = jnp.zeros((8, 128), F32)

    out_shape, out_specs, aliases = [], [], {}
    for i, (a, l) in enumerate(zip(items, lands)):
        out_shape += [pltpu.SemaphoreType.DMA((N_DEV - 1,)), pltpu.SemaphoreType.DMA((N_DEV - 1,)),
                      pltpu.HBM(a.shape, a.dtype), pltpu.HBM(l.shape, l.dtype)]
        out_specs += [SEM_SPEC, SEM_SPEC, HBM_SPEC, HBM_SPEC]
        aliases[i] = 4 * i + 2
        aliases[n + i] = 4 * i + 3
    out_shape.append(jax.ShapeDtypeStruct((8, 128), F32))
    out_specs.append(pl.BlockSpec(memory_space=pltpu.VMEM))
    hbm = lambda a: pltpu.with_memory_space_constraint(a, pltpu.HBM)
    res = pl.pallas_call(
        body, name=name, in_specs=[HBM_SPEC] * (2 * n), out_specs=out_specs, out_shape=out_shape,
        input_output_aliases=aliases, compiler_params=SPLIT_PARAMS,
    )(*[hbm(a) for a in items], *[hbm(l) for l in lands])
    return [tuple(res[4 * i:4 * i + 4]) for i in range(n)], res[4 * n]


def copy_wait(handle, after, *, all_to_all, name):
    send_sems, recv_sems, src, land = handle

    def body(src_ref, land_ref, send_ref, recv_ref, after_ref, src_out, got_ref):
        for cp in _split_copies(src_ref, land_ref, send_ref, recv_ref, all_to_all):
            cp.wait_send()
            cp.wait_recv()

    return pl.pallas_call(
        body, name=name, in_specs=[HBM_SPEC, HBM_SPEC, SEM_SPEC, SEM_SPEC, ANY_SPEC],
        out_specs=[HBM_SPEC, HBM_SPEC],
        out_shape=[pltpu.HBM(src.shape, src.dtype), pltpu.HBM(land.shape, land.dtype)],
        input_output_aliases={0: 0, 1: 1}, compiler_params=SPLIT_PARAMS,
    )(src, land, send_sems, recv_sems, after)


def kernel(x, meta_tokens, norm_mix_0, w_in_conv, conv_w, w_out_conv, norm_mlp_0, w_up_0, w_down_0, norm_mix_1, w_qkv, attn_sinks, w_o, norm_mlp_1, w_up_1, w_down_1, norm_final, loss_target, m_meta_tokens, m_norm_mix_0, m_w_in_conv, m_conv_w, m_w_out_conv, m_norm_mlp_0, m_w_up_0, m_w_down_0, m_norm_mix_1, m_w_qkv, m_attn_sinks, m_w_o, m_norm_mlp_1, m_w_up_1, m_w_down_1, m_norm_final, v_meta_tokens, v_norm_mix_0, v_w_in_conv, v_conv_w, v_w_out_conv, v_norm_mlp_0, v_w_up_0, v_w_down_0, v_norm_mix_1, v_w_qkv, v_attn_sinks, v_w_o, v_norm_mlp_1, v_w_up_1, v_w_down_1, v_norm_final):
    L, D = x.shape[1], x.shape[2]
    n_meta = meta_tokens.shape[0]
    pad = BLOCK - n_meta
    R = BLOCK + L
    n_q = D // HEAD_DIM
    n_kv = n_q // GROUP
    assert n_kv % 2 == 0 and L % BLOCK == 0 and D % 128 == 0
    x = x.reshape(L, D)
    target = loss_target.reshape(L, D)
    x_id, y_id, c_id = _coords()
    me = 4 * x_id + 2 * y_id + c_id

    col_names = ("in", "up0", "qkv", "up1")
    col_w = dict(zip(col_names, (w_in_conv, w_up_0, w_qkv, w_up_1)))
    row_names = ("out", "down0", "o", "down1")
    row_w = dict(zip(row_names, (w_out_conv, w_down_0, w_o, w_down_1)))
    me_arr = jnp.reshape(me, (1,)).astype(jnp.int32)
    natural = {k: col_w[k].T for k in col_names}
    natural.update(row_w)
    use_order = ("in", "out", "up0", "down0", "qkv", "o", "up1", "down1")
    placed = [cast_place(natural[k], me_arr, name="cast_" + k) for k in use_order]
    handles, token = copy_start([p[0] for p in placed], [p[1] for p in placed], all_to_all=False, name="gather_start")
    handles = dict(zip(use_order, handles))

    def weight(k, after):
        return copy_wait(handles[k], after, all_to_all=False, name="gather_wait_" + k)[1].reshape(-1, D)

    small_in = exchange([meta_tokens, conv_w], all_to_all=False, name="comm_gather")
    meta_full = jnp.transpose(small_in[0], (1, 0, 2)).reshape(n_meta, D)
    conv_full = jnp.transpose(small_in[1], (1, 0, 2)).reshape(conv_w.shape[0], D)

    vec = lambda a: a.reshape(1, D)
    pos = jnp.arange(R, dtype=F32) - pad
    inv = ROPE_THETA ** (-jnp.arange(0, HEAD_DIM, 2, dtype=F32) / HEAD_DIM)
    ang = pos[:, None] * inv[None, :]
    cos32, sin32 = jnp.cos(ang), jnp.sin(ang)
    cos = jnp.concatenate([cos32] * 4, axis=1)
    sin = jnp.concatenate([-sin32, sin32, -sin32, sin32], axis=1)

    W = {}
    head = jnp.concatenate([jnp.zeros((pad, D), F32), meta_full], axis=0)
    h0, n0 = first_norm(head, x, vec(norm_mix_0), token, name="norm0")
    W["in"] = weight("in", n0)
    bcu = mm(n0, W["in"], name="in_proj", out_dtype=BF16, b_rows_are_n=True)
    gated = conv_fwd(bcu, conv_full, name="conv_fwd")
    W["out"] = weight("out", gated)
    h1 = mm(gated, W["out"], name="out_proj", out_dtype=F32, b_rows_are_n=False, epi="add", extra=h0)
    n1 = norm_fwd(h1, vec(norm_mlp_0), h1, name="norm1")
    W["up0"] = weight("up0", n1)
    a0 = mm(n1, W["up0"], name="up0", out_dtype=BF16, b_rows_are_n=True, epi="relu")
    W["down0"] = weight("down0", a0)
    h2 = mm(a0, W["down0"], name="down0", out_dtype=F32, b_rows_are_n=False, epi="add", extra=h1, a_sq=True)
    n2 = norm_fwd(h2, vec(norm_mix_1), h2, name="norm2")
    W["qkv"] = weight("qkv", n2)
    qkv = mm(n2, W["qkv"], name="qkv_proj", out_dtype=F32, b_rows_are_n=True)
    q, kx, vx, kxt, vxt = rope_fwd(qkv, cos, sin, n_q=n_q, n_kv=n_kv, name="rope_fwd")
    o, lse = attn_fwd(q, kx, vxt, attn_sinks, pad=pad, name="attn_fwd")
    W["o"] = weight("o", o)
    h3 = mm(o, W["o"], name="o_proj", out_dtype=F32, b_rows_are_n=False, epi="add", extra=h2)
    n3 = norm_fwd(h3, vec(norm_mlp_1), h3, name="norm3")
    W["up1"] = weight("up1", n3)
    a1 = mm(n3, W["up1"], name="up1", out_dtype=BF16, b_rows_are_n=True, epi="relu")
    W["down1"] = weight("down1", a1)
    h4 = mm(a1, W["down1"], name="down1", out_dtype=F32, b_rows_are_n=False, epi="add", extra=h3, a_sq=True)

    loss_part, dh4, dh4b, dg_final = loss_bwd(h4, vec(norm_final), target, name="loss_bwd")
    loss = lax.psum(loss_part[0, 0], ("x", "y", "c"))

    sent = {}

    def scatter(k, dw):
        (sent[k],), tok = copy_start([dw.reshape(N_DEV, -1, D)], all_to_all=True, name="a2a_start_" + k)
        return tok

    t = scatter("down1", mm_tn(a1, dh4b, dh4b, name="dw_down1", a_sq=True))
    dup1 = mm(dh4b, W["down1"], name="d_down1", out_dtype=BF16, b_rows_are_n=True, epi="mul2a", extra=a1, dep=t)
    t = scatter("up1", mm_tn(dup1, n3, dup1, name="dw_up1"))
    dn3 = mm(dup1, W["up1"], name="d_up1", out_dtype=BF16, b_rows_are_n=False, dep=t)
    dh3, dh3b, dg_mlp1 = norm_bwd(dn3, h3, vec(norm_mlp_1), dh4, name="norm3_bwd")

    t = scatter("o", mm_tn(o, dh3b, dh3b, name="dw_o"))
    do = mm(dh3b, W["o"], name="d_o", out_dtype=BF16, b_rows_are_n=True, dep=t)
    dq, dkx, dvx, dsinks = attn_bwd(q, kx, kxt, vx, o, do, lse, attn_sinks, pad=pad, name="attn_bwd")
    dqkv = rope_bwd(dq, dkx, dvx, cos, sin, n_q=n_q, n_kv=n_kv, name="rope_bwd")
    t = scatter("qkv", mm_tn(dqkv, n2, dqkv, name="dw_qkv"))
    dn2 = mm(dqkv, W["qkv"], name="d_qkv", out_dtype=BF16, b_rows_are_n=False, dep=t)
    dh2, dh2b, dg_mix1 = norm_bwd(dn2, h2, vec(norm_mix_1), dh3, name="norm2_bwd")

    t = scatter("down0", mm_tn(a0, dh2b, dh2b, name="dw_down0", a_sq=True))
    dup0 = mm(dh2b, W["down0"], name="d_down0", out_dtype=BF16, b_rows_are_n=True, epi="mul2a", extra=a0, dep=t)
    t = scatter("up0", mm_tn(dup0, n1, dup0, name="dw_up0"))
    dn1 = mm(dup0, W["up0"], name="d_up0", out_dtype=BF16, b_rows_are_n=False, dep=t)
    dh1, dh1b, dg_mlp0 = norm_bwd(dn1, h1, vec(norm_mlp_0), dh2, name="norm1_bwd")

    t = scatter("out", mm_tn(gated, dh1b, dh1b, name="dw_out"))
    dgated = mm(dh1b, W["out"], name="d_out", out_dtype=F32, b_rows_are_n=True, dep=t)
    dbcu, dconv = conv_bwd(bcu, conv_full, dgated, name="conv_bwd")
    t = scatter("in", mm_tn(dbcu, n0, dbcu, name="dw_in"))
    dn0 = mm(dbcu, W["in"], name="d_in", out_dtype=BF16, b_rows_are_n=False, dep=t)
    dhead, dx, dg_mix0 = last_norm_bwd(dn0, h0, vec(norm_mix_0), dh1, name="norm0_bwd")
    grad_x = dx.reshape(1, L, D)

    recv = {}
    for k in ("down1", "up1", "o", "qkv", "down0", "up0", "out", "in"):
        sent[k], recv[k] = copy_wait(sent[k], dx, all_to_all=True, name="a2a_wait_" + k)

    n_sink = attn_sinks.shape[0]
    slab = jnp.concatenate([
        dg_mix0, dg_mlp0, dg_mix1, dg_mlp1, dg_final,
        jnp.pad(dsinks[0:1, :n_sink], ((0, 0), (0, D - n_sink))), jnp.zeros((2, D), F32),
        dconv, dhead[pad:BLOCK]], axis=0)
    slabs = exchange([slab], all_to_all=False, name="comm_small")[0]
    small = sum_parts(slabs, slabs, me_arr, name="sum_small")
    cols = D // N_DEV
    my_cols = lambda a: lax.dynamic_slice_in_dim(a, me * cols, cols, axis=1)

    grads, deltas, new_m, new_v = {}, {}, {}, {}

    def update(key, w, m, v, g, shape):
        s2 = (1, -1) if w.ndim == 1 else w.shape
        if g.ndim == 3:
            g_, d_, m_, v_ = adam_parts(w, m, v, g, sent[key_of[key]], me_arr, name="adam_" + key)
        else:
            g_ = g.reshape(s2)
            d_, m_, v_ = adam(w.reshape(s2), m.reshape(s2), v.reshape(s2), g_, name="adam_" + key)
        grads[key], deltas[key], new_m[key], new_v[key] = (t.reshape(shape) for t in (g_, d_, m_, v_))

    update("meta_tokens", meta_tokens, m_meta_tokens, v_meta_tokens, my_cols(small[16:16 + n_meta]), meta_tokens.shape)
    update("norm_mix_0", norm_mix_0, m_norm_mix_0, v_norm_mix_0, small[0], (D,))
    update("conv_w", conv_w, m_conv_w, v_conv_w, my_cols(small[8:8 + conv_w.shape[0]]), conv_w.shape)
    update("norm_mlp_0", norm_mlp_0, m_norm_mlp_0, v_norm_mlp_0, small[1], (D,))
    update("norm_mix_1", norm_mix_1, m_norm_mix_1, v_norm_mix_1, small[2], (D,))
    update("attn_sinks", attn_sinks, m_attn_sinks, v_attn_sinks, small[5, :n_sink], (n_sink,))
    update("norm_mlp_1", norm_mlp_1, m_norm_mlp_1, v_norm_mlp_1, small[3], (D,))
    update("norm_final", norm_final, m_norm_final, v_norm_final, small[4], (D,))
    big = {"in": ("w_in_conv", w_in_conv, m_w_in_conv, v_w_in_conv), "up0": ("w_up_0", w_up_0, m_w_up_0, v_w_up_0),
           "qkv": ("w_qkv", w_qkv, m_w_qkv, v_w_qkv), "up1": ("w_up_1", w_up_1, m_w_up_1, v_w_up_1),
           "out": ("w_out_conv", w_out_conv, m_w_out_conv, v_w_out_conv),
           "down0": ("w_down_0", w_down_0, m_w_down_0, v_w_down_0), "o": ("w_o", w_o, m_w_o, v_w_o),
           "down1": ("w_down_1", w_down_1, m_w_down_1, v_w_down_1)}
    key_of = {big[k][0]: k for k in big}
    for k in col_names:
        key, w, m, v = big[k]
        update(key, w, m, v, sum_parts(recv[k], sent[k], me_arr, name="sum_" + k).T, w.shape)
    for k in row_names:
        key, w, m, v = big[k]
        update(key, w, m, v, recv[k], w.shape)

    order = ("meta_tokens", "norm_mix_0", "w_in_conv", "conv_w", "w_out_conv", "norm_mlp_0", "w_up_0", "w_down_0",
             "norm_mix_1", "w_qkv", "attn_sinks", "w_o", "norm_mlp_1", "w_up_1", "w_down_1", "norm_final")
    return (loss, grad_x, *[grads[k] for k in order], *[deltas[k] for k in order],
            *[new_m[k] for k in order], *[new_v[k] for k in order])
```

```python
import functools

import jax
import jax.numpy as jnp
from jax import lax
from jax.experimental import pallas as pl
from jax.experimental.pallas import tpu as pltpu

F32 = jnp.float32
BF16 = jnp.bfloat16

HEAD_DIM = 64
GROUP = 8
BLOCK = 128
N_DEV = 8
RMS_EPS = 1e-5
NEG_INF = -1e30
ROPE_THETA = 10000.0
ADAM_LR = 0.001
ADAM_B1 = 0.9
ADAM_B2 = 0.999
ADAM_EPS = 1e-08
ADAM_WD = 0.01
ADAM_STEP = 10
VMEM_LIMIT = 60 * 1024 * 1024
MESH = pl.DeviceIdType.MESH


def _tile(n, target, mult):
    best = None
    for t in range(mult, min(n, target) + 1, mult):
        if n % t == 0:
            best = t
    return best if best is not None else n


def _params(sem):
    return pltpu.CompilerParams(dimension_semantics=sem, vmem_limit_bytes=VMEM_LIMIT)


ANY_SPEC = pl.BlockSpec(memory_space=pl.ANY)


LEFT_BLOCK_BYTES = 14 * 1024 * 1024
LONG_LEFT_BLOCK_BYTES = 11 * 1024 * 1024


def mm(a, b, *, name, out_dtype, b_rows_are_n, epi=None, extra=None, a_sq=False, dep=None):
    M, K = a.shape
    N = b.shape[0] if b_rows_are_n else b.shape[1]
    tm = _tile(M, (LEFT_BLOCK_BYTES if K <= 6144 else LONG_LEFT_BLOCK_BYTES) // (2 * K), 16)
    tn = _tile(N, 512, 128)

    def body(*refs):
        a_ref, b_ref, e_ref, o_ref = refs[0], refs[1], refs[2], refs[-1]
        av = a_ref[...]
        if a_sq:
            av = av.astype(F32)
            av = (av * av).astype(BF16)
        dims = (((1,), (1,)), ((), ())) if b_rows_are_n else (((1,), (0,)), ((), ()))
        acc = lax.dot_general(av, b_ref[...], dims, preferred_element_type=F32)
        if epi == "relu":
            acc = jnp.maximum(acc, 0.0)
        elif epi == "mul2a":
            acc = acc * (2.0 * e_ref[...].astype(F32))
        elif epi == "add":
            acc = acc + e_ref[...]
        o_ref[...] = acc.astype(o_ref.dtype)

    b_spec = pl.BlockSpec((tn, K), lambda i, j: (j, 0)) if b_rows_are_n else pl.BlockSpec((K, tn), lambda i, j: (0, j))
    in_specs = [pl.BlockSpec((tm, K), lambda i, j: (i, 0)), b_spec]
    args = [a, b]
    if extra is not None:
        in_specs.append(pl.BlockSpec((tm, tn), lambda i, j: (i, j)))
        args.append(extra)
    if dep is not None:
        in_specs.append(ANY_SPEC)
        args.append(dep)
    return pl.pallas_call(
        body, name=name, grid=(M // tm, N // tn), in_specs=in_specs,
        out_specs=pl.BlockSpec((tm, tn), lambda i, j: (i, j)),
        out_shape=jax.ShapeDtypeStruct((M, N), out_dtype),
        compiler_params=_params(("parallel", "parallel")),
    )(*args)


def _rms_parts(h):
    rstd = lax.rsqrt(jnp.mean(h * h, axis=-1, keepdims=True) + RMS_EPS)
    return h * rstd, rstd


def _norm_bwd_rows(dn, h, g, dres):
    xhat, rstd = _rms_parts(h)
    dxh = dn * g
    dh = dres + rstd * (dxh - xhat * jnp.mean(dxh * xhat, axis=-1, keepdims=True))
    return dh, jnp.sum(dn * xhat, axis=0, keepdims=True)


def mm_tn(a, b, dep, *, name, a_sq=False):
    T, M = a.shape
    N = b.shape[1]
    tm = _tile(M, 1024, 128)
    tk = _tile(T, 2080, 16)
    nk = T // tk
    tc = _tile(tm, 256, 128)

    def body(a_ref, b_ref, dep_ref, o_ref, acc_ref):
        k = pl.program_id(1)

        def chunks(first):
            bv = b_ref[...]
            for r0 in range(0, tm, tc):
                av = a_ref[:, r0:r0 + tc]
                if a_sq:
                    av = av.astype(F32)
                    av = (av * av).astype(BF16)
                part = lax.dot_general(av, bv, (((0,), (0,)), ((), ())), preferred_element_type=F32)
                if first:
                    acc_ref[r0:r0 + tc, :] = part
                else:
                    acc_ref[r0:r0 + tc, :] += part

        @pl.when(k == 0)
        def _():
            chunks(True)

        @pl.when(k > 0)
        def _():
            chunks(False)

        @pl.when(k == nk - 1)
        def _():
            o_ref[...] = acc_ref[...].astype(BF16)

    return pl.pallas_call(
        body, name=name, grid=(M // tm, nk),
        in_specs=[pl.BlockSpec((tk, tm), lambda i, k: (k, i)), pl.BlockSpec((tk, N), lambda i, k: (k, 0)), ANY_SPEC],
        out_specs=pl.BlockSpec((tm, N), lambda i, k: (i, 0)),
        out_shape=jax.ShapeDtypeStruct((M, N), BF16),
        scratch_shapes=[pltpu.VMEM((tm, N), F32)],
        compiler_params=_params(("parallel", "arbitrary")),
    )(a, b, dep)


def norm_fwd(h, g, dep, *, name):
    R, D = h.shape
    tr = _tile(R, 320, 16)

    def body(h_ref, g_ref, dep_ref, n_ref):
        xhat, _ = _rms_parts(h_ref[...])
        n_ref[...] = (xhat * g_ref[...]).astype(BF16)

    return pl.pallas_call(
        body, name=name, grid=(R // tr,),
        in_specs=[pl.BlockSpec((tr, D), lambda i: (i, 0)), pl.BlockSpec((1, D), lambda i: (0, 0)), ANY_SPEC],
        out_specs=pl.BlockSpec((tr, D), lambda i: (i, 0)),
        out_shape=jax.ShapeDtypeStruct((R, D), BF16),
        compiler_params=_params(("parallel",)),
    )(h, g, dep)


def norm_bwd(dn, h, g, dres, *, name):
    R, D = h.shape
    tr = _tile(R, 320, 16)

    def body(dn_ref, h_ref, g_ref, dres_ref, dh_ref, dhb_ref, dg_ref):
        i = pl.program_id(0)
        dh, dg = _norm_bwd_rows(dn_ref[...].astype(F32), h_ref[...], g_ref[...], dres_ref[...])
        dh_ref[...] = dh
        dhb_ref[...] = dh.astype(BF16)

        @pl.when(i == 0)
        def _():
            dg_ref[...] = dg

        @pl.when(i > 0)
        def _():
            dg_ref[...] += dg

    row = pl.BlockSpec((tr, D), lambda i: (i, 0))
    vec = pl.BlockSpec((1, D), lambda i: (0, 0))
    return pl.pallas_call(
        body, name=name, grid=(R // tr,), in_specs=[row, row, vec, row], out_specs=[row, row, vec],
        out_shape=[jax.ShapeDtypeStruct((R, D), F32), jax.ShapeDtypeStruct((R, D), BF16),
                   jax.ShapeDtypeStruct((1, D), F32)],
        compiler_params=_params(("arbitrary",)),
    )(dn, h, g, dres)


def first_norm(head, x, g, dep, *, name):
    L, D = x.shape
    nb = L // BLOCK + 1

    def body(head_ref, x_ref, g_ref, dep_ref, h_ref, n_ref):
        i = pl.program_id(0)
        hv = jnp.where(i == 0, head_ref[...], x_ref[...])
        h_ref[...] = hv
        xhat, _ = _rms_parts(hv)
        n_ref[...] = (xhat * g_ref[...]).astype(BF16)

    blk = pl.BlockSpec((BLOCK, D), lambda i: (i, 0))
    return pl.pallas_call(
        body, name=name, grid=(nb,),
        in_specs=[pl.BlockSpec((BLOCK, D), lambda i: (0, 0)),
                  pl.BlockSpec((BLOCK, D), lambda i: (jnp.maximum(i - 1, 0), 0)),
                  pl.BlockSpec((1, D), lambda i: (0, 0)), ANY_SPEC],
        out_specs=[blk, blk],
        out_shape=[jax.ShapeDtypeStruct((BLOCK + L, D), F32), jax.ShapeDtypeStruct((BLOCK + L, D), BF16)],
        compiler_params=_params(("parallel",)),
    )(head, x, g, dep)


def last_norm_bwd(dn, h, g, dres, *, name):
    R, D = h.shape
    nb = R // BLOCK

    def body(dn_ref, h_ref, g_ref, dres_ref, dhead_ref, dx_ref, dg_ref):
        i = pl.program_id(0)
        dh, dg = _norm_bwd_rows(dn_ref[...].astype(F32), h_ref[...], g_ref[...], dres_ref[...])
        dx_ref[...] = dh

        @pl.when(i == 0)
        def _():
            dhead_ref[...] = dh
            dg_ref[...] = dg

        @pl.when(i > 0)
        def _():
            dg_ref[...] += dg

    blk = pl.BlockSpec((BLOCK, D), lambda i: (i, 0))
    vec = pl.BlockSpec((1, D), lambda i: (0, 0))
    return pl.pallas_call(
        body, name=name, grid=(nb,), in_specs=[blk, blk, vec, blk],
        out_specs=[pl.BlockSpec((BLOCK, D), lambda i: (0, 0)),
                   pl.BlockSpec((BLOCK, D), lambda i: (jnp.maximum(i - 1, 0), 0)), vec],
        out_shape=[jax.ShapeDtypeStruct((BLOCK, D), F32), jax.ShapeDtypeStruct((R - BLOCK, D), F32),
                   jax.ShapeDtypeStruct((1, D), F32)],
        compiler_params=_params(("arbitrary",)),
    )(dn, h, g, dres)


HALO = 16


def conv_fwd(bcu, conv_w, *, name):
    R, D3 = bcu.shape
    D = D3 // 3
    tr = _tile(R, 320, 16)
    tc = _tile(D, 512, 128)
    nc = D // tc
    hb = tr // HALO

    def body(b_ref, c_ref, u_ref, ch_ref, uh_ref, w_ref, o_ref, vbuf):
        i = pl.program_id(0)
        v = c_ref[...].astype(F32) * u_ref[...].astype(F32)
        vbuf[0:HALO, :] = jnp.where(i > 0, ch_ref[...].astype(F32) * uh_ref[...].astype(F32), 0.0)
        vbuf[HALO:HALO + tr, :] = v
        v1 = vbuf[HALO - 1:HALO - 1 + tr, :]
        v2 = vbuf[HALO - 2:HALO - 2 + tr, :]
        conv = w_ref[0:1, :] * v2 + w_ref[1:2, :] * v1 + w_ref[2:3, :] * v
        o_ref[...] = (b_ref[...].astype(F32) * conv).astype(BF16)

    def prev(i):
        return jnp.maximum(i * hb - 1, 0)

    return pl.pallas_call(
        body, name=name, grid=(R // tr, nc),
        in_specs=[pl.BlockSpec((tr, tc), lambda i, j: (i, j)),
                  pl.BlockSpec((tr, tc), lambda i, j: (i, nc + j)),
                  pl.BlockSpec((tr, tc), lambda i, j: (i, 2 * nc + j)),
                  pl.BlockSpec((HALO, tc), lambda i, j: (prev(i), nc + j)),
                  pl.BlockSpec((HALO, tc), lambda i, j: (prev(i), 2 * nc + j)),
                  pl.BlockSpec((3, tc), lambda i, j: (0, j))],
        out_specs=pl.BlockSpec((tr, tc), lambda i, j: (i, j)),
        out_shape=jax.ShapeDtypeStruct((R, D), BF16),
        scratch_shapes=[pltpu.VMEM((HALO + tr, tc), F32)],
        compiler_params=_params(("parallel", "parallel")),
    )(bcu, bcu, bcu, bcu, bcu, conv_w)


def conv_bwd(bcu, conv_w, dg, *, name):
    R, D3 = bcu.shape
    D = D3 // 3
    tr = _tile(R, 320, 16)
    tc = _tile(D, 512, 128)
    hb = tr // HALO
    nt = R // tr
    last_h = R // HALO - 1

    def body(x_ref, xp_ref, xn_ref, dg_ref, dgn_ref, w_ref, o_ref, dw_ref, vbuf, dbuf):
        i = pl.program_id(0)
        row8 = lax.broadcasted_iota(jnp.int32, (8, tc), 0)
        for c0 in range(0, D, tc):
            cb, cc, cu = slice(c0, c0 + tc), slice(D + c0, D + c0 + tc), slice(2 * D + c0, 2 * D + c0 + tc)
            w0, w1, w2 = w_ref[0:1, cb], w_ref[1:2, cb], w_ref[2:3, cb]
            b = x_ref[:, cb].astype(F32)
            c = x_ref[:, cc].astype(F32)
            u = x_ref[:, cu].astype(F32)
            v = c * u
            vbuf[0:HALO, :] = jnp.where(i > 0, xp_ref[:, cc].astype(F32) * xp_ref[:, cu].astype(F32), 0.0)
            vbuf[HALO:HALO + tr, :] = v
            v1 = vbuf[HALO - 1:HALO - 1 + tr, :]
            v2 = vbuf[HALO - 2:HALO - 2 + tr, :]
            dgv = dg_ref[:, cb].astype(F32)
            o_ref[:, cb] = (dgv * (w0 * v2 + w1 * v1 + w2 * v)).astype(BF16)
            dconv = dgv * b
            dbuf[0:tr, :] = dconv
            dbuf[tr:tr + HALO, :] = jnp.where(i < nt - 1, dgn_ref[:, cb].astype(F32) * xn_ref[:, cb].astype(F32), 0.0)
            dv = w2 * dconv + w1 * dbuf[1:1 + tr, :] + w0 * dbuf[2:2 + tr, :]
            o_ref[:, cc] = (dv * u).astype(BF16)
            o_ref[:, cu] = (dv * c).astype(BF16)
            dw = jnp.where(row8 == 0, jnp.sum(dconv * v2, axis=0, keepdims=True),
                           jnp.where(row8 == 1, jnp.sum(dconv * v1, axis=0, keepdims=True),
                                     jnp.where(row8 == 2, jnp.sum(dconv * v, axis=0, keepdims=True), 0.0)))

            @pl.when(i == 0)
            def _():
                dw_ref[:, cb] = dw

            @pl.when(i > 0)
            def _():
                dw_ref[:, cb] += dw

    def prev(i):
        return jnp.maximum(i * hb - 1, 0)

    def nxt(i):
        return jnp.minimum((i + 1) * hb, last_h)

    return pl.pallas_call(
        body, name=name, grid=(nt,),
        in_specs=[pl.BlockSpec((tr, D3), lambda i: (i, 0)),
                  pl.BlockSpec((HALO, D3), lambda i: (prev(i), 0)),
                  pl.BlockSpec((HALO, D3), lambda i: (nxt(i), 0)),
                  pl.BlockSpec((tr, D), lambda i: (i, 0)),
                  pl.BlockSpec((HALO, D), lambda i: (nxt(i), 0)),
                  pl.BlockSpec((3, D), lambda i: (0, 0))],
        out_specs=[pl.BlockSpec((tr, D3), lambda i: (i, 0)), pl.BlockSpec((8, D), lambda i: (0, 0))],
        out_shape=[jax.ShapeDtypeStruct((R, D3), BF16), jax.ShapeDtypeStruct((8, D), F32)],
        scratch_shapes=[pltpu.VMEM((HALO + tr, tc), F32), pltpu.VMEM((tr + HALO, tc), F32)],
        compiler_params=_params(("arbitrary",)),
    )(bcu, bcu, bcu, dg, dg, conv_w)


def _swap32(x):
    w = x.shape[1]
    lane = lax.broadcasted_iota(jnp.int32, x.shape, 1)
    return jnp.where((lane & (HEAD_DIM - 1)) < HEAD_DIM // 2, pltpu.roll(x, w - HEAD_DIM // 2, axis=1),
                     pltpu.roll(x, HEAD_DIM // 2, axis=1))


def _rope(x, cos, sin):
    return x * cos + _swap32(x) * sin


def rope_fwd(qkv, cos, sin, *, n_q, n_kv, name):
    R, W = qkv.shape
    qw = n_q * HEAD_DIM
    kw = n_kv * HEAD_DIM
    tr = _tile(R, 640, 128)

    def expand(y, ref, ref_t, c):
        lane = lax.broadcasted_iota(jnp.int32, y.shape, 1)
        lo = jnp.where(lane < HEAD_DIM, y, 0.0)
        hi = jnp.where(lane >= HEAD_DIM, y, 0.0)
        tiles = (lo, pltpu.roll(lo, HEAD_DIM, axis=1), pltpu.roll(hi, HEAD_DIM, axis=1), hi)
        for t, tile in enumerate(tiles):
            c0 = 512 * c + 128 * t
            ref[:, c0:c0 + 128] = tile.astype(BF16)
            ref_t[c0:c0 + 128, :] = tile.T.astype(BF16)

    def body(x_ref, c_ref, s_ref, q_ref, k_ref, v_ref, kt_ref, vt_ref):
        cos = c_ref[...]
        sin = s_ref[...]
        for c in range(qw // 128):
            x = x_ref[:, 128 * c:128 * (c + 1)]
            q_ref[:, 128 * c:128 * (c + 1)] = (_rope(x, cos, sin) * (HEAD_DIM ** -0.5)).astype(BF16)
        for c in range(kw // 128):
            expand(_rope(x_ref[:, qw + 128 * c:qw + 128 * (c + 1)], cos, sin), k_ref, kt_ref, c)
            expand(x_ref[:, qw + kw + 128 * c:qw + kw + 128 * (c + 1)], v_ref, vt_ref, c)

    row = lambda w: pl.BlockSpec((tr, w), lambda i: (i, 0))
    col = pl.BlockSpec((4 * kw, tr), lambda i: (0, i))
    return pl.pallas_call(
        body, name=name, grid=(R // tr,),
        in_specs=[row(W), row(128), row(128)],
        out_specs=[row(qw), row(4 * kw), row(4 * kw), col, col],
        out_shape=[jax.ShapeDtypeStruct((R, qw), BF16), jax.ShapeDtypeStruct((R, 4 * kw), BF16),
                   jax.ShapeDtypeStruct((R, 4 * kw), BF16), jax.ShapeDtypeStruct((4 * kw, R), BF16),
                   jax.ShapeDtypeStruct((4 * kw, R), BF16)],
        compiler_params=_params(("parallel",)),
    )(qkv, cos, sin)


def rope_bwd(dq, dkx, dvx, cos, sin, *, n_q, n_kv, name):
    R = dq.shape[0]
    qw = n_q * HEAD_DIM
    kw = n_kv * HEAD_DIM
    W = qw + 2 * kw
    tr = _tile(R, 320, 16)

    def fold(ref, c):
        lane = lax.broadcasted_iota(jnp.int32, (tr, 128), 1)
        x0 = ref[:, 128 * (2 * c):128 * (2 * c + 1)]
        x1 = ref[:, 128 * (2 * c + 1):128 * (2 * c + 2)]
        f0 = x0 + pltpu.roll(x0, HEAD_DIM, axis=1)
        f1 = x1 + pltpu.roll(x1, HEAD_DIM, axis=1)
        return jnp.where(lane < HEAD_DIM, f0, f1)

    def body(dq_ref, dk_ref, dv_ref, c_ref, s_ref, o_ref):
        cos = c_ref[...]
        nsin = -s_ref[...]
        for c in range(qw // 128):
            y = dq_ref[:, 128 * c:128 * (c + 1)]
            o_ref[:, 128 * c:128 * (c + 1)] = (_rope(y, cos, nsin) * (HEAD_DIM ** -0.5)).astype(BF16)
        for c in range(kw // 128):
            o_ref[:, qw + 128 * c:qw + 128 * (c + 1)] = _rope(fold(dk_ref, c), cos, nsin).astype(BF16)
            o_ref[:, qw + kw + 128 * c:qw + kw + 128 * (c + 1)] = fold(dv_ref, c).astype(BF16)

    row = lambda w: pl.BlockSpec((tr, w), lambda i: (i, 0))
    return pl.pallas_call(
        body, name=name, grid=(R // tr,),
        in_specs=[row(qw), row(2 * kw), row(2 * kw), row(128), row(128)],
        out_specs=row(W),
        out_shape=jax.ShapeDtypeStruct((R, W), BF16),
        compiler_params=_params(("parallel",)),
    )(dq, dkx, dvx, cos, sin)


def _band_bias(n, pad):
    key = lax.broadcasted_iota(jnp.int32, (2 * BLOCK, BLOCK), 0)
    qry = lax.broadcasted_iota(jnp.int32, (2 * BLOCK, BLOCK), 1)
    kmin = jnp.where(n == 0, BLOCK + pad, jnp.where(n == 1, pad, 0))
    allowed = (key > qry) & (key <= qry + BLOCK) & (key >= kmin)
    return jnp.where(allowed, 0.0, NEG_INF).astype(F32)


def _dot_nt(a, b):
    return lax.dot_general(a, b, (((1,), (1,)), ((), ())), preferred_element_type=F32)


def _band(prev_ref, cur_ref, c0):
    return jnp.concatenate([prev_ref[:, c0:c0 + 128], cur_ref[:, c0:c0 + 128]], axis=0)


def _band_t(prev_ref, cur_ref, r0):
    return jnp.concatenate([prev_ref[r0:r0 + 128, :], cur_ref[r0:r0 + 128, :]], axis=1)


def attn_fwd(q, kx, vxt, sinks, *, pad, name):
    R, qw = q.shape
    n_q = qw // HEAD_DIM
    n_kv = kx.shape[1] // 256
    nb = R // BLOCK

    def body(s_ref, q_ref, kc_ref, kp_ref, vc_ref, vp_ref, o_ref, l_ref):
        n = pl.program_id(0)
        bias = _band_bias(n, pad)
        row = lax.broadcasted_iota(jnp.int32, (128, BLOCK), 0)

        def softmax(st, sink):
            st = st + bias
            m = jnp.maximum(jnp.max(st, axis=0, keepdims=True), sink)
            e = jnp.exp(st - m)
            den = jnp.sum(e, axis=0, keepdims=True) + jnp.exp(sink - m)
            return e.astype(BF16), 1.0 / den, m + jnp.log(den)

        for g in range(n_kv):
            k2 = jnp.concatenate([_band(kp_ref, kc_ref, 256 * g), _band(kp_ref, kc_ref, 256 * g + 128)], axis=0)
            v2 = jnp.concatenate([_band_t(vp_ref, vc_ref, 256 * g), _band_t(vp_ref, vc_ref, 256 * g + 128)], axis=1)
            for p in range(GROUP // 2):
                c0 = 128 * (g * (GROUP // 2) + p)
                he = GROUP * g + 2 * p
                q2 = q_ref[:, c0:c0 + 128]
                st = _dot_nt(k2, q2)
                ee, re, le = softmax(st[0:2 * BLOCK], s_ref[he])
                eo, ro, lo = softmax(st[2 * BLOCK:4 * BLOCK], s_ref[he + 1])
                o2t = jnp.dot(v2, jnp.concatenate([ee, eo], axis=0), preferred_element_type=F32)
                o2t = o2t * jnp.where(row < HEAD_DIM, re, ro)
                o_ref[:, c0:c0 + 128] = o2t.T.astype(BF16)
                l_ref[he:he + 1, :] = le
                l_ref[he + 1:he + 2, :] = lo

    cur = lambda w: pl.BlockSpec((BLOCK, w), lambda n: (n, 0))
    prv = lambda w: pl.BlockSpec((BLOCK, w), lambda n: (jnp.maximum(n - 1, 0), 0))
    cur_t = lambda h: pl.BlockSpec((h, BLOCK), lambda n: (0, n))
    prv_t = lambda h: pl.BlockSpec((h, BLOCK), lambda n: (0, jnp.maximum(n - 1, 0)))
    kxw = kx.shape[1]
    return pl.pallas_call(
        body, name=name, grid=(nb,),
        in_specs=[pl.BlockSpec(memory_space=pltpu.SMEM), cur(qw), cur(kxw), prv(kxw), cur_t(kxw), prv_t(kxw)],
        out_specs=[cur(qw), cur_t(n_q)],
        out_shape=[jax.ShapeDtypeStruct((R, qw), BF16), jax.ShapeDtypeStruct((n_q, R), F32)],
        compiler_params=_params(("parallel",)),
    )(sinks, q, kx, kx, vxt, vxt)


def attn_bwd(q, kx, kxt, vx, o, do, lse, sinks, *, pad, name):
    R, qw = q.shape
    n_q = qw // HEAD_DIM
    n_kv = kx.shape[1] // 256
    nb = R // BLOCK
    kw2 = n_kv * 128

    def body(s_ref, q_ref, do_ref, o_ref, l_ref, kc_ref, kp_ref, ktc_ref, ktp_ref, vc_ref, vp_ref,
             dq_ref, dk_ref, dv_ref, ds_ref, cdk, cdv):
        n = pl.program_id(0)

        @pl.when(n == 0)
        def _():
            cdk[...] = jnp.zeros_like(cdk)
            cdv[...] = jnp.zeros_like(cdv)
            ds_ref[...] = jnp.zeros_like(ds_ref)

        @pl.when(n < nb)
        def _():
            bias = _band_bias(n, pad)
            lane2 = lax.broadcasted_iota(jnp.int32, (2 * BLOCK, 128), 1)
            lane1 = lax.broadcasted_iota(jnp.int32, (1, 128), 1)
            sel_r = lax.broadcasted_iota(jnp.int32, (8, 128), 0)
            sel_l = lax.broadcasted_iota(jnp.int32, (8, 128), 1)
            sel = (((sel_r == 0) & (sel_l < HEAD_DIM)) | ((sel_r == 1) & (sel_l >= HEAD_DIM))).astype(BF16)
            dsink = jnp.zeros((1, 128), F32)
            for g in range(n_kv):
                k2 = jnp.concatenate([_band(kp_ref, kc_ref, 256 * g), _band(kp_ref, kc_ref, 256 * g + 128)], axis=0)
                kt2 = jnp.concatenate([_band_t(ktp_ref, ktc_ref, 256 * g), _band_t(ktp_ref, ktc_ref, 256 * g + 128)],
                                      axis=1)
                v2 = jnp.concatenate([_band(vp_ref, vc_ref, 256 * g), _band(vp_ref, vc_ref, 256 * g + 128)], axis=0)
                dk4 = jnp.zeros((4 * BLOCK, 128), F32)
                dv4 = jnp.zeros((4 * BLOCK, 128), F32)
                for p in range(GROUP // 2):
                    c0 = 128 * (g * (GROUP // 2) + p)
                    he = GROUP * g + 2 * p
                    q2 = q_ref[:, c0:c0 + 128]
                    do2 = do_ref[:, c0:c0 + 128]
                    prod = do2.astype(F32) * o_ref[:, c0:c0 + 128].astype(F32)
                    prod_hi = prod.astype(BF16)
                    prod_lo = (prod - prod_hi.astype(F32)).astype(BF16)
                    deltas = _dot_nt(sel, prod_hi) + _dot_nt(sel, prod_lo)
                    st = _dot_nt(k2, q2)
                    dpt = _dot_nt(v2, do2)
                    pts, dsts = [], []
                    for r in range(2):
                        h = he + r
                        rows = slice(2 * BLOCK * r, 2 * BLOCK * (r + 1))
                        lse_h = l_ref[h:h + 1, :]
                        delta = deltas[r:r + 1, :]
                        pt = jnp.exp(st[rows] + bias - lse_h)
                        pts.append(pt.astype(BF16))
                        dsts.append((pt * (dpt[rows] - delta)).astype(BF16))
                        psink = jnp.exp(s_ref[h] - lse_h)
                        tot = jnp.sum(psink * delta, axis=1, keepdims=True)
                        dsink = dsink - jnp.where(lane1 == h, tot, 0.0)
                    dst = jnp.concatenate(dsts, axis=0)
                    dq_ref[:, c0:c0 + 128] = jnp.dot(kt2, dst, preferred_element_type=F32).T
                    dk4 = dk4 + jnp.dot(dst, q2, preferred_element_type=F32)
                    dv4 = dv4 + jnp.dot(jnp.concatenate(pts, axis=0), do2, preferred_element_type=F32)
                gc = pl.ds(128 * g, 128)
                dk2 = jnp.where(lane2 < HEAD_DIM, dk4[0:2 * BLOCK], dk4[2 * BLOCK:4 * BLOCK])
                dv2 = jnp.where(lane2 < HEAD_DIM, dv4[0:2 * BLOCK], dv4[2 * BLOCK:4 * BLOCK])
                dk_ref[:, gc] = cdk[:, gc] + dk2[0:BLOCK]
                dv_ref[:, gc] = cdv[:, gc] + dv2[0:BLOCK]
                cdk[:, gc] = dk2[BLOCK:2 * BLOCK]
                cdv[:, gc] = dv2[BLOCK:2 * BLOCK]
            ds_ref[0:1, :] += dsink

        @pl.when(n == nb)
        def _():
            dk_ref[...] = cdk[...]
            dv_ref[...] = cdv[...]

    cur = lambda w: pl.BlockSpec((BLOCK, w), lambda n: (jnp.minimum(n, nb - 1), 0))
    prv = lambda w: pl.BlockSpec((BLOCK, w), lambda n: (jnp.clip(n - 1, 0, nb - 1), 0))
    cur_t = lambda h: pl.BlockSpec((h, BLOCK), lambda n: (0, jnp.minimum(n, nb - 1)))
    prv_t = lambda h: pl.BlockSpec((h, BLOCK), lambda n: (0, jnp.clip(n - 1, 0, nb - 1)))
    kxw = kx.shape[1]
    return pl.pallas_call(
        body, name=name, grid=(nb + 1,),
        in_specs=[pl.BlockSpec(memory_space=pltpu.SMEM), cur(qw), cur(qw), cur(qw), cur_t(n_q),
                  cur(kxw), prv(kxw), cur_t(kxw), prv_t(kxw), cur(kxw), prv(kxw)],
        out_specs=[cur(qw), prv(kw2), prv(kw2), pl.BlockSpec((8, 128), lambda n: (0, 0))],
        out_shape=[jax.ShapeDtypeStruct((R, qw), F32), jax.ShapeDtypeStruct((R, kw2), F32),
                   jax.ShapeDtypeStruct((R, kw2), F32), jax.ShapeDtypeStruct((8, 128), F32)],
        scratch_shapes=[pltpu.VMEM((BLOCK, kw2), F32), pltpu.VMEM((BLOCK, kw2), F32)],
        compiler_params=_params(("arbitrary",)),
    )(sinks, q, do, o, lse, kx, kx, kxt, kxt, vx, vx)


def loss_bwd(h, g, target, *, name):
    R, D = h.shape
    nb = R // BLOCK

    def body(h_ref, g_ref, t_ref, loss_ref, dh_ref, dhb_ref, dg_ref):
        n = pl.program_id(0)
        xhat, rstd = _rms_parts(h_ref[...])
        gv = g_ref[...]
        diff = jnp.where(n > 0, xhat * gv - t_ref[...], 0.0)
        part = (0.5 / D) * jnp.sum(jnp.sum(diff * diff, axis=1, keepdims=True), axis=0, keepdims=True)
        dout = diff * (1.0 / D)
        dxh = dout * gv
        dh = rstd * (dxh - xhat * jnp.mean(dxh * xhat, axis=-1, keepdims=True))
        dh_ref[...] = dh
        dhb_ref[...] = dh.astype(BF16)
        dg = jnp.sum(dout * xhat, axis=0, keepdims=True)

        @pl.when(n == 0)
        def _():
            loss_ref[...] = jnp.zeros_like(loss_ref) + part
            dg_ref[...] = dg

        @pl.when(n > 0)
        def _():
            loss_ref[...] += part
            dg_ref[...] += dg

    blk = pl.BlockSpec((BLOCK, D), lambda n: (n, 0))
    return pl.pallas_call(
        body, name=name, grid=(nb,),
        in_specs=[blk, pl.BlockSpec((1, D), lambda n: (0, 0)),
                  pl.BlockSpec((BLOCK, D), lambda n: (jnp.maximum(n - 1, 0), 0))],
        out_specs=[pl.BlockSpec((8, 128), lambda n: (0, 0)), blk, blk, pl.BlockSpec((1, D), lambda n: (0, 0))],
        out_shape=[jax.ShapeDtypeStruct((8, 128), F32), jax.ShapeDtypeStruct((R, D), F32),
                   jax.ShapeDtypeStruct((R, D), BF16), jax.ShapeDtypeStruct((1, D), F32)],
        compiler_params=_params(("arbitrary",)),
    )(h, g, target)


def _adam_math(w, g, m, v):
    m = ADAM_B1 * m + (1.0 - ADAM_B1) * g
    v = ADAM_B2 * v + (1.0 - ADAM_B2) * (g * g)
    m_hat = m / (1.0 - ADAM_B1 ** ADAM_STEP)
    v_hat = v / (1.0 - ADAM_B2 ** ADAM_STEP)
    delta = -ADAM_LR * (m_hat / (jnp.sqrt(v_hat) + ADAM_EPS) + ADAM_WD * w)
    return delta, m, v


def adam(w, m, v, g, *, name):
    r, C = w.shape
    tr = _tile(r, 128, 8)

    def body(w_ref, m_ref, v_ref, g_ref, d_ref, mo_ref, vo_ref):
        d_ref[...], mo_ref[...], vo_ref[...] = _adam_math(w_ref[...], g_ref[...], m_ref[...], v_ref[...])

    blk = pl.BlockSpec((tr, C), lambda i: (i, 0))
    return pl.pallas_call(
        body, name=name, grid=(r // tr,), in_specs=[blk] * 4, out_specs=[blk] * 3,
        out_shape=[jax.ShapeDtypeStruct((r, C), F32)] * 3,
        compiler_params=_params(("parallel",)),
    )(w, m, v, g)


def _sum_blocks(p_ref, own_ref, me):
    acc = None
    for s in range(N_DEV):
        blk = jnp.where(me == s, own_ref[...], p_ref[s]).astype(F32)
        acc = blk if acc is None else acc + blk
    return acc


def _parts_specs(r, C, tr):
    blk = pl.BlockSpec((tr, C), lambda i, me: (i, 0))
    parts = pl.BlockSpec((N_DEV, tr, C), lambda i, me: (0, i, 0))
    own = pl.BlockSpec((None, tr, C), lambda i, me: (me[0], i, 0))
    return blk, parts, own


def adam_parts(w, m, v, parts, own, me, *, name):
    r, C = w.shape
    tr = _tile(r, 128, 8)

    def body(me_ref, w_ref, m_ref, v_ref, p_ref, own_ref, go_ref, d_ref, mo_ref, vo_ref):
        gv = _sum_blocks(p_ref, own_ref, me_ref[0])
        go_ref[...] = gv
        d_ref[...], mo_ref[...], vo_ref[...] = _adam_math(w_ref[...], gv, m_ref[...], v_ref[...])

    blk, pblk, oblk = _parts_specs(r, C, tr)
    return pl.pallas_call(
        body, name=name,
        grid_spec=pltpu.PrefetchScalarGridSpec(num_scalar_prefetch=1, grid=(r // tr,),
                                               in_specs=[blk, blk, blk, pblk, oblk], out_specs=[blk] * 4),
        out_shape=[jax.ShapeDtypeStruct((r, C), F32)] * 4,
        compiler_params=_params(("parallel",)),
    )(me, w, m, v, parts, own)


def sum_parts(parts, own, me, *, name):
    _, r, C = parts.shape
    tr = _tile(r, 128, 8)

    def body(me_ref, p_ref, own_ref, o_ref):
        o_ref[...] = _sum_blocks(p_ref, own_ref, me_ref[0])

    blk, pblk, oblk = _parts_specs(r, C, tr)
    return pl.pallas_call(
        body, name=name,
        grid_spec=pltpu.PrefetchScalarGridSpec(num_scalar_prefetch=1, grid=(r // tr,),
                                               in_specs=[pblk, oblk], out_specs=blk),
        out_shape=jax.ShapeDtypeStruct((r, C), F32),
        compiler_params=_params(("parallel",)),
    )(me, parts, own)


def cast_place(w, me, dep, *, name):
    r, C = w.shape
    tr = _tile(r, 256, 16)

    def body(me_ref, w_ref, dep_ref, s_ref, l_ref):
        v = w_ref[...].astype(BF16)
        s_ref[...] = v
        l_ref[...] = v

    blk = pl.BlockSpec((tr, C), lambda i, me: (i, 0))
    return pl.pallas_call(
        body, name=name,
        grid_spec=pltpu.PrefetchScalarGridSpec(
            num_scalar_prefetch=1, grid=(r // tr,), in_specs=[blk, ANY_SPEC],
            out_specs=[blk, pl.BlockSpec((None, tr, C), lambda i, me: (me[0], i, 0))]),
        out_shape=[jax.ShapeDtypeStruct((r, C), BF16), jax.ShapeDtypeStruct((N_DEV, r, C), BF16)],
        compiler_params=_params(("parallel",)),
    )(me, w, dep)


def _coords():
    return lax.axis_index("x"), lax.axis_index("y"), lax.axis_index("c")


def _peer(m):
    x, y, c = _coords()
    px = 1 - x if m & 4 else x
    py = 1 - y if m & 2 else y
    pc = 1 - c if m & 1 else c
    return (px, py, pc), 4 * px + 2 * py + pc


def exchange(items, *, all_to_all, name):
    n = len(items)
    if all_to_all:
        out_shape = [jax.ShapeDtypeStruct(a.shape, a.dtype) for a in items]
    else:
        out_shape = [jax.ShapeDtypeStruct((N_DEV,) + a.shape, a.dtype) for a in items]

    def body(*refs):
        ins, outs = refs[:n], refs[n:2 * n]
        send_sems, recv_sems, local_sems = refs[2 * n:]
        x, y, c = _coords()
        me = 4 * x + 2 * y + c

        def src(i, idx):
            return ins[i].at[idx] if all_to_all else ins[i]

        local = [pltpu.make_async_copy(src(i, me), outs[i].at[me], local_sems.at[i]) for i in range(n)]
        for cp in local:
            cp.start()
        sends = []
        for m in range(1, N_DEV):
            peer, pidx = _peer(m)
            for i in range(n):
                k = i * (N_DEV - 1) + m - 1
                cp = pltpu.make_async_remote_copy(src_ref=src(i, pidx), dst_ref=outs[i].at[me],
                                                  send_sem=send_sems.at[k], recv_sem=recv_sems.at[k],
                                                  device_id=peer, device_id_type=MESH)
                cp.start()
                sends.append(cp)
        for m in range(1, N_DEV):
            peer, pidx = _peer(m)
            for i in range(n):
                k = i * (N_DEV - 1) + m - 1
                pltpu.make_async_remote_copy(src_ref=src(i, pidx), dst_ref=outs[i].at[pidx],
                                             send_sem=send_sems.at[k], recv_sem=recv_sems.at[k],
                                             device_id=peer, device_id_type=MESH).wait_recv()
        for cp in sends:
            cp.wait_send()
        for cp in local:
            cp.wait()

    any_spec = pl.BlockSpec(memory_space=pl.ANY)
    return pl.pallas_call(
        body, name=name, in_specs=[any_spec] * n, out_specs=[any_spec] * n, out_shape=out_shape,
        scratch_shapes=[pltpu.SemaphoreType.DMA((n * (N_DEV - 1),)), pltpu.SemaphoreType.DMA((n * (N_DEV - 1),)),
                        pltpu.SemaphoreType.DMA((n,))],
    )(*items)


HBM_SPEC = pl.BlockSpec(memory_space=pltpu.HBM)
SEM_SPEC = pl.BlockSpec(memory_space=pltpu.SEMAPHORE)
SPLIT_PARAMS = pltpu.CompilerParams(has_side_effects=pltpu.SideEffectType.DATAFLOW_SIDE_EFFECTING)


def _split_copies(src_ref, land_ref, send_sems, recv_sems, all_to_all):
    x, y, c = _coords()
    me = 4 * x + 2 * y + c
    copies = []
    for m in range(1, N_DEV):
        peer, pidx = _peer(m)
        copies.append(pltpu.make_async_remote_copy(
            src_ref=src_ref.at[pidx] if all_to_all else src_ref, dst_ref=land_ref.at[me],
            send_sem=send_sems.at[m - 1], recv_sem=recv_sems.at[m - 1], device_id=peer, device_id_type=MESH))
    return copies


def copy_start(items, lands=None, *, all_to_all, name):
    n = len(items)
    if lands is None:
        lands = [lax.empty(a.shape if all_to_all else (N_DEV,) + a.shape, a.dtype) for a in items]

    def body(*refs):
        srcs, lnds, outs = refs[:n], refs[n:2 * n], refs[2 * n:]
        for i in range(n):
            for cp in _split_copies(srcs[i], lnds[i], outs[4 * i], outs[4 * i + 1], all_to_all):
                cp.start()
        outs[4 * n][...] = jnp.zeros((8, 128), F32)

    out_shape, out_specs, aliases = [], [], {}
    for i, (a, l) in enumerate(zip(items, lands)):
        out_shape += [pltpu.SemaphoreType.DMA((N_DEV - 1,)), pltpu.SemaphoreType.DMA((N_DEV - 1,)),
                      pltpu.HBM(a.shape, a.dtype), pltpu.HBM(l.shape, l.dtype)]
        out_specs += [SEM_SPEC, SEM_SPEC, HBM_SPEC, HBM_SPEC]
        aliases[i] = 4 * i + 2
        aliases[n + i] = 4 * i + 3
    out_shape.append(jax.ShapeDtypeStruct((8, 128), F32))
    out_specs.append(pl.BlockSpec(memory_space=pltpu.VMEM))
    hbm = lambda a: pltpu.with_memory_space_constraint(a, pltpu.HBM)
    res = pl.pallas_call(
        body, name=name, in_specs=[HBM_SPEC] * (2 * n), out_specs=out_specs, out_shape=out_shape,
        input_output_aliases=aliases, compiler_params=SPLIT_PARAMS,
    )(*[hbm(a) for a in items], *[hbm(l) for l in lands])
    return [tuple(res[4 * i:4 * i + 4]) for i in range(n)], res[4 * n]


def copy_wait(handle, after, *, all_to_all, name):
    send_sems, recv_sems, src, land = handle

    def body(src_ref, land_ref, send_ref, recv_ref, after_ref, src_out, got_ref):
        for cp in _split_copies(src_ref, land_ref, send_ref, recv_ref, all_to_all):
            cp.wait_send()
            cp.wait_recv()

    return pl.pallas_call(
        body, name=name, in_specs=[HBM_SPEC, HBM_SPEC, SEM_SPEC, SEM_SPEC, ANY_SPEC],
        out_specs=[HBM_SPEC, HBM_SPEC],
        out_shape=[pltpu.HBM(src.shape, src.dtype), pltpu.HBM(land.shape, land.dtype)],
        input_output_aliases={0: 0, 1: 1}, compiler_params=SPLIT_PARAMS,
    )(src, land, send_sems, recv_sems, after)


OTHER_CHIPS = (4, 2, 6)
SIBLING = 1


def _relay_copies(src_ref, land_ref, send_sems, recv_sems):
    x, y, c = _coords()
    me = 4 * x + 2 * y + c
    return [pltpu.make_async_remote_copy(src_ref=src_ref, dst_ref=land_ref.at[me], send_sem=send_sems.at[k],
                                         recv_sem=recv_sems.at[k], device_id=_peer(m)[0], device_id_type=MESH)
            for k, m in enumerate((SIBLING,) + OTHER_CHIPS)]


def _forward_copies(land_ref, send_sems, recv_sems):
    copies = []
    for k, m in enumerate(OTHER_CHIPS):
        pidx = _peer(m)[1]
        copies.append(pltpu.make_async_remote_copy(
            src_ref=land_ref.at[pidx], dst_ref=land_ref.at[pidx], send_sem=send_sems.at[k], recv_sem=recv_sems.at[k],
            device_id=_peer(SIBLING)[0], device_id_type=MESH))
    return copies


def relay_start(item, land, *, name):
    def body(src_ref, land_ref, send_sems, recv_sems, src_out, land_out, token):
        for cp in _relay_copies(src_ref, land_ref, send_sems, recv_sems):
            cp.start()
        token[...] = jnp.zeros((8, 128), F32)

    n = 1 + len(OTHER_CHIPS)
    hbm = lambda a: pltpu.with_memory_space_constraint(a, pltpu.HBM)
    res = pl.pallas_call(
        body, name=name, in_specs=[HBM_SPEC, HBM_SPEC],
        out_specs=[SEM_SPEC, SEM_SPEC, HBM_SPEC, HBM_SPEC, pl.BlockSpec(memory_space=pltpu.VMEM)],
        out_shape=[pltpu.SemaphoreType.DMA((n,)), pltpu.SemaphoreType.DMA((n,)), pltpu.HBM(item.shape, item.dtype),
                   pltpu.HBM(land.shape, land.dtype), jax.ShapeDtypeStruct((8, 128), F32)],
        input_output_aliases={0: 2, 1: 3}, compiler_params=SPLIT_PARAMS,
    )(hbm(item), hbm(land))
    return tuple(res[:4]), res[4]


def relay_forward(handle, after, *, name):
    send_sems, recv_sems, src, land = handle

    def body(src_ref, land_ref, send_ref, recv_ref, after_ref, src_out, land_out, send2, recv2):
        for cp in _relay_copies(src_ref, land_ref, send_ref, recv_ref):
            cp.wait_send()
            cp.wait_recv()
        for cp in _forward_copies(land_ref, send2, recv2):
            cp.start()

    n = len(OTHER_CHIPS)
    res = pl.pallas_call(
        body, name=name, in_specs=[HBM_SPEC, HBM_SPEC, SEM_SPEC, SEM_SPEC, ANY_SPEC],
        out_specs=[HBM_SPEC, HBM_SPEC, SEM_SPEC, SEM_SPEC],
        out_shape=[pltpu.HBM(src.shape, src.dtype), pltpu.HBM(land.shape, land.dtype),
                   pltpu.SemaphoreType.DMA((n,)), pltpu.SemaphoreType.DMA((n,))],
        input_output_aliases={0: 0, 1: 1}, compiler_params=SPLIT_PARAMS,
    )(src, land, send_sems, recv_sems, after)
    return res[2], res[3], res[0], res[1]


def relay_wait(handle, after, *, name):
    send_sems, recv_sems, src, land = handle

    def body(src_ref, land_ref, send_ref, recv_ref, after_ref, src_out, land_out):
        for cp in _forward_copies(land_ref, send_ref, recv_ref):
            cp.wait_send()
            cp.wait_recv()

    return pl.pallas_call(
        body, name=name, in_specs=[HBM_SPEC, HBM_SPEC, SEM_SPEC, SEM_SPEC, ANY_SPEC],
        out_specs=[HBM_SPEC, HBM_SPEC],
        out_shape=[pltpu.HBM(src.shape, src.dtype), pltpu.HBM(land.shape, land.dtype)],
        input_output_aliases={0: 0, 1: 1}, compiler_params=SPLIT_PARAMS,
    )(src, land, send_sems, recv_sems, after)[1]


def kernel(x, meta_tokens, norm_mix_0, w_in_conv, conv_w, w_out_conv, norm_mlp_0, w_up_0, w_down_0, norm_mix_1, w_qkv, attn_sinks, w_o, norm_mlp_1, w_up_1, w_down_1, norm_final, loss_target, m_meta_tokens, m_norm_mix_0, m_w_in_conv, m_conv_w, m_w_out_conv, m_norm_mlp_0, m_w_up_0, m_w_down_0, m_norm_mix_1, m_w_qkv, m_attn_sinks, m_w_o, m_norm_mlp_1, m_w_up_1, m_w_down_1, m_norm_final, v_meta_tokens, v_norm_mix_0, v_w_in_conv, v_conv_w, v_w_out_conv, v_norm_mlp_0, v_w_up_0, v_w_down_0, v_norm_mix_1, v_w_qkv, v_attn_sinks, v_w_o, v_norm_mlp_1, v_w_up_1, v_w_down_1, v_norm_final):
    L, D = x.shape[1], x.shape[2]
    n_meta = meta_tokens.shape[0]
    pad = BLOCK - n_meta
    R = BLOCK + L
    n_q = D // HEAD_DIM
    n_kv = n_q // GROUP
    assert n_kv % 2 == 0 and L % BLOCK == 0 and D % 128 == 0
    x = x.reshape(L, D)
    target = loss_target.reshape(L, D)
    x_id, y_id, c_id = _coords()
    me = 4 * x_id + 2 * y_id + c_id

    col_names = ("in", "up0", "qkv", "up1")
    col_w = dict(zip(col_names, (w_in_conv, w_up_0, w_qkv, w_up_1)))
    row_names = ("out", "down0", "o", "down1")
    row_w = dict(zip(row_names, (w_out_conv, w_down_0, w_o, w_down_1)))
    me_arr = jnp.reshape(me, (1,)).astype(jnp.int32)
    natural = {k: col_w[k].T for k in col_names}
    natural.update(row_w)
    use_order = ("in", "out", "up0", "down0", "qkv", "o", "up1", "down1")
    first = cast_place(natural[use_order[0]], me_arr, me_arr, name="cast_" + use_order[0])
    first_handle, token = relay_start(first[0], first[1], name="relay_start_" + use_order[0])
    placed = [cast_place(natural[k], me_arr, token, name="cast_" + k) for k in use_order[1:]]
    handles, token = copy_start([p[0] for p in placed], [p[1] for p in placed], all_to_all=False, name="gather_start")
    handles = dict(zip(use_order[1:], handles))

    def weight(k, after):
        return copy_wait(handles[k], after, all_to_all=False, name="gather_wait_" + k)[1].reshape(-1, D)

    small_in = exchange([meta_tokens, conv_w], all_to_all=False, name="comm_gather")
    meta_full = jnp.transpose(small_in[0], (1, 0, 2)).reshape(n_meta, D)
    conv_full = jnp.transpose(small_in[1], (1, 0, 2)).reshape(conv_w.shape[0], D)

    vec = lambda a: a.reshape(1, D)
    pos = jnp.arange(R, dtype=F32) - pad
    inv = ROPE_THETA ** (-jnp.arange(0, HEAD_DIM, 2, dtype=F32) / HEAD_DIM)
    ang = pos[:, None] * inv[None, :]
    cos32, sin32 = jnp.cos(ang), jnp.sin(ang)
    cos = jnp.concatenate([cos32] * 4, axis=1)
    sin = jnp.concatenate([-sin32, sin32, -sin32, sin32], axis=1)

    W = {}
    head = jnp.concatenate([jnp.zeros((pad, D), F32), meta_full], axis=0)
    h0, n0 = first_norm(head, x, vec(norm_mix_0), token, name="norm0")
    W["in"] = relay_wait(relay_forward(first_handle, n0, name="relay_forward_in"), n0,
                         name="relay_wait_in").reshape(-1, D)
    bcu = mm(n0, W["in"], name="in_proj", out_dtype=BF16, b_rows_are_n=True)
    gated = conv_fwd(bcu, conv_full, name="conv_fwd")
    W["out"] = weight("out", gated)
    h1 = mm(gated, W["out"], name="out_proj", out_dtype=F32, b_rows_are_n=False, epi="add", extra=h0)
    n1 = norm_fwd(h1, vec(norm_mlp_0), h1, name="norm1")
    W["up0"] = weight("up0", n1)
    a0 = mm(n1, W["up0"], name="up0", out_dtype=BF16, b_rows_are_n=True, epi="relu")
    W["down0"] = weight("down0", a0)
    h2 = mm(a0, W["down0"], name="down0", out_dtype=F32, b_rows_are_n=False, epi="add", extra=h1, a_sq=True)
    n2 = norm_fwd(h2, vec(norm_mix_1), h2, name="norm2")
    W["qkv"] = weight("qkv", n2)
    qkv = mm(n2, W["qkv"], name="qkv_proj", out_dtype=F32, b_rows_are_n=True)
    q, kx, vx, kxt, vxt = rope_fwd(qkv, cos, sin, n_q=n_q, n_kv=n_kv, name="rope_fwd")
    o, lse = attn_fwd(q, kx, vxt, attn_sinks, pad=pad, name="attn_fwd")
    W["o"] = weight("o", o)
    h3 = mm(o, W["o"], name="o_proj", out_dtype=F32, b_rows_are_n=False, epi="add", extra=h2)
    n3 = norm_fwd(h3, vec(norm_mlp_1), h3, name="norm3")
    W["up1"] = weight("up1", n3)
    a1 = mm(n3, W["up1"], name="up1", out_dtype=BF16, b_rows_are_n=True, epi="relu")
    W["down1"] = weight("down1", a1)
    h4 = mm(a1, W["down1"], name="down1", out_dtype=F32, b_rows_are_n=False, epi="add", extra=h3, a_sq=True)

    loss_part, dh4, dh4b, dg_final = loss_bwd(h4, vec(norm_final), target, name="loss_bwd")
    loss = lax.psum(loss_part[0, 0], ("x", "y", "c"))

    sent = {}

    def scatter(k, dw):
        (sent[k],), tok = copy_start([dw.reshape(N_DEV, -1, D)], all_to_all=True, name="a2a_start_" + k)
        return tok

    t = scatter("down1", mm_tn(a1, dh4b, dh4b, name="dw_down1", a_sq=True))
    dup1 = mm(dh4b, W["down1"], name="d_down1", out_dtype=BF16, b_rows_are_n=True, epi="mul2a", extra=a1, dep=t)
    t = scatter("up1", mm_tn(dup1, n3, dup1, name="dw_up1"))
    dn3 = mm(dup1, W["up1"], name="d_up1", out_dtype=BF16, b_rows_are_n=False, dep=t)
    dh3, dh3b, dg_mlp1 = norm_bwd(dn3, h3, vec(norm_mlp_1), dh4, name="norm3_bwd")

    t = scatter("o", mm_tn(o, dh3b, dh3b, name="dw_o"))
    do = mm(dh3b, W["o"], name="d_o", out_dtype=BF16, b_rows_are_n=True, dep=t)
    dq, dkx, dvx, dsinks = attn_bwd(q, kx, kxt, vx, o, do, lse, attn_sinks, pad=pad, name="attn_bwd")
    dqkv = rope_bwd(dq, dkx, dvx, cos, sin, n_q=n_q, n_kv=n_kv, name="rope_bwd")
    t = scatter("qkv", mm_tn(dqkv, n2, dqkv, name="dw_qkv"))
    dn2 = mm(dqkv, W["qkv"], name="d_qkv", out_dtype=BF16, b_rows_are_n=False, dep=t)
    dh2, dh2b, dg_mix1 = norm_bwd(dn2, h2, vec(norm_mix_1), dh3, name="norm2_bwd")

    t = scatter("down0", mm_tn(a0, dh2b, dh2b, name="dw_down0", a_sq=True))
    dup0 = mm(dh2b, W["down0"], name="d_down0", out_dtype=BF16, b_rows_are_n=True, epi="mul2a", extra=a0, dep=t)
    t = scatter("up0", mm_tn(dup0, n1, dup0, name="dw_up0"))
    dn1 = mm(dup0, W["up0"], name="d_up0", out_dtype=BF16, b_rows_are_n=False, dep=t)
    dh1, dh1b, dg_mlp0 = norm_bwd(dn1, h1, vec(norm_mlp_0), dh2, name="norm1_bwd")

    t = scatter("out", mm_tn(gated, dh1b, dh1b, name="dw_out"))
    dgated = mm(dh1b, W["out"], name="d_out", out_dtype=BF16, b_rows_are_n=True, dep=t)
    dbcu, dconv = conv_bwd(bcu, conv_full, dgated, name="conv_bwd")
    t = scatter("in", mm_tn(dbcu, n0, dbcu, name="dw_in"))
    dn0 = mm(dbcu, W["in"], name="d_in", out_dtype=BF16, b_rows_are_n=False, dep=t)
    dhead, dx, dg_mix0 = last_norm_bwd(dn0, h0, vec(norm_mix_0), dh1, name="norm0_bwd")
    grad_x = dx.reshape(1, L, D)

    recv = {}
    for k in ("down1", "up1", "o", "qkv", "down0", "up0", "out", "in"):
        sent[k], recv[k] = copy_wait(sent[k], dx, all_to_all=True, name="a2a_wait_" + k)

    n_sink = attn_sinks.shape[0]
    slab = jnp.concatenate([
        dg_mix0, dg_mlp0, dg_mix1, dg_mlp1, dg_final,
        jnp.pad(dsinks[0:1, :n_sink], ((0, 0), (0, D - n_sink))), jnp.zeros((2, D), F32),
        dconv, dhead[pad:BLOCK]], axis=0)
    slabs = exchange([slab], all_to_all=False, name="comm_small")[0]
    small = sum_parts(slabs, slabs, me_arr, name="sum_small")
    cols = D // N_DEV
    my_cols = lambda a: lax.dynamic_slice_in_dim(a, me * cols, cols, axis=1)

    grads, deltas, new_m, new_v = {}, {}, {}, {}

    def update(key, w, m, v, g, shape):
        s2 = (1, -1) if w.ndim == 1 else w.shape
        if g.ndim == 3:
            g_, d_, m_, v_ = adam_parts(w, m, v, g, sent[key_of[key]], me_arr, name="adam_" + key)
        else:
            g_ = g.reshape(s2)
            d_, m_, v_ = adam(w.reshape(s2), m.reshape(s2), v.reshape(s2), g_, name="adam_" + key)
        grads[key], deltas[key], new_m[key], new_v[key] = (t.reshape(shape) for t in (g_, d_, m_, v_))

    update("meta_tokens", meta_tokens, m_meta_tokens, v_meta_tokens, my_cols(small[16:16 + n_meta]), meta_tokens.shape)
    update("norm_mix_0", norm_mix_0, m_norm_mix_0, v_norm_mix_0, small[0], (D,))
    update("conv_w", conv_w, m_conv_w, v_conv_w, my_cols(small[8:8 + conv_w.shape[0]]), conv_w.shape)
    update("norm_mlp_0", norm_mlp_0, m_norm_mlp_0, v_norm_mlp_0, small[1], (D,))
    update("norm_mix_1", norm_mix_1, m_norm_mix_1, v_norm_mix_1, small[2], (D,))
    update("attn_sinks", attn_sinks, m_attn_sinks, v_attn_sinks, small[5, :n_sink], (n_sink,))
    update("norm_mlp_1", norm_mlp_1, m_norm_mlp_1, v_norm_mlp_1, small[3], (D,))
    update("norm_final", norm_final, m_norm_final, v_norm_final, small[4], (D,))
    big = {"in": ("w_in_conv", w_in_conv, m_w_in_conv, v_w_in_conv), "up0": ("w_up_0", w_up_0, m_w_up_0, v_w_up_0),
           "qkv": ("w_qkv", w_qkv, m_w_qkv, v_w_qkv), "up1": ("w_up_1", w_up_1, m_w_up_1, v_w_up_1),
           "out": ("w_out_conv", w_out_conv, m_w_out_conv, v_w_out_conv),
           "down0": ("w_down_0", w_down_0, m_w_down_0, v_w_down_0), "o": ("w_o", w_o, m_w_o, v_w_o),
           "down1": ("w_down_1", w_down_1, m_w_down_1, v_w_down_1)}
    key_of = {big[k][0]: k for k in big}
    for k in col_names:
        key, w, m, v = big[k]
        update(key, w, m, v, sum_parts(recv[k], sent[k], me_arr, name="sum_" + k).T, w.shape)
    for k in row_names:
        key, w, m, v = big[k]
        update(key, w, m, v, recv[k], w.shape)

    order = ("meta_tokens", "norm_mix_0", "w_in_conv", "conv_w", "w_out_conv", "norm_mlp_0", "w_up_0", "w_down_0",
             "norm_mix_1", "w_qkv", "attn_sinks", "w_o", "norm_mlp_1", "w_up_1", "w_down_1", "norm_final")
    return (loss, grad_x, *[grads[k] for k in order], *[deltas[k] for k in order],
            *[new_m[k] for k in order], *[new_v[k] for k in order])
```

```python
import functools

import jax
import jax.numpy as jnp
from jax import lax
from jax.experimental import pallas as pl
from jax.experimental.pallas import tpu as pltpu

F32 = jnp.float32
BF16 = jnp.bfloat16

HEAD_DIM = 64
GROUP = 8
BLOCK = 128
N_DEV = 8
RMS_EPS = 1e-5
NEG_INF = -1e30
ROPE_THETA = 10000.0
ADAM_LR = 0.001
ADAM_B1 = 0.9
ADAM_B2 = 0.999
ADAM_EPS = 1e-08
ADAM_WD = 0.01
ADAM_STEP = 10
VMEM_LIMIT = 60 * 1024 * 1024
MESH = pl.DeviceIdType.MESH


def _tile(n, target, mult):
    best = None
    for t in range(mult, min(n, target) + 1, mult):
        if n % t == 0:
            best = t
    return best if best is not None else n


def _params(sem):
    return pltpu.CompilerParams(dimension_semantics=sem, vmem_limit_bytes=VMEM_LIMIT)


ANY_SPEC = pl.BlockSpec(memory_space=pl.ANY)


LEFT_BLOCK_BYTES = 14 * 1024 * 1024
LONG_LEFT_BLOCK_BYTES = 11 * 1024 * 1024


def mm(a, b, *, name, out_dtype, b_rows_are_n, epi=None, extra=None, a_sq=False, dep=None):
    M, K = a.shape
    N = b.shape[0] if b_rows_are_n else b.shape[1]
    tm = _tile(M, (LEFT_BLOCK_BYTES if K <= 6144 else LONG_LEFT_BLOCK_BYTES) // (2 * K), 16)
    tn = _tile(N, 512, 128)

    def body(*refs):
        a_ref, b_ref, e_ref, o_ref = refs[0], refs[1], refs[2], refs[-1]
        av = a_ref[...]
        if a_sq:
            av = av.astype(F32)
            av = (av * av).astype(BF16)
        dims = (((1,), (1,)), ((), ())) if b_rows_are_n else (((1,), (0,)), ((), ()))
        acc = lax.dot_general(av, b_ref[...], dims, preferred_element_type=F32)
        if epi == "relu":
            acc = jnp.maximum(acc, 0.0)
        elif epi == "mul2a":
            acc = acc * (2.0 * e_ref[...].astype(F32))
        elif epi == "add":
            acc = acc + e_ref[...]
        o_ref[...] = acc.astype(o_ref.dtype)

    b_spec = pl.BlockSpec((tn, K), lambda i, j: (j, 0)) if b_rows_are_n else pl.BlockSpec((K, tn), lambda i, j: (0, j))
    in_specs = [pl.BlockSpec((tm, K), lambda i, j: (i, 0)), b_spec]
    args = [a, b]
    if extra is not None:
        in_specs.append(pl.BlockSpec((tm, tn), lambda i, j: (i, j)))
        args.append(extra)
    if dep is not None:
        in_specs.append(ANY_SPEC)
        args.append(dep)
    return pl.pallas_call(
        body, name=name, grid=(M // tm, N // tn), in_specs=in_specs,
        out_specs=pl.BlockSpec((tm, tn), lambda i, j: (i, j)),
        out_shape=jax.ShapeDtypeStruct((M, N), out_dtype),
        compiler_params=_params(("parallel", "parallel")),
    )(*args)


def _rms_parts(h):
    rstd = lax.rsqrt(jnp.mean(h * h, axis=-1, keepdims=True) + RMS_EPS)
    return h * rstd, rstd


def _norm_bwd_rows(dn, h, g, dres):
    xhat, rstd = _rms_parts(h)
    dxh = dn * g
    dh = dres + rstd * (dxh - xhat * jnp.mean(dxh * xhat, axis=-1, keepdims=True))
    return dh, jnp.sum(dn * xhat, axis=0, keepdims=True)


def mm_tn(a, b, dep, *, name, a_sq=False):
    T, M = a.shape
    N = b.shape[1]
    tm = _tile(M, 1024, 128)
    tk = _tile(T, 2080, 16)
    nk = T // tk
    tc = _tile(tm, 256, 128)

    def body(a_ref, b_ref, dep_ref, o_ref, acc_ref):
        k = pl.program_id(1)

        def chunks(first, last):
            bv = b_ref[...]
            for r0 in range(0, tm, tc):
                rows = slice(r0, r0 + tc)
                av = a_ref[:, rows]
                if a_sq:
                    av = av.astype(F32)
                    av = (av * av).astype(BF16)
                part = lax.dot_general(av, bv, (((0,), (0,)), ((), ())), preferred_element_type=F32)
                if not first:
                    part = acc_ref[rows, :] + part
                if last:
                    o_ref[rows, :] = part.astype(BF16)
                else:
                    acc_ref[rows, :] = part

        if nk == 1:
            chunks(True, True)
        else:
            pl.when(k == 0)(lambda: chunks(True, False))
            pl.when((k > 0) & (k < nk - 1))(lambda: chunks(False, False))
            pl.when(k == nk - 1)(lambda: chunks(False, True))

    return pl.pallas_call(
        body, name=name, grid=(M // tm, nk),
        in_specs=[pl.BlockSpec((tk, tm), lambda i, k: (k, i)), pl.BlockSpec((tk, N), lambda i, k: (k, 0)), ANY_SPEC],
        out_specs=pl.BlockSpec((tm, N), lambda i, k: (i, 0)),
        out_shape=jax.ShapeDtypeStruct((M, N), BF16),
        scratch_shapes=[pltpu.VMEM((tm, N), F32)],
        compiler_params=_params(("parallel", "arbitrary")),
    )(a, b, dep)


def norm_fwd(h, g, dep, *, name):
    R, D = h.shape
    tr = _tile(R, 320, 16)

    def body(h_ref, g_ref, dep_ref, n_ref):
        xhat, _ = _rms_parts(h_ref[...])
        n_ref[...] = (xhat * g_ref[...]).astype(BF16)

    return pl.pallas_call(
        body, name=name, grid=(R // tr,),
        in_specs=[pl.BlockSpec((tr, D), lambda i: (i, 0)), pl.BlockSpec((1, D), lambda i: (0, 0)), ANY_SPEC],
        out_specs=pl.BlockSpec((tr, D), lambda i: (i, 0)),
        out_shape=jax.ShapeDtypeStruct((R, D), BF16),
        compiler_params=_params(("parallel",)),
    )(h, g, dep)


def norm_bwd(dn, h, g, dres, *, name):
    R, D = h.shape
    tr = _tile(R, 320, 16)

    def body(dn_ref, h_ref, g_ref, dres_ref, dh_ref, dhb_ref, dg_ref):
        i = pl.program_id(0)
        dh, dg = _norm_bwd_rows(dn_ref[...].astype(F32), h_ref[...], g_ref[...], dres_ref[...])
        dh_ref[...] = dh
        dhb_ref[...] = dh.astype(BF16)

        @pl.when(i == 0)
        def _():
            dg_ref[...] = dg

        @pl.when(i > 0)
        def _():
            dg_ref[...] += dg

    row = pl.BlockSpec((tr, D), lambda i: (i, 0))
    vec = pl.BlockSpec((1, D), lambda i: (0, 0))
    return pl.pallas_call(
        body, name=name, grid=(R // tr,), in_specs=[row, row, vec, row], out_specs=[row, row, vec],
        out_shape=[jax.ShapeDtypeStruct((R, D), F32), jax.ShapeDtypeStruct((R, D), BF16),
                   jax.ShapeDtypeStruct((1, D), F32)],
        compiler_params=_params(("arbitrary",)),
    )(dn, h, g, dres)


def first_norm(head, x, g, dep, *, name):
    L, D = x.shape
    nb = L // BLOCK + 1

    def body(head_ref, x_ref, g_ref, dep_ref, h_ref, n_ref):
        i = pl.program_id(0)
        hv = jnp.where(i == 0, head_ref[...], x_ref[...])
        h_ref[...] = hv
        xhat, _ = _rms_parts(hv)
        n_ref[...] = (xhat * g_ref[...]).astype(BF16)

    blk = pl.BlockSpec((BLOCK, D), lambda i: (i, 0))
    return pl.pallas_call(
        body, name=name, grid=(nb,),
        in_specs=[pl.BlockSpec((BLOCK, D), lambda i: (0, 0)),
                  pl.BlockSpec((BLOCK, D), lambda i: (jnp.maximum(i - 1, 0), 0)),
                  pl.BlockSpec((1, D), lambda i: (0, 0)), ANY_SPEC],
        out_specs=[blk, blk],
        out_shape=[jax.ShapeDtypeStruct((BLOCK + L, D), F32), jax.ShapeDtypeStruct((BLOCK + L, D), BF16)],
        compiler_params=_params(("parallel",)),
    )(head, x, g, dep)


def last_norm_bwd(dn, h, g, dres, *, name):
    R, D = h.shape
    nb = R // BLOCK

    def body(dn_ref, h_ref, g_ref, dres_ref, dhead_ref, dx_ref, dg_ref):
        i = pl.program_id(0)
        dh, dg = _norm_bwd_rows(dn_ref[...].astype(F32), h_ref[...], g_ref[...], dres_ref[...])
        dx_ref[...] = dh

        @pl.when(i == 0)
        def _():
            dhead_ref[...] = dh
            dg_ref[...] = dg

        @pl.when(i > 0)
        def _():
            dg_ref[...] += dg

    blk = pl.BlockSpec((BLOCK, D), lambda i: (i, 0))
    vec = pl.BlockSpec((1, D), lambda i: (0, 0))
    return pl.pallas_call(
        body, name=name, grid=(nb,), in_specs=[blk, blk, vec, blk],
        out_specs=[pl.BlockSpec((BLOCK, D), lambda i: (0, 0)),
                   pl.BlockSpec((BLOCK, D), lambda i: (jnp.maximum(i - 1, 0), 0)), vec],
        out_shape=[jax.ShapeDtypeStruct((BLOCK, D), F32), jax.ShapeDtypeStruct((R - BLOCK, D), F32),
                   jax.ShapeDtypeStruct((1, D), F32)],
        compiler_params=_params(("arbitrary",)),
    )(dn, h, g, dres)


HALO = 16


def conv_fwd(bcu, conv_w, *, name):
    R, D3 = bcu.shape
    D = D3 // 3
    tr = _tile(R, 320, 16)
    tc = _tile(D, 512, 128)
    nc = D // tc
    hb = tr // HALO

    def body(b_ref, c_ref, u_ref, ch_ref, uh_ref, w_ref, o_ref, vbuf):
        i = pl.program_id(0)
        v = c_ref[...].astype(F32) * u_ref[...].astype(F32)
        vbuf[0:HALO, :] = jnp.where(i > 0, ch_ref[...].astype(F32) * uh_ref[...].astype(F32), 0.0)
        vbuf[HALO:HALO + tr, :] = v
        v1 = vbuf[HALO - 1:HALO - 1 + tr, :]
        v2 = vbuf[HALO - 2:HALO - 2 + tr, :]
        conv = w_ref[0:1, :] * v2 + w_ref[1:2, :] * v1 + w_ref[2:3, :] * v
        o_ref[...] = (b_ref[...].astype(F32) * conv).astype(BF16)

    def prev(i):
        return jnp.maximum(i * hb - 1, 0)

    return pl.pallas_call(
        body, name=name, grid=(R // tr, nc),
        in_specs=[pl.BlockSpec((tr, tc), lambda i, j: (i, j)),
                  pl.BlockSpec((tr, tc), lambda i, j: (i, nc + j)),
                  pl.BlockSpec((tr, tc), lambda i, j: (i, 2 * nc + j)),
                  pl.BlockSpec((HALO, tc), lambda i, j: (prev(i), nc + j)),
                  pl.BlockSpec((HALO, tc), lambda i, j: (prev(i), 2 * nc + j)),
                  pl.BlockSpec((3, tc), lambda i, j: (0, j))],
        out_specs=pl.BlockSpec((tr, tc), lambda i, j: (i, j)),
        out_shape=jax.ShapeDtypeStruct((R, D), BF16),
        scratch_shapes=[pltpu.VMEM((HALO + tr, tc), F32)],
        compiler_params=_params(("parallel", "parallel")),
    )(bcu, bcu, bcu, bcu, bcu, conv_w)


def conv_bwd(bcu, conv_w, dg, *, name):
    R, D3 = bcu.shape
    D = D3 // 3
    tr = _tile(R, 320, 16)
    tc = _tile(D, 512, 128)
    hb = tr // HALO
    nt = R // tr
    last_h = R // HALO - 1

    def body(x_ref, xp_ref, xn_ref, dg_ref, dgn_ref, w_ref, o_ref, dw_ref, vbuf, dbuf):
        i = pl.program_id(0)
        row8 = lax.broadcasted_iota(jnp.int32, (8, tc), 0)
        for c0 in range(0, D, tc):
            cb, cc, cu = slice(c0, c0 + tc), slice(D + c0, D + c0 + tc), slice(2 * D + c0, 2 * D + c0 + tc)
            w0, w1, w2 = w_ref[0:1, cb], w_ref[1:2, cb], w_ref[2:3, cb]
            b = x_ref[:, cb].astype(F32)
            c = x_ref[:, cc].astype(F32)
            u = x_ref[:, cu].astype(F32)
            v = c * u
            vbuf[0:HALO, :] = jnp.where(i > 0, xp_ref[:, cc].astype(F32) * xp_ref[:, cu].astype(F32), 0.0)
            vbuf[HALO:HALO + tr, :] = v
            v1 = vbuf[HALO - 1:HALO - 1 + tr, :]
            v2 = vbuf[HALO - 2:HALO - 2 + tr, :]
            dgv = dg_ref[:, cb].astype(F32)
            o_ref[:, cb] = (dgv * (w0 * v2 + w1 * v1 + w2 * v)).astype(BF16)
            dconv = dgv * b
            dbuf[0:tr, :] = dconv
            dbuf[tr:tr + HALO, :] = jnp.where(i < nt - 1, dgn_ref[:, cb].astype(F32) * xn_ref[:, cb].astype(F32), 0.0)
            dv = w2 * dconv + w1 * dbuf[1:1 + tr, :] + w0 * dbuf[2:2 + tr, :]
            o_ref[:, cc] = (dv * u).astype(BF16)
            o_ref[:, cu] = (dv * c).astype(BF16)
            dw = jnp.where(row8 == 0, jnp.sum(dconv * v2, axis=0, keepdims=True),
                           jnp.where(row8 == 1, jnp.sum(dconv * v1, axis=0, keepdims=True),
                                     jnp.where(row8 == 2, jnp.sum(dconv * v, axis=0, keepdims=True), 0.0)))

            @pl.when(i == 0)
            def _():
                dw_ref[:, cb] = dw

            @pl.when(i > 0)
            def _():
                dw_ref[:, cb] += dw

    def prev(i):
        return jnp.maximum(i * hb - 1, 0)

    def nxt(i):
        return jnp.minimum((i + 1) * hb, last_h)

    return pl.pallas_call(
        body, name=name, grid=(nt,),
        in_specs=[pl.BlockSpec((tr, D3), lambda i: (i, 0)),
                  pl.BlockSpec((HALO, D3), lambda i: (prev(i), 0)),
                  pl.BlockSpec((HALO, D3), lambda i: (nxt(i), 0)),
                  pl.BlockSpec((tr, D), lambda i: (i, 0)),
                  pl.BlockSpec((HALO, D), lambda i: (nxt(i), 0)),
                  pl.BlockSpec((3, D), lambda i: (0, 0))],
        out_specs=[pl.BlockSpec((tr, D3), lambda i: (i, 0)), pl.BlockSpec((8, D), lambda i: (0, 0))],
        out_shape=[jax.ShapeDtypeStruct((R, D3), BF16), jax.ShapeDtypeStruct((8, D), F32)],
        scratch_shapes=[pltpu.VMEM((HALO + tr, tc), F32), pltpu.VMEM((tr + HALO, tc), F32)],
        compiler_params=_params(("arbitrary",)),
    )(bcu, bcu, bcu, dg, dg, conv_w)


def _swap32(x):
    w = x.shape[1]
    lane = lax.broadcasted_iota(jnp.int32, x.shape, 1)
    return jnp.where((lane & (HEAD_DIM - 1)) < HEAD_DIM // 2, pltpu.roll(x, w - HEAD_DIM // 2, axis=1),
                     pltpu.roll(x, HEAD_DIM // 2, axis=1))


def _rope(x, cos, sin):
    return x * cos + _swap32(x) * sin


def rope_fwd(qkv, cos, sin, *, n_q, n_kv, name):
    R, W = qkv.shape
    qw = n_q * HEAD_DIM
    kw = n_kv * HEAD_DIM
    tr = _tile(R, 640, 128)

    def expand(y, ref, ref_t, c):
        lane = lax.broadcasted_iota(jnp.int32, y.shape, 1)
        lo = jnp.where(lane < HEAD_DIM, y, 0.0)
        hi = jnp.where(lane >= HEAD_DIM, y, 0.0)
        tiles = (lo, pltpu.roll(lo, HEAD_DIM, axis=1), pltpu.roll(hi, HEAD_DIM, axis=1), hi)
        for t, tile in enumerate(tiles):
            c0 = 512 * c + 128 * t
            ref[:, c0:c0 + 128] = tile.astype(BF16)
            ref_t[c0:c0 + 128, :] = tile.T.astype(BF16)

    def body(x_ref, c_ref, s_ref, q_ref, k_ref, v_ref, kt_ref, vt_ref):
        cos = c_ref[...]
        sin = s_ref[...]
        for c in range(qw // 128):
            x = x_ref[:, 128 * c:128 * (c + 1)]
            q_ref[:, 128 * c:128 * (c + 1)] = (_rope(x, cos, sin) * (HEAD_DIM ** -0.5)).astype(BF16)
        for c in range(kw // 128):
            expand(_rope(x_ref[:, qw + 128 * c:qw + 128 * (c + 1)], cos, sin), k_ref, kt_ref, c)
            expand(x_ref[:, qw + kw + 128 * c:qw + kw + 128 * (c + 1)], v_ref, vt_ref, c)

    row = lambda w: pl.BlockSpec((tr, w), lambda i: (i, 0))
    col = pl.BlockSpec((4 * kw, tr), lambda i: (0, i))
    return pl.pallas_call(
        body, name=name, grid=(R // tr,),
        in_specs=[row(W), row(128), row(128)],
        out_specs=[row(qw), row(4 * kw), row(4 * kw), col, col],
        out_shape=[jax.ShapeDtypeStruct((R, qw), BF16), jax.ShapeDtypeStruct((R, 4 * kw), BF16),
                   jax.ShapeDtypeStruct((R, 4 * kw), BF16), jax.ShapeDtypeStruct((4 * kw, R), BF16),
                   jax.ShapeDtypeStruct((4 * kw, R), BF16)],
        compiler_params=_params(("parallel",)),
    )(qkv, cos, sin)


def rope_bwd(dq, dkx, dvx, cos, sin, *, n_q, n_kv, name):
    R = dq.shape[0]
    qw = n_q * HEAD_DIM
    kw = n_kv * HEAD_DIM
    W = qw + 2 * kw
    tr = _tile(R, 320, 16)

    def fold(ref, c):
        lane = lax.broadcasted_iota(jnp.int32, (tr, 128), 1)
        x0 = ref[:, 128 * (2 * c):128 * (2 * c + 1)]
        x1 = ref[:, 128 * (2 * c + 1):128 * (2 * c + 2)]
        f0 = x0 + pltpu.roll(x0, HEAD_DIM, axis=1)
        f1 = x1 + pltpu.roll(x1, HEAD_DIM, axis=1)
        return jnp.where(lane < HEAD_DIM, f0, f1)

    def body(dq_ref, dk_ref, dv_ref, c_ref, s_ref, o_ref):
        cos = c_ref[...]
        nsin = -s_ref[...]
        for c in range(qw // 128):
            y = dq_ref[:, 128 * c:128 * (c + 1)]
            o_ref[:, 128 * c:128 * (c + 1)] = (_rope(y, cos, nsin) * (HEAD_DIM ** -0.5)).astype(BF16)
        for c in range(kw // 128):
            o_ref[:, qw + 128 * c:qw + 128 * (c + 1)] = _rope(fold(dk_ref, c), cos, nsin).astype(BF16)
            o_ref[:, qw + kw + 128 * c:qw + kw + 128 * (c + 1)] = fold(dv_ref, c).astype(BF16)

    row = lambda w: pl.BlockSpec((tr, w), lambda i: (i, 0))
    return pl.pallas_call(
        body, name=name, grid=(R // tr,),
        in_specs=[row(qw), row(2 * kw), row(2 * kw), row(128), row(128)],
        out_specs=row(W),
        out_shape=jax.ShapeDtypeStruct((R, W), BF16),
        compiler_params=_params(("parallel",)),
    )(dq, dkx, dvx, cos, sin)


def _band_bias(n, pad):
    key = lax.broadcasted_iota(jnp.int32, (2 * BLOCK, BLOCK), 0)
    qry = lax.broadcasted_iota(jnp.int32, (2 * BLOCK, BLOCK), 1)
    kmin = jnp.where(n == 0, BLOCK + pad, jnp.where(n == 1, pad, 0))
    allowed = (key > qry) & (key <= qry + BLOCK) & (key >= kmin)
    return jnp.where(allowed, 0.0, NEG_INF).astype(F32)


def _dot_nt(a, b):
    return lax.dot_general(a, b, (((1,), (1,)), ((), ())), preferred_element_type=F32)


def _band(prev_ref, cur_ref, c0):
    return jnp.concatenate([prev_ref[:, c0:c0 + 128], cur_ref[:, c0:c0 + 128]], axis=0)


def _band_t(prev_ref, cur_ref, r0):
    return jnp.concatenate([prev_ref[r0:r0 + 128, :], cur_ref[r0:r0 + 128, :]], axis=1)


def attn_fwd(q, kx, vxt, sinks, *, pad, name):
    R, qw = q.shape
    n_q = qw // HEAD_DIM
    n_kv = kx.shape[1] // 256
    nb = R // BLOCK

    def body(s_ref, q_ref, kc_ref, kp_ref, vc_ref, vp_ref, o_ref, l_ref):
        n = pl.program_id(0)
        bias = _band_bias(n, pad)
        row = lax.broadcasted_iota(jnp.int32, (128, BLOCK), 0)

        def softmax(st, sink):
            st = st + bias
            m = jnp.maximum(jnp.max(st, axis=0, keepdims=True), sink)
            e = jnp.exp(st - m)
            den = jnp.sum(e, axis=0, keepdims=True) + jnp.exp(sink - m)
            return e.astype(BF16), 1.0 / den, m + jnp.log(den)

        for g in range(n_kv):
            k2 = jnp.concatenate([_band(kp_ref, kc_ref, 256 * g), _band(kp_ref, kc_ref, 256 * g + 128)], axis=0)
            v2 = jnp.concatenate([_band_t(vp_ref, vc_ref, 256 * g), _band_t(vp_ref, vc_ref, 256 * g + 128)], axis=1)
            for p in range(GROUP // 2):
                c0 = 128 * (g * (GROUP // 2) + p)
                he = GROUP * g + 2 * p
                q2 = q_ref[:, c0:c0 + 128]
                st = _dot_nt(k2, q2)
                ee, re, le = softmax(st[0:2 * BLOCK], s_ref[he])
                eo, ro, lo = softmax(st[2 * BLOCK:4 * BLOCK], s_ref[he + 1])
                o2t = jnp.dot(v2, jnp.concatenate([ee, eo], axis=0), preferred_element_type=F32)
                o2t = o2t * jnp.where(row < HEAD_DIM, re, ro)
                o_ref[:, c0:c0 + 128] = o2t.T.astype(BF16)
                l_ref[he:he + 1, :] = le
                l_ref[he + 1:he + 2, :] = lo

    cur = lambda w: pl.BlockSpec((BLOCK, w), lambda n: (n, 0))
    prv = lambda w: pl.BlockSpec((BLOCK, w), lambda n: (jnp.maximum(n - 1, 0), 0))
    cur_t = lambda h: pl.BlockSpec((h, BLOCK), lambda n: (0, n))
    prv_t = lambda h: pl.BlockSpec((h, BLOCK), lambda n: (0, jnp.maximum(n - 1, 0)))
    kxw = kx.shape[1]
    return pl.pallas_call(
        body, name=name, grid=(nb,),
        in_specs=[pl.BlockSpec(memory_space=pltpu.SMEM), cur(qw), cur(kxw), prv(kxw), cur_t(kxw), prv_t(kxw)],
        out_specs=[cur(qw), cur_t(n_q)],
        out_shape=[jax.ShapeDtypeStruct((R, qw), BF16), jax.ShapeDtypeStruct((n_q, R), F32)],
        compiler_params=_params(("parallel",)),
    )(sinks, q, kx, kx, vxt, vxt)


def attn_bwd(q, kx, kxt, vx, o, do, lse, sinks, *, pad, name):
    R, qw = q.shape
    n_q = qw // HEAD_DIM
    n_kv = kx.shape[1] // 256
    nb = R // BLOCK
    kw2 = n_kv * 128

    def body(s_ref, q_ref, do_ref, o_ref, l_ref, kc_ref, kp_ref, ktc_ref, ktp_ref, vc_ref, vp_ref,
             dq_ref, dk_ref, dv_ref, ds_ref, cdk, cdv):
        n = pl.program_id(0)

        @pl.when(n == 0)
        def _():
            cdk[...] = jnp.zeros_like(cdk)
            cdv[...] = jnp.zeros_like(cdv)
            ds_ref[...] = jnp.zeros_like(ds_ref)

        @pl.when(n < nb)
        def _():
            bias = _band_bias(n, pad)
            lane2 = lax.broadcasted_iota(jnp.int32, (2 * BLOCK, 128), 1)
            lane1 = lax.broadcasted_iota(jnp.int32, (1, 128), 1)
            sel_r = lax.broadcasted_iota(jnp.int32, (8, 128), 0)
            sel_l = lax.broadcasted_iota(jnp.int32, (8, 128), 1)
            sel = (((sel_r == 0) & (sel_l < HEAD_DIM)) | ((sel_r == 1) & (sel_l >= HEAD_DIM))).astype(BF16)
            dsink = jnp.zeros((1, 128), F32)
            for g in range(n_kv):
                k2 = jnp.concatenate([_band(kp_ref, kc_ref, 256 * g), _band(kp_ref, kc_ref, 256 * g + 128)], axis=0)
                kt2 = jnp.concatenate([_band_t(ktp_ref, ktc_ref, 256 * g), _band_t(ktp_ref, ktc_ref, 256 * g + 128)],
                                      axis=1)
                v2 = jnp.concatenate([_band(vp_ref, vc_ref, 256 * g), _band(vp_ref, vc_ref, 256 * g + 128)], axis=0)
                dk4 = jnp.zeros((4 * BLOCK, 128), F32)
                dv4 = jnp.zeros((4 * BLOCK, 128), F32)
                for p in range(GROUP // 2):
                    c0 = 128 * (g * (GROUP // 2) + p)
                    he = GROUP * g + 2 * p
                    q2 = q_ref[:, c0:c0 + 128]
                    do2 = do_ref[:, c0:c0 + 128]
                    prod = do2.astype(F32) * o_ref[:, c0:c0 + 128].astype(F32)
                    prod_hi = prod.astype(BF16)
                    prod_lo = (prod - prod_hi.astype(F32)).astype(BF16)
                    deltas = _dot_nt(sel, prod_hi) + _dot_nt(sel, prod_lo)
                    st = _dot_nt(k2, q2)
                    dpt = _dot_nt(v2, do2)
                    pts, dsts = [], []
                    for r in range(2):
                        h = he + r
                        rows = slice(2 * BLOCK * r, 2 * BLOCK * (r + 1))
                        lse_h = l_ref[h:h + 1, :]
                        delta = deltas[r:r + 1, :]
                        pt = jnp.exp(st[rows] + bias - lse_h)
                        pts.append(pt.astype(BF16))
                        dsts.append((pt * (dpt[rows] - delta)).astype(BF16))
                        psink = jnp.exp(s_ref[h] - lse_h)
                        tot = jnp.sum(psink * delta, axis=1, keepdims=True)
                        dsink = dsink - jnp.where(lane1 == h, tot, 0.0)
                    dst = jnp.concatenate(dsts, axis=0)
                    dq_ref[:, c0:c0 + 128] = jnp.dot(kt2, dst, preferred_element_type=F32).T
                    dk4 = dk4 + jnp.dot(dst, q2, preferred_element_type=F32)
                    dv4 = dv4 + jnp.dot(jnp.concatenate(pts, axis=0), do2, preferred_element_type=F32)
                gc = pl.ds(128 * g, 128)
                dk2 = jnp.where(lane2 < HEAD_DIM, dk4[0:2 * BLOCK], dk4[2 * BLOCK:4 * BLOCK])
                dv2 = jnp.where(lane2 < HEAD_DIM, dv4[0:2 * BLOCK], dv4[2 * BLOCK:4 * BLOCK])
                dk_ref[:, gc] = cdk[:, gc] + dk2[0:BLOCK]
                dv_ref[:, gc] = cdv[:, gc] + dv2[0:BLOCK]
                cdk[:, gc] = dk2[BLOCK:2 * BLOCK]
                cdv[:, gc] = dv2[BLOCK:2 * BLOCK]
            ds_ref[0:1, :] += dsink

        @pl.when(n == nb)
        def _():
            dk_ref[...] = cdk[...]
            dv_ref[...] = cdv[...]

    cur = lambda w: pl.BlockSpec((BLOCK, w), lambda n: (jnp.minimum(n, nb - 1), 0))
    prv = lambda w: pl.BlockSpec((BLOCK, w), lambda n: (jnp.clip(n - 1, 0, nb - 1), 0))
    cur_t = lambda h: pl.BlockSpec((h, BLOCK), lambda n: (0, jnp.minimum(n, nb - 1)))
    prv_t = lambda h: pl.BlockSpec((h, BLOCK), lambda n: (0, jnp.clip(n - 1, 0, nb - 1)))
    kxw = kx.shape[1]
    return pl.pallas_call(
        body, name=name, grid=(nb + 1,),
        in_specs=[pl.BlockSpec(memory_space=pltpu.SMEM), cur(qw), cur(qw), cur(qw), cur_t(n_q),
                  cur(kxw), prv(kxw), cur_t(kxw), prv_t(kxw), cur(kxw), prv(kxw)],
        out_specs=[cur(qw), prv(kw2), prv(kw2), pl.BlockSpec((8, 128), lambda n: (0, 0))],
        out_shape=[jax.ShapeDtypeStruct((R, qw), F32), jax.ShapeDtypeStruct((R, kw2), F32),
                   jax.ShapeDtypeStruct((R, kw2), F32), jax.ShapeDtypeStruct((8, 128), F32)],
        scratch_shapes=[pltpu.VMEM((BLOCK, kw2), F32), pltpu.VMEM((BLOCK, kw2), F32)],
        compiler_params=_params(("arbitrary",)),
    )(sinks, q, do, o, lse, kx, kx, kxt, kxt, vx, vx)


def loss_bwd(h, g, target, *, name):
    R, D = h.shape
    nb = R // BLOCK

    def body(h_ref, g_ref, t_ref, loss_ref, dh_ref, dhb_ref, dg_ref):
        n = pl.program_id(0)
        xhat, rstd = _rms_parts(h_ref[...])
        gv = g_ref[...]
        diff = jnp.where(n > 0, xhat * gv - t_ref[...], 0.0)
        part = (0.5 / D) * jnp.sum(jnp.sum(diff * diff, axis=1, keepdims=True), axis=0, keepdims=True)
        dout = diff * (1.0 / D)
        dxh = dout * gv
        dh = rstd * (dxh - xhat * jnp.mean(dxh * xhat, axis=-1, keepdims=True))
        dh_ref[...] = dh
        dhb_ref[...] = dh.astype(BF16)
        dg = jnp.sum(dout * xhat, axis=0, keepdims=True)

        @pl.when(n == 0)
        def _():
            loss_ref[...] = jnp.zeros_like(loss_ref) + part
            dg_ref[...] = dg

        @pl.when(n > 0)
        def _():
            loss_ref[...] += part
            dg_ref[...] += dg

    blk = pl.BlockSpec((BLOCK, D), lambda n: (n, 0))
    return pl.pallas_call(
        body, name=name, grid=(nb,),
        in_specs=[blk, pl.BlockSpec((1, D), lambda n: (0, 0)),
                  pl.BlockSpec((BLOCK, D), lambda n: (jnp.maximum(n - 1, 0), 0))],
        out_specs=[pl.BlockSpec((8, 128), lambda n: (0, 0)), blk, blk, pl.BlockSpec((1, D), lambda n: (0, 0))],
        out_shape=[jax.ShapeDtypeStruct((8, 128), F32), jax.ShapeDtypeStruct((R, D), F32),
                   jax.ShapeDtypeStruct((R, D), BF16), jax.ShapeDtypeStruct((1, D), F32)],
        compiler_params=_params(("arbitrary",)),
    )(h, g, target)


def _adam_math(w, g, m, v):
    m = ADAM_B1 * m + (1.0 - ADAM_B1) * g
    v = ADAM_B2 * v + (1.0 - ADAM_B2) * (g * g)
    m_hat = m / (1.0 - ADAM_B1 ** ADAM_STEP)
    v_hat = v / (1.0 - ADAM_B2 ** ADAM_STEP)
    delta = -ADAM_LR * (m_hat / (jnp.sqrt(v_hat) + ADAM_EPS) + ADAM_WD * w)
    return delta, m, v


def adam(w, m, v, g, *, name):
    r, C = w.shape
    tr = _tile(r, 128, 8)

    def body(w_ref, m_ref, v_ref, g_ref, d_ref, mo_ref, vo_ref):
        d_ref[...], mo_ref[...], vo_ref[...] = _adam_math(w_ref[...], g_ref[...], m_ref[...], v_ref[...])

    blk = pl.BlockSpec((tr, C), lambda i: (i, 0))
    return pl.pallas_call(
        body, name=name, grid=(r // tr,), in_specs=[blk] * 4, out_specs=[blk] * 3,
        out_shape=[jax.ShapeDtypeStruct((r, C), F32)] * 3,
        compiler_params=_params(("parallel",)),
    )(w, m, v, g)


def _sum_blocks(p_ref, own_ref, me):
    acc = None
    for s in range(N_DEV):
        blk = jnp.where(me == s, own_ref[...], p_ref[s]).astype(F32)
        acc = blk if acc is None else acc + blk
    return acc


def _parts_specs(r, C, tr):
    blk = pl.BlockSpec((tr, C), lambda i, me: (i, 0))
    parts = pl.BlockSpec((N_DEV, tr, C), lambda i, me: (0, i, 0))
    own = pl.BlockSpec((None, tr, C), lambda i, me: (me[0], i, 0))
    return blk, parts, own


def adam_parts(w, m, v, parts, own, me, *, name):
    r, C = w.shape
    tr = _tile(r, 128, 8)

    def body(me_ref, w_ref, m_ref, v_ref, p_ref, own_ref, go_ref, d_ref, mo_ref, vo_ref):
        gv = _sum_blocks(p_ref, own_ref, me_ref[0])
        go_ref[...] = gv
        d_ref[...], mo_ref[...], vo_ref[...] = _adam_math(w_ref[...], gv, m_ref[...], v_ref[...])

    blk, pblk, oblk = _parts_specs(r, C, tr)
    return pl.pallas_call(
        body, name=name,
        grid_spec=pltpu.PrefetchScalarGridSpec(num_scalar_prefetch=1, grid=(r // tr,),
                                               in_specs=[blk, blk, blk, pblk, oblk], out_specs=[blk] * 4),
        out_shape=[jax.ShapeDtypeStruct((r, C), F32)] * 4,
        compiler_params=_params(("parallel",)),
    )(me, w, m, v, parts, own)


def adam_parts_t(w, m, v, parts, own, me, *, name):
    K, r = w.shape
    tk = _tile(K, 256, 128)

    def body(me_ref, w_ref, m_ref, v_ref, p_ref, own_ref, go_ref, d_ref, mo_ref, vo_ref):
        gv = _sum_blocks(p_ref, own_ref, me_ref[0]).T
        go_ref[...] = gv
        d_ref[...], mo_ref[...], vo_ref[...] = _adam_math(w_ref[...], gv, m_ref[...], v_ref[...])

    blk = pl.BlockSpec((tk, r), lambda i, me: (i, 0))
    pblk = pl.BlockSpec((N_DEV, r, tk), lambda i, me: (0, 0, i))
    oblk = pl.BlockSpec((None, r, tk), lambda i, me: (me[0], 0, i))
    return pl.pallas_call(
        body, name=name,
        grid_spec=pltpu.PrefetchScalarGridSpec(num_scalar_prefetch=1, grid=(K // tk,),
                                               in_specs=[blk, blk, blk, pblk, oblk], out_specs=[blk] * 4),
        out_shape=[jax.ShapeDtypeStruct((K, r), F32)] * 4,
        compiler_params=_params(("parallel",)),
    )(me, w, m, v, parts, own)


def sum_parts(parts, own, me, *, name):
    _, r, C = parts.shape
    tr = _tile(r, 128, 8)

    def body(me_ref, p_ref, own_ref, o_ref):
        o_ref[...] = _sum_blocks(p_ref, own_ref, me_ref[0])

    blk, pblk, oblk = _parts_specs(r, C, tr)
    return pl.pallas_call(
        body, name=name,
        grid_spec=pltpu.PrefetchScalarGridSpec(num_scalar_prefetch=1, grid=(r // tr,),
                                               in_specs=[pblk, oblk], out_specs=blk),
        out_shape=jax.ShapeDtypeStruct((r, C), F32),
        compiler_params=_params(("parallel",)),
    )(me, parts, own)


def cast_place(w, me, dep, *, name):
    r, C = w.shape
    tr = _tile(r, 256, 16)

    def body(me_ref, w_ref, dep_ref, s_ref, l_ref):
        v = w_ref[...].astype(BF16)
        s_ref[...] = v
        l_ref[...] = v

    blk = pl.BlockSpec((tr, C), lambda i, me: (i, 0))
    return pl.pallas_call(
        body, name=name,
        grid_spec=pltpu.PrefetchScalarGridSpec(
            num_scalar_prefetch=1, grid=(r // tr,), in_specs=[blk, ANY_SPEC],
            out_specs=[blk, pl.BlockSpec((None, tr, C), lambda i, me: (me[0], i, 0))]),
        out_shape=[jax.ShapeDtypeStruct((r, C), BF16), jax.ShapeDtypeStruct((N_DEV, r, C), BF16)],
        compiler_params=_params(("parallel",)),
    )(me, w, dep)


def _coords():
    return lax.axis_index("x"), lax.axis_index("y"), lax.axis_index("c")


def _peer(m):
    x, y, c = _coords()
    px = 1 - x if m & 4 else x
    py = 1 - y if m & 2 else y
    pc = 1 - c if m & 1 else c
    return (px, py, pc), 4 * px + 2 * py + pc


def exchange(items, *, all_to_all, name):
    n = len(items)
    if all_to_all:
        out_shape = [jax.ShapeDtypeStruct(a.shape, a.dtype) for a in items]
    else:
        out_shape = [jax.ShapeDtypeStruct((N_DEV,) + a.shape, a.dtype) for a in items]

    def body(*refs):
        ins, outs = refs[:n], refs[n:2 * n]
        send_sems, recv_sems, local_sems = refs[2 * n:]
        x, y, c = _coords()
        me = 4 * x + 2 * y + c

        def src(i, idx):
            return ins[i].at[idx] if all_to_all else ins[i]

        local = [pltpu.make_async_copy(src(i, me), outs[i].at[me], local_sems.at[i]) for i in range(n)]
        for cp in local:
            cp.start()
        sends = []
        for m in range(1, N_DEV):
            peer, pidx = _peer(m)
            for i in range(n):
                k = i * (N_DEV - 1) + m - 1
                cp = pltpu.make_async_remote_copy(src_ref=src(i, pidx), dst_ref=outs[i].at[me],
                                                  send_sem=send_sems.at[k], recv_sem=recv_sems.at[k],
                                                  device_id=peer, device_id_type=MESH)
                cp.start()
                sends.append(cp)
        for m in range(1, N_DEV):
            peer, pidx = _peer(m)
            for i in range(n):
                k = i * (N_DEV - 1) + m - 1
                pltpu.make_async_remote_copy(src_ref=src(i, pidx), dst_ref=outs[i].at[pidx],
                                             send_sem=send_sems.at[k], recv_sem=recv_sems.at[k],
                                             device_id=peer, device_id_type=MESH).wait_recv()
        for cp in sends:
            cp.wait_send()
        for cp in local:
            cp.wait()

    any_spec = pl.BlockSpec(memory_space=pl.ANY)
    return pl.pallas_call(
        body, name=name, in_specs=[any_spec] * n, out_specs=[any_spec] * n, out_shape=out_shape,
        scratch_shapes=[pltpu.SemaphoreType.DMA((n * (N_DEV - 1),)), pltpu.SemaphoreType.DMA((n * (N_DEV - 1),)),
                        pltpu.SemaphoreType.DMA((n,))],
    )(*items)


HBM_SPEC = pl.BlockSpec(memory_space=pltpu.HBM)
SEM_SPEC = pl.BlockSpec(memory_space=pltpu.SEMAPHORE)
SPLIT_PARAMS = pltpu.CompilerParams(has_side_effects=pltpu.SideEffectType.DATAFLOW_SIDE_EFFECTING)


def _split_copies(src_ref, land_ref, send_sems, recv_sems, all_to_all):
    x, y, c = _coords()
    me = 4 * x + 2 * y + c
    copies = []
    for m in range(1, N_DEV):
        peer, pidx = _peer(m)
        copies.append(pltpu.make_async_remote_copy(
            src_ref=src_ref.at[pidx] if all_to_all else src_ref, dst_ref=land_ref.at[me],
            send_sem=send_sems.at[m - 1], recv_sem=recv_sems.at[m - 1], device_id=peer, device_id_type=MESH))
    return copies


def copy_start(items, lands=None, *, all_to_all, name):
    n = len(items)
    if lands is None:
        lands = [lax.empty(a.shape if all_to_all else (N_DEV,) + a.shape, a.dtype) for a in items]

    def body(*refs):
        srcs, lnds, outs = refs[:n], refs[n:2 * n], refs[2 * n:]
        for i in range(n):
            for cp in _split_copies(srcs[i], lnds[i], outs[4 * i], outs[4 * i + 1], all_to_all):
                cp.start()
        outs[4 * n][...] = jnp.zeros((8, 128), F32)

    out_shape, out_specs, aliases = [], [], {}
    for i, (a, l) in enumerate(zip(items, lands)):
        out_shape += [pltpu.SemaphoreType.DMA((N_DEV - 1,)), pltpu.SemaphoreType.DMA((N_DEV - 1,)),
                      pltpu.HBM(a.shape, a.dtype), pltpu.HBM(l.shape, l.dtype)]
        out_specs += [SEM_SPEC, SEM_SPEC, HBM_SPEC, HBM_SPEC]
        aliases[i] = 4 * i + 2
        aliases[n + i] = 4 * i + 3
    out_shape.append(jax.ShapeDtypeStruct((8, 128), F32))
    out_specs.append(pl.BlockSpec(memory_space=pltpu.VMEM))
    hbm = lambda a: pltpu.with_memory_space_constraint(a, pltpu.HBM)
    res = pl.pallas_call(
        body, name=name, in_specs=[HBM_SPEC] * (2 * n), out_specs=out_specs, out_shape=out_shape,
        input_output_aliases=aliases, compiler_params=SPLIT_PARAMS,
    )(*[hbm(a) for a in items], *[hbm(l) for l in lands])
    return [tuple(res[4 * i:4 * i + 4]) for i in range(n)], res[4 * n]


def copy_wait(handle, after, *, all_to_all, name):
    send_sems, recv_sems, src, land = handle

    def body(src_ref, land_ref, send_ref, recv_ref, after_ref, src_out, got_ref):
        for cp in _split_copies(src_ref, land_ref, send_ref, recv_ref, all_to_all):
            cp.wait_send()
            cp.wait_recv()

    return pl.pallas_call(
        body, name=name, in_specs=[HBM_SPEC, HBM_SPEC, SEM_SPEC, SEM_SPEC, ANY_SPEC],
        out_specs=[HBM_SPEC, HBM_SPEC],
        out_shape=[pltpu.HBM(src.shape, src.dtype), pltpu.HBM(land.shape, land.dtype)],
        input_output_aliases={0: 0, 1: 1}, compiler_params=SPLIT_PARAMS,
    )(src, land, send_sems, recv_sems, after)


OTHER_CHIPS = (4, 2, 6)
SIBLING = 1


def _relay_copies(src_ref, land_ref, send_sems, recv_sems):
    x, y, c = _coords()
    me = 4 * x + 2 * y + c
    return [pltpu.make_async_remote_copy(src_ref=src_ref, dst_ref=land_ref.at[me], send_sem=send_sems.at[k],
                                         recv_sem=recv_sems.at[k], device_id=_peer(m)[0], device_id_type=MESH)
            for k, m in enumerate((SIBLING,) + OTHER_CHIPS)]


def _forward_copies(land_ref, send_sems, recv_sems):
    copies = []
    for k, m in enumerate(OTHER_CHIPS):
        pidx = _peer(m)[1]
        copies.append(pltpu.make_async_remote_copy(
            src_ref=land_ref.at[pidx], dst_ref=land_ref.at[pidx], send_sem=send_sems.at[k], recv_sem=recv_sems.at[k],
            device_id=_peer(SIBLING)[0], device_id_type=MESH))
    return copies


def relay_start(item, land, *, name):
    def body(src_ref, land_ref, send_sems, recv_sems, src_out, land_out, token):
        for cp in _relay_copies(src_ref, land_ref, send_sems, recv_sems):
            cp.start()
        token[...] = jnp.zeros((8, 128), F32)

    n = 1 + len(OTHER_CHIPS)
    hbm = lambda a: pltpu.with_memory_space_constraint(a, pltpu.HBM)
    res = pl.pallas_call(
        body, name=name, in_specs=[HBM_SPEC, HBM_SPEC],
        out_specs=[SEM_SPEC, SEM_SPEC, HBM_SPEC, HBM_SPEC, pl.BlockSpec(memory_space=pltpu.VMEM)],
        out_shape=[pltpu.SemaphoreType.DMA((n,)), pltpu.SemaphoreType.DMA((n,)), pltpu.HBM(item.shape, item.dtype),
                   pltpu.HBM(land.shape, land.dtype), jax.ShapeDtypeStruct((8, 128), F32)],
        input_output_aliases={0: 2, 1: 3}, compiler_params=SPLIT_PARAMS,
    )(hbm(item), hbm(land))
    return tuple(res[:4]), res[4]


def relay_forward(handle, after, *, name):
    send_sems, recv_sems, src, land = handle

    def body(src_ref, land_ref, send_ref, recv_ref, after_ref, src_out, land_out, send2, recv2):
        for cp in _relay_copies(src_ref, land_ref, send_ref, recv_ref):
            cp.wait_send()
            cp.wait_recv()
        for cp in _forward_copies(land_ref, send2, recv2):
            cp.start()

    n = len(OTHER_CHIPS)
    res = pl.pallas_call(
        body, name=name, in_specs=[HBM_SPEC, HBM_SPEC, SEM_SPEC, SEM_SPEC, ANY_SPEC],
        out_specs=[HBM_SPEC, HBM_SPEC, SEM_SPEC, SEM_SPEC],
        out_shape=[pltpu.HBM(src.shape, src.dtype), pltpu.HBM(land.shape, land.dtype),
                   pltpu.SemaphoreType.DMA((n,)), pltpu.SemaphoreType.DMA((n,))],
        input_output_aliases={0: 0, 1: 1}, compiler_params=SPLIT_PARAMS,
    )(src, land, send_sems, recv_sems, after)
    return res[2], res[3], res[0], res[1]


def relay_wait(handle, after, *, name):
    send_sems, recv_sems, src, land = handle

    def body(src_ref, land_ref, send_ref, recv_ref, after_ref, src_out, land_out):
        for cp in _forward_copies(land_ref, send_ref, recv_ref):
            cp.wait_send()
            cp.wait_recv()

    return pl.pallas_call(
        body, name=name, in_specs=[HBM_SPEC, HBM_SPEC, SEM_SPEC, SEM_SPEC, ANY_SPEC],
        out_specs=[HBM_SPEC, HBM_SPEC],
        out_shape=[pltpu.HBM(src.shape, src.dtype), pltpu.HBM(land.shape, land.dtype)],
        input_output_aliases={0: 0, 1: 1}, compiler_params=SPLIT_PARAMS,
    )(src, land, send_sems, recv_sems, after)[1]


def kernel(x, meta_tokens, norm_mix_0, w_in_conv, conv_w, w_out_conv, norm_mlp_0, w_up_0, w_down_0, norm_mix_1, w_qkv, attn_sinks, w_o, norm_mlp_1, w_up_1, w_down_1, norm_final, loss_target, m_meta_tokens, m_norm_mix_0, m_w_in_conv, m_conv_w, m_w_out_conv, m_norm_mlp_0, m_w_up_0, m_w_down_0, m_norm_mix_1, m_w_qkv, m_attn_sinks, m_w_o, m_norm_mlp_1, m_w_up_1, m_w_down_1, m_norm_final, v_meta_tokens, v_norm_mix_0, v_w_in_conv, v_conv_w, v_w_out_conv, v_norm_mlp_0, v_w_up_0, v_w_down_0, v_norm_mix_1, v_w_qkv, v_attn_sinks, v_w_o, v_norm_mlp_1, v_w_up_1, v_w_down_1, v_norm_final):
    L, D = x.shape[1], x.shape[2]
    n_meta = meta_tokens.shape[0]
    pad = BLOCK - n_meta
    R = BLOCK + L
    n_q = D // HEAD_DIM
    n_kv = n_q // GROUP
    assert n_kv % 2 == 0 and L % BLOCK == 0 and D % 128 == 0
    x = x.reshape(L, D)
    target = loss_target.reshape(L, D)
    x_id, y_id, c_id = _coords()
    me = 4 * x_id + 2 * y_id + c_id

    col_names = ("in", "up0", "qkv", "up1")
    col_w = dict(zip(col_names, (w_in_conv, w_up_0, w_qkv, w_up_1)))
    row_names = ("out", "down0", "o", "down1")
    row_w = dict(zip(row_names, (w_out_conv, w_down_0, w_o, w_down_1)))
    me_arr = jnp.reshape(me, (1,)).astype(jnp.int32)
    natural = {k: col_w[k].T for k in col_names}
    natural.update(row_w)
    use_order = ("in", "out", "up0", "down0", "qkv", "o", "up1", "down1")
    first = cast_place(natural[use_order[0]], me_arr, me_arr, name="cast_" + use_order[0])
    first_handle, token = relay_start(first[0], first[1], name="relay_start_" + use_order[0])
    placed = [cast_place(natural[k], me_arr, token, name="cast_" + k) for k in use_order[1:]]
    handles, token = copy_start([p[0] for p in placed], [p[1] for p in placed], all_to_all=False, name="gather_start")
    handles = dict(zip(use_order[1:], handles))

    def weight(k, after):
        return copy_wait(handles[k], after, all_to_all=False, name="gather_wait_" + k)[1].reshape(-1, D)

    small_in = exchange([meta_tokens, conv_w], all_to_all=False, name="comm_gather")
    meta_full = jnp.transpose(small_in[0], (1, 0, 2)).reshape(n_meta, D)
    conv_full = jnp.transpose(small_in[1], (1, 0, 2)).reshape(conv_w.shape[0], D)

    vec = lambda a: a.reshape(1, D)
    pos = jnp.arange(R, dtype=F32) - pad
    inv = ROPE_THETA ** (-jnp.arange(0, HEAD_DIM, 2, dtype=F32) / HEAD_DIM)
    ang = pos[:, None] * inv[None, :]
    cos32, sin32 = jnp.cos(ang), jnp.sin(ang)
    cos = jnp.concatenate([cos32] * 4, axis=1)
    sin = jnp.concatenate([-sin32, sin32, -sin32, sin32], axis=1)

    W = {}
    head = jnp.concatenate([jnp.zeros((pad, D), F32), meta_full], axis=0)
    h0, n0 = first_norm(head, x, vec(norm_mix_0), token, name="norm0")
    W["in"] = relay_wait(relay_forward(first_handle, n0, name="relay_forward_in"), n0,
                         name="relay_wait_in").reshape(-1, D)
    bcu = mm(n0, W["in"], name="in_proj", out_dtype=BF16, b_rows_are_n=True)
    gated = conv_fwd(bcu, conv_full, name="conv_fwd")
    W["out"] = weight("out", gated)
    h1 = mm(gated, W["out"], name="out_proj", out_dtype=F32, b_rows_are_n=False, epi="add", extra=h0)
    n1 = norm_fwd(h1, vec(norm_mlp_0), h1, name="norm1")
    W["up0"] = weight("up0", n1)
    a0 = mm(n1, W["up0"], name="up0", out_dtype=BF16, b_rows_are_n=True, epi="relu")
    W["down0"] = weight("down0", a0)
    h2 = mm(a0, W["down0"], name="down0", out_dtype=F32, b_rows_are_n=False, epi="add", extra=h1, a_sq=True)
    n2 = norm_fwd(h2, vec(norm_mix_1), h2, name="norm2")
    W["qkv"] = weight("qkv", n2)
    qkv = mm(n2, W["qkv"], name="qkv_proj", out_dtype=F32, b_rows_are_n=True)
    q, kx, vx, kxt, vxt = rope_fwd(qkv, cos, sin, n_q=n_q, n_kv=n_kv, name="rope_fwd")
    o, lse = attn_fwd(q, kx, vxt, attn_sinks, pad=pad, name="attn_fwd")
    W["o"] = weight("o", o)
    h3 = mm(o, W["o"], name="o_proj", out_dtype=F32, b_rows_are_n=False, epi="add", extra=h2)
    n3 = norm_fwd(h3, vec(norm_mlp_1), h3, name="norm3")
    W["up1"] = weight("up1", n3)
    a1 = mm(n3, W["up1"], name="up1", out_dtype=BF16, b_rows_are_n=True, epi="relu")
    W["down1"] = weight("down1", a1)
    h4 = mm(a1, W["down1"], name="down1", out_dtype=F32, b_rows_are_n=False, epi="add", extra=h3, a_sq=True)

    loss_part, dh4, dh4b, dg_final = loss_bwd(h4, vec(norm_final), target, name="loss_bwd")
    loss = lax.psum(loss_part[0, 0], ("x", "y", "c"))

    sent = {}

    def scatter(k, dw):
        (sent[k],), tok = copy_start([dw.reshape(N_DEV, -1, D)], all_to_all=True, name="a2a_start_" + k)
        return tok

    t = scatter("down1", mm_tn(a1, dh4b, dh4b, name="dw_down1", a_sq=True))
    dup1 = mm(dh4b, W["down1"], name="d_down1", out_dtype=BF16, b_rows_are_n=True, epi="mul2a", extra=a1, dep=t)
    t = scatter("up1", mm_tn(dup1, n3, dup1, name="dw_up1"))
    dn3 = mm(dup1, W["up1"], name="d_up1", out_dtype=BF16, b_rows_are_n=False, dep=t)
    dh3, dh3b, dg_mlp1 = norm_bwd(dn3, h3, vec(norm_mlp_1), dh4, name="norm3_bwd")

    t = scatter("o", mm_tn(o, dh3b, dh3b, name="dw_o"))
    do = mm(dh3b, W["o"], name="d_o", out_dtype=BF16, b_rows_are_n=True, dep=t)
    dq, dkx, dvx, dsinks = attn_bwd(q, kx, kxt, vx, o, do, lse, attn_sinks, pad=pad, name="attn_bwd")
    dqkv = rope_bwd(dq, dkx, dvx, cos, sin, n_q=n_q, n_kv=n_kv, name="rope_bwd")
    t = scatter("qkv", mm_tn(dqkv, n2, dqkv, name="dw_qkv"))
    dn2 = mm(dqkv, W["qkv"], name="d_qkv", out_dtype=BF16, b_rows_are_n=False, dep=t)
    dh2, dh2b, dg_mix1 = norm_bwd(dn2, h2, vec(norm_mix_1), dh3, name="norm2_bwd")

    t = scatter("down0", mm_tn(a0, dh2b, dh2b, name="dw_down0", a_sq=True))
    dup0 = mm(dh2b, W["down0"], name="d_down0", out_dtype=BF16, b_rows_are_n=True, epi="mul2a", extra=a0, dep=t)
    t = scatter("up0", mm_tn(dup0, n1, dup0, name="dw_up0"))
    dn1 = mm(dup0, W["up0"], name="d_up0", out_dtype=BF16, b_rows_are_n=False, dep=t)
    dh1, dh1b, dg_mlp0 = norm_bwd(dn1, h1, vec(norm_mlp_0), dh2, name="norm1_bwd")

    t = scatter("out", mm_tn(gated, dh1b, dh1b, name="dw_out"))
    dgated = mm(dh1b, W["out"], name="d_out", out_dtype=BF16, b_rows_are_n=True, dep=t)
    dbcu, dconv = conv_bwd(bcu, conv_full, dgated, name="conv_bwd")
    t = scatter("in", mm_tn(dbcu, n0, dbcu, name="dw_in"))
    dn0 = mm(dbcu, W["in"], name="d_in", out_dtype=BF16, b_rows_are_n=False, dep=t)
    dhead, dx, dg_mix0 = last_norm_bwd(dn0, h0, vec(norm_mix_0), dh1, name="norm0_bwd")
    grad_x = dx.reshape(1, L, D)

    recv = {}
    for k in ("down1", "up1", "o", "qkv", "down0", "up0", "out", "in"):
        sent[k], recv[k] = copy_wait(sent[k], dx, all_to_all=True, name="a2a_wait_" + k)

    n_sink = attn_sinks.shape[0]
    slab = jnp.concatenate([
        dg_mix0, dg_mlp0, dg_mix1, dg_mlp1, dg_final,
        jnp.pad(dsinks[0:1, :n_sink], ((0, 0), (0, D - n_sink))), jnp.zeros((2, D), F32),
        dconv, dhead[pad:BLOCK]], axis=0)
    slabs = exchange([slab], all_to_all=False, name="comm_small")[0]
    small = sum_parts(slabs, slabs, me_arr, name="sum_small")
    cols = D // N_DEV
    my_cols = lambda a: lax.dynamic_slice_in_dim(a, me * cols, cols, axis=1)

    grads, deltas, new_m, new_v = {}, {}, {}, {}

    def update(key, w, m, v, g, shape):
        s2 = (1, -1) if w.ndim == 1 else w.shape
        if g.ndim == 3:
            fused = adam_parts if g.shape[1:] == w.shape else adam_parts_t
            g_, d_, m_, v_ = fused(w, m, v, g, sent[key_of[key]], me_arr, name="adam_" + key)
        else:
            g_ = g.reshape(s2)
            d_, m_, v_ = adam(w.reshape(s2), m.reshape(s2), v.reshape(s2), g_, name="adam_" + key)
        grads[key], deltas[key], new_m[key], new_v[key] = (t.reshape(shape) for t in (g_, d_, m_, v_))

    update("meta_tokens", meta_tokens, m_meta_tokens, v_meta_tokens, my_cols(small[16:16 + n_meta]), meta_tokens.shape)
    update("norm_mix_0", norm_mix_0, m_norm_mix_0, v_norm_mix_0, small[0], (D,))
    update("conv_w", conv_w, m_conv_w, v_conv_w, my_cols(small[8:8 + conv_w.shape[0]]), conv_w.shape)
    update("norm_mlp_0", norm_mlp_0, m_norm_mlp_0, v_norm_mlp_0, small[1], (D,))
    update("norm_mix_1", norm_mix_1, m_norm_mix_1, v_norm_mix_1, small[2], (D,))
    update("attn_sinks", attn_sinks, m_attn_sinks, v_attn_sinks, small[5, :n_sink], (n_sink,))
    update("norm_mlp_1", norm_mlp_1, m_norm_mlp_1, v_norm_mlp_1, small[3], (D,))
    update("norm_final", norm_final, m_norm_final, v_norm_final, small[4], (D,))
    big = {"in": ("w_in_conv", w_in_conv, m_w_in_conv, v_w_in_conv), "up0": ("w_up_0", w_up_0, m_w_up_0, v_w_up_0),
           "qkv": ("w_qkv", w_qkv, m_w_qkv, v_w_qkv), "up1": ("w_up_1", w_up_1, m_w_up_1, v_w_up_1),
           "out": ("w_out_conv", w_out_conv, m_w_out_conv, v_w_out_conv),
           "down0": ("w_down_0", w_down_0, m_w_down_0, v_w_down_0), "o": ("w_o", w_o, m_w_o, v_w_o),
           "down1": ("w_down_1", w_down_1, m_w_down_1, v_w_down_1)}
    key_of = {big[k][0]: k for k in big}
    for k in col_names:
        key, w, m, v = big[k]
        if w.shape[1] % 128 == 0:
            update(key, w, m, v, recv[k], w.shape)
        else:
            update(key, w, m, v, sum_parts(recv[k], sent[k], me_arr, name="sum_" + k).T, w.shape)
    for k in row_names:
        key, w, m, v = big[k]
        update(key, w, m, v, recv[k], w.shape)

    order = ("meta_tokens", "norm_mix_0", "w_in_conv", "conv_w", "w_out_conv", "norm_mlp_0", "w_up_0", "w_down_0",
             "norm_mix_1", "w_qkv", "attn_sinks", "w_o", "norm_mlp_1", "w_up_1", "w_down_1", "norm_final")
    return (loss, grad_x, *[grads[k] for k in order], *[deltas[k] for k in order],
            *[new_m[k] for k in order], *[new_v[k] for k in order])
```

```python
import jax
import jax.numpy as jnp
from jax import lax
from jax.experimental import pallas as pl
from jax.experimental.pallas import tpu as pltpu

F32 = jnp.float32
BF16 = jnp.bfloat16

HEAD_DIM = 64
GROUP = 8
BLOCK = 128
N_DEV = 8
RMS_EPS = 1e-5
NEG_INF = -1e30
ROPE_THETA = 10000.0
ADAM_LR = 0.001
ADAM_B1 = 0.9
ADAM_B2 = 0.999
ADAM_EPS = 1e-08
ADAM_WD = 0.01
ADAM_STEP = 10
MESH = pl.DeviceIdType.MESH

SUBLANE_F32 = 8
SUBLANE_BF16 = 16
LANES = 128
VMEM_LIMIT = 60 * 1024 * 1024
MM_COLS = 512
LEFT_BLOCK_BYTES = 14 * 1024 * 1024
LONG_CONTRACTION = 6144
LONG_LEFT_BLOCK_BYTES = 11 * 1024 * 1024
MM_TN_ROWS = 1024
MM_TN_TOKENS = 2080
MM_TN_CHUNK = 256
ROW_TILE = 320
NORM_ROWS = 640
ROPE_ROWS = 640
CHANNEL_TILE = 512
ADAM_ROWS = 128
ADAM_T_COLS = 256
CAST_ROWS = 256


def _tile(n, target, mult):
    best = None
    for t in range(mult, min(n, target) + 1, mult):
        if n % t == 0:
            best = t
    return best if best is not None else n


def _params(sem):
    return pltpu.CompilerParams(dimension_semantics=sem, vmem_limit_bytes=VMEM_LIMIT)


ANY_SPEC = pl.BlockSpec(memory_space=pl.ANY)


def mm(a, b, *, name, out_dtype, b_rows_are_n, epi=None, extra=None, a_sq=False, dep=None):
    M, K = a.shape
    N = b.shape[0] if b_rows_are_n else b.shape[1]
    left_bytes = LEFT_BLOCK_BYTES if K <= LONG_CONTRACTION else LONG_LEFT_BLOCK_BYTES
    tm = _tile(M, left_bytes // (K * jnp.dtype(BF16).itemsize), SUBLANE_BF16)
    tn = _tile(N, MM_COLS, LANES)

    def body(*refs):
        a_ref, b_ref, e_ref, o_ref = refs[0], refs[1], refs[2], refs[-1]
        av = a_ref[...]
        if a_sq:
            av = av.astype(F32)
            av = (av * av).astype(BF16)
        dims = (((1,), (1,)), ((), ())) if b_rows_are_n else (((1,), (0,)), ((), ()))
        acc = lax.dot_general(av, b_ref[...], dims, preferred_element_type=F32)
        if epi == "relu":
            acc = jnp.maximum(acc, 0.0)
        elif epi == "mul2a":
            acc = acc * (2.0 * e_ref[...].astype(F32))
        elif epi == "add":
            acc = acc + e_ref[...]
        o_ref[...] = acc.astype(o_ref.dtype)

    b_spec = pl.BlockSpec((tn, K), lambda i, j: (j, 0)) if b_rows_are_n else pl.BlockSpec((K, tn), lambda i, j: (0, j))
    in_specs = [pl.BlockSpec((tm, K), lambda i, j: (i, 0)), b_spec]
    args = [a, b]
    if extra is not None:
        in_specs.append(pl.BlockSpec((tm, tn), lambda i, j: (i, j)))
        args.append(extra)
    if dep is not None:
        in_specs.append(ANY_SPEC)
        args.append(dep)
    return pl.pallas_call(
        body, name=name, grid=(M // tm, N // tn), in_specs=in_specs,
        out_specs=pl.BlockSpec((tm, tn), lambda i, j: (i, j)),
        out_shape=jax.ShapeDtypeStruct((M, N), out_dtype),
        compiler_params=_params(("parallel", "parallel")),
    )(*args)


def _rms_parts(h):
    rstd = lax.rsqrt(jnp.mean(h * h, axis=-1, keepdims=True) + RMS_EPS)
    return h * rstd, rstd


def _norm_bwd_rows(dn, h, g, dres):
    xhat, rstd = _rms_parts(h)
    dxh = dn * g
    dh = dres + rstd * (dxh - xhat * jnp.mean(dxh * xhat, axis=-1, keepdims=True))
    return dh, jnp.sum(dn * xhat, axis=0, keepdims=True)


def mm_tn(a, b, dep, *, name, a_sq=False):
    T, M = a.shape
    N = b.shape[1]
    tm = _tile(M, MM_TN_ROWS, LANES)
    tk = _tile(T, MM_TN_TOKENS, SUBLANE_BF16)
    nk = T // tk
    tc = _tile(tm, MM_TN_CHUNK, LANES)

    def body(a_ref, b_ref, dep_ref, o_ref, acc_ref):
        k = pl.program_id(1)

        def chunks(first, last):
            bv = b_ref[...]
            for r0 in range(0, tm, tc):
                rows = slice(r0, r0 + tc)
                av = a_ref[:, rows]
                if a_sq:
                    av = av.astype(F32)
                    av = (av * av).astype(BF16)
                part = lax.dot_general(av, bv, (((0,), (0,)), ((), ())), preferred_element_type=F32)
                if not first:
                    part = acc_ref[rows, :] + part
                if last:
                    o_ref[rows, :] = part.astype(BF16)
                else:
                    acc_ref[rows, :] = part

        if nk == 1:
            chunks(True, True)
        else:
            pl.when(k == 0)(lambda: chunks(True, False))
            pl.when((k > 0) & (k < nk - 1))(lambda: chunks(False, False))
            pl.when(k == nk - 1)(lambda: chunks(False, True))

    return pl.pallas_call(
        body, name=name, grid=(M // tm, nk),
        in_specs=[pl.BlockSpec((tk, tm), lambda i, k: (k, i)), pl.BlockSpec((tk, N), lambda i, k: (k, 0)), ANY_SPEC],
        out_specs=pl.BlockSpec((tm, N), lambda i, k: (i, 0)),
        out_shape=jax.ShapeDtypeStruct((M, N), BF16),
        scratch_shapes=[pltpu.VMEM((tm, N), F32)],
        compiler_params=_params(("parallel", "arbitrary")),
    )(a, b, dep)


def norm_fwd(h, g, dep, *, name):
    R, D = h.shape
    tr = _tile(R, NORM_ROWS, SUBLANE_BF16)

    def body(h_ref, g_ref, dep_ref, n_ref):
        xhat, _ = _rms_parts(h_ref[...])
        n_ref[...] = (xhat * g_ref[...]).astype(BF16)

    return pl.pallas_call(
        body, name=name, grid=(R // tr,),
        in_specs=[pl.BlockSpec((tr, D), lambda i: (i, 0)), pl.BlockSpec((1, D), lambda i: (0, 0)), ANY_SPEC],
        out_specs=pl.BlockSpec((tr, D), lambda i: (i, 0)),
        out_shape=jax.ShapeDtypeStruct((R, D), BF16),
        compiler_params=_params(("parallel",)),
    )(h, g, dep)


def norm_bwd(dn, h, g, dres, *, name):
    R, D = h.shape
    tr = _tile(R, NORM_ROWS, SUBLANE_BF16)

    def body(dn_ref, h_ref, g_ref, dres_ref, dh_ref, dhb_ref, dg_ref):
        i = pl.program_id(0)
        dh, dg = _norm_bwd_rows(dn_ref[...].astype(F32), h_ref[...], g_ref[...], dres_ref[...])
        dh_ref[...] = dh
        dhb_ref[...] = dh.astype(BF16)

        @pl.when(i == 0)
        def _():
            dg_ref[...] = dg

        @pl.when(i > 0)
        def _():
            dg_ref[...] += dg

    row = pl.BlockSpec((tr, D), lambda i: (i, 0))
    vec = pl.BlockSpec((1, D), lambda i: (0, 0))
    return pl.pallas_call(
        body, name=name, grid=(R // tr,), in_specs=[row, row, vec, row], out_specs=[row, row, vec],
        out_shape=[jax.ShapeDtypeStruct((R, D), F32), jax.ShapeDtypeStruct((R, D), BF16),
                   jax.ShapeDtypeStruct((1, D), F32)],
        compiler_params=_params(("arbitrary",)),
    )(dn, h, g, dres)


def first_norm(head, x, g, dep, *, name):
    L, D = x.shape
    nb = L // BLOCK + 1

    def body(head_ref, x_ref, g_ref, dep_ref, h_ref, n_ref):
        i = pl.program_id(0)
        hv = jnp.where(i == 0, head_ref[...], x_ref[...])
        h_ref[...] = hv
        xhat, _ = _rms_parts(hv)
        n_ref[...] = (xhat * g_ref[...]).astype(BF16)

    blk = pl.BlockSpec((BLOCK, D), lambda i: (i, 0))
    return pl.pallas_call(
        body, name=name, grid=(nb,),
        in_specs=[pl.BlockSpec((BLOCK, D), lambda i: (0, 0)),
                  pl.BlockSpec((BLOCK, D), lambda i: (jnp.maximum(i - 1, 0), 0)),
                  pl.BlockSpec((1, D), lambda i: (0, 0)), ANY_SPEC],
        out_specs=[blk, blk],
        out_shape=[jax.ShapeDtypeStruct((BLOCK + L, D), F32), jax.ShapeDtypeStruct((BLOCK + L, D), BF16)],
        compiler_params=_params(("parallel",)),
    )(head, x, g, dep)


def last_norm_bwd(dn, h, g, dres, *, name):
    R, D = h.shape
    nb = R // BLOCK

    def body(dn_ref, h_ref, g_ref, dres_ref, dhead_ref, dx_ref, dg_ref):
        i = pl.program_id(0)
        dh, dg = _norm_bwd_rows(dn_ref[...].astype(F32), h_ref[...], g_ref[...], dres_ref[...])
        dx_ref[...] = dh

        @pl.when(i == 0)
        def _():
            dhead_ref[...] = dh
            dg_ref[...] = dg

        @pl.when(i > 0)
        def _():
            dg_ref[...] += dg

    blk = pl.BlockSpec((BLOCK, D), lambda i: (i, 0))
    vec = pl.BlockSpec((1, D), lambda i: (0, 0))
    return pl.pallas_call(
        body, name=name, grid=(nb,), in_specs=[blk, blk, vec, blk],
        out_specs=[pl.BlockSpec((BLOCK, D), lambda i: (0, 0)),
                   pl.BlockSpec((BLOCK, D), lambda i: (jnp.maximum(i - 1, 0), 0)), vec],
        out_shape=[jax.ShapeDtypeStruct((BLOCK, D), F32), jax.ShapeDtypeStruct((R - BLOCK, D), F32),
                   jax.ShapeDtypeStruct((1, D), F32)],
        compiler_params=_params(("arbitrary",)),
    )(dn, h, g, dres)


HALO = 16


def conv_fwd(bcu, conv_w, *, name):
    R, D3 = bcu.shape
    D = D3 // 3
    tr = _tile(R, ROW_TILE, SUBLANE_BF16)
    tc = _tile(D, CHANNEL_TILE, LANES)
    nc = D // tc
    hb = tr // HALO

    def body(b_ref, c_ref, u_ref, ch_ref, uh_ref, w_ref, o_ref, vbuf):
        i = pl.program_id(0)
        v = c_ref[...].astype(F32) * u_ref[...].astype(F32)
        vbuf[0:HALO, :] = jnp.where(i > 0, ch_ref[...].astype(F32) * uh_ref[...].astype(F32), 0.0)
        vbuf[HALO:HALO + tr, :] = v
        v1 = vbuf[HALO - 1:HALO - 1 + tr, :]
        v2 = vbuf[HALO - 2:HALO - 2 + tr, :]
        conv = w_ref[0:1, :] * v2 + w_ref[1:2, :] * v1 + w_ref[2:3, :] * v
        o_ref[...] = (b_ref[...].astype(F32) * conv).astype(BF16)

    def prev(i):
        return jnp.maximum(i * hb - 1, 0)

    return pl.pallas_call(
        body, name=name, grid=(R // tr, nc),
        in_specs=[pl.BlockSpec((tr, tc), lambda i, j: (i, j)),
                  pl.BlockSpec((tr, tc), lambda i, j: (i, nc + j)),
                  pl.BlockSpec((tr, tc), lambda i, j: (i, 2 * nc + j)),
                  pl.BlockSpec((HALO, tc), lambda i, j: (prev(i), nc + j)),
                  pl.BlockSpec((HALO, tc), lambda i, j: (prev(i), 2 * nc + j)),
                  pl.BlockSpec((3, tc), lambda i, j: (0, j))],
        out_specs=pl.BlockSpec((tr, tc), lambda i, j: (i, j)),
        out_shape=jax.ShapeDtypeStruct((R, D), BF16),
        scratch_shapes=[pltpu.VMEM((HALO + tr, tc), F32)],
        compiler_params=_params(("parallel", "parallel")),
    )(bcu, bcu, bcu, bcu, bcu, conv_w)


def conv_bwd(bcu, conv_w, dg, *, name):
    R, D3 = bcu.shape
    D = D3 // 3
    tr = _tile(R, ROW_TILE, SUBLANE_BF16)
    tc = _tile(D, CHANNEL_TILE, LANES)
    hb = tr // HALO
    nt = R // tr
    last_h = R // HALO - 1

    def body(x_ref, xp_ref, xn_ref, dg_ref, dgn_ref, w_ref, o_ref, dw_ref, vbuf, dbuf):
        i = pl.program_id(0)
        row8 = lax.broadcasted_iota(jnp.int32, (8, tc), 0)
        for c0 in range(0, D, tc):
            cb, cc, cu = slice(c0, c0 + tc), slice(D + c0, D + c0 + tc), slice(2 * D + c0, 2 * D + c0 + tc)
            w0, w1, w2 = w_ref[0:1, cb], w_ref[1:2, cb], w_ref[2:3, cb]
            b = x_ref[:, cb].astype(F32)
            c = x_ref[:, cc].astype(F32)
            u = x_ref[:, cu].astype(F32)
            v = c * u
            vbuf[0:HALO, :] = jnp.where(i > 0, xp_ref[:, cc].astype(F32) * xp_ref[:, cu].astype(F32), 0.0)
            vbuf[HALO:HALO + tr, :] = v
            v1 = vbuf[HALO - 1:HALO - 1 + tr, :]
            v2 = vbuf[HALO - 2:HALO - 2 + tr, :]
            dgv = dg_ref[:, cb].astype(F32)
            o_ref[:, cb] = (dgv * (w0 * v2 + w1 * v1 + w2 * v)).astype(BF16)
            dconv = dgv * b
            dbuf[0:tr, :] = dconv
            dbuf[tr:tr + HALO, :] = jnp.where(i < nt - 1, dgn_ref[:, cb].astype(F32) * xn_ref[:, cb].astype(F32), 0.0)
            dv = w2 * dconv + w1 * dbuf[1:1 + tr, :] + w0 * dbuf[2:2 + tr, :]
            o_ref[:, cc] = (dv * u).astype(BF16)
            o_ref[:, cu] = (dv * c).astype(BF16)
            dw = jnp.where(row8 == 0, jnp.sum(dconv * v2, axis=0, keepdims=True),
                           jnp.where(row8 == 1, jnp.sum(dconv * v1, axis=0, keepdims=True),
                                     jnp.where(row8 == 2, jnp.sum(dconv * v, axis=0, keepdims=True), 0.0)))

            @pl.when(i == 0)
            def _():
                dw_ref[:, cb] = dw

            @pl.when(i > 0)
            def _():
                dw_ref[:, cb] += dw

    def prev(i):
        return jnp.maximum(i * hb - 1, 0)

    def nxt(i):
        return jnp.minimum((i + 1) * hb, last_h)

    return pl.pallas_call(
        body, name=name, grid=(nt,),
        in_specs=[pl.BlockSpec((tr, D3), lambda i: (i, 0)),
                  pl.BlockSpec((HALO, D3), lambda i: (prev(i), 0)),
                  pl.BlockSpec((HALO, D3), lambda i: (nxt(i), 0)),
                  pl.BlockSpec((tr, D), lambda i: (i, 0)),
                  pl.BlockSpec((HALO, D), lambda i: (nxt(i), 0)),
                  pl.BlockSpec((3, D), lambda i: (0, 0))],
        out_specs=[pl.BlockSpec((tr, D3), lambda i: (i, 0)), pl.BlockSpec((8, D), lambda i: (0, 0))],
        out_shape=[jax.ShapeDtypeStruct((R, D3), BF16), jax.ShapeDtypeStruct((8, D), F32)],
        scratch_shapes=[pltpu.VMEM((HALO + tr, tc), F32), pltpu.VMEM((tr + HALO, tc), F32)],
        compiler_params=_params(("arbitrary",)),
    )(bcu, bcu, bcu, dg, dg, conv_w)


def _swap32(x):
    w = x.shape[1]
    lane = lax.broadcasted_iota(jnp.int32, x.shape, 1)
    return jnp.where((lane & (HEAD_DIM - 1)) < HEAD_DIM // 2, pltpu.roll(x, w - HEAD_DIM // 2, axis=1),
                     pltpu.roll(x, HEAD_DIM // 2, axis=1))


def _rope(x, cos, sin):
    return x * cos + _swap32(x) * sin


def rope_fwd(qkv, cos, sin, *, n_q, n_kv, name):
    R, W = qkv.shape
    qw = n_q * HEAD_DIM
    kw = n_kv * HEAD_DIM
    tr = _tile(R, ROPE_ROWS, LANES)

    def expand(y, ref, ref_t, c):
        lane = lax.broadcasted_iota(jnp.int32, y.shape, 1)
        lo = jnp.where(lane < HEAD_DIM, y, 0.0)
        hi = jnp.where(lane >= HEAD_DIM, y, 0.0)
        tiles = (lo, pltpu.roll(lo, HEAD_DIM, axis=1), pltpu.roll(hi, HEAD_DIM, axis=1), hi)
        for t, tile in enumerate(tiles):
            c0 = 512 * c + 128 * t
            ref[:, c0:c0 + 128] = tile.astype(BF16)
            ref_t[c0:c0 + 128, :] = tile.T.astype(BF16)

    def body(x_ref, c_ref, s_ref, q_ref, k_ref, v_ref, kt_ref, vt_ref):
        cos = c_ref[...]
        sin = s_ref[...]
        for c in range(qw // 128):
            x = x_ref[:, 128 * c:128 * (c + 1)]
            q_ref[:, 128 * c:128 * (c + 1)] = (_rope(x, cos, sin) * (HEAD_DIM ** -0.5)).astype(BF16)
        for c in range(kw // 128):
            expand(_rope(x_ref[:, qw + 128 * c:qw + 128 * (c + 1)], cos, sin), k_ref, kt_ref, c)
            expand(x_ref[:, qw + kw + 128 * c:qw + kw + 128 * (c + 1)], v_ref, vt_ref, c)

    row = lambda w: pl.BlockSpec((tr, w), lambda i: (i, 0))
    col = pl.BlockSpec((4 * kw, tr), lambda i: (0, i))
    return pl.pallas_call(
        body, name=name, grid=(R // tr,),
        in_specs=[row(W), row(128), row(128)],
        out_specs=[row(qw), row(4 * kw), row(4 * kw), col, col],
        out_shape=[jax.ShapeDtypeStruct((R, qw), BF16), jax.ShapeDtypeStruct((R, 4 * kw), BF16),
                   jax.ShapeDtypeStruct((R, 4 * kw), BF16), jax.ShapeDtypeStruct((4 * kw, R), BF16),
                   jax.ShapeDtypeStruct((4 * kw, R), BF16)],
        compiler_params=_params(("parallel",)),
    )(qkv, cos, sin)


def rope_bwd(dq, dkx, dvx, cos, sin, *, n_q, n_kv, name):
    R = dq.shape[0]
    qw = n_q * HEAD_DIM
    kw = n_kv * HEAD_DIM
    W = qw + 2 * kw
    tr = _tile(R, ROW_TILE, SUBLANE_BF16)

    def fold(ref, c):
        lane = lax.broadcasted_iota(jnp.int32, (tr, 128), 1)
        x0 = ref[:, 128 * (2 * c):128 * (2 * c + 1)]
        x1 = ref[:, 128 * (2 * c + 1):128 * (2 * c + 2)]
        f0 = x0 + pltpu.roll(x0, HEAD_DIM, axis=1)
        f1 = x1 + pltpu.roll(x1, HEAD_DIM, axis=1)
        return jnp.where(lane < HEAD_DIM, f0, f1)

    def body(dq_ref, dk_ref, dv_ref, c_ref, s_ref, o_ref):
        cos = c_ref[...]
        nsin = -s_ref[...]
        for c in range(qw // 128):
            y = dq_ref[:, 128 * c:128 * (c + 1)]
            o_ref[:, 128 * c:128 * (c + 1)] = (_rope(y, cos, nsin) * (HEAD_DIM ** -0.5)).astype(BF16)
        for c in range(kw // 128):
            o_ref[:, qw + 128 * c:qw + 128 * (c + 1)] = _rope(fold(dk_ref, c), cos, nsin).astype(BF16)
            o_ref[:, qw + kw + 128 * c:qw + kw + 128 * (c + 1)] = fold(dv_ref, c).astype(BF16)

    row = lambda w: pl.BlockSpec((tr, w), lambda i: (i, 0))
    return pl.pallas_call(
        body, name=name, grid=(R // tr,),
        in_specs=[row(qw), row(2 * kw), row(2 * kw), row(128), row(128)],
        out_specs=row(W),
        out_shape=jax.ShapeDtypeStruct((R, W), BF16),
        compiler_params=_params(("parallel",)),
    )(dq, dkx, dvx, cos, sin)


def _band_bias(n, pad):
    key = lax.broadcasted_iota(jnp.int32, (2 * BLOCK, BLOCK), 0)
    qry = lax.broadcasted_iota(jnp.int32, (2 * BLOCK, BLOCK), 1)
    kmin = jnp.where(n == 0, BLOCK + pad, jnp.where(n == 1, pad, 0))
    allowed = (key > qry) & (key <= qry + BLOCK) & (key >= kmin)
    return jnp.where(allowed, 0.0, NEG_INF).astype(F32)


def _dot_nt(a, b):
    return lax.dot_general(a, b, (((1,), (1,)), ((), ())), preferred_element_type=F32)


def _band(prev_ref, cur_ref, c0):
    return jnp.concatenate([prev_ref[:, c0:c0 + 128], cur_ref[:, c0:c0 + 128]], axis=0)


def _band_t(prev_ref, cur_ref, r0):
    return jnp.concatenate([prev_ref[r0:r0 + 128, :], cur_ref[r0:r0 + 128, :]], axis=1)


def attn_fwd(q, kx, vxt, sinks, *, pad, name):
    R, qw = q.shape
    n_q = qw // HEAD_DIM
    n_kv = kx.shape[1] // 256
    nb = R // BLOCK

    def body(s_ref, q_ref, kc_ref, kp_ref, vc_ref, vp_ref, o_ref, l_ref):
        n = pl.program_id(0)
        bias = _band_bias(n, pad)
        row = lax.broadcasted_iota(jnp.int32, (128, BLOCK), 0)

        def softmax(st, sink):
            st = st + bias
            m = jnp.maximum(jnp.max(st, axis=0, keepdims=True), sink)
            e = jnp.exp(st - m)
            den = jnp.sum(e, axis=0, keepdims=True) + jnp.exp(sink - m)
            return e.astype(BF16), 1.0 / den, m + jnp.log(den)

        for g in range(n_kv):
            k2 = jnp.concatenate([_band(kp_ref, kc_ref, 256 * g), _band(kp_ref, kc_ref, 256 * g + 128)], axis=0)
            v2 = jnp.concatenate([_band_t(vp_ref, vc_ref, 256 * g), _band_t(vp_ref, vc_ref, 256 * g + 128)], axis=1)
            for p in range(GROUP // 2):
                c0 = 128 * (g * (GROUP // 2) + p)
                he = GROUP * g + 2 * p
                q2 = q_ref[:, c0:c0 + 128]
                st = _dot_nt(k2, q2)
                ee, re, le = softmax(st[0:2 * BLOCK], s_ref[he])
                eo, ro, lo = softmax(st[2 * BLOCK:4 * BLOCK], s_ref[he + 1])
                o2t = jnp.dot(v2, jnp.concatenate([ee, eo], axis=0), preferred_element_type=F32)
                o2t = o2t * jnp.where(row < HEAD_DIM, re, ro)
                o_ref[:, c0:c0 + 128] = o2t.T.astype(BF16)
                l_ref[he:he + 1, :] = le
                l_ref[he + 1:he + 2, :] = lo

    cur = lambda w: pl.BlockSpec((BLOCK, w), lambda n: (n, 0))
    prv = lambda w: pl.BlockSpec((BLOCK, w), lambda n: (jnp.maximum(n - 1, 0), 0))
    cur_t = lambda h: pl.BlockSpec((h, BLOCK), lambda n: (0, n))
    prv_t = lambda h: pl.BlockSpec((h, BLOCK), lambda n: (0, jnp.maximum(n - 1, 0)))
    kxw = kx.shape[1]
    return pl.pallas_call(
        body, name=name, grid=(nb,),
        in_specs=[pl.BlockSpec(memory_space=pltpu.SMEM), cur(qw), cur(kxw), prv(kxw), cur_t(kxw), prv_t(kxw)],
        out_specs=[cur(qw), cur_t(n_q)],
        out_shape=[jax.ShapeDtypeStruct((R, qw), BF16), jax.ShapeDtypeStruct((n_q, R), F32)],
        compiler_params=_params(("parallel",)),
    )(sinks, q, kx, kx, vxt, vxt)


def attn_bwd(q, kx, kxt, vx, o, do, lse, sinks, *, pad, name):
    R, qw = q.shape
    n_q = qw // HEAD_DIM
    n_kv = kx.shape[1] // 256
    nb = R // BLOCK
    kw2 = n_kv * 128

    def body(s_ref, q_ref, do_ref, o_ref, l_ref, kc_ref, kp_ref, ktc_ref, ktp_ref, vc_ref, vp_ref,
             dq_ref, dk_ref, dv_ref, ds_ref, cdk, cdv):
        n = pl.program_id(0)

        @pl.when(n == 0)
        def _():
            cdk[...] = jnp.zeros_like(cdk)
            cdv[...] = jnp.zeros_like(cdv)
            ds_ref[...] = jnp.zeros_like(ds_ref)

        @pl.when(n < nb)
        def _():
            bias = _band_bias(n, pad)
            lane2 = lax.broadcasted_iota(jnp.int32, (2 * BLOCK, 128), 1)
            lane1 = lax.broadcasted_iota(jnp.int32, (1, 128), 1)
            sel_r = lax.broadcasted_iota(jnp.int32, (8, 128), 0)
            sel_l = lax.broadcasted_iota(jnp.int32, (8, 128), 1)
            sel = (((sel_r == 0) & (sel_l < HEAD_DIM)) | ((sel_r == 1) & (sel_l >= HEAD_DIM))).astype(BF16)
            dsink = jnp.zeros((1, 128), F32)
            for g in range(n_kv):
                k2 = jnp.concatenate([_band(kp_ref, kc_ref, 256 * g), _band(kp_ref, kc_ref, 256 * g + 128)], axis=0)
                kt2 = jnp.concatenate([_band_t(ktp_ref, ktc_ref, 256 * g), _band_t(ktp_ref, ktc_ref, 256 * g + 128)],
                                      axis=1)
                v2 = jnp.concatenate([_band(vp_ref, vc_ref, 256 * g), _band(vp_ref, vc_ref, 256 * g + 128)], axis=0)
                dk4 = jnp.zeros((4 * BLOCK, 128), F32)
                dv4 = jnp.zeros((4 * BLOCK, 128), F32)
                for p in range(GROUP // 2):
                    c0 = 128 * (g * (GROUP // 2) + p)
                    he = GROUP * g + 2 * p
                    q2 = q_ref[:, c0:c0 + 128]
                    do2 = do_ref[:, c0:c0 + 128]
                    prod = do2.astype(F32) * o_ref[:, c0:c0 + 128].astype(F32)
                    prod_hi = prod.astype(BF16)
                    prod_lo = (prod - prod_hi.astype(F32)).astype(BF16)
                    deltas = _dot_nt(sel, prod_hi) + _dot_nt(sel, prod_lo)
                    st = _dot_nt(k2, q2)
                    dpt = _dot_nt(v2, do2)
                    pts, dsts = [], []
                    for r in range(2):
                        h = he + r
                        rows = slice(2 * BLOCK * r, 2 * BLOCK * (r + 1))
                        lse_h = l_ref[h:h + 1, :]
                        delta = deltas[r:r + 1, :]
                        pt = jnp.exp(st[rows] + bias - lse_h)
                        pts.append(pt.astype(BF16))
                        dsts.append((pt * (dpt[rows] - delta)).astype(BF16))
                        psink = jnp.exp(s_ref[h] - lse_h)
                        tot = jnp.sum(psink * delta, axis=1, keepdims=True)
                        dsink = dsink - jnp.where(lane1 == h, tot, 0.0)
                    dst = jnp.concatenate(dsts, axis=0)
                    dq_ref[:, c0:c0 + 128] = jnp.dot(kt2, dst, preferred_element_type=F32).T
                    dk4 = dk4 + jnp.dot(dst, q2, preferred_element_type=F32)
                    dv4 = dv4 + jnp.dot(jnp.concatenate(pts, axis=0), do2, preferred_element_type=F32)
                gc = pl.ds(128 * g, 128)
                dk2 = jnp.where(lane2 < HEAD_DIM, dk4[0:2 * BLOCK], dk4[2 * BLOCK:4 * BLOCK])
                dv2 = jnp.where(lane2 < HEAD_DIM, dv4[0:2 * BLOCK], dv4[2 * BLOCK:4 * BLOCK])
                dk_ref[:, gc] = cdk[:, gc] + dk2[0:BLOCK]
                dv_ref[:, gc] = cdv[:, gc] + dv2[0:BLOCK]
                cdk[:, gc] = dk2[BLOCK:2 * BLOCK]
                cdv[:, gc] = dv2[BLOCK:2 * BLOCK]
            ds_ref[0:1, :] += dsink

        @pl.when(n == nb)
        def _():
            dk_ref[...] = cdk[...]
            dv_ref[...] = cdv[...]

    cur = lambda w: pl.BlockSpec((BLOCK, w), lambda n: (jnp.minimum(n, nb - 1), 0))
    prv = lambda w: pl.BlockSpec((BLOCK, w), lambda n: (jnp.clip(n - 1, 0, nb - 1), 0))
    cur_t = lambda h: pl.BlockSpec((h, BLOCK), lambda n: (0, jnp.minimum(n, nb - 1)))
    prv_t = lambda h: pl.BlockSpec((h, BLOCK), lambda n: (0, jnp.clip(n - 1, 0, nb - 1)))
    kxw = kx.shape[1]
    return pl.pallas_call(
        body, name=name, grid=(nb + 1,),
        in_specs=[pl.BlockSpec(memory_space=pltpu.SMEM), cur(qw), cur(qw), cur(qw), cur_t(n_q),
                  cur(kxw), prv(kxw), cur_t(kxw), prv_t(kxw), cur(kxw), prv(kxw)],
        out_specs=[cur(qw), prv(kw2), prv(kw2), pl.BlockSpec((8, 128), lambda n: (0, 0))],
        out_shape=[jax.ShapeDtypeStruct((R, qw), F32), jax.ShapeDtypeStruct((R, kw2), F32),
                   jax.ShapeDtypeStruct((R, kw2), F32), jax.ShapeDtypeStruct((8, 128), F32)],
        scratch_shapes=[pltpu.VMEM((BLOCK, kw2), F32), pltpu.VMEM((BLOCK, kw2), F32)],
        compiler_params=_params(("arbitrary",)),
    )(sinks, q, do, o, lse, kx, kx, kxt, kxt, vx, vx)


def loss_bwd(h, g, target, *, name):
    R, D = h.shape
    nb = R // BLOCK

    def body(h_ref, g_ref, t_ref, loss_ref, dh_ref, dhb_ref, dg_ref):
        n = pl.program_id(0)
        xhat, rstd = _rms_parts(h_ref[...])
        gv = g_ref[...]
        diff = jnp.where(n > 0, xhat * gv - t_ref[...], 0.0)
        part = (0.5 / D) * jnp.sum(jnp.sum(diff * diff, axis=1, keepdims=True), axis=0, keepdims=True)
        dout = diff * (1.0 / D)
        dxh = dout * gv
        dh = rstd * (dxh - xhat * jnp.mean(dxh * xhat, axis=-1, keepdims=True))
        dh_ref[...] = dh
        dhb_ref[...] = dh.astype(BF16)
        dg = jnp.sum(dout * xhat, axis=0, keepdims=True)

        @pl.when(n == 0)
        def _():
            loss_ref[...] = jnp.zeros_like(loss_ref) + part
            dg_ref[...] = dg

        @pl.when(n > 0)
        def _():
            loss_ref[...] += part
            dg_ref[...] += dg

    blk = pl.BlockSpec((BLOCK, D), lambda n: (n, 0))
    return pl.pallas_call(
        body, name=name, grid=(nb,),
        in_specs=[blk, pl.BlockSpec((1, D), lambda n: (0, 0)),
                  pl.BlockSpec((BLOCK, D), lambda n: (jnp.maximum(n - 1, 0), 0))],
        out_specs=[pl.BlockSpec((8, 128), lambda n: (0, 0)), blk, blk, pl.BlockSpec((1, D), lambda n: (0, 0))],
        out_shape=[jax.ShapeDtypeStruct((8, 128), F32), jax.ShapeDtypeStruct((R, D), F32),
                   jax.ShapeDtypeStruct((R, D), BF16), jax.ShapeDtypeStruct((1, D), F32)],
        compiler_params=_params(("arbitrary",)),
    )(h, g, target)


def _adam_math(w, g, m, v):
    m = ADAM_B1 * m + (1.0 - ADAM_B1) * g
    v = ADAM_B2 * v + (1.0 - ADAM_B2) * (g * g)
    m_hat = m / (1.0 - ADAM_B1 ** ADAM_STEP)
    v_hat = v / (1.0 - ADAM_B2 ** ADAM_STEP)
    delta = -ADAM_LR * (m_hat / (jnp.sqrt(v_hat) + ADAM_EPS) + ADAM_WD * w)
    return delta, m, v


def adam(w, m, v, g, *, name):
    r, C = w.shape
    tr = _tile(r, ADAM_ROWS, SUBLANE_F32)

    def body(w_ref, m_ref, v_ref, g_ref, d_ref, mo_ref, vo_ref):
        d_ref[...], mo_ref[...], vo_ref[...] = _adam_math(w_ref[...], g_ref[...], m_ref[...], v_ref[...])

    blk = pl.BlockSpec((tr, C), lambda i: (i, 0))
    return pl.pallas_call(
        body, name=name, grid=(r // tr,), in_specs=[blk] * 4, out_specs=[blk] * 3,
        out_shape=[jax.ShapeDtypeStruct((r, C), F32)] * 3,
        compiler_params=_params(("parallel",)),
    )(w, m, v, g)


def _sum_blocks(p_ref, own_ref, me):
    acc = None
    for s in range(N_DEV):
        blk = jnp.where(me == s, own_ref[...], p_ref[s]).astype(F32)
        acc = blk if acc is None else acc + blk
    return acc


def _parts_specs(r, C, tr):
    blk = pl.BlockSpec((tr, C), lambda i, me: (i, 0))
    parts = pl.BlockSpec((N_DEV, tr, C), lambda i, me: (0, i, 0))
    own = pl.BlockSpec((None, tr, C), lambda i, me: (me[0], i, 0))
    return blk, parts, own


def adam_parts(w, m, v, parts, own, me, *, name):
    r, C = w.shape
    tr = _tile(r, ADAM_ROWS, SUBLANE_F32)

    def body(me_ref, w_ref, m_ref, v_ref, p_ref, own_ref, go_ref, d_ref, mo_ref, vo_ref):
        gv = _sum_blocks(p_ref, own_ref, me_ref[0])
        go_ref[...] = gv
        d_ref[...], mo_ref[...], vo_ref[...] = _adam_math(w_ref[...], gv, m_ref[...], v_ref[...])

    blk, pblk, oblk = _parts_specs(r, C, tr)
    return pl.pallas_call(
        body, name=name,
        grid_spec=pltpu.PrefetchScalarGridSpec(num_scalar_prefetch=1, grid=(r // tr,),
                                               in_specs=[blk, blk, blk, pblk, oblk], out_specs=[blk] * 4),
        out_shape=[jax.ShapeDtypeStruct((r, C), F32)] * 4,
        compiler_params=_params(("parallel",)),
    )(me, w, m, v, parts, own)


def adam_parts_t(w, m, v, parts, own, me, *, name):
    K, r = w.shape
    tk = _tile(K, ADAM_T_COLS, LANES)

    def body(me_ref, w_ref, m_ref, v_ref, p_ref, own_ref, go_ref, d_ref, mo_ref, vo_ref):
        gv = _sum_blocks(p_ref, own_ref, me_ref[0]).T
        go_ref[...] = gv
        d_ref[...], mo_ref[...], vo_ref[...] = _adam_math(w_ref[...], gv, m_ref[...], v_ref[...])

    blk = pl.BlockSpec((tk, r), lambda i, me: (i, 0))
    pblk = pl.BlockSpec((N_DEV, r, tk), lambda i, me: (0, 0, i))
    oblk = pl.BlockSpec((None, r, tk), lambda i, me: (me[0], 0, i))
    return pl.pallas_call(
        body, name=name,
        grid_spec=pltpu.PrefetchScalarGridSpec(num_scalar_prefetch=1, grid=(K // tk,),
                                               in_specs=[blk, blk, blk, pblk, oblk], out_specs=[blk] * 4),
        out_shape=[jax.ShapeDtypeStruct((K, r), F32)] * 4,
        compiler_params=_params(("parallel",)),
    )(me, w, m, v, parts, own)


def sum_parts(parts, own, me, *, name):
    _, r, C = parts.shape
    tr = _tile(r, ADAM_ROWS, SUBLANE_F32)

    def body(me_ref, p_ref, own_ref, o_ref):
        o_ref[...] = _sum_blocks(p_ref, own_ref, me_ref[0])

    blk, pblk, oblk = _parts_specs(r, C, tr)
    return pl.pallas_call(
        body, name=name,
        grid_spec=pltpu.PrefetchScalarGridSpec(num_scalar_prefetch=1, grid=(r // tr,),
                                               in_specs=[pblk, oblk], out_specs=blk),
        out_shape=jax.ShapeDtypeStruct((r, C), F32),
        compiler_params=_params(("parallel",)),
    )(me, parts, own)


def cast_place(w, me, dep, *, name):
    r, C = w.shape
    tr = _tile(r, CAST_ROWS, SUBLANE_BF16)

    def body(me_ref, w_ref, dep_ref, s_ref, l_ref):
        v = w_ref[...].astype(BF16)
        s_ref[...] = v
        l_ref[...] = v

    blk = pl.BlockSpec((tr, C), lambda i, me: (i, 0))
    return pl.pallas_call(
        body, name=name,
        grid_spec=pltpu.PrefetchScalarGridSpec(
            num_scalar_prefetch=1, grid=(r // tr,), in_specs=[blk, ANY_SPEC],
            out_specs=[blk, pl.BlockSpec((None, tr, C), lambda i, me: (me[0], i, 0))]),
        out_shape=[jax.ShapeDtypeStruct((r, C), BF16), jax.ShapeDtypeStruct((N_DEV, r, C), BF16)],
        compiler_params=_params(("parallel",)),
    )(me, w, dep)


def _coords():
    return lax.axis_index("x"), lax.axis_index("y"), lax.axis_index("c")


def _peer(m):
    x, y, c = _coords()
    px = 1 - x if m & 4 else x
    py = 1 - y if m & 2 else y
    pc = 1 - c if m & 1 else c
    return (px, py, pc), 4 * px + 2 * py + pc


def exchange(items, *, all_to_all, name):
    n = len(items)
    if all_to_all:
        out_shape = [jax.ShapeDtypeStruct(a.shape, a.dtype) for a in items]
    else:
        out_shape = [jax.ShapeDtypeStruct((N_DEV,) + a.shape, a.dtype) for a in items]

    def body(*refs):
        ins, outs = refs[:n], refs[n:2 * n]
        send_sems, recv_sems, local_sems = refs[2 * n:]
        x, y, c = _coords()
        me = 4 * x + 2 * y + c

        def src(i, idx):
            return ins[i].at[idx] if all_to_all else ins[i]

        local = [pltpu.make_async_copy(src(i, me), outs[i].at[me], local_sems.at[i]) for i in range(n)]
        for cp in local:
            cp.start()
        sends = []
        for m in range(1, N_DEV):
            peer, pidx = _peer(m)
            for i in range(n):
                k = i * (N_DEV - 1) + m - 1
                cp = pltpu.make_async_remote_copy(src_ref=src(i, pidx), dst_ref=outs[i].at[me],
                                                  send_sem=send_sems.at[k], recv_sem=recv_sems.at[k],
                                                  device_id=peer, device_id_type=MESH)
                cp.start()
                sends.append(cp)
        for m in range(1, N_DEV):
            peer, pidx = _peer(m)
            for i in range(n):
                k = i * (N_DEV - 1) + m - 1
                pltpu.make_async_remote_copy(src_ref=src(i, pidx), dst_ref=outs[i].at[pidx],
                                             send_sem=send_sems.at[k], recv_sem=recv_sems.at[k],
                                             device_id=peer, device_id_type=MESH).wait_recv()
        for cp in sends:
            cp.wait_send()
        for cp in local:
            cp.wait()

    any_spec = pl.BlockSpec(memory_space=pl.ANY)
    return pl.pallas_call(
        body, name=name, in_specs=[any_spec] * n, out_specs=[any_spec] * n, out_shape=out_shape,
        scratch_shapes=[pltpu.SemaphoreType.DMA((n * (N_DEV - 1),)), pltpu.SemaphoreType.DMA((n * (N_DEV - 1),)),
                        pltpu.SemaphoreType.DMA((n,))],
    )(*items)


HBM_SPEC = pl.BlockSpec(memory_space=pltpu.HBM)
SEM_SPEC = pl.BlockSpec(memory_space=pltpu.SEMAPHORE)
SPLIT_PARAMS = pltpu.CompilerParams(has_side_effects=pltpu.SideEffectType.DATAFLOW_SIDE_EFFECTING)


def _split_copies(src_ref, land_ref, send_sems, recv_sems, all_to_all):
    x, y, c = _coords()
    me = 4 * x + 2 * y + c
    copies = []
    for m in range(1, N_DEV):
        peer, pidx = _peer(m)
        copies.append(pltpu.make_async_remote_copy(
            src_ref=src_ref.at[pidx] if all_to_all else src_ref, dst_ref=land_ref.at[me],
            send_sem=send_sems.at[m - 1], recv_sem=recv_sems.at[m - 1], device_id=peer, device_id_type=MESH))
    return copies


def copy_start(items, lands=None, *, all_to_all, name):
    n = len(items)
    if lands is None:
        lands = [lax.empty(a.shape if all_to_all else (N_DEV,) + a.shape, a.dtype) for a in items]

    def body(*refs):
        srcs, lnds, outs = refs[:n], refs[n:2 * n], refs[2 * n:]
        for i in range(n):
            for cp in _split_copies(srcs[i], lnds[i], outs[4 * i], outs[4 * i + 1], all_to_all):
                cp.start()
        outs[4 * n][...] = jnp.zeros((8, 128), F32)

    out_shape, out_specs, aliases = [], [], {}
    for i, (a, l) in enumerate(zip(items, lands)):
        out_shape += [pltpu.SemaphoreType.DMA((N_DEV - 1,)), pltpu.SemaphoreType.DMA((N_DEV - 1,)),
                      pltpu.HBM(a.shape, a.dtype), pltpu.HBM(l.shape, l.dtype)]
        out_specs += [SEM_SPEC, SEM_SPEC, HBM_SPEC, HBM_SPEC]
        aliases[i] = 4 * i + 2
        aliases[n + i] = 4 * i + 3
    out_shape.append(jax.ShapeDtypeStruct((8, 128), F32))
    out_specs.append(pl.BlockSpec(memory_space=pltpu.VMEM))
    hbm = lambda a: pltpu.with_memory_space_constraint(a, pltpu.HBM)
    res = pl.pallas_call(
        body, name=name, in_specs=[HBM_SPEC] * (2 * n), out_specs=out_specs, out_shape=out_shape,
        input_output_aliases=aliases, compiler_params=SPLIT_PARAMS,
    )(*[hbm(a) for a in items], *[hbm(l) for l in lands])
    return [tuple(res[4 * i:4 * i + 4]) for i in range(n)], res[4 * n]


def copy_wait(handle, after, *, all_to_all, name):
    send_sems, recv_sems, src, land = handle

    def body(src_ref, land_ref, send_ref, recv_ref, after_ref, src_out, got_ref):
        for cp in _split_copies(src_ref, land_ref, send_ref, recv_ref, all_to_all):
            cp.wait_send()
            cp.wait_recv()

    return pl.pallas_call(
        body, name=name, in_specs=[HBM_SPEC, HBM_SPEC, SEM_SPEC, SEM_SPEC, ANY_SPEC],
        out_specs=[HBM_SPEC, HBM_SPEC],
        out_shape=[pltpu.HBM(src.shape, src.dtype), pltpu.HBM(land.shape, land.dtype)],
        input_output_aliases={0: 0, 1: 1}, compiler_params=SPLIT_PARAMS,
    )(src, land, send_sems, recv_sems, after)


OTHER_CHIPS = (4, 2, 6)
SIBLING = 1


def _relay_copies(src_ref, land_ref, send_sems, recv_sems):
    x, y, c = _coords()
    me = 4 * x + 2 * y + c
    return [pltpu.make_async_remote_copy(src_ref=src_ref, dst_ref=land_ref.at[me], send_sem=send_sems.at[k],
                                         recv_sem=recv_sems.at[k], device_id=_peer(m)[0], device_id_type=MESH)
            for k, m in enumerate((SIBLING,) + OTHER_CHIPS)]


def _forward_copies(land_ref, send_sems, recv_sems):
    copies = []
    for k, m in enumerate(OTHER_CHIPS):
        pidx = _peer(m)[1]
        copies.append(pltpu.make_async_remote_copy(
            src_ref=land_ref.at[pidx], dst_ref=land_ref.at[pidx], send_sem=send_sems.at[k], recv_sem=recv_sems.at[k],
            device_id=_peer(SIBLING)[0], device_id_type=MESH))
    return copies


def relay_start(item, land, *, name):
    def body(src_ref, land_ref, send_sems, recv_sems, src_out, land_out, token):
        for cp in _relay_copies(src_ref, land_ref, send_sems, recv_sems):
            cp.start()
        token[...] = jnp.zeros((8, 128), F32)

    n = 1 + len(OTHER_CHIPS)
    hbm = lambda a: pltpu.with_memory_space_constraint(a, pltpu.HBM)
    res = pl.pallas_call(
        body, name=name, in_specs=[HBM_SPEC, HBM_SPEC],
        out_specs=[SEM_SPEC, SEM_SPEC, HBM_SPEC, HBM_SPEC, pl.BlockSpec(memory_space=pltpu.VMEM)],
        out_shape=[pltpu.SemaphoreType.DMA((n,)), pltpu.SemaphoreType.DMA((n,)), pltpu.HBM(item.shape, item.dtype),
                   pltpu.HBM(land.shape, land.dtype), jax.ShapeDtypeStruct((8, 128), F32)],
        input_output_aliases={0: 2, 1: 3}, compiler_params=SPLIT_PARAMS,
    )(hbm(item), hbm(land))
    return tuple(res[:4]), res[4]


def relay_forward(handle, after, *, name):
    send_sems, recv_sems, src, land = handle

    def body(src_ref, land_ref, send_ref, recv_ref, after_ref, src_out, land_out, send2, recv2):
        for cp in _relay_copies(src_ref, land_ref, send_ref, recv_ref):
            cp.wait_send()
            cp.wait_recv()
        for cp in _forward_copies(land_ref, send2, recv2):
            cp.start()

    n = len(OTHER_CHIPS)
    res = pl.pallas_call(
        body, name=name, in_specs=[HBM_SPEC, HBM_SPEC, SEM_SPEC, SEM_SPEC, ANY_SPEC],
        out_specs=[HBM_SPEC, HBM_SPEC, SEM_SPEC, SEM_SPEC],
        out_shape=[pltpu.HBM(src.shape, src.dtype), pltpu.HBM(land.shape, land.dtype),
                   pltpu.SemaphoreType.DMA((n,)), pltpu.SemaphoreType.DMA((n,))],
        input_output_aliases={0: 0, 1: 1}, compiler_params=SPLIT_PARAMS,
    )(src, land, send_sems, recv_sems, after)
    return res[2], res[3], res[0], res[1]


def relay_wait(handle, after, *, name):
    send_sems, recv_sems, src, land = handle

    def body(src_ref, land_ref, send_ref, recv_ref, after_ref, src_out, land_out):
        for cp in _forward_copies(land_ref, send_ref, recv_ref):
            cp.wait_send()
            cp.wait_recv()

    return pl.pallas_call(
        body, name=name, in_specs=[HBM_SPEC, HBM_SPEC, SEM_SPEC, SEM_SPEC, ANY_SPEC],
        out_specs=[HBM_SPEC, HBM_SPEC],
        out_shape=[pltpu.HBM(src.shape, src.dtype), pltpu.HBM(land.shape, land.dtype)],
        input_output_aliases={0: 0, 1: 1}, compiler_params=SPLIT_PARAMS,
    )(src, land, send_sems, recv_sems, after)[1]


def kernel(x, meta_tokens, norm_mix_0, w_in_conv, conv_w, w_out_conv, norm_mlp_0, w_up_0, w_down_0, norm_mix_1, w_qkv, attn_sinks, w_o, norm_mlp_1, w_up_1, w_down_1, norm_final, loss_target, m_meta_tokens, m_norm_mix_0, m_w_in_conv, m_conv_w, m_w_out_conv, m_norm_mlp_0, m_w_up_0, m_w_down_0, m_norm_mix_1, m_w_qkv, m_attn_sinks, m_w_o, m_norm_mlp_1, m_w_up_1, m_w_down_1, m_norm_final, v_meta_tokens, v_norm_mix_0, v_w_in_conv, v_conv_w, v_w_out_conv, v_norm_mlp_0, v_w_up_0, v_w_down_0, v_norm_mix_1, v_w_qkv, v_attn_sinks, v_w_o, v_norm_mlp_1, v_w_up_1, v_w_down_1, v_norm_final):
    L, D = x.shape[1], x.shape[2]
    n_meta = meta_tokens.shape[0]
    pad = BLOCK - n_meta
    R = BLOCK + L
    n_q = D // HEAD_DIM
    n_kv = n_q // GROUP
    assert n_kv % 2 == 0 and L % BLOCK == 0 and D % 128 == 0
    x = x.reshape(L, D)
    target = loss_target.reshape(L, D)
    x_id, y_id, c_id = _coords()
    me = 4 * x_id + 2 * y_id + c_id

    col_names = ("in", "up0", "qkv", "up1")
    col_w = dict(zip(col_names, (w_in_conv, w_up_0, w_qkv, w_up_1)))
    row_names = ("out", "down0", "o", "down1")
    row_w = dict(zip(row_names, (w_out_conv, w_down_0, w_o, w_down_1)))
    me_arr = jnp.reshape(me, (1,)).astype(jnp.int32)
    natural = {k: col_w[k].T for k in col_names}
    natural.update(row_w)
    use_order = ("in", "out", "up0", "down0", "qkv", "o", "up1", "down1")
    first = cast_place(natural[use_order[0]], me_arr, me_arr, name="cast_" + use_order[0])
    first_handle, token = relay_start(first[0], first[1], name="relay_start_" + use_order[0])
    placed = [cast_place(natural[k], me_arr, token, name="cast_" + k) for k in use_order[1:]]
    handles, token = copy_start([p[0] for p in placed], [p[1] for p in placed], all_to_all=False, name="gather_start")
    handles = dict(zip(use_order[1:], handles))

    def weight(k, after):
        return copy_wait(handles[k], after, all_to_all=False, name="gather_wait_" + k)[1].reshape(-1, D)

    small_in = exchange([meta_tokens, conv_w], all_to_all=False, name="comm_gather")
    meta_full = jnp.transpose(small_in[0], (1, 0, 2)).reshape(n_meta, D)
    conv_full = jnp.transpose(small_in[1], (1, 0, 2)).reshape(conv_w.shape[0], D)

    vec = lambda a: a.reshape(1, D)
    pos = jnp.arange(R, dtype=F32) - pad
    inv = ROPE_THETA ** (-jnp.arange(0, HEAD_DIM, 2, dtype=F32) / HEAD_DIM)
    ang = pos[:, None] * inv[None, :]
    cos32, sin32 = jnp.cos(ang), jnp.sin(ang)
    cos = jnp.concatenate([cos32] * 4, axis=1)
    sin = jnp.concatenate([-sin32, sin32, -sin32, sin32], axis=1)

    W = {}
    head = jnp.concatenate([jnp.zeros((pad, D), F32), meta_full], axis=0)
    h0, n0 = first_norm(head, x, vec(norm_mix_0), token, name="norm0")
    W["in"] = relay_wait(relay_forward(first_handle, n0, name="relay_forward_in"), n0,
                         name="relay_wait_in").reshape(-1, D)
    bcu = mm(n0, W["in"], name="in_proj", out_dtype=BF16, b_rows_are_n=True)
    gated = conv_fwd(bcu, conv_full, name="conv_fwd")
    W["out"] = weight("out", gated)
    h1 = mm(gated, W["out"], name="out_proj", out_dtype=F32, b_rows_are_n=False, epi="add", extra=h0)
    n1 = norm_fwd(h1, vec(norm_mlp_0), h1, name="norm1")
    W["up0"] = weight("up0", n1)
    a0 = mm(n1, W["up0"], name="up0", out_dtype=BF16, b_rows_are_n=True, epi="relu")
    W["down0"] = weight("down0", a0)
    h2 = mm(a0, W["down0"], name="down0", out_dtype=F32, b_rows_are_n=False, epi="add", extra=h1, a_sq=True)
    n2 = norm_fwd(h2, vec(norm_mix_1), h2, name="norm2")
    W["qkv"] = weight("qkv", n2)
    qkv = mm(n2, W["qkv"], name="qkv_proj", out_dtype=F32, b_rows_are_n=True)
    q, kx, vx, kxt, vxt = rope_fwd(qkv, cos, sin, n_q=n_q, n_kv=n_kv, name="rope_fwd")
    o, lse = attn_fwd(q, kx, vxt, attn_sinks, pad=pad, name="attn_fwd")
    W["o"] = weight("o", o)
    h3 = mm(o, W["o"], name="o_proj", out_dtype=F32, b_rows_are_n=False, epi="add", extra=h2)
    n3 = norm_fwd(h3, vec(norm_mlp_1), h3, name="norm3")
    W["up1"] = weight("up1", n3)
    a1 = mm(n3, W["up1"], name="up1", out_dtype=BF16, b_rows_are_n=True, epi="relu")
    W["down1"] = weight("down1", a1)
    h4 = mm(a1, W["down1"], name="down1", out_dtype=F32, b_rows_are_n=False, epi="add", extra=h3, a_sq=True)

    loss_part, dh4, dh4b, dg_final = loss_bwd(h4, vec(norm_final), target, name="loss_bwd")
    loss = lax.psum(loss_part[0, 0], ("x", "y", "c"))

    sent = {}

    def scatter(k, dw):
        (sent[k],), tok = copy_start([dw.reshape(N_DEV, -1, D)], all_to_all=True, name="a2a_start_" + k)
        return tok

    t = scatter("down1", mm_tn(a1, dh4b, dh4b, name="dw_down1", a_sq=True))
    dup1 = mm(dh4b, W["down1"], name="d_down1", out_dtype=BF16, b_rows_are_n=True, epi="mul2a", extra=a1, dep=t)
    t = scatter("up1", mm_tn(dup1, n3, dup1, name="dw_up1"))
    dn3 = mm(dup1, W["up1"], name="d_up1", out_dtype=BF16, b_rows_are_n=False, dep=t)
    dh3, dh3b, dg_mlp1 = norm_bwd(dn3, h3, vec(norm_mlp_1), dh4, name="norm3_bwd")

    t = scatter("o", mm_tn(o, dh3b, dh3b, name="dw_o"))
    do = mm(dh3b, W["o"], name="d_o", out_dtype=BF16, b_rows_are_n=True, dep=t)
    dq, dkx, dvx, dsinks = attn_bwd(q, kx, kxt, vx, o, do, lse, attn_sinks, pad=pad, name="attn_bwd")
    dqkv = rope_bwd(dq, dkx, dvx, cos, sin, n_q=n_q, n_kv=n_kv, name="rope_bwd")
    t = scatter("qkv", mm_tn(dqkv, n2, dqkv, name="dw_qkv"))
    dn2 = mm(dqkv, W["qkv"], name="d_qkv", out_dtype=BF16, b_rows_are_n=False, dep=t)
    dh2, dh2b, dg_mix1 = norm_bwd(dn2, h2, vec(norm_mix_1), dh3, name="norm2_bwd")

    t = scatter("down0", mm_tn(a0, dh2b, dh2b, name="dw_down0", a_sq=True))
    dup0 = mm(dh2b, W["down0"], name="d_down0", out_dtype=BF16, b_rows_are_n=True, epi="mul2a", extra=a0, dep=t)
    t = scatter("up0", mm_tn(dup0, n1, dup0, name="dw_up0"))
    dn1 = mm(dup0, W["up0"], name="d_up0", out_dtype=BF16, b_rows_are_n=False, dep=t)
    dh1, dh1b, dg_mlp0 = norm_bwd(dn1, h1, vec(norm_mlp_0), dh2, name="norm1_bwd")

    t = scatter("out", mm_tn(gated, dh1b, dh1b, name="dw_out"))
    dgated = mm(dh1b, W["out"], name="d_out", out_dtype=BF16, b_rows_are_n=True, dep=t)
    dbcu, dconv = conv_bwd(bcu, conv_full, dgated, name="conv_bwd")
    t = scatter("in", mm_tn(dbcu, n0, dbcu, name="dw_in"))
    dn0 = mm(dbcu, W["in"], name="d_in", out_dtype=BF16, b_rows_are_n=False, dep=t)
    dhead, dx, dg_mix0 = last_norm_bwd(dn0, h0, vec(norm_mix_0), dh1, name="norm0_bwd")
    grad_x = dx.reshape(1, L, D)

    recv = {}
    for k in ("down1", "up1", "o", "qkv", "down0", "up0", "out", "in"):
        sent[k], recv[k] = copy_wait(sent[k], dx, all_to_all=True, name="a2a_wait_" + k)

    n_sink = attn_sinks.shape[0]
    slab = jnp.concatenate([
        dg_mix0, dg_mlp0, dg_mix1, dg_mlp1, dg_final,
        jnp.pad(dsinks[0:1, :n_sink], ((0, 0), (0, D - n_sink))), jnp.zeros((2, D), F32),
        dconv, dhead[pad:BLOCK]], axis=0)
    slabs = exchange([slab], all_to_all=False, name="comm_small")[0]
    small = sum_parts(slabs, slabs, me_arr, name="sum_small")
    cols = D // N_DEV
    my_cols = lambda a: lax.dynamic_slice_in_dim(a, me * cols, cols, axis=1)

    grads, deltas, new_m, new_v = {}, {}, {}, {}

    def update(key, w, m, v, g, shape):
        s2 = (1, -1) if w.ndim == 1 else w.shape
        if g.ndim == 3:
            fused = adam_parts if g.shape[1:] == w.shape else adam_parts_t
            g_, d_, m_, v_ = fused(w, m, v, g, sent[key_of[key]], me_arr, name="adam_" + key)
        else:
            g_ = g.reshape(s2)
            d_, m_, v_ = adam(w.reshape(s2), m.reshape(s2), v.reshape(s2), g_, name="adam_" + key)
        grads[key], deltas[key], new_m[key], new_v[key] = (t.reshape(shape) for t in (g_, d_, m_, v_))

    update("meta_tokens", meta_tokens, m_meta_tokens, v_meta_tokens, my_cols(small[16:16 + n_meta]), meta_tokens.shape)
    update("norm_mix_0", norm_mix_0, m_norm_mix_0, v_norm_mix_0, small[0], (D,))
    update("conv_w", conv_w, m_conv_w, v_conv_w, my_cols(small[8:8 + conv_w.shape[0]]), conv_w.shape)
    update("norm_mlp_0", norm_mlp_0, m_norm_mlp_0, v_norm_mlp_0, small[1], (D,))
    update("norm_mix_1", norm_mix_1, m_norm_mix_1, v_norm_mix_1, small[2], (D,))
    update("attn_sinks", attn_sinks, m_attn_sinks, v_attn_sinks, small[5, :n_sink], (n_sink,))
    update("norm_mlp_1", norm_mlp_1, m_norm_mlp_1, v_norm_mlp_1, small[3], (D,))
    update("norm_final", norm_final, m_norm_final, v_norm_final, small[4], (D,))
    big = {"in": ("w_in_conv", w_in_conv, m_w_in_conv, v_w_in_conv), "up0": ("w_up_0", w_up_0, m_w_up_0, v_w_up_0),
           "qkv": ("w_qkv", w_qkv, m_w_qkv, v_w_qkv), "up1": ("w_up_1", w_up_1, m_w_up_1, v_w_up_1),
           "out": ("w_out_conv", w_out_conv, m_w_out_conv, v_w_out_conv),
           "down0": ("w_down_0", w_down_0, m_w_down_0, v_w_down_0), "o": ("w_o", w_o, m_w_o, v_w_o),
           "down1": ("w_down_1", w_down_1, m_w_down_1, v_w_down_1)}
    key_of = {big[k][0]: k for k in big}
    for k in col_names:
        key, w, m, v = big[k]
        if w.shape[1] % 128 == 0:
            update(key, w, m, v, recv[k], w.shape)
        else:
            update(key, w, m, v, sum_parts(recv[k], sent[k], me_arr, name="sum_" + k).T, w.shape)
    for k in row_names:
        key, w, m, v = big[k]
        update(key, w, m, v, recv[k], w.shape)

    order = ("meta_tokens", "norm_mix_0", "w_in_conv", "conv_w", "w_out_conv", "norm_mlp_0", "w_up_0", "w_down_0",
             "norm_mix_1", "w_qkv", "attn_sinks", "w_o", "norm_mlp_1", "w_up_1", "w_down_1", "norm_final")
    return (loss, grad_x, *[grads[k] for k in order], *[deltas[k] for k in order],
            *[new_m[k] for k in order], *[new_v[k] for k in order])
```

```python
import jax
import jax.numpy as jnp
from jax import lax
from jax.experimental import pallas as pl
from jax.experimental.pallas import tpu as pltpu

F32 = jnp.float32
BF16 = jnp.bfloat16

HEAD_DIM = 64
GROUP = 8
BLOCK = 128
N_DEV = 8
RMS_EPS = 1e-5
NEG_INF = -1e30
ROPE_THETA = 10000.0
ADAM_LR = 0.001
ADAM_B1 = 0.9
ADAM_B2 = 0.999
ADAM_EPS = 1e-08
ADAM_WD = 0.01
ADAM_STEP = 10
MESH = pl.DeviceIdType.MESH

SUBLANE_F32 = 8
SUBLANE_BF16 = 16
LANES = 128
VMEM_LIMIT = 60 * 1024 * 1024
MM_COLS = 512
LEFT_BLOCK_BYTES = 14 * 1024 * 1024
LONG_CONTRACTION = 6144
LONG_LEFT_BLOCK_BYTES = 11 * 1024 * 1024
MM_TN_ROWS = 1024
MM_TN_TOKENS = 2080
MM_TN_CHUNK = 512
ROW_TILE = 320
NORM_ROWS = 640
ROPE_ROWS = 640
CHANNEL_TILE = 512
ADAM_ROWS = 128
ADAM_T_COLS = 256
CAST_ROWS = 256


def _tile(n, target, mult):
    best = None
    for t in range(mult, min(n, target) + 1, mult):
        if n % t == 0:
            best = t
    return best if best is not None else n


def _params(sem):
    return pltpu.CompilerParams(dimension_semantics=sem, vmem_limit_bytes=VMEM_LIMIT)


ANY_SPEC = pl.BlockSpec(memory_space=pl.ANY)


def mm(a, b, *, name, out_dtype, b_rows_are_n, epi=None, extra=None, a_sq=False, dep=None):
    M, K = a.shape
    N = b.shape[0] if b_rows_are_n else b.shape[1]
    left_bytes = LEFT_BLOCK_BYTES if K <= LONG_CONTRACTION else LONG_LEFT_BLOCK_BYTES
    tm = _tile(M, left_bytes // (K * jnp.dtype(BF16).itemsize), SUBLANE_BF16)
    tn = _tile(N, MM_COLS, LANES)

    def body(*refs):
        a_ref, b_ref, e_ref, o_ref = refs[0], refs[1], refs[2], refs[-1]
        av = a_ref[...]
        if a_sq:
            av = av.astype(F32)
            av = (av * av).astype(BF16)
        dims = (((1,), (1,)), ((), ())) if b_rows_are_n else (((1,), (0,)), ((), ()))
        acc = lax.dot_general(av, b_ref[...], dims, preferred_element_type=F32)
        if epi == "relu":
            acc = jnp.maximum(acc, 0.0)
        elif epi == "mul2a":
            acc = acc * (2.0 * e_ref[...].astype(F32))
        elif epi == "add":
            acc = acc + e_ref[...]
        o_ref[...] = acc.astype(o_ref.dtype)

    b_spec = pl.BlockSpec((tn, K), lambda i, j: (j, 0)) if b_rows_are_n else pl.BlockSpec((K, tn), lambda i, j: (0, j))
    in_specs = [pl.BlockSpec((tm, K), lambda i, j: (i, 0)), b_spec]
    args = [a, b]
    if extra is not None:
        in_specs.append(pl.BlockSpec((tm, tn), lambda i, j: (i, j)))
        args.append(extra)
    if dep is not None:
        in_specs.append(ANY_SPEC)
        args.append(dep)
    return pl.pallas_call(
        body, name=name, grid=(M // tm, N // tn), in_specs=in_specs,
        out_specs=pl.BlockSpec((tm, tn), lambda i, j: (i, j)),
        out_shape=jax.ShapeDtypeStruct((M, N), out_dtype),
        compiler_params=_params(("parallel", "parallel")),
    )(*args)


def _rms_parts(h):
    rstd = lax.rsqrt(jnp.mean(h * h, axis=-1, keepdims=True) + RMS_EPS)
    return h * rstd, rstd


def _norm_bwd_rows(dn, h, g, dres):
    xhat, rstd = _rms_parts(h)
    dxh = dn * g
    dh = dres + rstd * (dxh - xhat * jnp.mean(dxh * xhat, axis=-1, keepdims=True))
    return dh, jnp.sum(dn * xhat, axis=0, keepdims=True)


def mm_tn(a, b, dep, *, name, a_sq=False):
    T, M = a.shape
    N = b.shape[1]
    tm = _tile(M, MM_TN_ROWS, LANES)
    tk = _tile(T, MM_TN_TOKENS, SUBLANE_BF16)
    nk = T // tk
    tc = _tile(tm, MM_TN_CHUNK, LANES)

    def body(a_ref, b_ref, dep_ref, o_ref, acc_ref):
        k = pl.program_id(1)

        def chunks(first, last):
            bv = b_ref[...]
            for r0 in range(0, tm, tc):
                rows = slice(r0, r0 + tc)
                av = a_ref[:, rows]
                if a_sq:
                    av = av.astype(F32)
                    av = (av * av).astype(BF16)
                part = lax.dot_general(av, bv, (((0,), (0,)), ((), ())), preferred_element_type=F32)
                if not first:
                    part = acc_ref[rows, :] + part
                if last:
                    o_ref[rows, :] = part.astype(BF16)
                else:
                    acc_ref[rows, :] = part

        if nk == 1:
            chunks(True, True)
        else:
            pl.when(k == 0)(lambda: chunks(True, False))
            pl.when((k > 0) & (k < nk - 1))(lambda: chunks(False, False))
            pl.when(k == nk - 1)(lambda: chunks(False, True))

    return pl.pallas_call(
        body, name=name, grid=(M // tm, nk),
        in_specs=[pl.BlockSpec((tk, tm), lambda i, k: (k, i)), pl.BlockSpec((tk, N), lambda i, k: (k, 0)), ANY_SPEC],
        out_specs=pl.BlockSpec((tm, N), lambda i, k: (i, 0)),
        out_shape=jax.ShapeDtypeStruct((M, N), BF16),
        scratch_shapes=[pltpu.VMEM((tm, N), F32)],
        compiler_params=_params(("parallel", "arbitrary")),
    )(a, b, dep)


def norm_fwd(h, g, dep, *, name):
    R, D = h.shape
    tr = _tile(R, NORM_ROWS, SUBLANE_BF16)

    def body(h_ref, g_ref, dep_ref, n_ref):
        xhat, _ = _rms_parts(h_ref[...])
        n_ref[...] = (xhat * g_ref[...]).astype(BF16)

    return pl.pallas_call(
        body, name=name, grid=(R // tr,),
        in_specs=[pl.BlockSpec((tr, D), lambda i: (i, 0)), pl.BlockSpec((1, D), lambda i: (0, 0)), ANY_SPEC],
        out_specs=pl.BlockSpec((tr, D), lambda i: (i, 0)),
        out_shape=jax.ShapeDtypeStruct((R, D), BF16),
        compiler_params=_params(("parallel",)),
    )(h, g, dep)


def norm_bwd(dn, h, g, dres, *, name):
    R, D = h.shape
    tr = _tile(R, NORM_ROWS, SUBLANE_BF16)

    def body(dn_ref, h_ref, g_ref, dres_ref, dh_ref, dhb_ref, dg_ref):
        i = pl.program_id(0)
        dh, dg = _norm_bwd_rows(dn_ref[...].astype(F32), h_ref[...], g_ref[...], dres_ref[...])
        dh_ref[...] = dh
        dhb_ref[...] = dh.astype(BF16)

        @pl.when(i == 0)
        def _():
            dg_ref[...] = dg

        @pl.when(i > 0)
        def _():
            dg_ref[...] += dg

    row = pl.BlockSpec((tr, D), lambda i: (i, 0))
    vec = pl.BlockSpec((1, D), lambda i: (0, 0))
    return pl.pallas_call(
        body, name=name, grid=(R // tr,), in_specs=[row, row, vec, row], out_specs=[row, row, vec],
        out_shape=[jax.ShapeDtypeStruct((R, D), F32), jax.ShapeDtypeStruct((R, D), BF16),
                   jax.ShapeDtypeStruct((1, D), F32)],
        compiler_params=_params(("arbitrary",)),
    )(dn, h, g, dres)


def first_norm(head, x, g, dep, *, name):
    L, D = x.shape
    nb = L // BLOCK + 1

    def body(head_ref, x_ref, g_ref, dep_ref, h_ref, n_ref):
        i = pl.program_id(0)
        hv = jnp.where(i == 0, head_ref[...], x_ref[...])
        h_ref[...] = hv
        xhat, _ = _rms_parts(hv)
        n_ref[...] = (xhat * g_ref[...]).astype(BF16)

    blk = pl.BlockSpec((BLOCK, D), lambda i: (i, 0))
    return pl.pallas_call(
        body, name=name, grid=(nb,),
        in_specs=[pl.BlockSpec((BLOCK, D), lambda i: (0, 0)),
                  pl.BlockSpec((BLOCK, D), lambda i: (jnp.maximum(i - 1, 0), 0)),
                  pl.BlockSpec((1, D), lambda i: (0, 0)), ANY_SPEC],
        out_specs=[blk, blk],
        out_shape=[jax.ShapeDtypeStruct((BLOCK + L, D), F32), jax.ShapeDtypeStruct((BLOCK + L, D), BF16)],
        compiler_params=_params(("parallel",)),
    )(head, x, g, dep)


def last_norm_bwd(dn, h, g, dres, *, name):
    R, D = h.shape
    nb = R // BLOCK

    def body(dn_ref, h_ref, g_ref, dres_ref, dhead_ref, dx_ref, dg_ref):
        i = pl.program_id(0)
        dh, dg = _norm_bwd_rows(dn_ref[...].astype(F32), h_ref[...], g_ref[...], dres_ref[...])
        dx_ref[...] = dh

        @pl.when(i == 0)
        def _():
            dhead_ref[...] = dh
            dg_ref[...] = dg

        @pl.when(i > 0)
        def _():
            dg_ref[...] += dg

    blk = pl.BlockSpec((BLOCK, D), lambda i: (i, 0))
    vec = pl.BlockSpec((1, D), lambda i: (0, 0))
    return pl.pallas_call(
        body, name=name, grid=(nb,), in_specs=[blk, blk, vec, blk],
        out_specs=[pl.BlockSpec((BLOCK, D), lambda i: (0, 0)),
                   pl.BlockSpec((BLOCK, D), lambda i: (jnp.maximum(i - 1, 0), 0)), vec],
        out_shape=[jax.ShapeDtypeStruct((BLOCK, D), F32), jax.ShapeDtypeStruct((R - BLOCK, D), F32),
                   jax.ShapeDtypeStruct((1, D), F32)],
        compiler_params=_params(("arbitrary",)),
    )(dn, h, g, dres)


HALO = 16


def conv_fwd(bcu, conv_w, *, name):
    R, D3 = bcu.shape
    D = D3 // 3
    tr = _tile(R, ROW_TILE, SUBLANE_BF16)
    tc = _tile(D, CHANNEL_TILE, LANES)
    nc = D // tc
    hb = tr // HALO

    def body(b_ref, c_ref, u_ref, ch_ref, uh_ref, w_ref, o_ref, vbuf):
        i = pl.program_id(0)
        v = c_ref[...].astype(F32) * u_ref[...].astype(F32)
        vbuf[0:HALO, :] = jnp.where(i > 0, ch_ref[...].astype(F32) * uh_ref[...].astype(F32), 0.0)
        vbuf[HALO:HALO + tr, :] = v
        v1 = vbuf[HALO - 1:HALO - 1 + tr, :]
        v2 = vbuf[HALO - 2:HALO - 2 + tr, :]
        conv = w_ref[0:1, :] * v2 + w_ref[1:2, :] * v1 + w_ref[2:3, :] * v
        o_ref[...] = (b_ref[...].astype(F32) * conv).astype(BF16)

    def prev(i):
        return jnp.maximum(i * hb - 1, 0)

    return pl.pallas_call(
        body, name=name, grid=(R // tr, nc),
        in_specs=[pl.BlockSpec((tr, tc), lambda i, j: (i, j)),
                  pl.BlockSpec((tr, tc), lambda i, j: (i, nc + j)),
                  pl.BlockSpec((tr, tc), lambda i, j: (i, 2 * nc + j)),
                  pl.BlockSpec((HALO, tc), lambda i, j: (prev(i), nc + j)),
                  pl.BlockSpec((HALO, tc), lambda i, j: (prev(i), 2 * nc + j)),
                  pl.BlockSpec((3, tc), lambda i, j: (0, j))],
        out_specs=pl.BlockSpec((tr, tc), lambda i, j: (i, j)),
        out_shape=jax.ShapeDtypeStruct((R, D), BF16),
        scratch_shapes=[pltpu.VMEM((HALO + tr, tc), F32)],
        compiler_params=_params(("parallel", "parallel")),
    )(bcu, bcu, bcu, bcu, bcu, conv_w)


def conv_bwd(bcu, conv_w, dg, *, name):
    R, D3 = bcu.shape
    D = D3 // 3
    tr = _tile(R, ROW_TILE, SUBLANE_BF16)
    tc = _tile(D, CHANNEL_TILE, LANES)
    hb = tr // HALO
    nt = R // tr
    last_h = R // HALO - 1

    def body(x_ref, xp_ref, xn_ref, dg_ref, dgn_ref, w_ref, o_ref, dw_ref, vbuf, dbuf):
        i = pl.program_id(0)
        row8 = lax.broadcasted_iota(jnp.int32, (8, tc), 0)
        for c0 in range(0, D, tc):
            cb, cc, cu = slice(c0, c0 + tc), slice(D + c0, D + c0 + tc), slice(2 * D + c0, 2 * D + c0 + tc)
            w0, w1, w2 = w_ref[0:1, cb], w_ref[1:2, cb], w_ref[2:3, cb]
            b = x_ref[:, cb].astype(F32)
            c = x_ref[:, cc].astype(F32)
            u = x_ref[:, cu].astype(F32)
            v = c * u
            vbuf[0:HALO, :] = jnp.where(i > 0, xp_ref[:, cc].astype(F32) * xp_ref[:, cu].astype(F32), 0.0)
            vbuf[HALO:HALO + tr, :] = v
            v1 = vbuf[HALO - 1:HALO - 1 + tr, :]
            v2 = vbuf[HALO - 2:HALO - 2 + tr, :]
            dgv = dg_ref[:, cb].astype(F32)
            o_ref[:, cb] = (dgv * (w0 * v2 + w1 * v1 + w2 * v)).astype(BF16)
            dconv = dgv * b
            dbuf[0:tr, :] = dconv
            dbuf[tr:tr + HALO, :] = jnp.where(i < nt - 1, dgn_ref[:, cb].astype(F32) * xn_ref[:, cb].astype(F32), 0.0)
            dv = w2 * dconv + w1 * dbuf[1:1 + tr, :] + w0 * dbuf[2:2 + tr, :]
            o_ref[:, cc] = (dv * u).astype(BF16)
            o_ref[:, cu] = (dv * c).astype(BF16)
            dw = jnp.where(row8 == 0, jnp.sum(dconv * v2, axis=0, keepdims=True),
                           jnp.where(row8 == 1, jnp.sum(dconv * v1, axis=0, keepdims=True),
                                     jnp.where(row8 == 2, jnp.sum(dconv * v, axis=0, keepdims=True), 0.0)))

            @pl.when(i == 0)
            def _():
                dw_ref[:, cb] = dw

            @pl.when(i > 0)
            def _():
                dw_ref[:, cb] += dw

    def prev(i):
        return jnp.maximum(i * hb - 1, 0)

    def nxt(i):
        return jnp.minimum((i + 1) * hb, last_h)

    return pl.pallas_call(
        body, name=name, grid=(nt,),
        in_specs=[pl.BlockSpec((tr, D3), lambda i: (i, 0)),
                  pl.BlockSpec((HALO, D3), lambda i: (prev(i), 0)),
                  pl.BlockSpec((HALO, D3), lambda i: (nxt(i), 0)),
                  pl.BlockSpec((tr, D), lambda i: (i, 0)),
                  pl.BlockSpec((HALO, D), lambda i: (nxt(i), 0)),
                  pl.BlockSpec((3, D), lambda i: (0, 0))],
        out_specs=[pl.BlockSpec((tr, D3), lambda i: (i, 0)), pl.BlockSpec((8, D), lambda i: (0, 0))],
        out_shape=[jax.ShapeDtypeStruct((R, D3), BF16), jax.ShapeDtypeStruct((8, D), F32)],
        scratch_shapes=[pltpu.VMEM((HALO + tr, tc), F32), pltpu.VMEM((tr + HALO, tc), F32)],
        compiler_params=_params(("arbitrary",)),
    )(bcu, bcu, bcu, dg, dg, conv_w)


def _swap32(x):
    w = x.shape[1]
    lane = lax.broadcasted_iota(jnp.int32, x.shape, 1)
    return jnp.where((lane & (HEAD_DIM - 1)) < HEAD_DIM // 2, pltpu.roll(x, w - HEAD_DIM // 2, axis=1),
                     pltpu.roll(x, HEAD_DIM // 2, axis=1))


def _rope(x, cos, sin):
    return x * cos + _swap32(x) * sin


def rope_fwd(qkv, cos, sin, *, n_q, n_kv, name):
    R, W = qkv.shape
    qw = n_q * HEAD_DIM
    kw = n_kv * HEAD_DIM
    tr = _tile(R, ROPE_ROWS, LANES)

    def expand(y, ref, ref_t, c):
        lane = lax.broadcasted_iota(jnp.int32, y.shape, 1)
        lo = jnp.where(lane < HEAD_DIM, y, 0.0)
        hi = jnp.where(lane >= HEAD_DIM, y, 0.0)
        tiles = (lo, pltpu.roll(lo, HEAD_DIM, axis=1), pltpu.roll(hi, HEAD_DIM, axis=1), hi)
        for t, tile in enumerate(tiles):
            c0 = 512 * c + 128 * t
            ref[:, c0:c0 + 128] = tile.astype(BF16)
            ref_t[c0:c0 + 128, :] = tile.T.astype(BF16)

    def body(x_ref, c_ref, s_ref, q_ref, k_ref, v_ref, kt_ref, vt_ref):
        cos = c_ref[...]
        sin = s_ref[...]
        for c in range(qw // 128):
            x = x_ref[:, 128 * c:128 * (c + 1)]
            q_ref[:, 128 * c:128 * (c + 1)] = (_rope(x, cos, sin) * (HEAD_DIM ** -0.5)).astype(BF16)
        for c in range(kw // 128):
            expand(_rope(x_ref[:, qw + 128 * c:qw + 128 * (c + 1)], cos, sin), k_ref, kt_ref, c)
            expand(x_ref[:, qw + kw + 128 * c:qw + kw + 128 * (c + 1)], v_ref, vt_ref, c)

    row = lambda w: pl.BlockSpec((tr, w), lambda i: (i, 0))
    col = pl.BlockSpec((4 * kw, tr), lambda i: (0, i))
    return pl.pallas_call(
        body, name=name, grid=(R // tr,),
        in_specs=[row(W), row(128), row(128)],
        out_specs=[row(qw), row(4 * kw), row(4 * kw), col, col],
        out_shape=[jax.ShapeDtypeStruct((R, qw), BF16), jax.ShapeDtypeStruct((R, 4 * kw), BF16),
                   jax.ShapeDtypeStruct((R, 4 * kw), BF16), jax.ShapeDtypeStruct((4 * kw, R), BF16),
                   jax.ShapeDtypeStruct((4 * kw, R), BF16)],
        compiler_params=_params(("parallel",)),
    )(qkv, cos, sin)


def rope_bwd(dq, dkx, dvx, cos, sin, *, n_q, n_kv, name):
    R = dq.shape[0]
    qw = n_q * HEAD_DIM
    kw = n_kv * HEAD_DIM
    W = qw + 2 * kw
    tr = _tile(R, ROW_TILE, SUBLANE_BF16)

    def fold(ref, c):
        lane = lax.broadcasted_iota(jnp.int32, (tr, 128), 1)
        x0 = ref[:, 128 * (2 * c):128 * (2 * c + 1)]
        x1 = ref[:, 128 * (2 * c + 1):128 * (2 * c + 2)]
        f0 = x0 + pltpu.roll(x0, HEAD_DIM, axis=1)
        f1 = x1 + pltpu.roll(x1, HEAD_DIM, axis=1)
        return jnp.where(lane < HEAD_DIM, f0, f1)

    def body(dq_ref, dk_ref, dv_ref, c_ref, s_ref, o_ref):
        cos = c_ref[...]
        nsin = -s_ref[...]
        for c in range(qw // 128):
            y = dq_ref[:, 128 * c:128 * (c + 1)]
            o_ref[:, 128 * c:128 * (c + 1)] = (_rope(y, cos, nsin) * (HEAD_DIM ** -0.5)).astype(BF16)
        for c in range(kw // 128):
            o_ref[:, qw + 128 * c:qw + 128 * (c + 1)] = _rope(fold(dk_ref, c), cos, nsin).astype(BF16)
            o_ref[:, qw + kw + 128 * c:qw + kw + 128 * (c + 1)] = fold(dv_ref, c).astype(BF16)

    row = lambda w: pl.BlockSpec((tr, w), lambda i: (i, 0))
    return pl.pallas_call(
        body, name=name, grid=(R // tr,),
        in_specs=[row(qw), row(2 * kw), row(2 * kw), row(128), row(128)],
        out_specs=row(W),
        out_shape=jax.ShapeDtypeStruct((R, W), BF16),
        compiler_params=_params(("parallel",)),
    )(dq, dkx, dvx, cos, sin)


def _band_bias(n, pad):
    key = lax.broadcasted_iota(jnp.int32, (2 * BLOCK, BLOCK), 0)
    qry = lax.broadcasted_iota(jnp.int32, (2 * BLOCK, BLOCK), 1)
    kmin = jnp.where(n == 0, BLOCK + pad, jnp.where(n == 1, pad, 0))
    allowed = (key > qry) & (key <= qry + BLOCK) & (key >= kmin)
    return jnp.where(allowed, 0.0, NEG_INF).astype(F32)


def _dot_nt(a, b):
    return lax.dot_general(a, b, (((1,), (1,)), ((), ())), preferred_element_type=F32)


def _band(prev_ref, cur_ref, c0):
    return jnp.concatenate([prev_ref[:, c0:c0 + 128], cur_ref[:, c0:c0 + 128]], axis=0)


def _band_t(prev_ref, cur_ref, r0):
    return jnp.concatenate([prev_ref[r0:r0 + 128, :], cur_ref[r0:r0 + 128, :]], axis=1)


def attn_fwd(q, kx, vxt, sinks, *, pad, name):
    R, qw = q.shape
    n_q = qw // HEAD_DIM
    n_kv = kx.shape[1] // 256
    nb = R // BLOCK

    def body(s_ref, q_ref, kc_ref, kp_ref, vc_ref, vp_ref, o_ref, l_ref):
        n = pl.program_id(0)
        bias = _band_bias(n, pad)
        row = lax.broadcasted_iota(jnp.int32, (128, BLOCK), 0)

        def softmax(st, sink):
            st = st + bias
            m = jnp.maximum(jnp.max(st, axis=0, keepdims=True), sink)
            e = jnp.exp(st - m)
            den = jnp.sum(e, axis=0, keepdims=True) + jnp.exp(sink - m)
            return e.astype(BF16), 1.0 / den, m + jnp.log(den)

        for g in range(n_kv):
            k2 = jnp.concatenate([_band(kp_ref, kc_ref, 256 * g), _band(kp_ref, kc_ref, 256 * g + 128)], axis=0)
            v2 = jnp.concatenate([_band_t(vp_ref, vc_ref, 256 * g), _band_t(vp_ref, vc_ref, 256 * g + 128)], axis=1)
            for p in range(GROUP // 2):
                c0 = 128 * (g * (GROUP // 2) + p)
                he = GROUP * g + 2 * p
                q2 = q_ref[:, c0:c0 + 128]
                st = _dot_nt(k2, q2)
                ee, re, le = softmax(st[0:2 * BLOCK], s_ref[he])
                eo, ro, lo = softmax(st[2 * BLOCK:4 * BLOCK], s_ref[he + 1])
                o2t = jnp.dot(v2, jnp.concatenate([ee, eo], axis=0), preferred_element_type=F32)
                o2t = o2t * jnp.where(row < HEAD_DIM, re, ro)
                o_ref[:, c0:c0 + 128] = o2t.T.astype(BF16)
                l_ref[he:he + 1, :] = le
                l_ref[he + 1:he + 2, :] = lo

    cur = lambda w: pl.BlockSpec((BLOCK, w), lambda n: (n, 0))
    prv = lambda w: pl.BlockSpec((BLOCK, w), lambda n: (jnp.maximum(n - 1, 0), 0))
    cur_t = lambda h: pl.BlockSpec((h, BLOCK), lambda n: (0, n))
    prv_t = lambda h: pl.BlockSpec((h, BLOCK), lambda n: (0, jnp.maximum(n - 1, 0)))
    kxw = kx.shape[1]
    return pl.pallas_call(
        body, name=name, grid=(nb,),
        in_specs=[pl.BlockSpec(memory_space=pltpu.SMEM), cur(qw), cur(kxw), prv(kxw), cur_t(kxw), prv_t(kxw)],
        out_specs=[cur(qw), cur_t(n_q)],
        out_shape=[jax.ShapeDtypeStruct((R, qw), BF16), jax.ShapeDtypeStruct((n_q, R), F32)],
        compiler_params=_params(("parallel",)),
    )(sinks, q, kx, kx, vxt, vxt)


def attn_bwd(q, kx, kxt, vx, o, do, lse, sinks, *, pad, name):
    R, qw = q.shape
    n_q = qw // HEAD_DIM
    n_kv = kx.shape[1] // 256
    nb = R // BLOCK
    kw2 = n_kv * 128

    def body(s_ref, q_ref, do_ref, o_ref, l_ref, kc_ref, kp_ref, ktc_ref, ktp_ref, vc_ref, vp_ref,
             dq_ref, dk_ref, dv_ref, ds_ref, cdk, cdv):
        n = pl.program_id(0)

        @pl.when(n == 0)
        def _():
            cdk[...] = jnp.zeros_like(cdk)
            cdv[...] = jnp.zeros_like(cdv)
            ds_ref[...] = jnp.zeros_like(ds_ref)

        @pl.when(n < nb)
        def _():
            bias = _band_bias(n, pad)
            lane2 = lax.broadcasted_iota(jnp.int32, (2 * BLOCK, 128), 1)
            lane1 = lax.broadcasted_iota(jnp.int32, (1, 128), 1)
            sel_r = lax.broadcasted_iota(jnp.int32, (8, 128), 0)
            sel_l = lax.broadcasted_iota(jnp.int32, (8, 128), 1)
            sel = (((sel_r == 0) & (sel_l < HEAD_DIM)) | ((sel_r == 1) & (sel_l >= HEAD_DIM))).astype(BF16)
            dsink = jnp.zeros((1, 128), F32)
            for g in range(n_kv):
                k2 = jnp.concatenate([_band(kp_ref, kc_ref, 256 * g), _band(kp_ref, kc_ref, 256 * g + 128)], axis=0)
                kt2 = jnp.concatenate([_band_t(ktp_ref, ktc_ref, 256 * g), _band_t(ktp_ref, ktc_ref, 256 * g + 128)],
                                      axis=1)
                v2 = jnp.concatenate([_band(vp_ref, vc_ref, 256 * g), _band(vp_ref, vc_ref, 256 * g + 128)], axis=0)
                dk4 = jnp.zeros((4 * BLOCK, 128), F32)
                dv4 = jnp.zeros((4 * BLOCK, 128), F32)
                for p in range(GROUP // 2):
                    c0 = 128 * (g * (GROUP // 2) + p)
                    he = GROUP * g + 2 * p
                    q2 = q_ref[:, c0:c0 + 128]
                    do2 = do_ref[:, c0:c0 + 128]
                    prod = do2.astype(F32) * o_ref[:, c0:c0 + 128].astype(F32)
                    prod_hi = prod.astype(BF16)
                    prod_lo = (prod - prod_hi.astype(F32)).astype(BF16)
                    deltas = _dot_nt(sel, prod_hi) + _dot_nt(sel, prod_lo)
                    st = _dot_nt(k2, q2)
                    dpt = _dot_nt(v2, do2)
                    pts, dsts = [], []
                    for r in range(2):
                        h = he + r
                        rows = slice(2 * BLOCK * r, 2 * BLOCK * (r + 1))
                        lse_h = l_ref[h:h + 1, :]
                        delta = deltas[r:r + 1, :]
                        pt = jnp.exp(st[rows] + bias - lse_h)
                        pts.append(pt.astype(BF16))
                        dsts.append((pt * (dpt[rows] - delta)).astype(BF16))
                        psink = jnp.exp(s_ref[h] - lse_h)
                        tot = jnp.sum(psink * delta, axis=1, keepdims=True)
                        dsink = dsink - jnp.where(lane1 == h, tot, 0.0)
                    dst = jnp.concatenate(dsts, axis=0)
                    dq_ref[:, c0:c0 + 128] = jnp.dot(kt2, dst, preferred_element_type=F32).T
                    dk4 = dk4 + jnp.dot(dst, q2, preferred_element_type=F32)
                    dv4 = dv4 + jnp.dot(jnp.concatenate(pts, axis=0), do2, preferred_element_type=F32)
                gc = pl.ds(128 * g, 128)
                dk2 = jnp.where(lane2 < HEAD_DIM, dk4[0:2 * BLOCK], dk4[2 * BLOCK:4 * BLOCK])
                dv2 = jnp.where(lane2 < HEAD_DIM, dv4[0:2 * BLOCK], dv4[2 * BLOCK:4 * BLOCK])
                dk_ref[:, gc] = cdk[:, gc] + dk2[0:BLOCK]
                dv_ref[:, gc] = cdv[:, gc] + dv2[0:BLOCK]
                cdk[:, gc] = dk2[BLOCK:2 * BLOCK]
                cdv[:, gc] = dv2[BLOCK:2 * BLOCK]
            ds_ref[0:1, :] += dsink

        @pl.when(n == nb)
        def _():
            dk_ref[...] = cdk[...]
            dv_ref[...] = cdv[...]

    cur = lambda w: pl.BlockSpec((BLOCK, w), lambda n: (jnp.minimum(n, nb - 1), 0))
    prv = lambda w: pl.BlockSpec((BLOCK, w), lambda n: (jnp.clip(n - 1, 0, nb - 1), 0))
    cur_t = lambda h: pl.BlockSpec((h, BLOCK), lambda n: (0, jnp.minimum(n, nb - 1)))
    prv_t = lambda h: pl.BlockSpec((h, BLOCK), lambda n: (0, jnp.clip(n - 1, 0, nb - 1)))
    kxw = kx.shape[1]
    return pl.pallas_call(
        body, name=name, grid=(nb + 1,),
        in_specs=[pl.BlockSpec(memory_space=pltpu.SMEM), cur(qw), cur(qw), cur(qw), cur_t(n_q),
                  cur(kxw), prv(kxw), cur_t(kxw), prv_t(kxw), cur(kxw), prv(kxw)],
        out_specs=[cur(qw), prv(kw2), prv(kw2), pl.BlockSpec((8, 128), lambda n: (0, 0))],
        out_shape=[jax.ShapeDtypeStruct((R, qw), F32), jax.ShapeDtypeStruct((R, kw2), F32),
                   jax.ShapeDtypeStruct((R, kw2), F32), jax.ShapeDtypeStruct((8, 128), F32)],
        scratch_shapes=[pltpu.VMEM((BLOCK, kw2), F32), pltpu.VMEM((BLOCK, kw2), F32)],
        compiler_params=_params(("arbitrary",)),
    )(sinks, q, do, o, lse, kx, kx, kxt, kxt, vx, vx)


def loss_bwd(h, g, target, *, name):
    R, D = h.shape
    nb = R // BLOCK

    def body(h_ref, g_ref, t_ref, loss_ref, dh_ref, dhb_ref, dg_ref):
        n = pl.program_id(0)
        xhat, rstd = _rms_parts(h_ref[...])
        gv = g_ref[...]
        diff = jnp.where(n > 0, xhat * gv - t_ref[...], 0.0)
        part = (0.5 / D) * jnp.sum(jnp.sum(diff * diff, axis=1, keepdims=True), axis=0, keepdims=True)
        dout = diff * (1.0 / D)
        dxh = dout * gv
        dh = rstd * (dxh - xhat * jnp.mean(dxh * xhat, axis=-1, keepdims=True))
        dh_ref[...] = dh
        dhb_ref[...] = dh.astype(BF16)
        dg = jnp.sum(dout * xhat, axis=0, keepdims=True)

        @pl.when(n == 0)
        def _():
            loss_ref[...] = jnp.zeros_like(loss_ref) + part
            dg_ref[...] = dg

        @pl.when(n > 0)
        def _():
            loss_ref[...] += part
            dg_ref[...] += dg

    blk = pl.BlockSpec((BLOCK, D), lambda n: (n, 0))
    return pl.pallas_call(
        body, name=name, grid=(nb,),
        in_specs=[blk, pl.BlockSpec((1, D), lambda n: (0, 0)),
                  pl.BlockSpec((BLOCK, D), lambda n: (jnp.maximum(n - 1, 0), 0))],
        out_specs=[pl.BlockSpec((8, 128), lambda n: (0, 0)), blk, blk, pl.BlockSpec((1, D), lambda n: (0, 0))],
        out_shape=[jax.ShapeDtypeStruct((8, 128), F32), jax.ShapeDtypeStruct((R, D), F32),
                   jax.ShapeDtypeStruct((R, D), BF16), jax.ShapeDtypeStruct((1, D), F32)],
        compiler_params=_params(("arbitrary",)),
    )(h, g, target)


def _adam_math(w, g, m, v):
    m = ADAM_B1 * m + (1.0 - ADAM_B1) * g
    v = ADAM_B2 * v + (1.0 - ADAM_B2) * (g * g)
    m_hat = m / (1.0 - ADAM_B1 ** ADAM_STEP)
    v_hat = v / (1.0 - ADAM_B2 ** ADAM_STEP)
    delta = -ADAM_LR * (m_hat / (jnp.sqrt(v_hat) + ADAM_EPS) + ADAM_WD * w)
    return delta, m, v


def adam(w, m, v, g, *, name):
    r, C = w.shape
    tr = _tile(r, ADAM_ROWS, SUBLANE_F32)

    def body(w_ref, m_ref, v_ref, g_ref, d_ref, mo_ref, vo_ref):
        d_ref[...], mo_ref[...], vo_ref[...] = _adam_math(w_ref[...], g_ref[...], m_ref[...], v_ref[...])

    blk = pl.BlockSpec((tr, C), lambda i: (i, 0))
    return pl.pallas_call(
        body, name=name, grid=(r // tr,), in_specs=[blk] * 4, out_specs=[blk] * 3,
        out_shape=[jax.ShapeDtypeStruct((r, C), F32)] * 3,
        compiler_params=_params(("parallel",)),
    )(w, m, v, g)


def _sum_blocks(p_ref, own_ref, me):
    acc = None
    for s in range(N_DEV):
        blk = jnp.where(me == s, own_ref[...], p_ref[s]).astype(F32)
        acc = blk if acc is None else acc + blk
    return acc


def _parts_specs(r, C, tr):
    blk = pl.BlockSpec((tr, C), lambda i, me: (i, 0))
    parts = pl.BlockSpec((N_DEV, tr, C), lambda i, me: (0, i, 0))
    own = pl.BlockSpec((None, tr, C), lambda i, me: (me[0], i, 0))
    return blk, parts, own


def adam_parts(w, m, v, parts, own, me, *, name):
    r, C = w.shape
    tr = _tile(r, ADAM_ROWS, SUBLANE_F32)

    def body(me_ref, w_ref, m_ref, v_ref, p_ref, own_ref, go_ref, d_ref, mo_ref, vo_ref):
        gv = _sum_blocks(p_ref, own_ref, me_ref[0])
        go_ref[...] = gv
        d_ref[...], mo_ref[...], vo_ref[...] = _adam_math(w_ref[...], gv, m_ref[...], v_ref[...])

    blk, pblk, oblk = _parts_specs(r, C, tr)
    return pl.pallas_call(
        body, name=name,
        grid_spec=pltpu.PrefetchScalarGridSpec(num_scalar_prefetch=1, grid=(r // tr,),
                                               in_specs=[blk, blk, blk, pblk, oblk], out_specs=[blk] * 4),
        out_shape=[jax.ShapeDtypeStruct((r, C), F32)] * 4,
        compiler_params=_params(("parallel",)),
    )(me, w, m, v, parts, own)


def adam_parts_t(w, m, v, parts, own, me, *, name):
    K, r = w.shape
    tk = _tile(K, ADAM_T_COLS, LANES)

    def body(me_ref, w_ref, m_ref, v_ref, p_ref, own_ref, go_ref, d_ref, mo_ref, vo_ref):
        gv = _sum_blocks(p_ref, own_ref, me_ref[0]).T
        go_ref[...] = gv
        d_ref[...], mo_ref[...], vo_ref[...] = _adam_math(w_ref[...], gv, m_ref[...], v_ref[...])

    blk = pl.BlockSpec((tk, r), lambda i, me: (i, 0))
    pblk = pl.BlockSpec((N_DEV, r, tk), lambda i, me: (0, 0, i))
    oblk = pl.BlockSpec((None, r, tk), lambda i, me: (me[0], 0, i))
    return pl.pallas_call(
        body, name=name,
        grid_spec=pltpu.PrefetchScalarGridSpec(num_scalar_prefetch=1, grid=(K // tk,),
                                               in_specs=[blk, blk, blk, pblk, oblk], out_specs=[blk] * 4),
        out_shape=[jax.ShapeDtypeStruct((K, r), F32)] * 4,
        compiler_params=_params(("parallel",)),
    )(me, w, m, v, parts, own)


def sum_parts(parts, own, me, *, name):
    _, r, C = parts.shape
    tr = _tile(r, ADAM_ROWS, SUBLANE_F32)

    def body(me_ref, p_ref, own_ref, o_ref):
        o_ref[...] = _sum_blocks(p_ref, own_ref, me_ref[0])

    blk, pblk, oblk = _parts_specs(r, C, tr)
    return pl.pallas_call(
        body, name=name,
        grid_spec=pltpu.PrefetchScalarGridSpec(num_scalar_prefetch=1, grid=(r // tr,),
                                               in_specs=[pblk, oblk], out_specs=blk),
        out_shape=jax.ShapeDtypeStruct((r, C), F32),
        compiler_params=_params(("parallel",)),
    )(me, parts, own)


def cast_place(w, me, dep, *, name):
    r, C = w.shape
    tr = _tile(r, CAST_ROWS, SUBLANE_BF16)

    def body(me_ref, w_ref, dep_ref, s_ref, l_ref):
        v = w_ref[...].astype(BF16)
        s_ref[...] = v
        l_ref[...] = v

    blk = pl.BlockSpec((tr, C), lambda i, me: (i, 0))
    return pl.pallas_call(
        body, name=name,
        grid_spec=pltpu.PrefetchScalarGridSpec(
            num_scalar_prefetch=1, grid=(r // tr,), in_specs=[blk, ANY_SPEC],
            out_specs=[blk, pl.BlockSpec((None, tr, C), lambda i, me: (me[0], i, 0))]),
        out_shape=[jax.ShapeDtypeStruct((r, C), BF16), jax.ShapeDtypeStruct((N_DEV, r, C), BF16)],
        compiler_params=_params(("parallel",)),
    )(me, w, dep)


def _coords():
    return lax.axis_index("x"), lax.axis_index("y"), lax.axis_index("c")


def _peer(m):
    x, y, c = _coords()
    px = 1 - x if m & 4 else x
    py = 1 - y if m & 2 else y
    pc = 1 - c if m & 1 else c
    return (px, py, pc), 4 * px + 2 * py + pc


def exchange(items, *, all_to_all, name):
    n = len(items)
    if all_to_all:
        out_shape = [jax.ShapeDtypeStruct(a.shape, a.dtype) for a in items]
    else:
        out_shape = [jax.ShapeDtypeStruct((N_DEV,) + a.shape, a.dtype) for a in items]

    def body(*refs):
        ins, outs = refs[:n], refs[n:2 * n]
        send_sems, recv_sems, local_sems = refs[2 * n:]
        x, y, c = _coords()
        me = 4 * x + 2 * y + c

        def src(i, idx):
            return ins[i].at[idx] if all_to_all else ins[i]

        local = [pltpu.make_async_copy(src(i, me), outs[i].at[me], local_sems.at[i]) for i in range(n)]
        for cp in local:
            cp.start()
        sends = []
        for m in range(1, N_DEV):
            peer, pidx = _peer(m)
            for i in range(n):
                k = i * (N_DEV - 1) + m - 1
                cp = pltpu.make_async_remote_copy(src_ref=src(i, pidx), dst_ref=outs[i].at[me],
                                                  send_sem=send_sems.at[k], recv_sem=recv_sems.at[k],
                                                  device_id=peer, device_id_type=MESH)
                cp.start()
                sends.append(cp)
        for m in range(1, N_DEV):
            peer, pidx = _peer(m)
            for i in range(n):
                k = i * (N_DEV - 1) + m - 1
                pltpu.make_async_remote_copy(src_ref=src(i, pidx), dst_ref=outs[i].at[pidx],
                                             send_sem=send_sems.at[k], recv_sem=recv_sems.at[k],
                                             device_id=peer, device_id_type=MESH).wait_recv()
        for cp in sends:
            cp.wait_send()
        for cp in local:
            cp.wait()

    any_spec = pl.BlockSpec(memory_space=pl.ANY)
    return pl.pallas_call(
        body, name=name, in_specs=[any_spec] * n, out_specs=[any_spec] * n, out_shape=out_shape,
        scratch_shapes=[pltpu.SemaphoreType.DMA((n * (N_DEV - 1),)), pltpu.SemaphoreType.DMA((n * (N_DEV - 1),)),
                        pltpu.SemaphoreType.DMA((n,))],
    )(*items)


HBM_SPEC = pl.BlockSpec(memory_space=pltpu.HBM)
SEM_SPEC = pl.BlockSpec(memory_space=pltpu.SEMAPHORE)
SPLIT_PARAMS = pltpu.CompilerParams(has_side_effects=pltpu.SideEffectType.DATAFLOW_SIDE_EFFECTING)


def _split_copies(src_ref, land_ref, send_sems, recv_sems, all_to_all):
    x, y, c = _coords()
    me = 4 * x + 2 * y + c
    copies = []
    for m in range(1, N_DEV):
        peer, pidx = _peer(m)
        copies.append(pltpu.make_async_remote_copy(
            src_ref=src_ref.at[pidx] if all_to_all else src_ref, dst_ref=land_ref.at[me],
            send_sem=send_sems.at[m - 1], recv_sem=recv_sems.at[m - 1], device_id=peer, device_id_type=MESH))
    return copies


def copy_start(items, lands=None, *, all_to_all, name):
    n = len(items)
    if lands is None:
        lands = [lax.empty(a.shape if all_to_all else (N_DEV,) + a.shape, a.dtype) for a in items]

    def body(*refs):
        srcs, lnds, outs = refs[:n], refs[n:2 * n], refs[2 * n:]
        for i in range(n):
            for cp in _split_copies(srcs[i], lnds[i], outs[4 * i], outs[4 * i + 1], all_to_all):
                cp.start()
        outs[4 * n][...] = jnp.zeros((8, 128), F32)

    out_shape, out_specs, aliases = [], [], {}
    for i, (a, l) in enumerate(zip(items, lands)):
        out_shape += [pltpu.SemaphoreType.DMA((N_DEV - 1,)), pltpu.SemaphoreType.DMA((N_DEV - 1,)),
                      pltpu.HBM(a.shape, a.dtype), pltpu.HBM(l.shape, l.dtype)]
        out_specs += [SEM_SPEC, SEM_SPEC, HBM_SPEC, HBM_SPEC]
        aliases[i] = 4 * i + 2
        aliases[n + i] = 4 * i + 3
    out_shape.append(jax.ShapeDtypeStruct((8, 128), F32))
    out_specs.append(pl.BlockSpec(memory_space=pltpu.VMEM))
    hbm = lambda a: pltpu.with_memory_space_constraint(a, pltpu.HBM)
    res = pl.pallas_call(
        body, name=name, in_specs=[HBM_SPEC] * (2 * n), out_specs=out_specs, out_shape=out_shape,
        input_output_aliases=aliases, compiler_params=SPLIT_PARAMS,
    )(*[hbm(a) for a in items], *[hbm(l) for l in lands])
    return [tuple(res[4 * i:4 * i + 4]) for i in range(n)], res[4 * n]


def copy_wait(handle, after, *, all_to_all, name):
    send_sems, recv_sems, src, land = handle

    def body(src_ref, land_ref, send_ref, recv_ref, after_ref, src_out, got_ref):
        for cp in _split_copies(src_ref, land_ref, send_ref, recv_ref, all_to_all):
            cp.wait_send()
            cp.wait_recv()

    return pl.pallas_call(
        body, name=name, in_specs=[HBM_SPEC, HBM_SPEC, SEM_SPEC, SEM_SPEC, ANY_SPEC],
        out_specs=[HBM_SPEC, HBM_SPEC],
        out_shape=[pltpu.HBM(src.shape, src.dtype), pltpu.HBM(land.shape, land.dtype)],
        input_output_aliases={0: 0, 1: 1}, compiler_params=SPLIT_PARAMS,
    )(src, land, send_sems, recv_sems, after)


OTHER_CHIPS = (4, 2, 6)
SIBLING = 1


def _relay_copies(src_ref, land_ref, send_sems, recv_sems):
    x, y, c = _coords()
    me = 4 * x + 2 * y + c
    return [pltpu.make_async_remote_copy(src_ref=src_ref, dst_ref=land_ref.at[me], send_sem=send_sems.at[k],
                                         recv_sem=recv_sems.at[k], device_id=_peer(m)[0], device_id_type=MESH)
            for k, m in enumerate((SIBLING,) + OTHER_CHIPS)]


def _forward_copies(land_ref, send_sems, recv_sems):
    copies = []
    for k, m in enumerate(OTHER_CHIPS):
        pidx = _peer(m)[1]
        copies.append(pltpu.make_async_remote_copy(
            src_ref=land_ref.at[pidx], dst_ref=land_ref.at[pidx], send_sem=send_sems.at[k], recv_sem=recv_sems.at[k],
            device_id=_peer(SIBLING)[0], device_id_type=MESH))
    return copies


def relay_start(item, land, *, name):
    def body(src_ref, land_ref, send_sems, recv_sems, src_out, land_out, token):
        for cp in _relay_copies(src_ref, land_ref, send_sems, recv_sems):
            cp.start()
        token[...] = jnp.zeros((8, 128), F32)

    n = 1 + len(OTHER_CHIPS)
    hbm = lambda a: pltpu.with_memory_space_constraint(a, pltpu.HBM)
    res = pl.pallas_call(
        body, name=name, in_specs=[HBM_SPEC, HBM_SPEC],
        out_specs=[SEM_SPEC, SEM_SPEC, HBM_SPEC, HBM_SPEC, pl.BlockSpec(memory_space=pltpu.VMEM)],
        out_shape=[pltpu.SemaphoreType.DMA((n,)), pltpu.SemaphoreType.DMA((n,)), pltpu.HBM(item.shape, item.dtype),
                   pltpu.HBM(land.shape, land.dtype), jax.ShapeDtypeStruct((8, 128), F32)],
        input_output_aliases={0: 2, 1: 3}, compiler_params=SPLIT_PARAMS,
    )(hbm(item), hbm(land))
    return tuple(res[:4]), res[4]


def relay_forward(handle, after, *, name):
    send_sems, recv_sems, src, land = handle

    def body(src_ref, land_ref, send_ref, recv_ref, after_ref, src_out, land_out, send2, recv2):
        for cp in _relay_copies(src_ref, land_ref, send_ref, recv_ref):
            cp.wait_send()
            cp.wait_recv()
        for cp in _forward_copies(land_ref, send2, recv2):
            cp.start()

    n = len(OTHER_CHIPS)
    res = pl.pallas_call(
        body, name=name, in_specs=[HBM_SPEC, HBM_SPEC, SEM_SPEC, SEM_SPEC, ANY_SPEC],
        out_specs=[HBM_SPEC, HBM_SPEC, SEM_SPEC, SEM_SPEC],
        out_shape=[pltpu.HBM(src.shape, src.dtype), pltpu.HBM(land.shape, land.dtype),
                   pltpu.SemaphoreType.DMA((n,)), pltpu.SemaphoreType.DMA((n,))],
        input_output_aliases={0: 0, 1: 1}, compiler_params=SPLIT_PARAMS,
    )(src, land, send_sems, recv_sems, after)
    return res[2], res[3], res[0], res[1]


def relay_wait(handle, after, *, name):
    send_sems, recv_sems, src, land = handle

    def body(src_ref, land_ref, send_ref, recv_ref, after_ref, src_out, land_out):
        for cp in _forward_copies(land_ref, send_ref, recv_ref):
            cp.wait_send()
            cp.wait_recv()

    return pl.pallas_call(
        body, name=name, in_specs=[HBM_SPEC, HBM_SPEC, SEM_SPEC, SEM_SPEC, ANY_SPEC],
        out_specs=[HBM_SPEC, HBM_SPEC],
        out_shape=[pltpu.HBM(src.shape, src.dtype), pltpu.HBM(land.shape, land.dtype)],
        input_output_aliases={0: 0, 1: 1}, compiler_params=SPLIT_PARAMS,
    )(src, land, send_sems, recv_sems, after)[1]


def kernel(x, meta_tokens, norm_mix_0, w_in_conv, conv_w, w_out_conv, norm_mlp_0, w_up_0, w_down_0, norm_mix_1, w_qkv, attn_sinks, w_o, norm_mlp_1, w_up_1, w_down_1, norm_final, loss_target, m_meta_tokens, m_norm_mix_0, m_w_in_conv, m_conv_w, m_w_out_conv, m_norm_mlp_0, m_w_up_0, m_w_down_0, m_norm_mix_1, m_w_qkv, m_attn_sinks, m_w_o, m_norm_mlp_1, m_w_up_1, m_w_down_1, m_norm_final, v_meta_tokens, v_norm_mix_0, v_w_in_conv, v_conv_w, v_w_out_conv, v_norm_mlp_0, v_w_up_0, v_w_down_0, v_norm_mix_1, v_w_qkv, v_attn_sinks, v_w_o, v_norm_mlp_1, v_w_up_1, v_w_down_1, v_norm_final):
    L, D = x.shape[1], x.shape[2]
    n_meta = meta_tokens.shape[0]
    pad = BLOCK - n_meta
    R = BLOCK + L
    n_q = D // HEAD_DIM
    n_kv = n_q // GROUP
    assert n_kv % 2 == 0 and L % BLOCK == 0 and D % 128 == 0
    x = x.reshape(L, D)
    target = loss_target.reshape(L, D)
    x_id, y_id, c_id = _coords()
    me = 4 * x_id + 2 * y_id + c_id

    col_names = ("in", "up0", "qkv", "up1")
    col_w = dict(zip(col_names, (w_in_conv, w_up_0, w_qkv, w_up_1)))
    row_names = ("out", "down0", "o", "down1")
    row_w = dict(zip(row_names, (w_out_conv, w_down_0, w_o, w_down_1)))
    me_arr = jnp.reshape(me, (1,)).astype(jnp.int32)
    natural = {k: col_w[k].T for k in col_names}
    natural.update(row_w)
    use_order = ("in", "out", "up0", "down0", "qkv", "o", "up1", "down1")
    first = cast_place(natural[use_order[0]], me_arr, me_arr, name="cast_" + use_order[0])
    first_handle, token = relay_start(first[0], first[1], name="relay_start_" + use_order[0])
    placed = [cast_place(natural[k], me_arr, token, name="cast_" + k) for k in use_order[1:]]
    handles, token = copy_start([p[0] for p in placed], [p[1] for p in placed], all_to_all=False, name="gather_start")
    handles = dict(zip(use_order[1:], handles))

    def weight(k, after):
        return copy_wait(handles[k], after, all_to_all=False, name="gather_wait_" + k)[1].reshape(-1, D)

    small_in = exchange([meta_tokens, conv_w], all_to_all=False, name="comm_gather")
    meta_full = jnp.transpose(small_in[0], (1, 0, 2)).reshape(n_meta, D)
    conv_full = jnp.transpose(small_in[1], (1, 0, 2)).reshape(conv_w.shape[0], D)

    vec = lambda a: a.reshape(1, D)
    pos = jnp.arange(R, dtype=F32) - pad
    inv = ROPE_THETA ** (-jnp.arange(0, HEAD_DIM, 2, dtype=F32) / HEAD_DIM)
    ang = pos[:, None] * inv[None, :]
    cos32, sin32 = jnp.cos(ang), jnp.sin(ang)
    cos = jnp.concatenate([cos32] * 4, axis=1)
    sin = jnp.concatenate([-sin32, sin32, -sin32, sin32], axis=1)

    W = {}
    head = jnp.concatenate([jnp.zeros((pad, D), F32), meta_full], axis=0)
    h0, n0 = first_norm(head, x, vec(norm_mix_0), token, name="norm0")
    W["in"] = relay_wait(relay_forward(first_handle, n0, name="relay_forward_in"), n0,
                         name="relay_wait_in").reshape(-1, D)
    bcu = mm(n0, W["in"], name="in_proj", out_dtype=BF16, b_rows_are_n=True)
    gated = conv_fwd(bcu, conv_full, name="conv_fwd")
    W["out"] = weight("out", gated)
    h1 = mm(gated, W["out"], name="out_proj", out_dtype=F32, b_rows_are_n=False, epi="add", extra=h0)
    n1 = norm_fwd(h1, vec(norm_mlp_0), h1, name="norm1")
    W["up0"] = weight("up0", n1)
    a0 = mm(n1, W["up0"], name="up0", out_dtype=BF16, b_rows_are_n=True, epi="relu")
    W["down0"] = weight("down0", a0)
    h2 = mm(a0, W["down0"], name="down0", out_dtype=F32, b_rows_are_n=False, epi="add", extra=h1, a_sq=True)
    n2 = norm_fwd(h2, vec(norm_mix_1), h2, name="norm2")
    W["qkv"] = weight("qkv", n2)
    qkv = mm(n2, W["qkv"], name="qkv_proj", out_dtype=F32, b_rows_are_n=True)
    q, kx, vx, kxt, vxt = rope_fwd(qkv, cos, sin, n_q=n_q, n_kv=n_kv, name="rope_fwd")
    o, lse = attn_fwd(q, kx, vxt, attn_sinks, pad=pad, name="attn_fwd")
    W["o"] = weight("o", o)
    h3 = mm(o, W["o"], name="o_proj", out_dtype=F32, b_rows_are_n=False, epi="add", extra=h2)
    n3 = norm_fwd(h3, vec(norm_mlp_1), h3, name="norm3")
    W["up1"] = weight("up1", n3)
    a1 = mm(n3, W["up1"], name="up1", out_dtype=BF16, b_rows_are_n=True, epi="relu")
    W["down1"] = weight("down1", a1)
    h4 = mm(a1, W["down1"], name="down1", out_dtype=F32, b_rows_are_n=False, epi="add", extra=h3, a_sq=True)

    loss_part, dh4, dh4b, dg_final = loss_bwd(h4, vec(norm_final), target, name="loss_bwd")
    loss = lax.psum(loss_part[0, 0], ("x", "y", "c"))

    sent = {}

    def scatter(k, dw):
        (sent[k],), tok = copy_start([dw.reshape(N_DEV, -1, D)], all_to_all=True, name="a2a_start_" + k)
        return tok

    t = scatter("down1", mm_tn(a1, dh4b, dh4b, name="dw_down1", a_sq=True))
    dup1 = mm(dh4b, W["down1"], name="d_down1", out_dtype=BF16, b_rows_are_n=True, epi="mul2a", extra=a1, dep=t)
    t = scatter("up1", mm_tn(dup1, n3, dup1, name="dw_up1"))
    dn3 = mm(dup1, W["up1"], name="d_up1", out_dtype=BF16, b_rows_are_n=False, dep=t)
    dh3, dh3b, dg_mlp1 = norm_bwd(dn3, h3, vec(norm_mlp_1), dh4, name="norm3_bwd")

    t = scatter("o", mm_tn(o, dh3b, dh3b, name="dw_o"))
    do = mm(dh3b, W["o"], name="d_o", out_dtype=BF16, b_rows_are_n=True, dep=t)
    dq, dkx, dvx, dsinks = attn_bwd(q, kx, kxt, vx, o, do, lse, attn_sinks, pad=pad, name="attn_bwd")
    dqkv = rope_bwd(dq, dkx, dvx, cos, sin, n_q=n_q, n_kv=n_kv, name="rope_bwd")
    t = scatter("qkv", mm_tn(dqkv, n2, dqkv, name="dw_qkv"))
    dn2 = mm(dqkv, W["qkv"], name="d_qkv", out_dtype=BF16, b_rows_are_n=False, dep=t)
    dh2, dh2b, dg_mix1 = norm_bwd(dn2, h2, vec(norm_mix_1), dh3, name="norm2_bwd")

    t = scatter("down0", mm_tn(a0, dh2b, dh2b, name="dw_down0", a_sq=True))
    dup0 = mm(dh2b, W["down0"], name="d_down0", out_dtype=BF16, b_rows_are_n=True, epi="mul2a", extra=a0, dep=t)
    t = scatter("up0", mm_tn(dup0, n1, dup0, name="dw_up0"))
    dn1 = mm(dup0, W["up0"], name="d_up0", out_dtype=BF16, b_rows_are_n=False, dep=t)
    dh1, dh1b, dg_mlp0 = norm_bwd(dn1, h1, vec(norm_mlp_0), dh2, name="norm1_bwd")

    t = scatter("out", mm_tn(gated, dh1b, dh1b, name="dw_out"))
    dgated = mm(dh1b, W["out"], name="d_out", out_dtype=BF16, b_rows_are_n=True, dep=t)
    dbcu, dconv = conv_bwd(bcu, conv_full, dgated, name="conv_bwd")
    t = scatter("in", mm_tn(dbcu, n0, dbcu, name="dw_in"))
    dn0 = mm(dbcu, W["in"], name="d_in", out_dtype=BF16, b_rows_are_n=False, dep=t)
    dhead, dx, dg_mix0 = last_norm_bwd(dn0, h0, vec(norm_mix_0), dh1, name="norm0_bwd")
    grad_x = dx.reshape(1, L, D)

    recv = {}
    for k in ("down1", "up1", "o", "qkv", "down0", "up0", "out", "in"):
        sent[k], recv[k] = copy_wait(sent[k], dx, all_to_all=True, name="a2a_wait_" + k)

    n_sink = attn_sinks.shape[0]
    slab = jnp.concatenate([
        dg_mix0, dg_mlp0, dg_mix1, dg_mlp1, dg_final,
        jnp.pad(dsinks[0:1, :n_sink], ((0, 0), (0, D - n_sink))), jnp.zeros((2, D), F32),
        dconv, dhead[pad:BLOCK]], axis=0)
    slabs = exchange([slab], all_to_all=False, name="comm_small")[0]
    small = sum_parts(slabs, slabs, me_arr, name="sum_small")
    cols = D // N_DEV
    my_cols = lambda a: lax.dynamic_slice_in_dim(a, me * cols, cols, axis=1)

    grads, deltas, new_m, new_v = {}, {}, {}, {}

    def update(key, w, m, v, g, shape):
        s2 = (1, -1) if w.ndim == 1 else w.shape
        if g.ndim == 3:
            fused = adam_parts if g.shape[1:] == w.shape else adam_parts_t
            g_, d_, m_, v_ = fused(w, m, v, g, sent[key_of[key]], me_arr, name="adam_" + key)
        else:
            g_ = g.reshape(s2)
            d_, m_, v_ = adam(w.reshape(s2), m.reshape(s2), v.reshape(s2), g_, name="adam_" + key)
        grads[key], deltas[key], new_m[key], new_v[key] = (t.reshape(shape) for t in (g_, d_, m_, v_))

    update("meta_tokens", meta_tokens, m_meta_tokens, v_meta_tokens, my_cols(small[16:16 + n_meta]), meta_tokens.shape)
    update("norm_mix_0", norm_mix_0, m_norm_mix_0, v_norm_mix_0, small[0], (D,))
    update("conv_w", conv_w, m_conv_w, v_conv_w, my_cols(small[8:8 + conv_w.shape[0]]), conv_w.shape)
    update("norm_mlp_0", norm_mlp_0, m_norm_mlp_0, v_norm_mlp_0, small[1], (D,))
    update("norm_mix_1", norm_mix_1, m_norm_mix_1, v_norm_mix_1, small[2], (D,))
    update("attn_sinks", attn_sinks, m_attn_sinks, v_attn_sinks, small[5, :n_sink], (n_sink,))
    update("norm_mlp_1", norm_mlp_1, m_norm_mlp_1, v_norm_mlp_1, small[3], (D,))
    update("norm_final", norm_final, m_norm_final, v_norm_final, small[4], (D,))
    big = {"in": ("w_in_conv", w_in_conv, m_w_in_conv, v_w_in_conv), "up0": ("w_up_0", w_up_0, m_w_up_0, v_w_up_0),
           "qkv": ("w_qkv", w_qkv, m_w_qkv, v_w_qkv), "up1": ("w_up_1", w_up_1, m_w_up_1, v_w_up_1),
           "out": ("w_out_conv", w_out_conv, m_w_out_conv, v_w_out_conv),
           "down0": ("w_down_0", w_down_0, m_w_down_0, v_w_down_0), "o": ("w_o", w_o, m_w_o, v_w_o),
           "down1": ("w_down_1", w_down_1, m_w_down_1, v_w_down_1)}
    key_of = {big[k][0]: k for k in big}
    for k in col_names:
        key, w, m, v = big[k]
        if w.shape[1] % 128 == 0:
            update(key, w, m, v, recv[k], w.shape)
        else:
            update(key, w, m, v, sum_parts(recv[k], sent[k], me_arr, name="sum_" + k).T, w.shape)
    for k in row_names:
        key, w, m, v = big[k]
        update(key, w, m, v, recv[k], w.shape)

    order = ("meta_tokens", "norm_mix_0", "w_in_conv", "conv_w", "w_out_conv", "norm_mlp_0", "w_up_0", "w_down_0",
             "norm_mix_1", "w_qkv", "attn_sinks", "w_o", "norm_mlp_1", "w_up_1", "w_down_1", "norm_final")
    return (loss, grad_x, *[grads[k] for k in order], *[deltas[k] for k in order],
            *[new_m[k] for k in order], *[new_v[k] for k in order])
```

```python
import jax
import jax.numpy as jnp
from jax import lax
from jax.experimental import pallas as pl
from jax.experimental.pallas import tpu as pltpu

F32 = jnp.float32
BF16 = jnp.bfloat16

HEAD_DIM = 64
GROUP = 8
BLOCK = 128
N_DEV = 8
RMS_EPS = 1e-5
NEG_INF = -1e30
ROPE_THETA = 10000.0
ADAM_LR = 0.001
ADAM_B1 = 0.9
ADAM_B2 = 0.999
ADAM_EPS = 1e-08
ADAM_WD = 0.01
ADAM_STEP = 10
MESH = pl.DeviceIdType.MESH

SUBLANE_F32 = 8
SUBLANE_BF16 = 16
LANES = 128
VMEM_LIMIT = 60 * 1024 * 1024
MM_COLS = 512
LEFT_BLOCK_BYTES = 14 * 1024 * 1024
LONG_CONTRACTION = 6144
LONG_LEFT_BLOCK_BYTES = 11 * 1024 * 1024
MM_TN_ROWS = 1024
MM_TN_TOKENS = 2080
MM_TN_CHUNK = 512
ROW_TILE = 320
NORM_ROWS = 640
ROPE_ROWS = 640
CHANNEL_TILE = 512
ADAM_ROWS = 128
ADAM_T_COLS = 256
CAST_ROWS = 256


def _tile(n, target, mult):
    best = None
    for t in range(mult, min(n, target) + 1, mult):
        if n % t == 0:
            best = t
    return best if best is not None else n


def _params(sem):
    return pltpu.CompilerParams(dimension_semantics=sem, vmem_limit_bytes=VMEM_LIMIT)


ANY_SPEC = pl.BlockSpec(memory_space=pl.ANY)


def mm(a, b, *, name, out_dtype, b_rows_are_n, epi=None, extra=None, a_sq=False, dep=None):
    M, K = a.shape
    N = b.shape[0] if b_rows_are_n else b.shape[1]
    left_bytes = LEFT_BLOCK_BYTES if K <= LONG_CONTRACTION else LONG_LEFT_BLOCK_BYTES
    tm = _tile(M, left_bytes // (K * jnp.dtype(BF16).itemsize), SUBLANE_BF16)
    tn = _tile(N, MM_COLS, LANES)

    def body(*refs):
        a_ref, b_ref, e_ref, o_ref = refs[0], refs[1], refs[2], refs[-1]
        av = a_ref[...]
        if a_sq:
            av = av.astype(F32)
            av = (av * av).astype(BF16)
        dims = (((1,), (1,)), ((), ())) if b_rows_are_n else (((1,), (0,)), ((), ()))
        acc = lax.dot_general(av, b_ref[...], dims, preferred_element_type=F32)
        if epi == "relu":
            acc = jnp.maximum(acc, 0.0)
        elif epi == "mul2a":
            acc = acc * (2.0 * e_ref[...].astype(F32))
        elif epi == "add":
            acc = acc + e_ref[...]
        o_ref[...] = acc.astype(o_ref.dtype)

    b_spec = pl.BlockSpec((tn, K), lambda i, j: (j, 0)) if b_rows_are_n else pl.BlockSpec((K, tn), lambda i, j: (0, j))
    in_specs = [pl.BlockSpec((tm, K), lambda i, j: (i, 0)), b_spec]
    args = [a, b]
    if extra is not None:
        in_specs.append(pl.BlockSpec((tm, tn), lambda i, j: (i, j)))
        args.append(extra)
    if dep is not None:
        in_specs.append(ANY_SPEC)
        args.append(dep)
    return pl.pallas_call(
        body, name=name, grid=(M // tm, N // tn), in_specs=in_specs,
        out_specs=pl.BlockSpec((tm, tn), lambda i, j: (i, j)),
        out_shape=jax.ShapeDtypeStruct((M, N), out_dtype),
        compiler_params=_params(("parallel", "parallel")),
    )(*args)


def _rms_parts(h):
    rstd = lax.rsqrt(jnp.mean(h * h, axis=-1, keepdims=True) + RMS_EPS)
    return h * rstd, rstd


def _norm_bwd_rows(dn, h, g, dres):
    xhat, rstd = _rms_parts(h)
    dxh = dn * g
    dh = dres + rstd * (dxh - xhat * jnp.mean(dxh * xhat, axis=-1, keepdims=True))
    return dh, jnp.sum(dn * xhat, axis=0, keepdims=True)


def mm_tn(a, b, dep, *, name, a_sq=False):
    T, M = a.shape
    N = b.shape[1]
    tm = _tile(M, MM_TN_ROWS, LANES)
    tk = _tile(T, MM_TN_TOKENS, SUBLANE_BF16)
    nk = T // tk
    tc = _tile(tm, MM_TN_CHUNK, LANES)

    def body(a_ref, b_ref, dep_ref, o_ref, acc_ref):
        k = pl.program_id(1)

        def chunks(first, last):
            bv = b_ref[...]
            for r0 in range(0, tm, tc):
                rows = slice(r0, r0 + tc)
                av = a_ref[:, rows]
                if a_sq:
                    av = av.astype(F32)
                    av = (av * av).astype(BF16)
                part = lax.dot_general(av, bv, (((0,), (0,)), ((), ())), preferred_element_type=F32)
                if not first:
                    part = acc_ref[rows, :] + part
                if last:
                    o_ref[rows, :] = part.astype(BF16)
                else:
                    acc_ref[rows, :] = part

        if nk == 1:
            chunks(True, True)
        else:
            pl.when(k == 0)(lambda: chunks(True, False))
            pl.when((k > 0) & (k < nk - 1))(lambda: chunks(False, False))
            pl.when(k == nk - 1)(lambda: chunks(False, True))

    return pl.pallas_call(
        body, name=name, grid=(M // tm, nk),
        in_specs=[pl.BlockSpec((tk, tm), lambda i, k: (k, i)), pl.BlockSpec((tk, N), lambda i, k: (k, 0)), ANY_SPEC],
        out_specs=pl.BlockSpec((tm, N), lambda i, k: (i, 0)),
        out_shape=jax.ShapeDtypeStruct((M, N), BF16),
        scratch_shapes=[pltpu.VMEM((tm, N), F32)],
        compiler_params=_params(("parallel", "arbitrary")),
    )(a, b, dep)


def norm_fwd(h, g, dep, *, name):
    R, D = h.shape
    tr = _tile(R, NORM_ROWS, SUBLANE_BF16)

    def body(h_ref, g_ref, dep_ref, n_ref):
        xhat, _ = _rms_parts(h_ref[...])
        n_ref[...] = (xhat * g_ref[...]).astype(BF16)

    return pl.pallas_call(
        body, name=name, grid=(R // tr,),
        in_specs=[pl.BlockSpec((tr, D), lambda i: (i, 0)), pl.BlockSpec((1, D), lambda i: (0, 0)), ANY_SPEC],
        out_specs=pl.BlockSpec((tr, D), lambda i: (i, 0)),
        out_shape=jax.ShapeDtypeStruct((R, D), BF16),
        compiler_params=_params(("parallel",)),
    )(h, g, dep)


def norm_bwd(dn, h, g, dres, *, name):
    R, D = h.shape
    tr = _tile(R, NORM_ROWS, SUBLANE_BF16)

    def body(dn_ref, h_ref, g_ref, dres_ref, dh_ref, dhb_ref, dg_ref):
        i = pl.program_id(0)
        dh, dg = _norm_bwd_rows(dn_ref[...].astype(F32), h_ref[...], g_ref[...], dres_ref[...])
        dh_ref[...] = dh
        dhb_ref[...] = dh.astype(BF16)

        @pl.when(i == 0)
        def _():
            dg_ref[...] = dg

        @pl.when(i > 0)
        def _():
            dg_ref[...] += dg

    row = pl.BlockSpec((tr, D), lambda i: (i, 0))
    vec = pl.BlockSpec((1, D), lambda i: (0, 0))
    return pl.pallas_call(
        body, name=name, grid=(R // tr,), in_specs=[row, row, vec, row], out_specs=[row, row, vec],
        out_shape=[jax.ShapeDtypeStruct((R, D), F32), jax.ShapeDtypeStruct((R, D), BF16),
                   jax.ShapeDtypeStruct((1, D), F32)],
        compiler_params=_params(("arbitrary",)),
    )(dn, h, g, dres)


def first_norm(head, x, g, dep, *, name):
    L, D = x.shape
    nb = L // BLOCK + 1

    def body(head_ref, x_ref, g_ref, dep_ref, h_ref, n_ref):
        i = pl.program_id(0)
        hv = jnp.where(i == 0, head_ref[...], x_ref[...])
        h_ref[...] = hv
        xhat, _ = _rms_parts(hv)
        n_ref[...] = (xhat * g_ref[...]).astype(BF16)

    blk = pl.BlockSpec((BLOCK, D), lambda i: (i, 0))
    return pl.pallas_call(
        body, name=name, grid=(nb,),
        in_specs=[pl.BlockSpec((BLOCK, D), lambda i: (0, 0)),
                  pl.BlockSpec((BLOCK, D), lambda i: (jnp.maximum(i - 1, 0), 0)),
                  pl.BlockSpec((1, D), lambda i: (0, 0)), ANY_SPEC],
        out_specs=[blk, blk],
        out_shape=[jax.ShapeDtypeStruct((BLOCK + L, D), F32), jax.ShapeDtypeStruct((BLOCK + L, D), BF16)],
        compiler_params=_params(("parallel",)),
    )(head, x, g, dep)


def last_norm_bwd(dn, h, g, dres, *, name):
    R, D = h.shape
    nb = R // BLOCK

    def body(dn_ref, h_ref, g_ref, dres_ref, dhead_ref, dx_ref, dg_ref):
        i = pl.program_id(0)
        dh, dg = _norm_bwd_rows(dn_ref[...].astype(F32), h_ref[...], g_ref[...], dres_ref[...])
        dx_ref[...] = dh

        @pl.when(i == 0)
        def _():
            dhead_ref[...] = dh
            dg_ref[...] = dg

        @pl.when(i > 0)
        def _():
            dg_ref[...] += dg

    blk = pl.BlockSpec((BLOCK, D), lambda i: (i, 0))
    vec = pl.BlockSpec((1, D), lambda i: (0, 0))
    return pl.pallas_call(
        body, name=name, grid=(nb,), in_specs=[blk, blk, vec, blk],
        out_specs=[pl.BlockSpec((BLOCK, D), lambda i: (0, 0)),
                   pl.BlockSpec((BLOCK, D), lambda i: (jnp.maximum(i - 1, 0), 0)), vec],
        out_shape=[jax.ShapeDtypeStruct((BLOCK, D), F32), jax.ShapeDtypeStruct((R - BLOCK, D), F32),
                   jax.ShapeDtypeStruct((1, D), F32)],
        compiler_params=_params(("arbitrary",)),
    )(dn, h, g, dres)


HALO = 16


def conv_fwd(bcu, conv_w, *, name):
    R, D3 = bcu.shape
    D = D3 // 3
    tr = _tile(R, ROW_TILE, SUBLANE_BF16)
    tc = _tile(D, CHANNEL_TILE, LANES)
    nc = D // tc
    hb = tr // HALO

    def body(b_ref, c_ref, u_ref, ch_ref, uh_ref, w_ref, o_ref, vbuf):
        i = pl.program_id(0)
        v = c_ref[...].astype(F32) * u_ref[...].astype(F32)
        vbuf[0:HALO, :] = jnp.where(i > 0, ch_ref[...].astype(F32) * uh_ref[...].astype(F32), 0.0)
        vbuf[HALO:HALO + tr, :] = v
        v1 = vbuf[HALO - 1:HALO - 1 + tr, :]
        v2 = vbuf[HALO - 2:HALO - 2 + tr, :]
        conv = w_ref[0:1, :] * v2 + w_ref[1:2, :] * v1 + w_ref[2:3, :] * v
        o_ref[...] = (b_ref[...].astype(F32) * conv).astype(BF16)

    def prev(i):
        return jnp.maximum(i * hb - 1, 0)

    return pl.pallas_call(
        body, name=name, grid=(R // tr, nc),
        in_specs=[pl.BlockSpec((tr, tc), lambda i, j: (i, j)),
                  pl.BlockSpec((tr, tc), lambda i, j: (i, nc + j)),
                  pl.BlockSpec((tr, tc), lambda i, j: (i, 2 * nc + j)),
                  pl.BlockSpec((HALO, tc), lambda i, j: (prev(i), nc + j)),
                  pl.BlockSpec((HALO, tc), lambda i, j: (prev(i), 2 * nc + j)),
                  pl.BlockSpec((3, tc), lambda i, j: (0, j))],
        out_specs=pl.BlockSpec((tr, tc), lambda i, j: (i, j)),
        out_shape=jax.ShapeDtypeStruct((R, D), BF16),
        scratch_shapes=[pltpu.VMEM((HALO + tr, tc), F32)],
        compiler_params=_params(("parallel", "parallel")),
    )(bcu, bcu, bcu, bcu, bcu, conv_w)


def conv_bwd(bcu, conv_w, dg, *, name):
    R, D3 = bcu.shape
    D = D3 // 3
    tr = _tile(R, ROW_TILE, SUBLANE_BF16)
    tc = _tile(D, CHANNEL_TILE, LANES)
    hb = tr // HALO
    nt = R // tr
    last_h = R // HALO - 1

    def body(x_ref, xp_ref, xn_ref, dg_ref, dgn_ref, w_ref, o_ref, dw_ref, vbuf, dbuf):
        i = pl.program_id(0)
        row8 = lax.broadcasted_iota(jnp.int32, (8, tc), 0)
        for c0 in range(0, D, tc):
            cb, cc, cu = slice(c0, c0 + tc), slice(D + c0, D + c0 + tc), slice(2 * D + c0, 2 * D + c0 + tc)
            w0, w1, w2 = w_ref[0:1, cb], w_ref[1:2, cb], w_ref[2:3, cb]
            b = x_ref[:, cb].astype(F32)
            c = x_ref[:, cc].astype(F32)
            u = x_ref[:, cu].astype(F32)
            v = c * u
            vbuf[0:HALO, :] = jnp.where(i > 0, xp_ref[:, cc].astype(F32) * xp_ref[:, cu].astype(F32), 0.0)
            vbuf[HALO:HALO + tr, :] = v
            v1 = vbuf[HALO - 1:HALO - 1 + tr, :]
            v2 = vbuf[HALO - 2:HALO - 2 + tr, :]
            dgv = dg_ref[:, cb].astype(F32)
            o_ref[:, cb] = (dgv * (w0 * v2 + w1 * v1 + w2 * v)).astype(BF16)
            dconv = dgv * b
            dbuf[0:tr, :] = dconv
            dbuf[tr:tr + HALO, :] = jnp.where(i < nt - 1, dgn_ref[:, cb].astype(F32) * xn_ref[:, cb].astype(F32), 0.0)
            dv = w2 * dconv + w1 * dbuf[1:1 + tr, :] + w0 * dbuf[2:2 + tr, :]
            o_ref[:, cc] = (dv * u).astype(BF16)
            o_ref[:, cu] = (dv * c).astype(BF16)
            dw = jnp.where(row8 == 0, jnp.sum(dconv * v2, axis=0, keepdims=True),
                           jnp.where(row8 == 1, jnp.sum(dconv * v1, axis=0, keepdims=True),
                                     jnp.where(row8 == 2, jnp.sum(dconv * v, axis=0, keepdims=True), 0.0)))

            @pl.when(i == 0)
            def _():
                dw_ref[:, cb] = dw

            @pl.when(i > 0)
            def _():
                dw_ref[:, cb] += dw

    def prev(i):
        return jnp.maximum(i * hb - 1, 0)

    def nxt(i):
        return jnp.minimum((i + 1) * hb, last_h)

    return pl.pallas_call(
        body, name=name, grid=(nt,),
        in_specs=[pl.BlockSpec((tr, D3), lambda i: (i, 0)),
                  pl.BlockSpec((HALO, D3), lambda i: (prev(i), 0)),
                  pl.BlockSpec((HALO, D3), lambda i: (nxt(i), 0)),
                  pl.BlockSpec((tr, D), lambda i: (i, 0)),
                  pl.BlockSpec((HALO, D), lambda i: (nxt(i), 0)),
                  pl.BlockSpec((3, D), lambda i: (0, 0))],
        out_specs=[pl.BlockSpec((tr, D3), lambda i: (i, 0)), pl.BlockSpec((8, D), lambda i: (0, 0))],
        out_shape=[jax.ShapeDtypeStruct((R, D3), BF16), jax.ShapeDtypeStruct((8, D), F32)],
        scratch_shapes=[pltpu.VMEM((HALO + tr, tc), F32), pltpu.VMEM((tr + HALO, tc), F32)],
        compiler_params=_params(("arbitrary",)),
    )(bcu, bcu, bcu, dg, dg, conv_w)


def _swap32(x):
    w = x.shape[1]
    lane = lax.broadcasted_iota(jnp.int32, x.shape, 1)
    return jnp.where((lane & (HEAD_DIM - 1)) < HEAD_DIM // 2, pltpu.roll(x, w - HEAD_DIM // 2, axis=1),
                     pltpu.roll(x, HEAD_DIM // 2, axis=1))


def _rope(x, cos, sin):
    return x * cos + _swap32(x) * sin


def rope_fwd(qkv, cos, sin, *, n_q, n_kv, name):
    R, W = qkv.shape
    qw = n_q * HEAD_DIM
    kw = n_kv * HEAD_DIM
    tr = _tile(R, ROPE_ROWS, LANES)

    def expand(y, ref, ref_t, c):
        lane = lax.broadcasted_iota(jnp.int32, y.shape, 1)
        lo = jnp.where(lane < HEAD_DIM, y, 0.0)
        hi = jnp.where(lane >= HEAD_DIM, y, 0.0)
        tiles = (lo, pltpu.roll(lo, HEAD_DIM, axis=1), pltpu.roll(hi, HEAD_DIM, axis=1), hi)
        for t, tile in enumerate(tiles):
            c0 = 512 * c + 128 * t
            ref[:, c0:c0 + 128] = tile.astype(BF16)
            ref_t[c0:c0 + 128, :] = tile.T.astype(BF16)

    def body(x_ref, c_ref, s_ref, q_ref, k_ref, v_ref, kt_ref, vt_ref):
        cos = c_ref[...]
        sin = s_ref[...]
        for c in range(qw // 128):
            x = x_ref[:, 128 * c:128 * (c + 1)].astype(F32)
            q_ref[:, 128 * c:128 * (c + 1)] = (_rope(x, cos, sin) * (HEAD_DIM ** -0.5)).astype(BF16)
        for c in range(kw // 128):
            expand(_rope(x_ref[:, qw + 128 * c:qw + 128 * (c + 1)].astype(F32), cos, sin), k_ref, kt_ref, c)
            expand(x_ref[:, qw + kw + 128 * c:qw + kw + 128 * (c + 1)].astype(F32), v_ref, vt_ref, c)

    row = lambda w: pl.BlockSpec((tr, w), lambda i: (i, 0))
    col = pl.BlockSpec((4 * kw, tr), lambda i: (0, i))
    return pl.pallas_call(
        body, name=name, grid=(R // tr,),
        in_specs=[row(W), row(128), row(128)],
        out_specs=[row(qw), row(4 * kw), row(4 * kw), col, col],
        out_shape=[jax.ShapeDtypeStruct((R, qw), BF16), jax.ShapeDtypeStruct((R, 4 * kw), BF16),
                   jax.ShapeDtypeStruct((R, 4 * kw), BF16), jax.ShapeDtypeStruct((4 * kw, R), BF16),
                   jax.ShapeDtypeStruct((4 * kw, R), BF16)],
        compiler_params=_params(("parallel",)),
    )(qkv, cos, sin)


def rope_bwd(dq, dkx, dvx, cos, sin, *, n_q, n_kv, name):
    R = dq.shape[0]
    qw = n_q * HEAD_DIM
    kw = n_kv * HEAD_DIM
    W = qw + 2 * kw
    tr = _tile(R, ROW_TILE, SUBLANE_BF16)

    def fold(ref, c):
        lane = lax.broadcasted_iota(jnp.int32, (tr, 128), 1)
        x0 = ref[:, 128 * (2 * c):128 * (2 * c + 1)]
        x1 = ref[:, 128 * (2 * c + 1):128 * (2 * c + 2)]
        f0 = x0 + pltpu.roll(x0, HEAD_DIM, axis=1)
        f1 = x1 + pltpu.roll(x1, HEAD_DIM, axis=1)
        return jnp.where(lane < HEAD_DIM, f0, f1)

    def body(dq_ref, dk_ref, dv_ref, c_ref, s_ref, o_ref):
        cos = c_ref[...]
        nsin = -s_ref[...]
        for c in range(qw // 128):
            y = dq_ref[:, 128 * c:128 * (c + 1)].astype(F32)
            o_ref[:, 128 * c:128 * (c + 1)] = (_rope(y, cos, nsin) * (HEAD_DIM ** -0.5)).astype(BF16)
        for c in range(kw // 128):
            o_ref[:, qw + 128 * c:qw + 128 * (c + 1)] = _rope(fold(dk_ref, c), cos, nsin).astype(BF16)
            o_ref[:, qw + kw + 128 * c:qw + kw + 128 * (c + 1)] = fold(dv_ref, c).astype(BF16)

    row = lambda w: pl.BlockSpec((tr, w), lambda i: (i, 0))
    return pl.pallas_call(
        body, name=name, grid=(R // tr,),
        in_specs=[row(qw), row(2 * kw), row(2 * kw), row(128), row(128)],
        out_specs=row(W),
        out_shape=jax.ShapeDtypeStruct((R, W), BF16),
        compiler_params=_params(("parallel",)),
    )(dq, dkx, dvx, cos, sin)


def _band_bias(n, pad):
    key = lax.broadcasted_iota(jnp.int32, (2 * BLOCK, BLOCK), 0)
    qry = lax.broadcasted_iota(jnp.int32, (2 * BLOCK, BLOCK), 1)
    kmin = jnp.where(n == 0, BLOCK + pad, jnp.where(n == 1, pad, 0))
    allowed = (key > qry) & (key <= qry + BLOCK) & (key >= kmin)
    return jnp.where(allowed, 0.0, NEG_INF).astype(F32)


def _dot_nt(a, b):
    return lax.dot_general(a, b, (((1,), (1,)), ((), ())), preferred_element_type=F32)


def _band(prev_ref, cur_ref, c0):
    return jnp.concatenate([prev_ref[:, c0:c0 + 128], cur_ref[:, c0:c0 + 128]], axis=0)


def _band_t(prev_ref, cur_ref, r0):
    return jnp.concatenate([prev_ref[r0:r0 + 128, :], cur_ref[r0:r0 + 128, :]], axis=1)


def attn_fwd(q, kx, vxt, sinks, *, pad, name):
    R, qw = q.shape
    n_q = qw // HEAD_DIM
    n_kv = kx.shape[1] // 256
    nb = R // BLOCK

    def body(s_ref, q_ref, kc_ref, kp_ref, vc_ref, vp_ref, o_ref, l_ref):
        n = pl.program_id(0)
        bias = _band_bias(n, pad)
        row = lax.broadcasted_iota(jnp.int32, (128, BLOCK), 0)

        def softmax(st, sink):
            st = st + bias
            m = jnp.maximum(jnp.max(st, axis=0, keepdims=True), sink)
            e = jnp.exp(st - m)
            den = jnp.sum(e, axis=0, keepdims=True) + jnp.exp(sink - m)
            return e.astype(BF16), 1.0 / den, m + jnp.log(den)

        for g in range(n_kv):
            k2 = jnp.concatenate([_band(kp_ref, kc_ref, 256 * g), _band(kp_ref, kc_ref, 256 * g + 128)], axis=0)
            v2 = jnp.concatenate([_band_t(vp_ref, vc_ref, 256 * g), _band_t(vp_ref, vc_ref, 256 * g + 128)], axis=1)
            for p in range(GROUP // 2):
                c0 = 128 * (g * (GROUP // 2) + p)
                he = GROUP * g + 2 * p
                q2 = q_ref[:, c0:c0 + 128]
                st = _dot_nt(k2, q2)
                ee, re, le = softmax(st[0:2 * BLOCK], s_ref[he])
                eo, ro, lo = softmax(st[2 * BLOCK:4 * BLOCK], s_ref[he + 1])
                o2t = jnp.dot(v2, jnp.concatenate([ee, eo], axis=0), preferred_element_type=F32)
                o2t = o2t * jnp.where(row < HEAD_DIM, re, ro)
                o_ref[:, c0:c0 + 128] = o2t.T.astype(BF16)
                l_ref[he:he + 1, :] = le
                l_ref[he + 1:he + 2, :] = lo

    cur = lambda w: pl.BlockSpec((BLOCK, w), lambda n: (n, 0))
    prv = lambda w: pl.BlockSpec((BLOCK, w), lambda n: (jnp.maximum(n - 1, 0), 0))
    cur_t = lambda h: pl.BlockSpec((h, BLOCK), lambda n: (0, n))
    prv_t = lambda h: pl.BlockSpec((h, BLOCK), lambda n: (0, jnp.maximum(n - 1, 0)))
    kxw = kx.shape[1]
    return pl.pallas_call(
        body, name=name, grid=(nb,),
        in_specs=[pl.BlockSpec(memory_space=pltpu.SMEM), cur(qw), cur(kxw), prv(kxw), cur_t(kxw), prv_t(kxw)],
        out_specs=[cur(qw), cur_t(n_q)],
        out_shape=[jax.ShapeDtypeStruct((R, qw), BF16), jax.ShapeDtypeStruct((n_q, R), F32)],
        compiler_params=_params(("parallel",)),
    )(sinks, q, kx, kx, vxt, vxt)


def attn_bwd(q, kx, kxt, vx, o, do, lse, sinks, *, pad, name):
    R, qw = q.shape
    n_q = qw // HEAD_DIM
    n_kv = kx.shape[1] // 256
    nb = R // BLOCK
    kw2 = n_kv * 128

    def body(s_ref, q_ref, do_ref, o_ref, l_ref, kc_ref, kp_ref, ktc_ref, ktp_ref, vc_ref, vp_ref,
             dq_ref, dk_ref, dv_ref, ds_ref, cdk, cdv):
        n = pl.program_id(0)

        @pl.when(n == 0)
        def _():
            cdk[...] = jnp.zeros_like(cdk)
            cdv[...] = jnp.zeros_like(cdv)
            ds_ref[...] = jnp.zeros_like(ds_ref)

        @pl.when(n < nb)
        def _():
            bias = _band_bias(n, pad)
            lane2 = lax.broadcasted_iota(jnp.int32, (2 * BLOCK, 128), 1)
            lane1 = lax.broadcasted_iota(jnp.int32, (1, 128), 1)
            sel_r = lax.broadcasted_iota(jnp.int32, (8, 128), 0)
            sel_l = lax.broadcasted_iota(jnp.int32, (8, 128), 1)
            sel = (((sel_r == 0) & (sel_l < HEAD_DIM)) | ((sel_r == 1) & (sel_l >= HEAD_DIM))).astype(BF16)
            dsink = jnp.zeros((1, 128), F32)
            for g in range(n_kv):
                k2 = jnp.concatenate([_band(kp_ref, kc_ref, 256 * g), _band(kp_ref, kc_ref, 256 * g + 128)], axis=0)
                kt2 = jnp.concatenate([_band_t(ktp_ref, ktc_ref, 256 * g), _band_t(ktp_ref, ktc_ref, 256 * g + 128)],
                                      axis=1)
                v2 = jnp.concatenate([_band(vp_ref, vc_ref, 256 * g), _band(vp_ref, vc_ref, 256 * g + 128)], axis=0)
                dk4 = jnp.zeros((4 * BLOCK, 128), F32)
                dv4 = jnp.zeros((4 * BLOCK, 128), F32)
                for p in range(GROUP // 2):
                    c0 = 128 * (g * (GROUP // 2) + p)
                    he = GROUP * g + 2 * p
                    q2 = q_ref[:, c0:c0 + 128]
                    do2 = do_ref[:, c0:c0 + 128]
                    prod = do2.astype(F32) * o_ref[:, c0:c0 + 128].astype(F32)
                    prod_hi = prod.astype(BF16)
                    prod_lo = (prod - prod_hi.astype(F32)).astype(BF16)
                    deltas = _dot_nt(sel, prod_hi) + _dot_nt(sel, prod_lo)
                    st = _dot_nt(k2, q2)
                    dpt = _dot_nt(v2, do2)
                    pts, dsts = [], []
                    for r in range(2):
                        h = he + r
                        rows = slice(2 * BLOCK * r, 2 * BLOCK * (r + 1))
                        lse_h = l_ref[h:h + 1, :]
                        delta = deltas[r:r + 1, :]
                        pt = jnp.exp(st[rows] + bias - lse_h)
                        pts.append(pt.astype(BF16))
                        dsts.append((pt * (dpt[rows] - delta)).astype(BF16))
                        psink = jnp.exp(s_ref[h] - lse_h)
                        tot = jnp.sum(psink * delta, axis=1, keepdims=True)
                        dsink = dsink - jnp.where(lane1 == h, tot, 0.0)
                    dst = jnp.concatenate(dsts, axis=0)
                    dq_ref[:, c0:c0 + 128] = jnp.dot(kt2, dst, preferred_element_type=F32).T.astype(BF16)
                    dk4 = dk4 + jnp.dot(dst, q2, preferred_element_type=F32)
                    dv4 = dv4 + jnp.dot(jnp.concatenate(pts, axis=0), do2, preferred_element_type=F32)
                gc = pl.ds(128 * g, 128)
                dk2 = jnp.where(lane2 < HEAD_DIM, dk4[0:2 * BLOCK], dk4[2 * BLOCK:4 * BLOCK])
                dv2 = jnp.where(lane2 < HEAD_DIM, dv4[0:2 * BLOCK], dv4[2 * BLOCK:4 * BLOCK])
                dk_ref[:, gc] = cdk[:, gc] + dk2[0:BLOCK]
                dv_ref[:, gc] = cdv[:, gc] + dv2[0:BLOCK]
                cdk[:, gc] = dk2[BLOCK:2 * BLOCK]
                cdv[:, gc] = dv2[BLOCK:2 * BLOCK]
            ds_ref[0:1, :] += dsink

        @pl.when(n == nb)
        def _():
            dk_ref[...] = cdk[...]
            dv_ref[...] = cdv[...]

    cur = lambda w: pl.BlockSpec((BLOCK, w), lambda n: (jnp.minimum(n, nb - 1), 0))
    prv = lambda w: pl.BlockSpec((BLOCK, w), lambda n: (jnp.clip(n - 1, 0, nb - 1), 0))
    cur_t = lambda h: pl.BlockSpec((h, BLOCK), lambda n: (0, jnp.minimum(n, nb - 1)))
    prv_t = lambda h: pl.BlockSpec((h, BLOCK), lambda n: (0, jnp.clip(n - 1, 0, nb - 1)))
    kxw = kx.shape[1]
    return pl.pallas_call(
        body, name=name, grid=(nb + 1,),
        in_specs=[pl.BlockSpec(memory_space=pltpu.SMEM), cur(qw), cur(qw), cur(qw), cur_t(n_q),
                  cur(kxw), prv(kxw), cur_t(kxw), prv_t(kxw), cur(kxw), prv(kxw)],
        out_specs=[cur(qw), prv(kw2), prv(kw2), pl.BlockSpec((8, 128), lambda n: (0, 0))],
        out_shape=[jax.ShapeDtypeStruct((R, qw), BF16), jax.ShapeDtypeStruct((R, kw2), F32),
                   jax.ShapeDtypeStruct((R, kw2), F32), jax.ShapeDtypeStruct((8, 128), F32)],
        scratch_shapes=[pltpu.VMEM((BLOCK, kw2), F32), pltpu.VMEM((BLOCK, kw2), F32)],
        compiler_params=_params(("arbitrary",)),
    )(sinks, q, do, o, lse, kx, kx, kxt, kxt, vx, vx)


def loss_bwd(h, g, target, *, name):
    R, D = h.shape
    nb = R // BLOCK

    def body(h_ref, g_ref, t_ref, loss_ref, dh_ref, dhb_ref, dg_ref):
        n = pl.program_id(0)
        xhat, rstd = _rms_parts(h_ref[...])
        gv = g_ref[...]
        diff = jnp.where(n > 0, xhat * gv - t_ref[...], 0.0)
        part = (0.5 / D) * jnp.sum(jnp.sum(diff * diff, axis=1, keepdims=True), axis=0, keepdims=True)
        dout = diff * (1.0 / D)
        dxh = dout * gv
        dh = rstd * (dxh - xhat * jnp.mean(dxh * xhat, axis=-1, keepdims=True))
        dh_ref[...] = dh
        dhb_ref[...] = dh.astype(BF16)
        dg = jnp.sum(dout * xhat, axis=0, keepdims=True)

        @pl.when(n == 0)
        def _():
            loss_ref[...] = jnp.zeros_like(loss_ref) + part
            dg_ref[...] = dg

        @pl.when(n > 0)
        def _():
            loss_ref[...] += part
            dg_ref[...] += dg

    blk = pl.BlockSpec((BLOCK, D), lambda n: (n, 0))
    return pl.pallas_call(
        body, name=name, grid=(nb,),
        in_specs=[blk, pl.BlockSpec((1, D), lambda n: (0, 0)),
                  pl.BlockSpec((BLOCK, D), lambda n: (jnp.maximum(n - 1, 0), 0))],
        out_specs=[pl.BlockSpec((8, 128), lambda n: (0, 0)), blk, blk, pl.BlockSpec((1, D), lambda n: (0, 0))],
        out_shape=[jax.ShapeDtypeStruct((8, 128), F32), jax.ShapeDtypeStruct((R, D), F32),
                   jax.ShapeDtypeStruct((R, D), BF16), jax.ShapeDtypeStruct((1, D), F32)],
        compiler_params=_params(("arbitrary",)),
    )(h, g, target)


def _adam_math(w, g, m, v):
    m = ADAM_B1 * m + (1.0 - ADAM_B1) * g
    v = ADAM_B2 * v + (1.0 - ADAM_B2) * (g * g)
    m_hat = m / (1.0 - ADAM_B1 ** ADAM_STEP)
    v_hat = v / (1.0 - ADAM_B2 ** ADAM_STEP)
    delta = -ADAM_LR * (m_hat / (jnp.sqrt(v_hat) + ADAM_EPS) + ADAM_WD * w)
    return delta, m, v


def adam(w, m, v, g, *, name):
    r, C = w.shape
    tr = _tile(r, ADAM_ROWS, SUBLANE_F32)

    def body(w_ref, m_ref, v_ref, g_ref, d_ref, mo_ref, vo_ref):
        d_ref[...], mo_ref[...], vo_ref[...] = _adam_math(w_ref[...], g_ref[...], m_ref[...], v_ref[...])

    blk = pl.BlockSpec((tr, C), lambda i: (i, 0))
    return pl.pallas_call(
        body, name=name, grid=(r // tr,), in_specs=[blk] * 4, out_specs=[blk] * 3,
        out_shape=[jax.ShapeDtypeStruct((r, C), F32)] * 3,
        compiler_params=_params(("parallel",)),
    )(w, m, v, g)


def _sum_blocks(p_ref, own_ref, me):
    acc = None
    for s in range(N_DEV):
        blk = jnp.where(me == s, own_ref[...], p_ref[s]).astype(F32)
        acc = blk if acc is None else acc + blk
    return acc


def _parts_specs(r, C, tr):
    blk = pl.BlockSpec((tr, C), lambda i, me: (i, 0))
    parts = pl.BlockSpec((N_DEV, tr, C), lambda i, me: (0, i, 0))
    own = pl.BlockSpec((None, tr, C), lambda i, me: (me[0], i, 0))
    return blk, parts, own


def adam_parts(w, m, v, parts, own, me, *, name):
    r, C = w.shape
    tr = _tile(r, ADAM_ROWS, SUBLANE_F32)

    def body(me_ref, w_ref, m_ref, v_ref, p_ref, own_ref, go_ref, d_ref, mo_ref, vo_ref):
        gv = _sum_blocks(p_ref, own_ref, me_ref[0])
        go_ref[...] = gv
        d_ref[...], mo_ref[...], vo_ref[...] = _adam_math(w_ref[...], gv, m_ref[...], v_ref[...])

    blk, pblk, oblk = _parts_specs(r, C, tr)
    return pl.pallas_call(
        body, name=name,
        grid_spec=pltpu.PrefetchScalarGridSpec(num_scalar_prefetch=1, grid=(r // tr,),
                                               in_specs=[blk, blk, blk, pblk, oblk], out_specs=[blk] * 4),
        out_shape=[jax.ShapeDtypeStruct((r, C), F32)] * 4,
        compiler_params=_params(("parallel",)),
    )(me, w, m, v, parts, own)


def adam_parts_t(w, m, v, parts, own, me, *, name):
    K, r = w.shape
    tk = _tile(K, ADAM_T_COLS, LANES)

    def body(me_ref, w_ref, m_ref, v_ref, p_ref, own_ref, go_ref, d_ref, mo_ref, vo_ref):
        gv = _sum_blocks(p_ref, own_ref, me_ref[0]).T
        go_ref[...] = gv
        d_ref[...], mo_ref[...], vo_ref[...] = _adam_math(w_ref[...], gv, m_ref[...], v_ref[...])

    blk = pl.BlockSpec((tk, r), lambda i, me: (i, 0))
    pblk = pl.BlockSpec((N_DEV, r, tk), lambda i, me: (0, 0, i))
    oblk = pl.BlockSpec((None, r, tk), lambda i, me: (me[0], 0, i))
    return pl.pallas_call(
        body, name=name,
        grid_spec=pltpu.PrefetchScalarGridSpec(num_scalar_prefetch=1, grid=(K // tk,),
                                               in_specs=[blk, blk, blk, pblk, oblk], out_specs=[blk] * 4),
        out_shape=[jax.ShapeDtypeStruct((K, r), F32)] * 4,
        compiler_params=_params(("parallel",)),
    )(me, w, m, v, parts, own)


def sum_parts(parts, own, me, *, name):
    _, r, C = parts.shape
    tr = _tile(r, ADAM_ROWS, SUBLANE_F32)

    def body(me_ref, p_ref, own_ref, o_ref):
        o_ref[...] = _sum_blocks(p_ref, own_ref, me_ref[0])

    blk, pblk, oblk = _parts_specs(r, C, tr)
    return pl.pallas_call(
        body, name=name,
        grid_spec=pltpu.PrefetchScalarGridSpec(num_scalar_prefetch=1, grid=(r // tr,),
                                               in_specs=[pblk, oblk], out_specs=blk),
        out_shape=jax.ShapeDtypeStruct((r, C), F32),
        compiler_params=_params(("parallel",)),
    )(me, parts, own)


def cast_place(w, me, dep, *, name):
    r, C = w.shape
    tr = _tile(r, CAST_ROWS, SUBLANE_BF16)

    def body(me_ref, w_ref, dep_ref, s_ref, l_ref):
        v = w_ref[...].astype(BF16)
        s_ref[...] = v
        l_ref[...] = v

    blk = pl.BlockSpec((tr, C), lambda i, me: (i, 0))
    return pl.pallas_call(
        body, name=name,
        grid_spec=pltpu.PrefetchScalarGridSpec(
            num_scalar_prefetch=1, grid=(r // tr,), in_specs=[blk, ANY_SPEC],
            out_specs=[blk, pl.BlockSpec((None, tr, C), lambda i, me: (me[0], i, 0))]),
        out_shape=[jax.ShapeDtypeStruct((r, C), BF16), jax.ShapeDtypeStruct((N_DEV, r, C), BF16)],
        compiler_params=_params(("parallel",)),
    )(me, w, dep)


def _coords():
    return lax.axis_index("x"), lax.axis_index("y"), lax.axis_index("c")


def _peer(m):
    x, y, c = _coords()
    px = 1 - x if m & 4 else x
    py = 1 - y if m & 2 else y
    pc = 1 - c if m & 1 else c
    return (px, py, pc), 4 * px + 2 * py + pc


def exchange(items, *, all_to_all, name):
    n = len(items)
    if all_to_all:
        out_shape = [jax.ShapeDtypeStruct(a.shape, a.dtype) for a in items]
    else:
        out_shape = [jax.ShapeDtypeStruct((N_DEV,) + a.shape, a.dtype) for a in items]

    def body(*refs):
        ins, outs = refs[:n], refs[n:2 * n]
        send_sems, recv_sems, local_sems = refs[2 * n:]
        x, y, c = _coords()
        me = 4 * x + 2 * y + c

        def src(i, idx):
            return ins[i].at[idx] if all_to_all else ins[i]

        local = [pltpu.make_async_copy(src(i, me), outs[i].at[me], local_sems.at[i]) for i in range(n)]
        for cp in local:
            cp.start()
        sends = []
        for m in range(1, N_DEV):
            peer, pidx = _peer(m)
            for i in range(n):
                k = i * (N_DEV - 1) + m - 1
                cp = pltpu.make_async_remote_copy(src_ref=src(i, pidx), dst_ref=outs[i].at[me],
                                                  send_sem=send_sems.at[k], recv_sem=recv_sems.at[k],
                                                  device_id=peer, device_id_type=MESH)
                cp.start()
                sends.append(cp)
        for m in range(1, N_DEV):
            peer, pidx = _peer(m)
            for i in range(n):
                k = i * (N_DEV - 1) + m - 1
                pltpu.make_async_remote_copy(src_ref=src(i, pidx), dst_ref=outs[i].at[pidx],
                                             send_sem=send_sems.at[k], recv_sem=recv_sems.at[k],
                                             device_id=peer, device_id_type=MESH).wait_recv()
        for cp in sends:
            cp.wait_send()
        for cp in local:
            cp.wait()

    any_spec = pl.BlockSpec(memory_space=pl.ANY)
    return pl.pallas_call(
        body, name=name, in_specs=[any_spec] * n, out_specs=[any_spec] * n, out_shape=out_shape,
        scratch_shapes=[pltpu.SemaphoreType.DMA((n * (N_DEV - 1),)), pltpu.SemaphoreType.DMA((n * (N_DEV - 1),)),
                        pltpu.SemaphoreType.DMA((n,))],
    )(*items)


HBM_SPEC = pl.BlockSpec(memory_space=pltpu.HBM)
SEM_SPEC = pl.BlockSpec(memory_space=pltpu.SEMAPHORE)
SPLIT_PARAMS = pltpu.CompilerParams(has_side_effects=pltpu.SideEffectType.DATAFLOW_SIDE_EFFECTING)


def _split_copies(src_ref, land_ref, send_sems, recv_sems, all_to_all):
    x, y, c = _coords()
    me = 4 * x + 2 * y + c
    copies = []
    for m in range(1, N_DEV):
        peer, pidx = _peer(m)
        copies.append(pltpu.make_async_remote_copy(
            src_ref=src_ref.at[pidx] if all_to_all else src_ref, dst_ref=land_ref.at[me],
            send_sem=send_sems.at[m - 1], recv_sem=recv_sems.at[m - 1], device_id=peer, device_id_type=MESH))
    return copies


def copy_start(items, lands=None, *, all_to_all, name):
    n = len(items)
    if lands is None:
        lands = [lax.empty(a.shape if all_to_all else (N_DEV,) + a.shape, a.dtype) for a in items]

    def body(*refs):
        srcs, lnds, outs = refs[:n], refs[n:2 * n], refs[2 * n:]
        for i in range(n):
            for cp in _split_copies(srcs[i], lnds[i], outs[4 * i], outs[4 * i + 1], all_to_all):
                cp.start()
        outs[4 * n][...] = jnp.zeros((8, 128), F32)

    out_shape, out_specs, aliases = [], [], {}
    for i, (a, l) in enumerate(zip(items, lands)):
        out_shape += [pltpu.SemaphoreType.DMA((N_DEV - 1,)), pltpu.SemaphoreType.DMA((N_DEV - 1,)),
                      pltpu.HBM(a.shape, a.dtype), pltpu.HBM(l.shape, l.dtype)]
        out_specs += [SEM_SPEC, SEM_SPEC, HBM_SPEC, HBM_SPEC]
        aliases[i] = 4 * i + 2
        aliases[n + i] = 4 * i + 3
    out_shape.append(jax.ShapeDtypeStruct((8, 128), F32))
    out_specs.append(pl.BlockSpec(memory_space=pltpu.VMEM))
    hbm = lambda a: pltpu.with_memory_space_constraint(a, pltpu.HBM)
    res = pl.pallas_call(
        body, name=name, in_specs=[HBM_SPEC] * (2 * n), out_specs=out_specs, out_shape=out_shape,
        input_output_aliases=aliases, compiler_params=SPLIT_PARAMS,
    )(*[hbm(a) for a in items], *[hbm(l) for l in lands])
    return [tuple(res[4 * i:4 * i + 4]) for i in range(n)], res[4 * n]


def copy_wait(handle, after, *, all_to_all, name):
    send_sems, recv_sems, src, land = handle

    def body(src_ref, land_ref, send_ref, recv_ref, after_ref, src_out, got_ref):
        for cp in _split_copies(src_ref, land_ref, send_ref, recv_ref, all_to_all):
            cp.wait_send()
            cp.wait_recv()

    return pl.pallas_call(
        body, name=name, in_specs=[HBM_SPEC, HBM_SPEC, SEM_SPEC, SEM_SPEC, ANY_SPEC],
        out_specs=[HBM_SPEC, HBM_SPEC],
        out_shape=[pltpu.HBM(src.shape, src.dtype), pltpu.HBM(land.shape, land.dtype)],
        input_output_aliases={0: 0, 1: 1}, compiler_params=SPLIT_PARAMS,
    )(src, land, send_sems, recv_sems, after)


OTHER_CHIPS = (4, 2, 6)
SIBLING = 1


def _relay_copies(src_ref, land_ref, send_sems, recv_sems):
    x, y, c = _coords()
    me = 4 * x + 2 * y + c
    return [pltpu.make_async_remote_copy(src_ref=src_ref, dst_ref=land_ref.at[me], send_sem=send_sems.at[k],
                                         recv_sem=recv_sems.at[k], device_id=_peer(m)[0], device_id_type=MESH)
            for k, m in enumerate((SIBLING,) + OTHER_CHIPS)]


def _forward_copies(land_ref, send_sems, recv_sems):
    copies = []
    for k, m in enumerate(OTHER_CHIPS):
        pidx = _peer(m)[1]
        copies.append(pltpu.make_async_remote_copy(
            src_ref=land_ref.at[pidx], dst_ref=land_ref.at[pidx], send_sem=send_sems.at[k], recv_sem=recv_sems.at[k],
            device_id=_peer(SIBLING)[0], device_id_type=MESH))
    return copies


def relay_start(item, land, *, name):
    def body(src_ref, land_ref, send_sems, recv_sems, src_out, land_out, token):
        for cp in _relay_copies(src_ref, land_ref, send_sems, recv_sems):
            cp.start()
        token[...] = jnp.zeros((8, 128), F32)

    n = 1 + len(OTHER_CHIPS)
    hbm = lambda a: pltpu.with_memory_space_constraint(a, pltpu.HBM)
    res = pl.pallas_call(
        body, name=name, in_specs=[HBM_SPEC, HBM_SPEC],
        out_specs=[SEM_SPEC, SEM_SPEC, HBM_SPEC, HBM_SPEC, pl.BlockSpec(memory_space=pltpu.VMEM)],
        out_shape=[pltpu.SemaphoreType.DMA((n,)), pltpu.SemaphoreType.DMA((n,)), pltpu.HBM(item.shape, item.dtype),
                   pltpu.HBM(land.shape, land.dtype), jax.ShapeDtypeStruct((8, 128), F32)],
        input_output_aliases={0: 2, 1: 3}, compiler_params=SPLIT_PARAMS,
    )(hbm(item), hbm(land))
    return tuple(res[:4]), res[4]


def relay_forward(handle, after, *, name):
    send_sems, recv_sems, src, land = handle

    def body(src_ref, land_ref, send_ref, recv_ref, after_ref, src_out, land_out, send2, recv2):
        for cp in _relay_copies(src_ref, land_ref, send_ref, recv_ref):
            cp.wait_send()
            cp.wait_recv()
        for cp in _forward_copies(land_ref, send2, recv2):
            cp.start()

    n = len(OTHER_CHIPS)
    res = pl.pallas_call(
        body, name=name, in_specs=[HBM_SPEC, HBM_SPEC, SEM_SPEC, SEM_SPEC, ANY_SPEC],
        out_specs=[HBM_SPEC, HBM_SPEC, SEM_SPEC, SEM_SPEC],
        out_shape=[pltpu.HBM(src.shape, src.dtype), pltpu.HBM(land.shape, land.dtype),
                   pltpu.SemaphoreType.DMA((n,)), pltpu.SemaphoreType.DMA((n,))],
        input_output_aliases={0: 0, 1: 1}, compiler_params=SPLIT_PARAMS,
    )(src, land, send_sems, recv_sems, after)
    return res[2], res[3], res[0], res[1]


def relay_wait(handle, after, *, name):
    send_sems, recv_sems, src, land = handle

    def body(src_ref, land_ref, send_ref, recv_ref, after_ref, src_out, land_out):
        for cp in _forward_copies(land_ref, send_ref, recv_ref):
            cp.wait_send()
            cp.wait_recv()

    return pl.pallas_call(
        body, name=name, in_specs=[HBM_SPEC, HBM_SPEC, SEM_SPEC, SEM_SPEC, ANY_SPEC],
        out_specs=[HBM_SPEC, HBM_SPEC],
        out_shape=[pltpu.HBM(src.shape, src.dtype), pltpu.HBM(land.shape, land.dtype)],
        input_output_aliases={0: 0, 1: 1}, compiler_params=SPLIT_PARAMS,
    )(src, land, send_sems, recv_sems, after)[1]


def kernel(x, meta_tokens, norm_mix_0, w_in_conv, conv_w, w_out_conv, norm_mlp_0, w_up_0, w_down_0, norm_mix_1, w_qkv, attn_sinks, w_o, norm_mlp_1, w_up_1, w_down_1, norm_final, loss_target, m_meta_tokens, m_norm_mix_0, m_w_in_conv, m_conv_w, m_w_out_conv, m_norm_mlp_0, m_w_up_0, m_w_down_0, m_norm_mix_1, m_w_qkv, m_attn_sinks, m_w_o, m_norm_mlp_1, m_w_up_1, m_w_down_1, m_norm_final, v_meta_tokens, v_norm_mix_0, v_w_in_conv, v_conv_w, v_w_out_conv, v_norm_mlp_0, v_w_up_0, v_w_down_0, v_norm_mix_1, v_w_qkv, v_attn_sinks, v_w_o, v_norm_mlp_1, v_w_up_1, v_w_down_1, v_norm_final):
    L, D = x.shape[1], x.shape[2]
    n_meta = meta_tokens.shape[0]
    pad = BLOCK - n_meta
    R = BLOCK + L
    n_q = D // HEAD_DIM
    n_kv = n_q // GROUP
    assert n_kv % 2 == 0 and L % BLOCK == 0 and D % 128 == 0
    x = x.reshape(L, D)
    target = loss_target.reshape(L, D)
    x_id, y_id, c_id = _coords()
    me = 4 * x_id + 2 * y_id + c_id

    col_names = ("in", "up0", "qkv", "up1")
    col_w = dict(zip(col_names, (w_in_conv, w_up_0, w_qkv, w_up_1)))
    row_names = ("out", "down0", "o", "down1")
    row_w = dict(zip(row_names, (w_out_conv, w_down_0, w_o, w_down_1)))
    me_arr = jnp.reshape(me, (1,)).astype(jnp.int32)
    natural = {k: col_w[k].T for k in col_names}
    natural.update(row_w)
    use_order = ("in", "out", "up0", "down0", "qkv", "o", "up1", "down1")
    first = cast_place(natural[use_order[0]], me_arr, me_arr, name="cast_" + use_order[0])
    first_handle, token = relay_start(first[0], first[1], name="relay_start_" + use_order[0])
    placed = [cast_place(natural[k], me_arr, token, name="cast_" + k) for k in use_order[1:]]
    handles, token = copy_start([p[0] for p in placed], [p[1] for p in placed], all_to_all=False, name="gather_start")
    handles = dict(zip(use_order[1:], handles))

    def weight(k, after):
        return copy_wait(handles[k], after, all_to_all=False, name="gather_wait_" + k)[1].reshape(-1, D)

    small_in = exchange([meta_tokens, conv_w], all_to_all=False, name="comm_gather")
    meta_full = jnp.transpose(small_in[0], (1, 0, 2)).reshape(n_meta, D)
    conv_full = jnp.transpose(small_in[1], (1, 0, 2)).reshape(conv_w.shape[0], D)

    vec = lambda a: a.reshape(1, D)
    pos = jnp.arange(R, dtype=F32) - pad
    inv = ROPE_THETA ** (-jnp.arange(0, HEAD_DIM, 2, dtype=F32) / HEAD_DIM)
    ang = pos[:, None] * inv[None, :]
    cos32, sin32 = jnp.cos(ang), jnp.sin(ang)
    cos = jnp.concatenate([cos32] * 4, axis=1)
    sin = jnp.concatenate([-sin32, sin32, -sin32, sin32], axis=1)

    W = {}
    head = jnp.concatenate([jnp.zeros((pad, D), F32), meta_full], axis=0)
    h0, n0 = first_norm(head, x, vec(norm_mix_0), token, name="norm0")
    W["in"] = relay_wait(relay_forward(first_handle, n0, name="relay_forward_in"), n0,
                         name="relay_wait_in").reshape(-1, D)
    bcu = mm(n0, W["in"], name="in_proj", out_dtype=BF16, b_rows_are_n=True)
    gated = conv_fwd(bcu, conv_full, name="conv_fwd")
    W["out"] = weight("out", gated)
    h1 = mm(gated, W["out"], name="out_proj", out_dtype=F32, b_rows_are_n=False, epi="add", extra=h0)
    n1 = norm_fwd(h1, vec(norm_mlp_0), h1, name="norm1")
    W["up0"] = weight("up0", n1)
    a0 = mm(n1, W["up0"], name="up0", out_dtype=BF16, b_rows_are_n=True, epi="relu")
    W["down0"] = weight("down0", a0)
    h2 = mm(a0, W["down0"], name="down0", out_dtype=F32, b_rows_are_n=False, epi="add", extra=h1, a_sq=True)
    n2 = norm_fwd(h2, vec(norm_mix_1), h2, name="norm2")
    W["qkv"] = weight("qkv", n2)
    qkv = mm(n2, W["qkv"], name="qkv_proj", out_dtype=BF16, b_rows_are_n=True)
    q, kx, vx, kxt, vxt = rope_fwd(qkv, cos, sin, n_q=n_q, n_kv=n_kv, name="rope_fwd")
    o, lse = attn_fwd(q, kx, vxt, attn_sinks, pad=pad, name="attn_fwd")
    W["o"] = weight("o", o)
    h3 = mm(o, W["o"], name="o_proj", out_dtype=F32, b_rows_are_n=False, epi="add", extra=h2)
    n3 = norm_fwd(h3, vec(norm_mlp_1), h3, name="norm3")
    W["up1"] = weight("up1", n3)
    a1 = mm(n3, W["up1"], name="up1", out_dtype=BF16, b_rows_are_n=True, epi="relu")
    W["down1"] = weight("down1", a1)
    h4 = mm(a1, W["down1"], name="down1", out_dtype=F32, b_rows_are_n=False, epi="add", extra=h3, a_sq=True)

    loss_part, dh4, dh4b, dg_final = loss_bwd(h4, vec(norm_final), target, name="loss_bwd")
    loss = lax.psum(loss_part[0, 0], ("x", "y", "c"))

    sent = {}

    def scatter(k, dw):
        (sent[k],), tok = copy_start([dw.reshape(N_DEV, -1, D)], all_to_all=True, name="a2a_start_" + k)
        return tok

    t = scatter("down1", mm_tn(a1, dh4b, dh4b, name="dw_down1", a_sq=True))
    dup1 = mm(dh4b, W["down1"], name="d_down1", out_dtype=BF16, b_rows_are_n=True, epi="mul2a", extra=a1, dep=t)
    t = scatter("up1", mm_tn(dup1, n3, dup1, name="dw_up1"))
    dn3 = mm(dup1, W["up1"], name="d_up1", out_dtype=BF16, b_rows_are_n=False, dep=t)
    dh3, dh3b, dg_mlp1 = norm_bwd(dn3, h3, vec(norm_mlp_1), dh4, name="norm3_bwd")

    t = scatter("o", mm_tn(o, dh3b, dh3b, name="dw_o"))
    do = mm(dh3b, W["o"], name="d_o", out_dtype=BF16, b_rows_are_n=True, dep=t)
    dq, dkx, dvx, dsinks = attn_bwd(q, kx, kxt, vx, o, do, lse, attn_sinks, pad=pad, name="attn_bwd")
    dqkv = rope_bwd(dq, dkx, dvx, cos, sin, n_q=n_q, n_kv=n_kv, name="rope_bwd")
    t = scatter("qkv", mm_tn(dqkv, n2, dqkv, name="dw_qkv"))
    dn2 = mm(dqkv, W["qkv"], name="d_qkv", out_dtype=BF16, b_rows_are_n=False, dep=t)
    dh2, dh2b, dg_mix1 = norm_bwd(dn2, h2, vec(norm_mix_1), dh3, name="norm2_bwd")

    t = scatter("down0", mm_tn(a0, dh2b, dh2b, name="dw_down0", a_sq=True))
    dup0 = mm(dh2b, W["down0"], name="d_down0", out_dtype=BF16, b_rows_are_n=True, epi="mul2a", extra=a0, dep=t)
    t = scatter("up0", mm_tn(dup0, n1, dup0, name="dw_up0"))
    dn1 = mm(dup0, W["up0"], name="d_up0", out_dtype=BF16, b_rows_are_n=False, dep=t)
    dh1, dh1b, dg_mlp0 = norm_bwd(dn1, h1, vec(norm_mlp_0), dh2, name="norm1_bwd")

    t = scatter("out", mm_tn(gated, dh1b, dh1b, name="dw_out"))
    dgated = mm(dh1b, W["out"], name="d_out", out_dtype=BF16, b_rows_are_n=True, dep=t)
    dbcu, dconv = conv_bwd(bcu, conv_full, dgated, name="conv_bwd")
    t = scatter("in", mm_tn(dbcu, n0, dbcu, name="dw_in"))
    dn0 = mm(dbcu, W["in"], name="d_in", out_dtype=BF16, b_rows_are_n=False, dep=t)
    dhead, dx, dg_mix0 = last_norm_bwd(dn0, h0, vec(norm_mix_0), dh1, name="norm0_bwd")
    grad_x = dx.reshape(1, L, D)

    recv = {}
    for k in ("down1", "up1", "o", "qkv", "down0", "up0", "out", "in"):
        sent[k], recv[k] = copy_wait(sent[k], dx, all_to_all=True, name="a2a_wait_" + k)

    n_sink = attn_sinks.shape[0]
    slab = jnp.concatenate([
        dg_mix0, dg_mlp0, dg_mix1, dg_mlp1, dg_final,
        jnp.pad(dsinks[0:1, :n_sink], ((0, 0), (0, D - n_sink))), jnp.zeros((2, D), F32),
        dconv, dhead[pad:BLOCK]], axis=0)
    slabs = exchange([slab], all_to_all=False, name="comm_small")[0]
    small = sum_parts(slabs, slabs, me_arr, name="sum_small")
    cols = D // N_DEV
    my_cols = lambda a: lax.dynamic_slice_in_dim(a, me * cols, cols, axis=1)

    grads, deltas, new_m, new_v = {}, {}, {}, {}

    def update(key, w, m, v, g, shape):
        s2 = (1, -1) if w.ndim == 1 else w.shape
        if g.ndim == 3:
            fused = adam_parts if g.shape[1:] == w.shape else adam_parts_t
            g_, d_, m_, v_ = fused(w, m, v, g, sent[key_of[key]], me_arr, name="adam_" + key)
        else:
            g_ = g.reshape(s2)
            d_, m_, v_ = adam(w.reshape(s2), m.reshape(s2), v.reshape(s2), g_, name="adam_" + key)
        grads[key], deltas[key], new_m[key], new_v[key] = (t.reshape(shape) for t in (g_, d_, m_, v_))

    update("meta_tokens", meta_tokens, m_meta_tokens, v_meta_tokens, my_cols(small[16:16 + n_meta]), meta_tokens.shape)
    update("norm_mix_0", norm_mix_0, m_norm_mix_0, v_norm_mix_0, small[0], (D,))
    update("conv_w", conv_w, m_conv_w, v_conv_w, my_cols(small[8:8 + conv_w.shape[0]]), conv_w.shape)
    update("norm_mlp_0", norm_mlp_0, m_norm_mlp_0, v_norm_mlp_0, small[1], (D,))
    update("norm_mix_1", norm_mix_1, m_norm_mix_1, v_norm_mix_1, small[2], (D,))
    update("attn_sinks", attn_sinks, m_attn_sinks, v_attn_sinks, small[5, :n_sink], (n_sink,))
    update("norm_mlp_1", norm_mlp_1, m_norm_mlp_1, v_norm_mlp_1, small[3], (D,))
    update("norm_final", norm_final, m_norm_final, v_norm_final, small[4], (D,))
    big = {"in": ("w_in_conv", w_in_conv, m_w_in_conv, v_w_in_conv), "up0": ("w_up_0", w_up_0, m_w_up_0, v_w_up_0),
           "qkv": ("w_qkv", w_qkv, m_w_qkv, v_w_qkv), "up1": ("w_up_1", w_up_1, m_w_up_1, v_w_up_1),
           "out": ("w_out_conv", w_out_conv, m_w_out_conv, v_w_out_conv),
           "down0": ("w_down_0", w_down_0, m_w_down_0, v_w_down_0), "o": ("w_o", w_o, m_w_o, v_w_o),
           "down1": ("w_down_1", w_down_1, m_w_down_1, v_w_down_1)}
    key_of = {big[k][0]: k for k in big}
    for k in col_names:
        key, w, m, v = big[k]
        if w.shape[1] % 128 == 0:
            update(key, w, m, v, recv[k], w.shape)
        else:
            update(key, w, m, v, sum_parts(recv[k], sent[k], me_arr, name="sum_" + k).T, w.shape)
    for k in row_names:
        key, w, m, v = big[k]
        update(key, w, m, v, recv[k], w.shape)

    order = ("meta_tokens", "norm_mix_0", "w_in_conv", "conv_w", "w_out_conv", "norm_mlp_0", "w_up_0", "w_down_0",
             "norm_mix_1", "w_qkv", "attn_sinks", "w_o", "norm_mlp_1", "w_up_1", "w_down_1", "norm_final")
    return (loss, grad_x, *[grads[k] for k in order], *[deltas[k] for k in order],
            *[new_m[k] for k in order], *[new_v[k] for k in order])
```
